```python
import jax, jax.numpy as jnp
from jax import lax
import numpy as np

D_MODEL = 1024
BATCH = 16
SEQ = 256
DEPTH = 1
DEC_BATCH = 4
DEC_SEQ = 1024
PAST_LEN = 512

GRID_W = 64
CHUNK = 128
N_HEADS = 8
QK_NOPE = 64
QK_ROPE = 32
V_DIM = 64
Q_RANK = 384
KV_RANK = 256
ATTN_W = N_HEADS * V_DIM
GMLP_W = D_MODEL - ATTN_W
GMLP_GROUPS = 4
GMLP_DG = GMLP_W // GMLP_GROUPS
D_FF = 2816
CONV_W = 3
ROPE_THETA = 10000.0
EPS = 1e-6
Q_BLOCK = 128
IN_W = Q_RANK + KV_RANK + QK_ROPE + 2 * GMLP_W
SPLITS = [Q_RANK, Q_RANK + KV_RANK, Q_RANK + KV_RANK + QK_ROPE, Q_RANK + KV_RANK + QK_ROPE + GMLP_W]

kernel_name = "hymba_mla_gmlp_convffn_dit_step"


def rmsnorm(x, g):
    xf = x.astype(jnp.float32)
    y = xf * lax.rsqrt(jnp.mean(xf * xf, axis=-1, keepdims=True) + EPS)
    return (y * g.astype(jnp.float32)).astype(x.dtype)


def group_rmsnorm(x, g):
    shape = x.shape
    xf = x.astype(jnp.float32).reshape(shape[:-1] + (GMLP_GROUPS, GMLP_DG))
    y = xf * lax.rsqrt(jnp.mean(xf * xf, axis=-1, keepdims=True) + EPS)
    return (y.reshape(shape) * g.astype(jnp.float32)).astype(x.dtype)


def rope_angles(length):
    rows = length // GRID_W
    pos = jnp.arange(rows * GRID_W)
    row = (pos // GRID_W).astype(jnp.float32)
    col = (pos % GRID_W).astype(jnp.float32)
    n_freq = QK_ROPE // 4
    inv = ROPE_THETA ** (-(jnp.arange(n_freq, dtype=jnp.float32) / n_freq))
    return row[:, None] * inv, col[:, None] * inv


def rotate(x, ang):
    half = x.shape[-1] // 2
    x1, x2 = x[..., :half], x[..., half:]
    cs, sn = jnp.cos(ang).astype(x.dtype), jnp.sin(ang).astype(x.dtype)
    return jnp.concatenate([x1 * cs - x2 * sn, x2 * cs + x1 * sn], axis=-1)


def rope_2d(x, ang_r, ang_c):
    half = QK_ROPE // 2
    return jnp.concatenate([rotate(x[..., :half], ang_r), rotate(x[..., half:], ang_c)], axis=-1)


def adaln_mod(cond, ada_w, ada_b):
    m = jax.nn.silu(cond) @ ada_w + ada_b
    return jnp.split(m, 6, axis=-1)


def mla_expand(ckv, w_ukv):
    B, L, _ = ckv.shape
    kv = (ckv @ w_ukv).reshape(B, L, N_HEADS, QK_NOPE + V_DIM)
    return kv[..., :QK_NOPE], kv[..., QK_NOPE:]


def mla_attention(q_nope, q_rope, k_nope, k_rope, v):
    B, L, H, _ = q_nope.shape
    nb = L // Q_BLOCK
    scale = (QK_NOPE + QK_ROPE) ** -0.5
    qn = q_nope.reshape(B, nb, Q_BLOCK, H, QK_NOPE).swapaxes(0, 1)
    qr = q_rope.reshape(B, nb, Q_BLOCK, H, QK_ROPE).swapaxes(0, 1)

    def block(args):
        qn_b, qr_b = args
        s = jnp.einsum("bqhd,bkhd->bhqk", qn_b, k_nope) + jnp.einsum("bqhr,bkr->bhqk", qr_b, k_rope)
        pr = jax.nn.softmax(s.astype(jnp.float32) * scale, axis=-1).astype(v.dtype)
        return jnp.einsum("bhqk,bkhd->bqhd", pr, v)

    o = lax.map(block, (qn, qr))
    return o.swapaxes(0, 1).reshape(B, L, H * V_DIM)


def spatial_gating(u, v, v_norm_g, w_s, b_s):
    B, L, _ = u.shape
    u = jax.nn.gelu(u, approximate=False)
    v = group_rmsnorm(jax.nn.gelu(v, approximate=False), v_norm_g)
    vc = v.reshape(B, L // CHUNK, CHUNK, GMLP_GROUPS, GMLP_DG)
    s = jnp.einsum("gij,bnjgc->bnigc", w_s, vc) + b_s.T[None, None, :, :, None]
    return u * s.reshape(B, L, GMLP_W)


def conv_ffn(h, w_up, conv_w, conv_b, w_down):
    L = h.shape[1]
    a = h @ w_up
    pad = CONV_W // 2
    ap = jnp.pad(a, ((0, 0), (pad, pad), (0, 0)))
    acc = conv_b
    for k in range(CONV_W):
        acc = acc + ap[:, k:k + L] * conv_w[k]
    gate, val = jnp.split(acc, 2, axis=-1)
    return (jax.nn.silu(gate) * val) @ w_down


def trunk_layer(x, mods, p, ctx=None, ang=None):
    shift1, scale1, gate1, shift2, scale2, gate2 = mods
    B, L, _ = x.shape
    h = rmsnorm(x, p["norm1_g"]) * (1 + scale1) + shift1
    q_lat, kv_lat, k_rope, u, v = jnp.split(h @ p["w_in"], SPLITS, axis=-1)
    q = (rmsnorm(q_lat, p["q_norm_g"]) @ p["w_uq"]).reshape(B, L, N_HEADS, QK_NOPE + QK_ROPE)
    q_nope, q_rope = q[..., :QK_NOPE], q[..., QK_NOPE:]
    ckv = rmsnorm(kv_lat, p["kv_norm_g"])
    k_nope, val = mla_expand(ckv, p["w_ukv"])
    k_rope_own = k_rope
    if ctx is not None:
        ang_r, ang_c = ang
        q_rope = rope_2d(q_rope, ang_r[:, None, :], ang_c[:, None, :])
        k_rope_lat = rope_2d(k_rope, ang_r, ang_c)
        ckv_ctx, krope_ctx = ctx
        kn_ctx, v_ctx = mla_expand(ckv_ctx, p["w_ukv"])
        k_nope = jnp.concatenate([kn_ctx, k_nope], axis=1)
        val = jnp.concatenate([v_ctx, val], axis=1)
        k_rope = jnp.concatenate([krope_ctx, k_rope_lat], axis=1)
    attn = mla_attention(q_nope, q_rope, k_nope, k_rope, val)
    g_out = spatial_gating(u, v, p["v_norm_g"], p["w_s"], p["b_s"])
    mix = jnp.concatenate([rmsnorm(attn, p["out_g_attn"]), rmsnorm(g_out, p["out_g_gmlp"])], axis=-1)
    x = x + gate1 * (mix @ p["w_o"])
    h2 = rmsnorm(x, p["norm2_g"]) * (1 + scale2) + shift2
    x = x + gate2 * conv_ffn(h2, p["w_up"], p["conv_w"], p["conv_b"], p["w_down"])
    return x, ckv, k_rope_own


def setup_inputs(seed: int = 0) -> dict:
    key = jax.random.key(seed)
    ks = jax.random.split(key, 32)

    def nrm(k, shape, scale=1.0):
        return jax.random.normal(k, shape, jnp.float32) * scale

    def gain(k, shape):
        return 1.0 + 0.02 * jax.random.normal(k, shape, jnp.float32)

    L = DEPTH
    return {
        "x_prompt": nrm(ks[0], (BATCH, SEQ, D_MODEL)),
        "x_sample": nrm(ks[1], (DEC_BATCH, DEC_SEQ, D_MODEL)),
        "cache_ckv": nrm(ks[2], (DEC_BATCH, DEPTH, PAST_LEN, KV_RANK)),
        "cache_krope": nrm(ks[3], (DEC_BATCH, DEPTH, PAST_LEN, QK_ROPE)),
        "c": nrm(ks[4], (DEC_BATCH, D_MODEL)),
        "c_ctx": nrm(ks[5], (D_MODEL,)),
        "ada_w": nrm(ks[6], (L, D_MODEL, 6 * D_MODEL), 0.5 * D_MODEL ** -0.5),
        "ada_b": nrm(ks[7], (L, 6 * D_MODEL), 0.02),
        "norm1_g": gain(ks[8], (L, D_MODEL)),
        "w_in": nrm(ks[9], (L, D_MODEL, IN_W), D_MODEL ** -0.5),
        "q_norm_g": gain(ks[10], (L, Q_RANK)),
        "w_uq": nrm(ks[11], (L, Q_RANK, N_HEADS * (QK_NOPE + QK_ROPE)), Q_RANK ** -0.5),
        "kv_norm_g": gain(ks[12], (L, KV_RANK)),
        "w_ukv": nrm(ks[13], (L, KV_RANK, N_HEADS * (QK_NOPE + V_DIM)), KV_RANK ** -0.5),
        "v_norm_g": gain(ks[14], (L, GMLP_W)),
        "w_s": nrm(ks[15], (L, GMLP_GROUPS, CHUNK, CHUNK), CHUNK ** -0.5),
        "b_s": gain(ks[16], (L, GMLP_GROUPS, CHUNK)),
        "out_g_attn": gain(ks[17], (L, ATTN_W)),
        "out_g_gmlp": gain(ks[18], (L, GMLP_W)),
        "w_o": nrm(ks[19], (L, D_MODEL, D_MODEL), D_MODEL ** -0.5),
        "norm2_g": gain(ks[20], (L, D_MODEL)),
        "w_up": nrm(ks[21], (L, D_MODEL, 2 * D_FF), D_MODEL ** -0.5),
        "conv_w": nrm(ks[22], (L, CONV_W, 2 * D_FF), CONV_W ** -0.5),
        "conv_b": nrm(ks[23], (L, 2 * D_FF), 0.02),
        "w_down": nrm(ks[24], (L, D_FF, D_MODEL), D_FF ** -0.5),
        "final_g": gain(ks[25], (D_MODEL,)),
    }


def reference(x_prompt, x_sample, cache_ckv, cache_krope, c, c_ctx, ada_w, ada_b, norm1_g, w_in,
              q_norm_g, w_uq, kv_norm_g, w_ukv, v_norm_g, w_s, b_s, out_g_attn, out_g_gmlp, w_o,
              norm2_g, w_up, conv_w, conv_b, w_down, final_g):
    ang = rope_angles(x_sample.shape[1])
    xp, xs = x_prompt, x_sample
    ckv_list, krope_list = [], []
    for l in range(DEPTH):
        p = {
            "norm1_g": norm1_g[l], "w_in": w_in[l], "q_norm_g": q_norm_g[l], "w_uq": w_uq[l],
            "kv_norm_g": kv_norm_g[l], "w_ukv": w_ukv[l], "v_norm_g": v_norm_g[l], "w_s": w_s[l],
            "b_s": b_s[l], "out_g_attn": out_g_attn[l], "out_g_gmlp": out_g_gmlp[l], "w_o": w_o[l],
            "norm2_g": norm2_g[l], "w_up": w_up[l], "conv_w": conv_w[l], "conv_b": conv_b[l],
            "w_down": w_down[l],
        }
        mods_ctx = adaln_mod(c_ctx, ada_w[l], ada_b[l])
        mods_lat = adaln_mod(c[:, None, :], ada_w[l], ada_b[l])
        xp, ckv_l, krope_l = trunk_layer(xp, mods_ctx, p)
        ckv_list.append(ckv_l)
        krope_list.append(krope_l)
        xs, _, _ = trunk_layer(xs, mods_lat, p, ctx=(cache_ckv[:, l], cache_krope[:, l]), ang=ang)
    new_ckv = jnp.stack(ckv_list, axis=1)
    new_krope = jnp.stack(krope_list, axis=1)
    y_prompt = rmsnorm(xp, final_g)
    y_sample = rmsnorm(xs, final_g)
    return (y_prompt, y_sample, new_ckv, new_krope)
```

```python
import functools

import jax
import jax.numpy as jnp
from jax import lax
from jax.experimental import pallas as pl
from jax.experimental.pallas import tpu as pltpu

D_MODEL = 1024
GRID_W = 64
CHUNK = 128
N_HEADS = 8
QK_NOPE = 64
QK_ROPE = 32
V_DIM = 64
Q_RANK = 384
KV_RANK = 256
ATTN_W = N_HEADS * V_DIM
GMLP_W = D_MODEL - ATTN_W
GMLP_GROUPS = 4
GMLP_DG = GMLP_W // GMLP_GROUPS
D_FF = 2816
CONV_W = 3
ROPE_THETA = 10000.0
EPS = 1e-6
SM_SCALE = (QK_NOPE + QK_ROPE) ** -0.5

LANES = 128
HEAD_PAD = LANES
ROPE_LANE0 = QK_NOPE
QK_W = N_HEADS * HEAD_PAD
N_MODS = 6
MOD_ROWS = 8
IN_QKV = Q_RANK + KV_RANK
IN_UV0 = IN_QKV
IN_KR0 = IN_QKV + 2 * GMLP_W
IN_W_R = IN_KR0 + LANES
FF_CHUNK = 256
N_FF_CHUNKS = D_FF // FF_CHUNK
VMEM_LIMIT = 56 * 1024 * 1024

F32 = jnp.float32
BF16 = jnp.bfloat16


def _dot(a, b):
    return jnp.dot(a, b, preferred_element_type=F32)


def _dot_nt(a, b):
    return lax.dot_general(a, b, (((1,), (1,)), ((), ())), preferred_element_type=F32)


def _rms(x, g):
    return x * lax.rsqrt(jnp.mean(x * x, axis=-1, keepdims=True) + EPS) * g


def _gelu(x):
    return 0.5 * x * (1.0 + lax.erf(x * (0.5 ** 0.5)))


def _silu(x):
    return x * (1.0 / (1.0 + jnp.exp(-x)))


def _params(n_axes):
    return pltpu.CompilerParams(dimension_semantics=("arbitrary",) * n_axes,
                                vmem_limit_bytes=VMEM_LIMIT)


def _adaln_kernel(cond_ref, w_ref, b_ref, out_ref):
    s = _silu(cond_ref[...]).astype(BF16)
    out_ref[...] = _dot(s, w_ref[...].astype(BF16)) + b_ref[...]


def _adaln(cond, ada_w, ada_b):
    return pl.pallas_call(
        _adaln_kernel,
        grid=(N_MODS,),
        in_specs=[
            pl.BlockSpec((MOD_ROWS, D_MODEL), lambda j: (0, 0)),
            pl.BlockSpec((D_MODEL, D_MODEL), lambda j: (0, j)),
            pl.BlockSpec((1, D_MODEL), lambda j: (0, j)),
        ],
        out_specs=pl.BlockSpec((MOD_ROWS, D_MODEL), lambda j: (0, j)),
        out_shape=jax.ShapeDtypeStruct((MOD_ROWS, N_MODS * D_MODEL), F32),
        compiler_params=_params(1),
        name="adaln",
    )(cond, ada_w, ada_b)


def _rope_parts(lane):
    first_half = (lane % 16) < 8
    return first_half


def _pre_kernel(*refs, rope, tm):
    if rope:
        (x_ref, mod_ref, n1g_ref, win_ref, qng_ref, wuq_ref, kvg_ref, wk_ref, wve_ref, wvo_ref,
         vng_ref, ws_ref, bst_ref, ogg_ref, cos_ref, sin_ref,
         q_out, k_out, ve_out, vo_out, gm_out, ckv_out, kr_out, go_sc) = refs
    else:
        (x_ref, mod_ref, n1g_ref, win_ref, qng_ref, wuq_ref, kvg_ref, wk_ref, wve_ref, wvo_ref,
         vng_ref, ws_ref, bst_ref, ogg_ref,
         q_out, k_out, ve_out, vo_out, gm_out, ckv_out, kr_out, go_sc) = refs
    shift1 = mod_ref[:, 0:D_MODEL]
    scale1 = mod_ref[:, D_MODEL:2 * D_MODEL]
    h = (_rms(x_ref[...], n1g_ref[...]) * (1.0 + scale1) + shift1).astype(BF16)

    lane = lax.broadcasted_iota(jnp.int32, (1, LANES), 1)
    rope_lanes = lane >= ROPE_LANE0
    first_half = (lane % 16) < 8

    def rotate(v, cos):
        swapped = jnp.where(first_half, pltpu.roll(v, LANES - 8, 1), pltpu.roll(v, 8, 1))
        return v * cos + swapped * sin_ref[...]

    qkv = _dot(h, win_ref[:, 0:IN_QKV])
    qn = _rms(qkv[:, 0:Q_RANK], qng_ref[...]).astype(BF16)
    ckv = _rms(qkv[:, Q_RANK:IN_QKV], kvg_ref[...])
    ckv_out[...] = ckv
    ckv_b = ckv.astype(BF16)
    q = _dot(qn, wuq_ref[...]) * SM_SCALE
    kr = _dot(h, win_ref[:, IN_KR0:IN_W_R])
    kr_out[...] = kr[:, 0:QK_ROPE]
    if rope:
        cos_q = cos_ref[...]
        kr_k = rotate(kr, jnp.where(rope_lanes, cos_q, 0.0))
    else:
        kr_k = jnp.where(rope_lanes, kr, 0.0)
    kn = _dot(ckv_b, wk_ref[...])
    for hd in range(N_HEADS):
        sl = slice(hd * HEAD_PAD, (hd + 1) * HEAD_PAD)
        qh = q[:, sl]
        if rope:
            qh = rotate(qh, cos_q)
        q_out[:, sl] = qh.astype(BF16)
        k_out[:, sl] = (kn[:, sl] + kr_k).astype(BF16)
    ve_out[...] = _dot(ckv_b, wve_ref[...]).astype(BF16)
    vo_out[...] = _dot(ckv_b, wvo_ref[...]).astype(BF16)

    uv = _dot(h, win_ref[:, IN_UV0:IN_KR0])
    gu = _gelu(uv[:, 0:GMLP_W])
    gv = _gelu(uv[:, GMLP_W:2 * GMLP_W])
    for g in range(GMLP_GROUPS):
        sl = slice(g * GMLP_DG, (g + 1) * GMLP_DG)
        vg = gv[:, sl]
        vg = (vg * lax.rsqrt(jnp.mean(vg * vg, axis=-1, keepdims=True) + EPS) * vng_ref[:, sl]).astype(BF16)
        bias = bst_ref[:, g:g + 1]
        for n in range(tm // CHUNK):
            rows = slice(n * CHUNK, (n + 1) * CHUNK)
            s = _dot(ws_ref[g], vg[rows, :]) + bias
            go_sc[rows, sl] = gu[rows, sl] * s
    gm_out[...] = _rms(go_sc[...], ogg_ref[...]).astype(BF16)


def _pre(x, mods3, mod_row, p, rope_tabs, tm):
    n = x.shape[0]
    rope = rope_tabs is not None
    const = lambda i: (0, 0)
    tile = lambda i: (i, 0)
    in_specs = [
        pl.BlockSpec((tm, D_MODEL), tile),
        pl.BlockSpec((None, 1, N_MODS * D_MODEL), lambda i: (mod_row(i), 0, 0)),
        pl.BlockSpec((1, D_MODEL), const),
        pl.BlockSpec((D_MODEL, IN_W_R), const),
        pl.BlockSpec((1, Q_RANK), const),
        pl.BlockSpec((Q_RANK, QK_W), const),
        pl.BlockSpec((1, KV_RANK), const),
        pl.BlockSpec((KV_RANK, QK_W), const),
        pl.BlockSpec((KV_RANK, ATTN_W), const),
        pl.BlockSpec((KV_RANK, ATTN_W), const),
        pl.BlockSpec((1, GMLP_W), const),
        pl.BlockSpec((GMLP_GROUPS, CHUNK, CHUNK), lambda i: (0, 0, 0)),
        pl.BlockSpec((CHUNK, GMLP_GROUPS), const),
        pl.BlockSpec((1, GMLP_W), const),
    ]
    args = [x, mods3, p["norm1_g"], p["w_in"], p["q_norm_g"], p["w_uq"], p["kv_norm_g"], p["w_k"],
            p["w_ve"], p["w_vo"], p["v_norm_g"], p["w_s"], p["b_st"], p["out_g_gmlp"]]
    if rope:
        cos_q, sin_q = rope_tabs
        per_seq = cos_q.shape[0] // tm
        in_specs += [pl.BlockSpec((tm, LANES), lambda i: (i % per_seq, 0))] * 2
        args += [cos_q, sin_q]
    widths = [(QK_W, BF16), (QK_W, BF16), (ATTN_W, BF16), (ATTN_W, BF16), (GMLP_W, BF16),
              (KV_RANK, F32), (QK_ROPE, F32)]
    return pl.pallas_call(
        functools.partial(_pre_kernel, rope=rope, tm=tm),
        grid=(n // tm,),
        in_specs=in_specs,
        out_specs=[pl.BlockSpec((tm, w), tile) for w, _ in widths],
        out_shape=[jax.ShapeDtypeStruct((n, w), dt) for w, dt in widths],
        scratch_shapes=[pltpu.VMEM((tm, GMLP_W), F32)],
        compiler_params=_params(1),
        name="pre_rope" if rope else "pre",
    )(*args)


def _expand_kernel(ckv_ref, kr_ref, wk_ref, wve_ref, wvo_ref, k_out, ve_out, vo_out):
    ckv_b = ckv_ref[...].astype(BF16)
    kn = _dot(ckv_b, wk_ref[...])
    kr = kr_ref[...]
    for hd in range(N_HEADS):
        sl = slice(hd * HEAD_PAD, (hd + 1) * HEAD_PAD)
        k_out[:, sl] = (kn[:, sl] + kr).astype(BF16)
    ve_out[...] = _dot(ckv_b, wve_ref[...]).astype(BF16)
    vo_out[...] = _dot(ckv_b, wvo_ref[...]).astype(BF16)


def _expand(ckv, kr_pad, p, tm):
    n = ckv.shape[0]
    const = lambda i: (0, 0)
    tile = lambda i: (i, 0)
    widths = [QK_W, ATTN_W, ATTN_W]
    return pl.pallas_call(
        _expand_kernel,
        grid=(n // tm,),
        in_specs=[
            pl.BlockSpec((tm, KV_RANK), tile),
            pl.BlockSpec((tm, LANES), tile),
            pl.BlockSpec((KV_RANK, QK_W), const),
            pl.BlockSpec((KV_RANK, ATTN_W), const),
            pl.BlockSpec((KV_RANK, ATTN_W), const),
        ],
        out_specs=[pl.BlockSpec((tm, w), tile) for w in widths],
        out_shape=[jax.ShapeDtypeStruct((n, w), BF16) for w in widths],
        compiler_params=_params(1),
        name="expand",
    )(ckv, kr_pad, p["w_k"], p["w_ve"], p["w_vo"])


def _attn_kernel(*refs, cached):
    if cached:
        q_ref, k_ref, ve_ref, vo_ref, kc_ref, vec_ref, voc_ref, oga_ref, out_ref, o_sc = refs
    else:
        q_ref, k_ref, ve_ref, vo_ref, oga_ref, out_ref, o_sc = refs
    lane = lax.broadcasted_iota(jnp.int32, (1, LANES), 1)
    even_lanes = lane < V_DIM
    for pair in range(N_HEADS // 2):
        psl = slice(pair * LANES, (pair + 1) * LANES)
        acc = None
        inv = []
        for par, (v_ref, vc_ref) in enumerate(((ve_ref, vec_ref if cached else None),
                                               (vo_ref, voc_ref if cached else None))):
            hd = 2 * pair + par
            hsl = slice(hd * HEAD_PAD, (hd + 1) * HEAD_PAD)
            qh = q_ref[:, hsl]
            s = _dot_nt(qh, k_ref[:, hsl])
            m = jnp.max(s, axis=-1, keepdims=True)
            if cached:
                sc = _dot_nt(qh, kc_ref[:, hsl])
                m = jnp.maximum(m, jnp.max(sc, axis=-1, keepdims=True))
            e = jnp.exp(s - m)
            l = jnp.sum(e, axis=-1, keepdims=True)
            o = _dot(e.astype(BF16), v_ref[:, psl])
            if cached:
                ec = jnp.exp(sc - m)
                l = l + jnp.sum(ec, axis=-1, keepdims=True)
                o = o + _dot(ec.astype(BF16), vc_ref[:, psl])
            acc = o if acc is None else acc + o
            inv.append(1.0 / l)
        o_sc[:, psl] = acc * jnp.where(even_lanes, inv[0], inv[1])
    out_ref[...] = _rms(o_sc[...], oga_ref[...]).astype(BF16)


def _attention(q, k, ve, vo, cache, oga, seq, tq):
    n = q.shape[0]
    nq = seq // tq
    qtile = lambda b, i: (b * nq + i, 0)
    kv = lambda b, i: (b, 0)
    const = lambda b, i: (0, 0)
    in_specs = [
        pl.BlockSpec((tq, QK_W), qtile),
        pl.BlockSpec((seq, QK_W), kv),
        pl.BlockSpec((seq, ATTN_W), kv),
        pl.BlockSpec((seq, ATTN_W), kv),
    ]
    args = [q, k, ve, vo]
    if cache is not None:
        kc, vec, voc = cache
        past = kc.shape[0] // (n // seq)
        in_specs += [pl.BlockSpec((past, QK_W), kv), pl.BlockSpec((past, ATTN_W), kv),
                     pl.BlockSpec((past, ATTN_W), kv)]
        args += [kc, vec, voc]
    in_specs.append(pl.BlockSpec((1, ATTN_W), const))
    args.append(oga)
    return pl.pallas_call(
        functools.partial(_attn_kernel, cached=cache is not None),
        grid=(n // seq, nq),
        in_specs=in_specs,
        out_specs=pl.BlockSpec((tq, ATTN_W), qtile),
        out_shape=jax.ShapeDtypeStruct((n, ATTN_W), BF16),
        scratch_shapes=[pltpu.VMEM((tq, ATTN_W), F32)],
        compiler_params=_params(2),
        name="attn_cached" if cache is not None else "attn",
    )(*args)


def _post_kernel(x_ref, an_ref, gm_ref, mod_ref, wo_ref, n2g_ref, wug_ref, wuv_ref, cwg_ref, cwv_ref,
                 cbg_ref, cbv_ref, wd_ref, fg_ref, out_ref, h2_sc, acc_sc, *, seq, tm):
    j = pl.program_id(1)

    @pl.when(j == 0)
    def _():
        y = _dot(an_ref[...], wo_ref[0:ATTN_W, :]) + _dot(gm_ref[...], wo_ref[ATTN_W:D_MODEL, :])
        x1 = x_ref[...] + mod_ref[:, 2 * D_MODEL:3 * D_MODEL] * y
        out_ref[...] = x1
        shift2 = mod_ref[:, 3 * D_MODEL:4 * D_MODEL]
        scale2 = mod_ref[:, 4 * D_MODEL:5 * D_MODEL]
        h2_sc[...] = (_rms(x1, n2g_ref[...]) * (1.0 + scale2) + shift2).astype(BF16)
        acc_sc[...] = jnp.zeros_like(acc_sc)

    h2 = h2_sc[...]
    pos = lax.broadcasted_iota(jnp.int32, (tm, 1), 0) % seq
    is_first = pos == 0
    is_last = pos == seq - 1

    def conv(a, w_ref, b_ref):
        prev = jnp.where(is_first, 0.0, pltpu.roll(a, 1, 0))
        nxt = jnp.where(is_last, 0.0, pltpu.roll(a, tm - 1, 0))
        return b_ref[...] + prev * w_ref[0:1, :] + a * w_ref[1:2, :] + nxt * w_ref[2:3, :]

    gate = conv(_dot(h2, wug_ref[...]), cwg_ref, cbg_ref)
    val = conv(_dot(h2, wuv_ref[...]), cwv_ref, cbv_ref)
    acc_sc[...] += _dot((_silu(gate) * val).astype(BF16), wd_ref[...])

    @pl.when(j == N_FF_CHUNKS - 1)
    def _():
        x2 = out_ref[...] + mod_ref[:, 5 * D_MODEL:6 * D_MODEL] * acc_sc[...]
        out_ref[...] = _rms(x2, fg_ref[...])


def _post(x, an, gm, mods3, mod_row, p, seq, tm):
    n = x.shape[0]
    const = lambda i, j: (0, 0)
    tile = lambda i, j: (i, 0)
    return pl.pallas_call(
        functools.partial(_post_kernel, seq=seq, tm=tm),
        grid=(n // tm, N_FF_CHUNKS),
        in_specs=[
            pl.BlockSpec((tm, D_MODEL), tile),
            pl.BlockSpec((tm, ATTN_W), tile),
            pl.BlockSpec((tm, GMLP_W), tile),
            pl.BlockSpec((None, 1, N_MODS * D_MODEL), lambda i, j: (mod_row(i), 0, 0)),
            pl.BlockSpec((D_MODEL, D_MODEL), const),
            pl.BlockSpec((1, D_MODEL), const),
            pl.BlockSpec((D_MODEL, FF_CHUNK), lambda i, j: (0, j)),
            pl.BlockSpec((D_MODEL, FF_CHUNK), lambda i, j: (0, N_FF_CHUNKS + j)),
            pl.BlockSpec((CONV_W, FF_CHUNK), lambda i, j: (0, j)),
            pl.BlockSpec((CONV_W, FF_CHUNK), lambda i, j: (0, N_FF_CHUNKS + j)),
            pl.BlockSpec((1, FF_CHUNK), lambda i, j: (0, j)),
            pl.BlockSpec((1, FF_CHUNK), lambda i, j: (0, N_FF_CHUNKS + j)),
            pl.BlockSpec((FF_CHUNK, D_MODEL), lambda i, j: (j, 0)),
            pl.BlockSpec((1, D_MODEL), const),
        ],
        out_specs=pl.BlockSpec((tm, D_MODEL), tile),
        out_shape=jax.ShapeDtypeStruct((n, D_MODEL), F32),
        scratch_shapes=[pltpu.VMEM((tm, D_MODEL), BF16), pltpu.VMEM((tm, D_MODEL), F32)],
        compiler_params=_params(2),
        name="post",
    )(x, an, gm, mods3, p["w_o"], p["norm2_g"], p["w_up"], p["w_up"], p["conv_w"], p["conv_w"],
      p["conv_b"], p["conv_b"], p["w_down"], p["final_g"])


def _rope_tables(length):
    pos = jnp.arange(length)
    row = (pos // GRID_W).astype(F32)
    col = (pos % GRID_W).astype(F32)
    n_freq = QK_ROPE // 4
    inv = ROPE_THETA ** (-(jnp.arange(n_freq, dtype=F32) / n_freq))
    ang_r, ang_c = row[:, None] * inv, col[:, None] * inv
    cos32 = jnp.concatenate([jnp.cos(ang_r)] * 2 + [jnp.cos(ang_c)] * 2, axis=-1)
    sin32 = jnp.concatenate([-jnp.sin(ang_r), jnp.sin(ang_r), -jnp.sin(ang_c), jnp.sin(ang_c)], axis=-1)
    tail = LANES - ROPE_LANE0 - QK_ROPE
    cos_q = jnp.concatenate([jnp.ones((length, ROPE_LANE0), F32), cos32, jnp.zeros((length, tail), F32)], -1)
    sin_q = jnp.concatenate([jnp.zeros((length, ROPE_LANE0), F32), sin32, jnp.zeros((length, tail), F32)], -1)
    return cos_q, sin_q


def _layer_params(l, norm1_g, w_in, q_norm_g, w_uq, kv_norm_g, w_ukv, v_norm_g, w_s, b_s, out_g_attn,
                  out_g_gmlp, w_o, norm2_g, w_up, conv_w, conv_b, w_down, final_g):
    wi = w_in[l]
    kr = wi[:, IN_QKV:IN_QKV + QK_ROPE]
    z32 = jnp.zeros((D_MODEL, QK_ROPE), F32)
    w_in_r = jnp.concatenate([wi[:, :IN_QKV], wi[:, IN_QKV + QK_ROPE:], kr, z32, kr, z32], axis=1).astype(BF16)
    w_uq_r = jnp.pad(w_uq[l].reshape(Q_RANK, N_HEADS, QK_NOPE + QK_ROPE),
                     ((0, 0), (0, 0), (0, HEAD_PAD - QK_NOPE - QK_ROPE))).reshape(Q_RANK, QK_W).astype(BF16)
    w_ukv3 = w_ukv[l].reshape(KV_RANK, N_HEADS, QK_NOPE + V_DIM)
    w_k = jnp.pad(w_ukv3[:, :, :QK_NOPE], ((0, 0), (0, 0), (0, HEAD_PAD - QK_NOPE))
                  ).reshape(KV_RANK, QK_W).astype(BF16)
    w_v = w_ukv3[:, :, QK_NOPE:]
    w_ve = jnp.pad(w_v[:, 0::2], ((0, 0), (0, 0), (0, V_DIM))).reshape(KV_RANK, ATTN_W).astype(BF16)
    w_vo = jnp.pad(w_v[:, 1::2], ((0, 0), (0, 0), (V_DIM, 0))).reshape(KV_RANK, ATTN_W).astype(BF16)
    row = lambda a: a.reshape(1, -1)
    return {
        "norm1_g": row(norm1_g[l]), "w_in": w_in_r, "q_norm_g": row(q_norm_g[l]), "w_uq": w_uq_r,
        "kv_norm_g": row(kv_norm_g[l]), "w_k": w_k, "w_ve": w_ve, "w_vo": w_vo,
        "v_norm_g": row(v_norm_g[l]), "w_s": w_s[l].astype(BF16), "b_st": b_s[l].T,
        "out_g_attn": row(out_g_attn[l]), "out_g_gmlp": row(out_g_gmlp[l]), "w_o": w_o[l].astype(BF16),
        "norm2_g": row(norm2_g[l]), "w_up": w_up[l].astype(BF16), "conv_w": conv_w[l],
        "conv_b": row(conv_b[l]), "w_down": w_down[l].astype(BF16), "final_g": row(final_g),
    }


def kernel(x_prompt, x_sample, cache_ckv, cache_krope, c, c_ctx, ada_w, ada_b, norm1_g, w_in, q_norm_g, w_uq, kv_norm_g, w_ukv, v_norm_g, w_s, b_s, out_g_attn, out_g_gmlp, w_o, norm2_g, w_up, conv_w, conv_b, w_down, final_g):
    batch, seq, _ = x_prompt.shape
    dec_batch, dec_seq, _ = x_sample.shape
    depth, past = cache_ckv.shape[1], cache_ckv.shape[2]
    assert depth == 1 and dec_batch + 1 <= MOD_ROWS
    l = 0
    p = _layer_params(l, norm1_g, w_in, q_norm_g, w_uq, kv_norm_g, w_ukv, v_norm_g, w_s, b_s, out_g_attn,
                      out_g_gmlp, w_o, norm2_g, w_up, conv_w, conv_b, w_down, final_g)

    cond = jnp.concatenate([c_ctx[None, :], c, jnp.zeros((MOD_ROWS - 1 - dec_batch, D_MODEL), F32)], axis=0)
    mods3 = _adaln(cond, ada_w[l], ada_b[l].reshape(1, -1)).reshape(MOD_ROWS, 1, N_MODS * D_MODEL)

    tm_pre, tm_post = 512, 1024
    xp = x_prompt.reshape(batch * seq, D_MODEL)
    qp, kp, vep, vop, gmp, ckv_p, kr_p = _pre(xp, mods3, lambda i: 0, p, None, tm_pre)
    anp = _attention(qp, kp, vep, vop, None, p["out_g_attn"], seq, seq)
    yp = _post(xp, anp, gmp, mods3, lambda i: 0, p, seq, tm_post)

    xs = x_sample.reshape(dec_batch * dec_seq, D_MODEL)
    qs, ks, ves, vos, gms, _, _ = _pre(xs, mods3, lambda i: 1 + i // (dec_seq // tm_pre), p,
                                       _rope_tables(dec_seq), tm_pre)
    kr_pad = jnp.pad(cache_krope[:, l].reshape(dec_batch * past, QK_ROPE),
                     ((0, 0), (ROPE_LANE0, LANES - ROPE_LANE0 - QK_ROPE)))
    cache = _expand(cache_ckv[:, l].reshape(dec_batch * past, KV_RANK), kr_pad, p, 512)
    ans = _attention(qs, ks, ves, vos, cache, p["out_g_attn"], dec_seq, 256)
    ys = _post(xs, ans, gms, mods3, lambda i: 1 + i // (dec_seq // tm_post), p, dec_seq, tm_post)

    return (yp.reshape(batch, seq, D_MODEL), ys.reshape(dec_batch, dec_seq, D_MODEL),
            ckv_p.reshape(batch, 1, seq, KV_RANK), kr_p.reshape(batch, 1, seq, QK_ROPE))
```

```python
import functools

import jax
import jax.numpy as jnp
from jax import lax
from jax.experimental import pallas as pl
from jax.experimental.pallas import tpu as pltpu

D_MODEL = 1024
GRID_W = 64
CHUNK = 128
N_HEADS = 8
QK_NOPE = 64
QK_ROPE = 32
V_DIM = 64
Q_RANK = 384
KV_RANK = 256
ATTN_W = N_HEADS * V_DIM
GMLP_W = D_MODEL - ATTN_W
GMLP_GROUPS = 4
GMLP_DG = GMLP_W // GMLP_GROUPS
D_FF = 2816
CONV_W = 3
ROPE_THETA = 10000.0
EPS = 1e-6
SM_SCALE = (QK_NOPE + QK_ROPE) ** -0.5
LOG2_E = 1.4426950408889634

LANES = 128
HEAD_PAD = LANES
ROPE_LANE0 = QK_NOPE
QK_W = N_HEADS * HEAD_PAD
N_MODS = 6
MOD_ROWS = 8
IN_QKV = Q_RANK + KV_RANK
IN_UV0 = IN_QKV
IN_KR0 = IN_QKV + 2 * GMLP_W
IN_W_R = IN_KR0 + LANES
FF_CHUNK = 256
HALO = 16
N_FF_CHUNKS = D_FF // FF_CHUNK
VMEM_LIMIT = 56 * 1024 * 1024

F32 = jnp.float32
BF16 = jnp.bfloat16


def _dot(a, b):
    return jnp.dot(a, b, preferred_element_type=F32)


def _dot_nt(a, b):
    return lax.dot_general(a, b, (((1,), (1,)), ((), ())), preferred_element_type=F32)


def _rms(x, g):
    return x * lax.rsqrt(jnp.mean(x * x, axis=-1, keepdims=True) + EPS) * g


def _gelu(x):
    return 0.5 * x * (1.0 + lax.erf(x * (0.5 ** 0.5)))


def _silu(x):
    return x * (1.0 / (1.0 + jnp.exp(-x)))


def _params(n_axes):
    return pltpu.CompilerParams(dimension_semantics=("arbitrary",) * n_axes,
                                vmem_limit_bytes=VMEM_LIMIT)


def _adaln_kernel(cond_ref, w_ref, b_ref, out_ref):
    s = _silu(cond_ref[...]).astype(BF16)
    out_ref[...] = _dot(s, w_ref[...].astype(BF16)) + b_ref[...]


def _adaln(cond, ada_w, ada_b):
    return pl.pallas_call(
        _adaln_kernel,
        grid=(N_MODS,),
        in_specs=[
            pl.BlockSpec((MOD_ROWS, D_MODEL), lambda j: (0, 0)),
            pl.BlockSpec((D_MODEL, D_MODEL), lambda j: (0, j)),
            pl.BlockSpec((1, D_MODEL), lambda j: (0, j)),
        ],
        out_specs=pl.BlockSpec((MOD_ROWS, D_MODEL), lambda j: (0, j)),
        out_shape=jax.ShapeDtypeStruct((MOD_ROWS, N_MODS * D_MODEL), F32),
        compiler_params=_params(1),
        name="adaln",
    )(cond, ada_w, ada_b)


def _pre_kernel(*refs, rope, tm):
    if rope:
        (x_ref, mod_ref, n1g_ref, win_ref, qng_ref, wuq_ref, kvg_ref, wk_ref, wve_ref, wvo_ref,
         vng_ref, ws_ref, bst_ref, ogg_ref, cos_ref, sin_ref,
         q_out, k_out, ve_out, vo_out, gm_out, ckv_out, kr_out, go_sc) = refs
    else:
        (x_ref, mod_ref, n1g_ref, win_ref, qng_ref, wuq_ref, kvg_ref, wk_ref, wve_ref, wvo_ref,
         vng_ref, ws_ref, bst_ref, ogg_ref,
         q_out, k_out, ve_out, vo_out, gm_out, ckv_out, kr_out, go_sc) = refs
    shift1 = mod_ref[:, 0:D_MODEL]
    scale1 = mod_ref[:, D_MODEL:2 * D_MODEL]
    h = (_rms(x_ref[...], n1g_ref[...]) * (1.0 + scale1) + shift1).astype(BF16)

    lane = lax.broadcasted_iota(jnp.int32, (1, LANES), 1)
    rope_lanes = lane >= ROPE_LANE0
    first_half = (lane % 16) < 8

    def rotate(v, cos):
        swapped = jnp.where(first_half, pltpu.roll(v, LANES - 8, 1), pltpu.roll(v, 8, 1))
        return v * cos + swapped * sin_ref[...]

    qkv = _dot(h, win_ref[:, 0:IN_QKV])
    qn = _rms(qkv[:, 0:Q_RANK], qng_ref[...]).astype(BF16)
    ckv = _rms(qkv[:, Q_RANK:IN_QKV], kvg_ref[...])
    ckv_out[...] = ckv
    ckv_b = ckv.astype(BF16)
    q = _dot(qn, wuq_ref[...]) * (SM_SCALE * LOG2_E)
    kr = _dot(h, win_ref[:, IN_KR0:IN_W_R])
    kr_out[...] = kr[:, 0:QK_ROPE]
    if rope:
        cos_q = cos_ref[...]
        kr_k = rotate(kr, jnp.where(rope_lanes, cos_q, 0.0))
    else:
        kr_k = jnp.where(rope_lanes, kr, 0.0)
    kn = _dot(ckv_b, wk_ref[...])
    for hd in range(N_HEADS):
        sl = slice(hd * HEAD_PAD, (hd + 1) * HEAD_PAD)
        qh = q[:, sl]
        if rope:
            qh = rotate(qh, cos_q)
        q_out[:, sl] = qh.astype(BF16)
        k_out[:, sl] = (kn[:, sl] + kr_k).astype(BF16)
    ve_out[...] = _dot(ckv_b, wve_ref[...]).astype(BF16)
    vo_out[...] = _dot(ckv_b, wvo_ref[...]).astype(BF16)

    uv = _dot(h, win_ref[:, IN_UV0:IN_KR0])
    gu = _gelu(uv[:, 0:GMLP_W])
    gv = _gelu(uv[:, GMLP_W:2 * GMLP_W])
    for g in range(GMLP_GROUPS):
        sl = slice(g * GMLP_DG, (g + 1) * GMLP_DG)
        vg = gv[:, sl]
        vg = (vg * lax.rsqrt(jnp.mean(vg * vg, axis=-1, keepdims=True) + EPS) * vng_ref[:, sl]).astype(BF16)
        bias = bst_ref[:, g:g + 1]
        for n in range(tm // CHUNK):
            rows = slice(n * CHUNK, (n + 1) * CHUNK)
            s = _dot(ws_ref[g], vg[rows, :]) + bias
            go_sc[rows, sl] = gu[rows, sl] * s
    gm_out[...] = _rms(go_sc[...], ogg_ref[...]).astype(BF16)


def _pre(x, mods3, mod_row, p, rope_tabs, tm):
    n = x.shape[0]
    rope = rope_tabs is not None
    const = lambda i: (0, 0)
    tile = lambda i: (i, 0)
    in_specs = [
        pl.BlockSpec((tm, D_MODEL), tile),
        pl.BlockSpec((None, 1, N_MODS * D_MODEL), lambda i: (mod_row(i), 0, 0)),
        pl.BlockSpec((1, D_MODEL), const),
        pl.BlockSpec((D_MODEL, IN_W_R), const),
        pl.BlockSpec((1, Q_RANK), const),
        pl.BlockSpec((Q_RANK, QK_W), const),
        pl.BlockSpec((1, KV_RANK), const),
        pl.BlockSpec((KV_RANK, QK_W), const),
        pl.BlockSpec((KV_RANK, ATTN_W), const),
        pl.BlockSpec((KV_RANK, ATTN_W), const),
        pl.BlockSpec((1, GMLP_W), const),
        pl.BlockSpec((GMLP_GROUPS, CHUNK, CHUNK), lambda i: (0, 0, 0)),
        pl.BlockSpec((CHUNK, GMLP_GROUPS), const),
        pl.BlockSpec((1, GMLP_W), const),
    ]
    args = [x, mods3, p["norm1_g"], p["w_in"], p["q_norm_g"], p["w_uq"], p["kv_norm_g"], p["w_k"],
            p["w_ve"], p["w_vo"], p["v_norm_g"], p["w_s"], p["b_st"], p["out_g_gmlp"]]
    if rope:
        cos_q, sin_q = rope_tabs
        per_seq = cos_q.shape[0] // tm
        in_specs += [pl.BlockSpec((tm, LANES), lambda i: (i % per_seq, 0))] * 2
        args += [cos_q, sin_q]
    widths = [(QK_W, BF16), (QK_W, BF16), (ATTN_W, BF16), (ATTN_W, BF16), (GMLP_W, BF16),
              (KV_RANK, F32), (QK_ROPE, F32)]
    return pl.pallas_call(
        functools.partial(_pre_kernel, rope=rope, tm=tm),
        grid=(n // tm,),
        in_specs=in_specs,
        out_specs=[pl.BlockSpec((tm, w), tile) for w, _ in widths],
        out_shape=[jax.ShapeDtypeStruct((n, w), dt) for w, dt in widths],
        scratch_shapes=[pltpu.VMEM((tm, GMLP_W), F32)],
        compiler_params=_params(1),
        name="pre_rope" if rope else "pre",
    )(*args)


def _expand_kernel(ckv_ref, kr_ref, wk_ref, wve_ref, wvo_ref, k_out, ve_out, vo_out):
    ckv_b = ckv_ref[...].astype(BF16)
    kn = _dot(ckv_b, wk_ref[...])
    kr = kr_ref[...]
    for hd in range(N_HEADS):
        sl = slice(hd * HEAD_PAD, (hd + 1) * HEAD_PAD)
        k_out[:, sl] = (kn[:, sl] + kr).astype(BF16)
    ve_out[...] = _dot(ckv_b, wve_ref[...]).astype(BF16)
    vo_out[...] = _dot(ckv_b, wvo_ref[...]).astype(BF16)


def _expand(ckv, kr_pad, p, tm):
    n = ckv.shape[0]
    const = lambda i: (0, 0)
    tile = lambda i: (i, 0)
    widths = [QK_W, ATTN_W, ATTN_W]
    return pl.pallas_call(
        _expand_kernel,
        grid=(n // tm,),
        in_specs=[
            pl.BlockSpec((tm, KV_RANK), tile),
            pl.BlockSpec((tm, LANES), tile),
            pl.BlockSpec((KV_RANK, QK_W), const),
            pl.BlockSpec((KV_RANK, ATTN_W), const),
            pl.BlockSpec((KV_RANK, ATTN_W), const),
        ],
        out_specs=[pl.BlockSpec((tm, w), tile) for w in widths],
        out_shape=[jax.ShapeDtypeStruct((n, w), BF16) for w in widths],
        compiler_params=_params(1),
        name="expand",
    )(ckv, kr_pad, p["w_k"], p["w_ve"], p["w_vo"])


def _attn_kernel(*refs, cached):
    if cached:
        q_ref, k_ref, ve_ref, vo_ref, kc_ref, vec_ref, voc_ref, oga_ref, out_ref, o_sc = refs
    else:
        q_ref, k_ref, ve_ref, vo_ref, oga_ref, out_ref, o_sc = refs
    lane = lax.broadcasted_iota(jnp.int32, (1, LANES), 1)
    even_lanes = lane < V_DIM
    for pair in range(N_HEADS // 2):
        psl = slice(pair * LANES, (pair + 1) * LANES)
        acc = None
        inv = []
        for par, (v_ref, vc_ref) in enumerate(((ve_ref, vec_ref if cached else None),
                                               (vo_ref, voc_ref if cached else None))):
            hd = 2 * pair + par
            hsl = slice(hd * HEAD_PAD, (hd + 1) * HEAD_PAD)
            qh = q_ref[:, hsl]
            s = _dot_nt(qh, k_ref[:, hsl])
            m = jnp.max(s, axis=-1, keepdims=True)
            if cached:
                sc = _dot_nt(qh, kc_ref[:, hsl])
                m = jnp.maximum(m, jnp.max(sc, axis=-1, keepdims=True))
            e = jnp.exp2(s - m)
            l = jnp.sum(e, axis=-1, keepdims=True)
            o = _dot(e.astype(BF16), v_ref[:, psl])
            if cached:
                ec = jnp.exp2(sc - m)
                l = l + jnp.sum(ec, axis=-1, keepdims=True)
                o = o + _dot(ec.astype(BF16), vc_ref[:, psl])
            acc = o if acc is None else acc + o
            inv.append(1.0 / l)
        o_sc[:, psl] = acc * jnp.where(even_lanes, inv[0], inv[1])
    out_ref[...] = _rms(o_sc[...], oga_ref[...]).astype(BF16)


def _attention(q, k, ve, vo, cache, oga, seq, tq):
    n = q.shape[0]
    nq = seq // tq
    qtile = lambda b, i: (b * nq + i, 0)
    kv = lambda b, i: (b, 0)
    const = lambda b, i: (0, 0)
    in_specs = [
        pl.BlockSpec((tq, QK_W), qtile),
        pl.BlockSpec((seq, QK_W), kv),
        pl.BlockSpec((seq, ATTN_W), kv),
        pl.BlockSpec((seq, ATTN_W), kv),
    ]
    args = [q, k, ve, vo]
    if cache is not None:
        kc, vec, voc = cache
        past = kc.shape[0] // (n // seq)
        in_specs += [pl.BlockSpec((past, QK_W), kv), pl.BlockSpec((past, ATTN_W), kv),
                     pl.BlockSpec((past, ATTN_W), kv)]
        args += [kc, vec, voc]
    in_specs.append(pl.BlockSpec((1, ATTN_W), const))
    args.append(oga)
    return pl.pallas_call(
        functools.partial(_attn_kernel, cached=cache is not None),
        grid=(n // seq, nq),
        in_specs=in_specs,
        out_specs=pl.BlockSpec((tq, ATTN_W), qtile),
        out_shape=jax.ShapeDtypeStruct((n, ATTN_W), BF16),
        scratch_shapes=[pltpu.VMEM((tq, ATTN_W), F32)],
        compiler_params=_params(2),
        name="attn_cached" if cache is not None else "attn",
    )(*args)


def _mix_kernel(x_ref, an_ref, gm_ref, mod_ref, wo_ref, n2g_ref, x1_out, h2_out):
    y = _dot(an_ref[...], wo_ref[0:ATTN_W, :]) + _dot(gm_ref[...], wo_ref[ATTN_W:D_MODEL, :])
    x1 = x_ref[...] + mod_ref[:, 2 * D_MODEL:3 * D_MODEL] * y
    x1_out[...] = x1
    shift2 = mod_ref[:, 3 * D_MODEL:4 * D_MODEL]
    scale2 = mod_ref[:, 4 * D_MODEL:5 * D_MODEL]
    h2_out[...] = (_rms(x1, n2g_ref[...]) * (1.0 + scale2) + shift2).astype(BF16)


def _mix(x, an, gm, mods3, mod_row, p, tm):
    n = x.shape[0]
    const = lambda i: (0, 0)
    tile = lambda i: (i, 0)
    return pl.pallas_call(
        _mix_kernel,
        grid=(n // tm,),
        in_specs=[
            pl.BlockSpec((tm, D_MODEL), tile),
            pl.BlockSpec((tm, ATTN_W), tile),
            pl.BlockSpec((tm, GMLP_W), tile),
            pl.BlockSpec((None, 1, N_MODS * D_MODEL), lambda i: (mod_row(i), 0, 0)),
            pl.BlockSpec((D_MODEL, D_MODEL), const),
            pl.BlockSpec((1, D_MODEL), const),
        ],
        out_specs=[pl.BlockSpec((tm, D_MODEL), tile), pl.BlockSpec((tm, D_MODEL), tile)],
        out_shape=[jax.ShapeDtypeStruct((n, D_MODEL), F32), jax.ShapeDtypeStruct((n, D_MODEL), BF16)],
        compiler_params=_params(1),
        name="mix",
    )(x, an, gm, mods3, p["w_o"], p["norm2_g"])


def _ffn_kernel(*refs, nseg, seg, tiles_per_seq):
    halo = tiles_per_seq > 1
    if halo:
        (x1_ref, h2_ref, hp_ref, hn_ref, mod_ref, wup_ref, cw_ref, cb_ref, wd_ref, fg_ref, out_ref,
         lhs_sc, a_sc, g_sc) = refs
    else:
        (x1_ref, h2_ref, mod_ref, wup_ref, cw_ref, cb_ref, wd_ref, fg_ref, out_ref,
         lhs_sc, a_sc, g_sc) = refs
    stride = seg + HALO
    rows = HALO + nseg * stride
    zeros = jnp.zeros((HALO, D_MODEL), BF16)
    lhs_sc[0:HALO, :] = zeros
    for s in range(nseg):
        base = HALO + s * stride
        lhs_sc[base:base + seg, :] = h2_ref[s * seg:(s + 1) * seg, :]
        lhs_sc[base + seg:base + stride, :] = zeros
    if halo:
        pos = pl.program_id(0) % tiles_per_seq

        @pl.when(pos != 0)
        def _():
            lhs_sc[0:HALO, :] = hp_ref[...]

        @pl.when(pos != tiles_per_seq - 1)
        def _():
            lhs_sc[HALO + seg:HALO + stride, :] = hn_ref[...]

    for c in range(N_FF_CHUNKS):
        slot = c % 2
        for k, col0 in enumerate((c * FF_CHUNK, D_FF + c * FF_CHUNK)):
            a = _dot(lhs_sc[...], wup_ref[:, col0:col0 + FF_CHUNK])
            for j in range(FF_CHUNK // LANES):
                a_sc[slot, 2 * k + j, :, :] = a[8:rows - 8, j * LANES:(j + 1) * LANES]
        for j in range(FF_CHUNK // LANES):
            gl = slice(c * FF_CHUNK + j * LANES, c * FF_CHUNK + (j + 1) * LANES)
            vl = slice(D_FF + gl.start, D_FF + gl.stop)
            for s in range(nseg):
                base = 8 + s * stride

                def conv(k, lanes):
                    acc = cb_ref[:, lanes]
                    for t in range(CONV_W):
                        acc = acc + a_sc[slot, k, base - 1 + t:base - 1 + t + seg, :] * cw_ref[t:t + 1, lanes]
                    return acc

                g = _silu(conv(j, gl)) * conv(2 + j, vl)
                g_sc[s * seg:(s + 1) * seg, gl] = g.astype(BF16)

    x2 = x1_ref[...] + mod_ref[:, 5 * D_MODEL:6 * D_MODEL] * _dot(g_sc[...], wd_ref[...])
    out_ref[...] = _rms(x2, fg_ref[...])


def _ffn(x1, h2, mods3, mod_row, p, seq, tm):
    n = x1.shape[0]
    tiles_per_seq = max(seq // tm, 1)
    seg = min(seq, tm)
    nseg = tm // seg
    rows = HALO + nseg * (seg + HALO)
    const = lambda i: (0, 0)
    tile = lambda i: (i, 0)
    resident = dict(pipeline_mode=pl.Buffered(1))
    in_specs = [pl.BlockSpec((tm, D_MODEL), tile), pl.BlockSpec((tm, D_MODEL), tile)]
    args = [x1, h2]
    if tiles_per_seq > 1:
        per = tm // HALO
        last = n // HALO - 1
        in_specs += [pl.BlockSpec((HALO, D_MODEL), lambda i: (jnp.maximum(i * per - 1, 0), 0)),
                     pl.BlockSpec((HALO, D_MODEL), lambda i: (jnp.minimum((i + 1) * per, last), 0))]
        args += [h2, h2]
    in_specs += [
        pl.BlockSpec((None, 1, N_MODS * D_MODEL), lambda i: (mod_row(i), 0, 0)),
        pl.BlockSpec((D_MODEL, 2 * D_FF), const, **resident),
        pl.BlockSpec((CONV_W, 2 * D_FF), const),
        pl.BlockSpec((1, 2 * D_FF), const),
        pl.BlockSpec((D_FF, D_MODEL), const, **resident),
        pl.BlockSpec((1, D_MODEL), const),
    ]
    args += [mods3, p["w_up"], p["conv_w"], p["conv_b"], p["w_down"], p["final_g"]]
    return pl.pallas_call(
        functools.partial(_ffn_kernel, nseg=nseg, seg=seg, tiles_per_seq=tiles_per_seq),
        grid=(n // tm,),
        in_specs=in_specs,
        out_specs=pl.BlockSpec((tm, D_MODEL), tile),
        out_shape=jax.ShapeDtypeStruct((n, D_MODEL), F32),
        scratch_shapes=[pltpu.VMEM((rows, D_MODEL), BF16),
                        pltpu.VMEM((2, 2 * FF_CHUNK // LANES, rows - 16, LANES), F32),
                        pltpu.VMEM((tm, D_FF), BF16)],
        compiler_params=_params(1),
        name="ffn_halo" if tiles_per_seq > 1 else "ffn",
    )(*args)


def _rope_tables(length):
    pos = jnp.arange(length)
    row = (pos // GRID_W).astype(F32)
    col = (pos % GRID_W).astype(F32)
    n_freq = QK_ROPE // 4
    inv = ROPE_THETA ** (-(jnp.arange(n_freq, dtype=F32) / n_freq))
    ang_r, ang_c = row[:, None] * inv, col[:, None] * inv
    cos32 = jnp.concatenate([jnp.cos(ang_r)] * 2 + [jnp.cos(ang_c)] * 2, axis=-1)
    sin32 = jnp.concatenate([-jnp.sin(ang_r), jnp.sin(ang_r), -jnp.sin(ang_c), jnp.sin(ang_c)], axis=-1)
    tail = LANES - ROPE_LANE0 - QK_ROPE
    cos_q = jnp.concatenate([jnp.ones((length, ROPE_LANE0), F32), cos32, jnp.zeros((length, tail), F32)], -1)
    sin_q = jnp.concatenate([jnp.zeros((length, ROPE_LANE0), F32), sin32, jnp.zeros((length, tail), F32)], -1)
    return cos_q, sin_q


def _layer_params(l, norm1_g, w_in, q_norm_g, w_uq, kv_norm_g, w_ukv, v_norm_g, w_s, b_s, out_g_attn,
                  out_g_gmlp, w_o, norm2_g, w_up, conv_w, conv_b, w_down, final_g):
    wi = w_in[l]
    kr = wi[:, IN_QKV:IN_QKV + QK_ROPE]
    z32 = jnp.zeros((D_MODEL, QK_ROPE), F32)
    w_in_r = jnp.concatenate([wi[:, :IN_QKV], wi[:, IN_QKV + QK_ROPE:], kr, z32, kr, z32], axis=1).astype(BF16)
    w_uq_r = jnp.pad(w_uq[l].reshape(Q_RANK, N_HEADS, QK_NOPE + QK_ROPE),
                     ((0, 0), (0, 0), (0, HEAD_PAD - QK_NOPE - QK_ROPE))).reshape(Q_RANK, QK_W).astype(BF16)
    w_ukv3 = w_ukv[l].reshape(KV_RANK, N_HEADS, QK_NOPE + V_DIM)
    w_k = jnp.pad(w_ukv3[:, :, :QK_NOPE], ((0, 0), (0, 0), (0, HEAD_PAD - QK_NOPE))
                  ).reshape(KV_RANK, QK_W).astype(BF16)
    w_v = w_ukv3[:, :, QK_NOPE:]
    w_ve = jnp.pad(w_v[:, 0::2], ((0, 0), (0, 0), (0, V_DIM))).reshape(KV_RANK, ATTN_W).astype(BF16)
    w_vo = jnp.pad(w_v[:, 1::2], ((0, 0), (0, 0), (V_DIM, 0))).reshape(KV_RANK, ATTN_W).astype(BF16)
    row = lambda a: a.reshape(1, -1)
    return {
        "norm1_g": row(norm1_g[l]), "w_in": w_in_r, "q_norm_g": row(q_norm_g[l]), "w_uq": w_uq_r,
        "kv_norm_g": row(kv_norm_g[l]), "w_k": w_k, "w_ve": w_ve, "w_vo": w_vo,
        "v_norm_g": row(v_norm_g[l]), "w_s": w_s[l].astype(BF16), "b_st": b_s[l].T,
        "out_g_attn": row(out_g_attn[l]), "out_g_gmlp": row(out_g_gmlp[l]), "w_o": w_o[l].astype(BF16),
        "norm2_g": row(norm2_g[l]), "w_up": w_up[l].astype(BF16), "conv_w": conv_w[l],
        "conv_b": row(conv_b[l]), "w_down": w_down[l].astype(BF16), "final_g": row(final_g),
    }


def kernel(x_prompt, x_sample, cache_ckv, cache_krope, c, c_ctx, ada_w, ada_b, norm1_g, w_in, q_norm_g, w_uq, kv_norm_g, w_ukv, v_norm_g, w_s, b_s, out_g_attn, out_g_gmlp, w_o, norm2_g, w_up, conv_w, conv_b, w_down, final_g):
    batch, seq, _ = x_prompt.shape
    dec_batch, dec_seq, _ = x_sample.shape
    depth, past = cache_ckv.shape[1], cache_ckv.shape[2]
    assert depth == 1 and dec_batch + 1 <= MOD_ROWS
    l = 0
    p = _layer_params(l, norm1_g, w_in, q_norm_g, w_uq, kv_norm_g, w_ukv, v_norm_g, w_s, b_s, out_g_attn,
                      out_g_gmlp, w_o, norm2_g, w_up, conv_w, conv_b, w_down, final_g)

    cond = jnp.concatenate([c_ctx[None, :], c, jnp.zeros((MOD_ROWS - 1 - dec_batch, D_MODEL), F32)], axis=0)
    mods3 = _adaln(cond, ada_w[l], ada_b[l].reshape(1, -1)).reshape(MOD_ROWS, 1, N_MODS * D_MODEL)

    tm_pre, tm_ffn = 512, 512
    xp = x_prompt.reshape(batch * seq, D_MODEL)
    qp, kp, vep, vop, gmp, ckv_p, kr_p = _pre(xp, mods3, lambda i: 0, p, None, tm_pre)
    anp = _attention(qp, kp, vep, vop, None, p["out_g_attn"], seq, seq)
    x1p, h2p = _mix(xp, anp, gmp, mods3, lambda i: 0, p, tm_pre)
    yp = _ffn(x1p, h2p, mods3, lambda i: 0, p, seq, tm_ffn)

    xs = x_sample.reshape(dec_batch * dec_seq, D_MODEL)
    qs, ks, ves, vos, gms, _, _ = _pre(xs, mods3, lambda i: 1 + i // (dec_seq // tm_pre), p,
                                       _rope_tables(dec_seq), tm_pre)
    kr_pad = jnp.pad(cache_krope[:, l].reshape(dec_batch * past, QK_ROPE),
                     ((0, 0), (ROPE_LANE0, LANES - ROPE_LANE0 - QK_ROPE)))
    cache = _expand(cache_ckv[:, l].reshape(dec_batch * past, KV_RANK), kr_pad, p, 512)
    ans = _attention(qs, ks, ves, vos, cache, p["out_g_attn"], dec_seq, 256)
    x1s, h2s = _mix(xs, ans, gms, mods3, lambda i: 1 + i // (dec_seq // tm_pre), p, tm_pre)
    ys = _ffn(x1s, h2s, mods3, lambda i: 1 + i // (dec_seq // tm_ffn), p, dec_seq, tm_ffn)

    return (yp.reshape(batch, seq, D_MODEL), ys.reshape(dec_batch, dec_seq, D_MODEL),
            ckv_p.reshape(batch, 1, seq, KV_RANK), kr_p.reshape(batch, 1, seq, QK_ROPE))
```

```python
import functools

import jax
import jax.numpy as jnp
from jax import lax
from jax.experimental import pallas as pl
from jax.experimental.pallas import tpu as pltpu

D_MODEL = 1024
GRID_W = 64
CHUNK = 128
N_HEADS = 8
QK_NOPE = 64
QK_ROPE = 32
V_DIM = 64
Q_RANK = 384
KV_RANK = 256
ATTN_W = N_HEADS * V_DIM
GMLP_W = D_MODEL - ATTN_W
GMLP_GROUPS = 4
GMLP_DG = GMLP_W // GMLP_GROUPS
D_FF = 2816
CONV_W = 3
ROPE_THETA = 10000.0
EPS = 1e-6
SM_SCALE = (QK_NOPE + QK_ROPE) ** -0.5
LOG2_E = 1.4426950408889634

LANES = 128
HEAD_PAD = LANES
ROPE_LANE0 = QK_NOPE
QK_W = N_HEADS * HEAD_PAD
N_MODS = 6
MOD_ROWS = 8
IN_QKV = Q_RANK + KV_RANK
IN_UV0 = IN_QKV
IN_KR0 = IN_QKV + 2 * GMLP_W
IN_W_R = IN_KR0 + LANES
FF_CHUNK = 256
HALO = 16
N_FF_CHUNKS = D_FF // FF_CHUNK
VMEM_LIMIT = 56 * 1024 * 1024

F32 = jnp.float32
BF16 = jnp.bfloat16


def _dot(a, b):
    return jnp.dot(a, b, preferred_element_type=F32)


def _dot_nt(a, b):
    return lax.dot_general(a, b, (((1,), (1,)), ((), ())), preferred_element_type=F32)


def _rms(x, g):
    return x * lax.rsqrt(jnp.mean(x * x, axis=-1, keepdims=True) + EPS) * g


def _gelu(x):
    return 0.5 * x * (1.0 + lax.erf(x * (0.5 ** 0.5)))


def _silu(x):
    return x * (1.0 / (1.0 + jnp.exp(-x)))


def _params(n_axes):
    return pltpu.CompilerParams(dimension_semantics=("arbitrary",) * n_axes,
                                vmem_limit_bytes=VMEM_LIMIT)


def _adaln_kernel(cond_ref, w_ref, b_ref, out_ref):
    s = _silu(cond_ref[...]).astype(BF16)
    out_ref[...] = _dot(s, w_ref[...].astype(BF16)) + b_ref[...]


def _adaln(cond, ada_w, ada_b):
    return pl.pallas_call(
        _adaln_kernel,
        grid=(N_MODS,),
        in_specs=[
            pl.BlockSpec((MOD_ROWS, D_MODEL), lambda j: (0, 0)),
            pl.BlockSpec((D_MODEL, D_MODEL), lambda j: (0, j)),
            pl.BlockSpec((1, D_MODEL), lambda j: (0, j)),
        ],
        out_specs=pl.BlockSpec((MOD_ROWS, D_MODEL), lambda j: (0, j)),
        out_shape=jax.ShapeDtypeStruct((MOD_ROWS, N_MODS * D_MODEL), F32),
        compiler_params=_params(1),
        name="adaln",
    )(cond, ada_w, ada_b)


def _pre_kernel(*refs, rope, tm):
    if rope:
        (x_ref, mod_ref, n1g_ref, win_ref, qng_ref, wuq_ref, kvg_ref, wk_ref, wve_ref, wvo_ref,
         vng_ref, ws_ref, bst_ref, ogg_ref, cos_ref, sin_ref,
         q_out, k_out, ve_out, vo_out, gm_out, ckv_out, kr_out, go_sc) = refs
    else:
        (x_ref, mod_ref, n1g_ref, win_ref, qng_ref, wuq_ref, kvg_ref, wk_ref, wve_ref, wvo_ref,
         vng_ref, ws_ref, bst_ref, ogg_ref,
         q_out, k_out, ve_out, vo_out, gm_out, ckv_out, kr_out, go_sc) = refs
    shift1 = mod_ref[:, 0:D_MODEL]
    scale1 = mod_ref[:, D_MODEL:2 * D_MODEL]
    h = (_rms(x_ref[...], n1g_ref[...]) * (1.0 + scale1) + shift1).astype(BF16)

    lane = lax.broadcasted_iota(jnp.int32, (1, LANES), 1)
    rope_lanes = lane >= ROPE_LANE0
    first_half = (lane % 16) < 8

    def rotate(v, cos):
        swapped = jnp.where(first_half, pltpu.roll(v, LANES - 8, 1), pltpu.roll(v, 8, 1))
        return v * cos + swapped * sin_ref[...]

    qkv = _dot(h, win_ref[:, 0:IN_QKV])
    qn = _rms(qkv[:, 0:Q_RANK], qng_ref[...]).astype(BF16)
    ckv = _rms(qkv[:, Q_RANK:IN_QKV], kvg_ref[...])
    ckv_out[...] = ckv
    ckv_b = ckv.astype(BF16)
    q = _dot(qn, wuq_ref[...]) * (SM_SCALE * LOG2_E)
    kr = _dot(h, win_ref[:, IN_KR0:IN_W_R])
    kr_out[...] = kr[:, 0:QK_ROPE]
    if rope:
        cos_q = cos_ref[...]
        kr_k = rotate(kr, jnp.where(rope_lanes, cos_q, 0.0))
    else:
        kr_k = jnp.where(rope_lanes, kr, 0.0)
    kn = _dot(ckv_b, wk_ref[...])
    for hd in range(N_HEADS):
        sl = slice(hd * HEAD_PAD, (hd + 1) * HEAD_PAD)
        qh = q[:, sl]
        if rope:
            qh = rotate(qh, cos_q)
        q_out[:, sl] = qh.astype(BF16)
        k_out[:, sl] = (kn[:, sl] + kr_k).astype(BF16)
    ve_out[...] = _dot(ckv_b, wve_ref[...]).astype(BF16)
    vo_out[...] = _dot(ckv_b, wvo_ref[...]).astype(BF16)

    uv = _dot(h, win_ref[:, IN_UV0:IN_KR0])
    gu = _gelu(uv[:, 0:GMLP_W])
    gv = _gelu(uv[:, GMLP_W:2 * GMLP_W])
    for g in range(GMLP_GROUPS):
        sl = slice(g * GMLP_DG, (g + 1) * GMLP_DG)
        vg = gv[:, sl]
        vg = (vg * lax.rsqrt(jnp.mean(vg * vg, axis=-1, keepdims=True) + EPS) * vng_ref[:, sl]).astype(BF16)
        bias = bst_ref[:, g:g + 1]
        for n in range(tm // CHUNK):
            rows = slice(n * CHUNK, (n + 1) * CHUNK)
            s = _dot(ws_ref[g], vg[rows, :]) + bias
            go_sc[rows, sl] = gu[rows, sl] * s
    gm_out[...] = _rms(go_sc[...], ogg_ref[...]).astype(BF16)


def _pre(x, mods3, mod_row, p, rope_tabs, tm):
    n = x.shape[0]
    rope = rope_tabs is not None
    const = lambda i: (0, 0)
    tile = lambda i: (i, 0)
    in_specs = [
        pl.BlockSpec((tm, D_MODEL), tile),
        pl.BlockSpec((None, 1, N_MODS * D_MODEL), lambda i: (mod_row(i), 0, 0)),
        pl.BlockSpec((1, D_MODEL), const),
        pl.BlockSpec((D_MODEL, IN_W_R), const),
        pl.BlockSpec((1, Q_RANK), const),
        pl.BlockSpec((Q_RANK, QK_W), const),
        pl.BlockSpec((1, KV_RANK), const),
        pl.BlockSpec((KV_RANK, QK_W), const),
        pl.BlockSpec((KV_RANK, ATTN_W), const),
        pl.BlockSpec((KV_RANK, ATTN_W), const),
        pl.BlockSpec((1, GMLP_W), const),
        pl.BlockSpec((GMLP_GROUPS, CHUNK, CHUNK), lambda i: (0, 0, 0)),
        pl.BlockSpec((CHUNK, GMLP_GROUPS), const),
        pl.BlockSpec((1, GMLP_W), const),
    ]
    args = [x, mods3, p["norm1_g"], p["w_in"], p["q_norm_g"], p["w_uq"], p["kv_norm_g"], p["w_k"],
            p["w_ve"], p["w_vo"], p["v_norm_g"], p["w_s"], p["b_st"], p["out_g_gmlp"]]
    if rope:
        cos_q, sin_q = rope_tabs
        per_seq = cos_q.shape[0] // tm
        in_specs += [pl.BlockSpec((tm, LANES), lambda i: (i % per_seq, 0))] * 2
        args += [cos_q, sin_q]
    widths = [(QK_W, BF16), (QK_W, BF16), (ATTN_W, BF16), (ATTN_W, BF16), (GMLP_W, BF16),
              (KV_RANK, F32), (QK_ROPE, F32)]
    return pl.pallas_call(
        functools.partial(_pre_kernel, rope=rope, tm=tm),
        grid=(n // tm,),
        in_specs=in_specs,
        out_specs=[pl.BlockSpec((tm, w), tile) for w, _ in widths],
        out_shape=[jax.ShapeDtypeStruct((n, w), dt) for w, dt in widths],
        scratch_shapes=[pltpu.VMEM((tm, GMLP_W), F32)],
        compiler_params=_params(1),
        name="pre_rope" if rope else "pre",
    )(*args)


def _expand_kernel(ckv_ref, kr_ref, wk_ref, wve_ref, wvo_ref, k_out, ve_out, vo_out):
    ckv_b = ckv_ref[...].astype(BF16)
    kn = _dot(ckv_b, wk_ref[...])
    kr = kr_ref[...]
    for hd in range(N_HEADS):
        sl = slice(hd * HEAD_PAD, (hd + 1) * HEAD_PAD)
        k_out[:, sl] = (kn[:, sl] + kr).astype(BF16)
    ve_out[...] = _dot(ckv_b, wve_ref[...]).astype(BF16)
    vo_out[...] = _dot(ckv_b, wvo_ref[...]).astype(BF16)


def _expand(ckv, kr_pad, p, tm):
    n = ckv.shape[0]
    const = lambda i: (0, 0)
    tile = lambda i: (i, 0)
    widths = [QK_W, ATTN_W, ATTN_W]
    return pl.pallas_call(
        _expand_kernel,
        grid=(n // tm,),
        in_specs=[
            pl.BlockSpec((tm, KV_RANK), tile),
            pl.BlockSpec((tm, LANES), tile),
            pl.BlockSpec((KV_RANK, QK_W), const),
            pl.BlockSpec((KV_RANK, ATTN_W), const),
            pl.BlockSpec((KV_RANK, ATTN_W), const),
        ],
        out_specs=[pl.BlockSpec((tm, w), tile) for w in widths],
        out_shape=[jax.ShapeDtypeStruct((n, w), BF16) for w in widths],
        compiler_params=_params(1),
        name="expand",
    )(ckv, kr_pad, p["w_k"], p["w_ve"], p["w_vo"])


def _attn_kernel(*refs, cached, group, seq, tq, pps):
    if cached:
        q_ref, k_ref, ve_ref, vo_ref, kc_ref, vec_ref, voc_ref, oga_ref, out_ref, o_sc = refs
    else:
        q_ref, k_ref, ve_ref, vo_ref, oga_ref, out_ref, o_sc = refs
    step = pl.program_id(2)
    lane = lax.broadcasted_iota(jnp.int32, (1, LANES), 1)
    even_lanes = lane < V_DIM
    for g in range(group):
        qrows = slice(g * tq, (g + 1) * tq)
        krows = slice(g * seq, (g + 1) * seq)
        for pp in range(pps):
            psl = slice(pp * LANES, (pp + 1) * LANES)
            acc = None
            inv = []
            for par, (v_ref, vc_ref) in enumerate(((ve_ref, vec_ref if cached else None),
                                                   (vo_ref, voc_ref if cached else None))):
                hd = 2 * pp + par
                hsl = slice(hd * HEAD_PAD, (hd + 1) * HEAD_PAD)
                qh = q_ref[qrows, hsl]
                s = _dot_nt(qh, k_ref[krows, hsl])
                m = jnp.max(s, axis=-1, keepdims=True)
                if cached:
                    sc = _dot_nt(qh, kc_ref[:, hsl])
                    m = jnp.maximum(m, jnp.max(sc, axis=-1, keepdims=True))
                e = jnp.exp2(s - m)
                l = jnp.sum(e, axis=-1, keepdims=True)
                o = _dot(e.astype(BF16), v_ref[krows, psl])
                if cached:
                    ec = jnp.exp2(sc - m)
                    l = l + jnp.sum(ec, axis=-1, keepdims=True)
                    o = o + _dot(ec.astype(BF16), vc_ref[:, psl])
                acc = o if acc is None else acc + o
                inv.append(1.0 / l)
            o_sc[step * pps + pp, qrows, :] = acc * jnp.where(even_lanes, inv[0], inv[1])

    @pl.when(step == N_HEADS // 2 // pps - 1)
    def _():
        blocks = [o_sc[j] for j in range(N_HEADS // 2)]
        ssq = sum(jnp.sum(b * b, axis=-1, keepdims=True) for b in blocks)
        r = lax.rsqrt(ssq * (1.0 / ATTN_W) + EPS)
        for j, b in enumerate(blocks):
            sl = slice(j * LANES, (j + 1) * LANES)
            out_ref[:, sl] = (b * r * oga_ref[:, sl]).astype(BF16)


def _attention(q, k, ve, vo, cache, oga, seq, tq, group, pps):
    n = q.shape[0]
    nq = seq // tq
    n_pairs = N_HEADS // 2
    qtile = lambda b, i, j: (b * nq + i, j)
    kv = lambda b, i, j: (b, j)
    qk_w, v_w = pps * 2 * HEAD_PAD, pps * LANES
    in_specs = [
        pl.BlockSpec((group * tq, qk_w), qtile),
        pl.BlockSpec((group * seq, qk_w), kv),
        pl.BlockSpec((group * seq, v_w), kv),
        pl.BlockSpec((group * seq, v_w), kv),
    ]
    args = [q, k, ve, vo]
    if cache is not None:
        kc, vec, voc = cache
        past = kc.shape[0] // (n // seq)
        in_specs += [pl.BlockSpec((past, qk_w), kv), pl.BlockSpec((past, v_w), kv),
                     pl.BlockSpec((past, v_w), kv)]
        args += [kc, vec, voc]
    in_specs.append(pl.BlockSpec((1, ATTN_W), lambda b, i, j: (0, 0)))
    args.append(oga)
    return pl.pallas_call(
        functools.partial(_attn_kernel, cached=cache is not None, group=group, seq=seq, tq=tq, pps=pps),
        grid=(n // (group * seq), nq, n_pairs // pps),
        in_specs=in_specs,
        out_specs=pl.BlockSpec((group * tq, ATTN_W), lambda b, i, j: (b * nq + i, 0)),
        out_shape=jax.ShapeDtypeStruct((n, ATTN_W), BF16),
        scratch_shapes=[pltpu.VMEM((n_pairs, group * tq, LANES), F32)],
        compiler_params=_params(3),
        name="attn_cached" if cache is not None else "attn",
    )(*args)


def _ffn_kernel(*refs, nseg, seg, tiles_per_seq):
    halo = tiles_per_seq > 1
    if halo:
        (x_ref, an_ref, gm_ref, xp_ref, xn_ref, anp_ref, ann_ref, gmp_ref, gmn_ref, mod_ref, wo_ref, n2g_ref,
         wup_ref, cw_ref, cb_ref, wd_ref, fg_ref, out_ref, lhs_sc, a_sc, g_sc, mix_sc) = refs
    else:
        (x_ref, an_ref, gm_ref, mod_ref, wo_ref, n2g_ref,
         wup_ref, cw_ref, cb_ref, wd_ref, fg_ref, out_ref, lhs_sc, a_sc, g_sc) = refs
    stride = seg + HALO
    rows = HALO + nseg * stride
    gate1 = mod_ref[:, 2 * D_MODEL:3 * D_MODEL]
    shift2 = mod_ref[:, 3 * D_MODEL:4 * D_MODEL]
    scale2 = mod_ref[:, 4 * D_MODEL:5 * D_MODEL]

    def residual_norm(x, y):
        x1 = x + gate1 * y
        return x1, (_rms(x1, n2g_ref[...]) * (1.0 + scale2) + shift2).astype(BF16)

    zeros = jnp.zeros((HALO, D_MODEL), BF16)
    if halo:
        mix_sc[0:HALO, 0:ATTN_W] = anp_ref[...]
        mix_sc[0:HALO, ATTN_W:D_MODEL] = gmp_ref[...]
        mix_sc[HALO:HALO + seg, 0:ATTN_W] = an_ref[...]
        mix_sc[HALO:HALO + seg, ATTN_W:D_MODEL] = gm_ref[...]
        mix_sc[HALO + seg:rows, 0:ATTN_W] = ann_ref[...]
        mix_sc[HALO + seg:rows, ATTN_W:D_MODEL] = gmn_ref[...]
        y = _dot(mix_sc[...], wo_ref[...])
        x1, h2 = residual_norm(x_ref[...], y[HALO:HALO + seg])
        out_ref[...] = x1
        lhs_sc[HALO:HALO + seg, :] = h2
        pos = pl.program_id(0) % tiles_per_seq
        lhs_sc[0:HALO, :] = zeros
        lhs_sc[HALO + seg:rows, :] = zeros

        @pl.when(pos != 0)
        def _():
            lhs_sc[0:HALO, :] = residual_norm(xp_ref[...], y[0:HALO])[1]

        @pl.when(pos != tiles_per_seq - 1)
        def _():
            lhs_sc[HALO + seg:rows, :] = residual_norm(xn_ref[...], y[HALO + seg:rows])[1]
    else:
        y = _dot(an_ref[...], wo_ref[0:ATTN_W, :]) + _dot(gm_ref[...], wo_ref[ATTN_W:D_MODEL, :])
        x1, h2 = residual_norm(x_ref[...], y)
        out_ref[...] = x1
        lhs_sc[0:HALO, :] = zeros
        for s in range(nseg):
            base = HALO + s * stride
            lhs_sc[base:base + seg, :] = h2[s * seg:(s + 1) * seg, :]
            lhs_sc[base + seg:base + stride, :] = zeros

    for c in range(N_FF_CHUNKS):
        slot = c % 2
        for k, col0 in enumerate((c * FF_CHUNK, D_FF + c * FF_CHUNK)):
            a = _dot(lhs_sc[...], wup_ref[:, col0:col0 + FF_CHUNK])
            for j in range(FF_CHUNK // LANES):
                a_sc[slot, 2 * k + j, :, :] = a[8:rows - 8, j * LANES:(j + 1) * LANES]
        for j in range(FF_CHUNK // LANES):
            gl = slice(c * FF_CHUNK + j * LANES, c * FF_CHUNK + (j + 1) * LANES)
            vl = slice(D_FF + gl.start, D_FF + gl.stop)
            for s in range(nseg):
                base = 8 + s * stride

                def conv(k, lanes):
                    acc = cb_ref[:, lanes]
                    for t in range(CONV_W):
                        acc = acc + a_sc[slot, k, base - 1 + t:base - 1 + t + seg, :] * cw_ref[t:t + 1, lanes]
                    return acc

                g = _silu(conv(j, gl)) * conv(2 + j, vl)
                g_sc[s * seg:(s + 1) * seg, gl] = g.astype(BF16)

    x2 = out_ref[...] + mod_ref[:, 5 * D_MODEL:6 * D_MODEL] * _dot(g_sc[...], wd_ref[...])
    out_ref[...] = _rms(x2, fg_ref[...])


def _ffn(x, an, gm, mods3, mod_row, p, seq, tm):
    n = x.shape[0]
    tiles_per_seq = max(seq // tm, 1)
    seg = min(seq, tm)
    nseg = tm // seg
    rows = HALO + nseg * (seg + HALO)
    const = lambda i: (0, 0)
    tile = lambda i: (i, 0)
    resident = dict(pipeline_mode=pl.Buffered(1))
    in_specs = [pl.BlockSpec((tm, D_MODEL), tile), pl.BlockSpec((tm, ATTN_W), tile),
                pl.BlockSpec((tm, GMLP_W), tile)]
    args = [x, an, gm]
    scratch = [pltpu.VMEM((rows, D_MODEL), BF16),
               pltpu.VMEM((2, 2 * FF_CHUNK // LANES, rows - 16, LANES), F32),
               pltpu.VMEM((tm, D_FF), BF16)]
    if tiles_per_seq > 1:
        per = tm // HALO
        last = n // HALO - 1
        prev = lambda i: (jnp.maximum(i * per - 1, 0), 0)
        nxt = lambda i: (jnp.minimum((i + 1) * per, last), 0)
        for arr, width in ((x, D_MODEL), (an, ATTN_W), (gm, GMLP_W)):
            in_specs += [pl.BlockSpec((HALO, width), prev), pl.BlockSpec((HALO, width), nxt)]
            args += [arr, arr]
        scratch.append(pltpu.VMEM((rows, D_MODEL), BF16))
    in_specs += [
        pl.BlockSpec((None, 1, N_MODS * D_MODEL), lambda i: (mod_row(i), 0, 0)),
        pl.BlockSpec((D_MODEL, D_MODEL), const, **resident),
        pl.BlockSpec((1, D_MODEL), const),
        pl.BlockSpec((D_MODEL, 2 * D_FF), const, **resident),
        pl.BlockSpec((CONV_W, 2 * D_FF), const),
        pl.BlockSpec((1, 2 * D_FF), const),
        pl.BlockSpec((D_FF, D_MODEL), const, **resident),
        pl.BlockSpec((1, D_MODEL), const),
    ]
    args += [mods3, p["w_o"], p["norm2_g"], p["w_up"], p["conv_w"], p["conv_b"], p["w_down"], p["final_g"]]
    return pl.pallas_call(
        functools.partial(_ffn_kernel, nseg=nseg, seg=seg, tiles_per_seq=tiles_per_seq),
        grid=(n // tm,),
        in_specs=in_specs,
        out_specs=pl.BlockSpec((tm, D_MODEL), tile),
        out_shape=jax.ShapeDtypeStruct((n, D_MODEL), F32),
        scratch_shapes=scratch,
        compiler_params=_params(1),
        name="ffn_halo" if tiles_per_seq > 1 else "ffn",
    )(*args)


def _rope_tables(length):
    pos = jnp.arange(length)
    row = (pos // GRID_W).astype(F32)
    col = (pos % GRID_W).astype(F32)
    n_freq = QK_ROPE // 4
    inv = ROPE_THETA ** (-(jnp.arange(n_freq, dtype=F32) / n_freq))
    ang_r, ang_c = row[:, None] * inv, col[:, None] * inv
    cos32 = jnp.concatenate([jnp.cos(ang_r)] * 2 + [jnp.cos(ang_c)] * 2, axis=-1)
    sin32 = jnp.concatenate([-jnp.sin(ang_r), jnp.sin(ang_r), -jnp.sin(ang_c), jnp.sin(ang_c)], axis=-1)
    tail = LANES - ROPE_LANE0 - QK_ROPE
    cos_q = jnp.concatenate([jnp.ones((length, ROPE_LANE0), F32), cos32, jnp.zeros((length, tail), F32)], -1)
    sin_q = jnp.concatenate([jnp.zeros((length, ROPE_LANE0), F32), sin32, jnp.zeros((length, tail), F32)], -1)
    return cos_q, sin_q


def _layer_params(l, norm1_g, w_in, q_norm_g, w_uq, kv_norm_g, w_ukv, v_norm_g, w_s, b_s, out_g_attn,
                  out_g_gmlp, w_o, norm2_g, w_up, conv_w, conv_b, w_down, final_g):
    wi = w_in[l]
    kr = wi[:, IN_QKV:IN_QKV + QK_ROPE]
    z32 = jnp.zeros((D_MODEL, QK_ROPE), F32)
    w_in_r = jnp.concatenate([wi[:, :IN_QKV], wi[:, IN_QKV + QK_ROPE:], kr, z32, kr, z32], axis=1).astype(BF16)
    w_uq_r = jnp.pad(w_uq[l].reshape(Q_RANK, N_HEADS, QK_NOPE + QK_ROPE),
                     ((0, 0), (0, 0), (0, HEAD_PAD - QK_NOPE - QK_ROPE))).reshape(Q_RANK, QK_W).astype(BF16)
    w_ukv3 = w_ukv[l].reshape(KV_RANK, N_HEADS, QK_NOPE + V_DIM)
    w_k = jnp.pad(w_ukv3[:, :, :QK_NOPE], ((0, 0), (0, 0), (0, HEAD_PAD - QK_NOPE))
                  ).reshape(KV_RANK, QK_W).astype(BF16)
    w_v = w_ukv3[:, :, QK_NOPE:]
    w_ve = jnp.pad(w_v[:, 0::2], ((0, 0), (0, 0), (0, V_DIM))).reshape(KV_RANK, ATTN_W).astype(BF16)
    w_vo = jnp.pad(w_v[:, 1::2], ((0, 0), (0, 0), (V_DIM, 0))).reshape(KV_RANK, ATTN_W).astype(BF16)
    row = lambda a: a.reshape(1, -1)
    return {
        "norm1_g": row(norm1_g[l]), "w_in": w_in_r, "q_norm_g": row(q_norm_g[l]), "w_uq": w_uq_r,
        "kv_norm_g": row(kv_norm_g[l]), "w_k": w_k, "w_ve": w_ve, "w_vo": w_vo,
        "v_norm_g": row(v_norm_g[l]), "w_s": w_s[l].astype(BF16), "b_st": b_s[l].T,
        "out_g_attn": row(out_g_attn[l]), "out_g_gmlp": row(out_g_gmlp[l]), "w_o": w_o[l].astype(BF16),
        "norm2_g": row(norm2_g[l]), "w_up": w_up[l].astype(BF16), "conv_w": conv_w[l],
        "conv_b": row(conv_b[l]), "w_down": w_down[l].astype(BF16), "final_g": row(final_g),
    }


def kernel(x_prompt, x_sample, cache_ckv, cache_krope, c, c_ctx, ada_w, ada_b, norm1_g, w_in, q_norm_g, w_uq, kv_norm_g, w_ukv, v_norm_g, w_s, b_s, out_g_attn, out_g_gmlp, w_o, norm2_g, w_up, conv_w, conv_b, w_down, final_g):
    batch, seq, _ = x_prompt.shape
    dec_batch, dec_seq, _ = x_sample.shape
    depth, past = cache_ckv.shape[1], cache_ckv.shape[2]
    assert depth == 1 and dec_batch + 1 <= MOD_ROWS
    l = 0
    p = _layer_params(l, norm1_g, w_in, q_norm_g, w_uq, kv_norm_g, w_ukv, v_norm_g, w_s, b_s, out_g_attn,
                      out_g_gmlp, w_o, norm2_g, w_up, conv_w, conv_b, w_down, final_g)

    cond = jnp.concatenate([c_ctx[None, :], c, jnp.zeros((MOD_ROWS - 1 - dec_batch, D_MODEL), F32)], axis=0)
    mods3 = _adaln(cond, ada_w[l], ada_b[l].reshape(1, -1)).reshape(MOD_ROWS, 1, N_MODS * D_MODEL)

    tm_pre, tm_ffn = 512, 512
    xp = x_prompt.reshape(batch * seq, D_MODEL)
    qp, kp, vep, vop, gmp, ckv_p, kr_p = _pre(xp, mods3, lambda i: 0, p, None, tm_pre)
    anp = _attention(qp, kp, vep, vop, None, p["out_g_attn"], seq, seq, 4, 4)
    yp = _ffn(xp, anp, gmp, mods3, lambda i: 0, p, seq, tm_ffn)

    xs = x_sample.reshape(dec_batch * dec_seq, D_MODEL)
    qs, ks, ves, vos, gms, _, _ = _pre(xs, mods3, lambda i: 1 + i // (dec_seq // tm_pre), p,
                                       _rope_tables(dec_seq), tm_pre)
    kr_pad = jnp.pad(cache_krope[:, l].reshape(dec_batch * past, QK_ROPE),
                     ((0, 0), (ROPE_LANE0, LANES - ROPE_LANE0 - QK_ROPE)))
    cache = _expand(cache_ckv[:, l].reshape(dec_batch * past, KV_RANK), kr_pad, p, 512)
    ans = _attention(qs, ks, ves, vos, cache, p["out_g_attn"], dec_seq, 512, 1, 4)
    ys = _ffn(xs, ans, gms, mods3, lambda i: 1 + i // (dec_seq // tm_ffn), p, dec_seq, tm_ffn)

    return (yp.reshape(batch, seq, D_MODEL), ys.reshape(dec_batch, dec_seq, D_MODEL),
            ckv_p.reshape(batch, 1, seq, KV_RANK), kr_p.reshape(batch, 1, seq, QK_ROPE))
```

```python
import functools

import numpy as np
import jax
import jax.numpy as jnp
from jax import lax
from jax.experimental import pallas as pl
from jax.experimental.pallas import tpu as pltpu

D_MODEL = 1024
GRID_W = 64
CHUNK = 128
N_HEADS = 8
QK_NOPE = 64
QK_ROPE = 32
V_DIM = 64
Q_RANK = 384
KV_RANK = 256
ATTN_W = N_HEADS * V_DIM
GMLP_W = D_MODEL - ATTN_W
GMLP_GROUPS = 4
GMLP_DG = GMLP_W // GMLP_GROUPS
D_FF = 2816
CONV_W = 3
ROPE_THETA = 10000.0
EPS = 1e-6
SM_SCALE = (QK_NOPE + QK_ROPE) ** -0.5
LOG2_E = 1.4426950408889634

LANES = 128
HEAD_PAD = LANES
ROPE_LANE0 = QK_NOPE
QK_W = N_HEADS * HEAD_PAD
N_MODS = 6
MOD_ROWS = 8
IN_QKV = Q_RANK + KV_RANK
FF_CHUNK = 256
HALO = 16
N_FF_CHUNKS = D_FF // FF_CHUNK
VMEM_LIMIT = 56 * 1024 * 1024

F32 = jnp.float32
BF16 = jnp.bfloat16


def _dot(a, b):
    return jnp.dot(a, b, preferred_element_type=F32)


def _dot_nt(a, b):
    return lax.dot_general(a, b, (((1,), (1,)), ((), ())), preferred_element_type=F32)


def _rms(x, g):
    return x * lax.rsqrt(jnp.mean(x * x, axis=-1, keepdims=True) + EPS) * g


def _gelu(x):
    return 0.5 * x * (1.0 + lax.erf(x * (0.5 ** 0.5)))


def _silu(x):
    return x * (1.0 / (1.0 + jnp.exp(-x)))


def _params(n_axes):
    return pltpu.CompilerParams(dimension_semantics=("arbitrary",) * n_axes,
                                vmem_limit_bytes=VMEM_LIMIT)


def _adaln_kernel(cond_ref, w_ref, b_ref, out_ref):
    s = _silu(cond_ref[...]).astype(BF16)
    out_ref[:, 0, :] = _dot(s, w_ref[...].astype(BF16)) + b_ref[...]


def _adaln(cond, ada_w, ada_b):
    return pl.pallas_call(
        _adaln_kernel,
        grid=(N_MODS,),
        in_specs=[
            pl.BlockSpec((MOD_ROWS, D_MODEL), lambda j: (0, 0)),
            pl.BlockSpec((D_MODEL, D_MODEL), lambda j: (0, j)),
            pl.BlockSpec((1, D_MODEL), lambda j: (0, j)),
        ],
        out_specs=pl.BlockSpec((MOD_ROWS, 1, D_MODEL), lambda j: (0, 0, j)),
        out_shape=jax.ShapeDtypeStruct((MOD_ROWS, 1, N_MODS * D_MODEL), F32),
        compiler_params=_params(1),
        name="adaln",
    )(cond, ada_w, ada_b)


def _pre_kernel(*refs, rope, tm):
    if rope:
        (x_ref, mod_ref, n1g_ref, wqkv_ref, wuv_ref, wkr_ref, qng_ref, wuq_ref, kvg_ref, wk_ref, wve_ref, wvo_ref,
         vng_ref, ws_ref, bst_ref, ogg_ref, cos_ref, sin_ref,
         q_out, k_out, ve_out, vo_out, gm_out, ckv_out, kr_out, go_sc) = refs
    else:
        (x_ref, mod_ref, n1g_ref, wqkv_ref, wuv_ref, wkr_ref, qng_ref, wuq_ref, kvg_ref, wk_ref, wve_ref, wvo_ref,
         vng_ref, ws_ref, bst_ref, ogg_ref,
         q_out, k_out, ve_out, vo_out, gm_out, ckv_out, kr_out, go_sc) = refs
    shift1 = mod_ref[:, 0:D_MODEL]
    scale1 = mod_ref[:, D_MODEL:2 * D_MODEL]
    h = (_rms(x_ref[...], n1g_ref[...]) * (1.0 + scale1) + shift1).astype(BF16)

    lane = lax.broadcasted_iota(jnp.int32, (1, LANES), 1)
    rope_lanes = lane >= ROPE_LANE0
    first_half = (lane % 16) < 8

    def rotate(v, cos):
        swapped = jnp.where(first_half, pltpu.roll(v, LANES - 8, 1), pltpu.roll(v, 8, 1))
        return v * cos + swapped * sin_ref[...]

    qkv = _dot(h, wqkv_ref[...])
    qn = _rms(qkv[:, 0:Q_RANK], qng_ref[...]).astype(BF16)
    ckv = _rms(qkv[:, Q_RANK:IN_QKV], kvg_ref[...])
    ckv_out[...] = ckv
    ckv_b = ckv.astype(BF16)
    q = _dot(qn, wuq_ref[...]) * (SM_SCALE * LOG2_E)
    kr = _dot(h, wkr_ref[...])
    kr_out[...] = kr[:, 0:QK_ROPE]
    if rope:
        cos_q = cos_ref[...]
        kr_k = rotate(kr, jnp.where(rope_lanes, cos_q, 0.0))
    else:
        kr_k = jnp.where(rope_lanes, kr, 0.0)
    kn = _dot(ckv_b, wk_ref[...])
    for hd in range(N_HEADS):
        sl = slice(hd * HEAD_PAD, (hd + 1) * HEAD_PAD)
        qh = q[:, sl]
        if rope:
            qh = rotate(qh, cos_q)
        q_out[:, sl] = qh.astype(BF16)
        k_out[:, sl] = (kn[:, sl] + kr_k).astype(BF16)
    ve_out[...] = _dot(ckv_b, wve_ref[...]).astype(BF16)
    vo_out[...] = _dot(ckv_b, wvo_ref[...]).astype(BF16)

    uv = _dot(h, wuv_ref[...])
    gu = _gelu(uv[:, 0:GMLP_W])
    gv = _gelu(uv[:, GMLP_W:2 * GMLP_W])
    for g in range(GMLP_GROUPS):
        sl = slice(g * GMLP_DG, (g + 1) * GMLP_DG)
        vg = gv[:, sl]
        vg = (vg * lax.rsqrt(jnp.mean(vg * vg, axis=-1, keepdims=True) + EPS) * vng_ref[:, sl]).astype(BF16)
        bias = bst_ref[:, g:g + 1]
        for n in range(tm // CHUNK):
            rows = slice(n * CHUNK, (n + 1) * CHUNK)
            s = _dot(ws_ref[g], vg[rows, :]) + bias
            go_sc[rows, sl] = gu[rows, sl] * s
    gm_out[...] = _rms(go_sc[...], ogg_ref[...]).astype(BF16)


def _pre(x, mods3, mod_row, p, rope_tabs, tm):
    n = x.shape[0]
    rope = rope_tabs is not None
    const = lambda i: (0, 0)
    tile = lambda i: (i, 0)
    in_specs = [
        pl.BlockSpec((tm, D_MODEL), tile),
        pl.BlockSpec((None, 1, N_MODS * D_MODEL), lambda i: (mod_row(i), 0, 0)),
        pl.BlockSpec((1, D_MODEL), const),
        pl.BlockSpec((D_MODEL, IN_QKV), const),
        pl.BlockSpec((D_MODEL, 2 * GMLP_W), const),
        pl.BlockSpec((D_MODEL, LANES), const),
        pl.BlockSpec((1, Q_RANK), const),
        pl.BlockSpec((Q_RANK, QK_W), const),
        pl.BlockSpec((1, KV_RANK), const),
        pl.BlockSpec((KV_RANK, QK_W), const),
        pl.BlockSpec((KV_RANK, ATTN_W), const),
        pl.BlockSpec((KV_RANK, ATTN_W), const),
        pl.BlockSpec((1, GMLP_W), const),
        pl.BlockSpec((GMLP_GROUPS, CHUNK, CHUNK), lambda i: (0, 0, 0)),
        pl.BlockSpec((CHUNK, GMLP_GROUPS), const),
        pl.BlockSpec((1, GMLP_W), const),
    ]
    args = [x, mods3, p["norm1_g"], p["w_qkv"], p["w_uv"], p["w_kr"], p["q_norm_g"], p["w_uq"], p["kv_norm_g"], p["w_k"],
            p["w_ve"], p["w_vo"], p["v_norm_g"], p["w_s"], p["b_st"], p["out_g_gmlp"]]
    if rope:
        cos_q, sin_q = rope_tabs
        per_seq = cos_q.shape[0] // tm
        in_specs += [pl.BlockSpec((tm, LANES), lambda i: (i % per_seq, 0))] * 2
        args += [cos_q, sin_q]
    widths = [(QK_W, BF16), (QK_W, BF16), (ATTN_W, BF16), (ATTN_W, BF16), (GMLP_W, BF16),
              (KV_RANK, F32), (QK_ROPE, F32)]
    return pl.pallas_call(
        functools.partial(_pre_kernel, rope=rope, tm=tm),
        grid=(n // tm,),
        in_specs=in_specs,
        out_specs=[pl.BlockSpec((tm, w), tile) for w, _ in widths],
        out_shape=[jax.ShapeDtypeStruct((n, w), dt) for w, dt in widths],
        scratch_shapes=[pltpu.VMEM((tm, GMLP_W), F32)],
        compiler_params=_params(1),
        name="pre_rope" if rope else "pre",
    )(*args)


def _expand_kernel(ckv_ref, kr_ref, wk_ref, wve_ref, wvo_ref, k_out, ve_out, vo_out):
    ckv_b = ckv_ref[...].astype(BF16)
    kn = _dot(ckv_b, wk_ref[...])
    kr = kr_ref[...]
    for hd in range(N_HEADS):
        sl = slice(hd * HEAD_PAD, (hd + 1) * HEAD_PAD)
        k_out[:, sl] = (kn[:, sl] + kr).astype(BF16)
    ve_out[...] = _dot(ckv_b, wve_ref[...]).astype(BF16)
    vo_out[...] = _dot(ckv_b, wvo_ref[...]).astype(BF16)


def _expand(ckv, kr_pad, p, tm):
    n = ckv.shape[0]
    const = lambda i: (0, 0)
    tile = lambda i: (i, 0)
    widths = [QK_W, ATTN_W, ATTN_W]
    return pl.pallas_call(
        _expand_kernel,
        grid=(n // tm,),
        in_specs=[
            pl.BlockSpec((tm, KV_RANK), tile),
            pl.BlockSpec((tm, LANES), tile),
            pl.BlockSpec((KV_RANK, QK_W), const),
            pl.BlockSpec((KV_RANK, ATTN_W), const),
            pl.BlockSpec((KV_RANK, ATTN_W), const),
        ],
        out_specs=[pl.BlockSpec((tm, w), tile) for w in widths],
        out_shape=[jax.ShapeDtypeStruct((n, w), BF16) for w in widths],
        compiler_params=_params(1),
        name="expand",
    )(ckv, kr_pad, p["w_k"], p["w_ve"], p["w_vo"])


def _attn_kernel(*refs, cached, group, seq, tq, pps):
    if cached:
        q_ref, k_ref, ve_ref, vo_ref, kc_ref, vec_ref, voc_ref, oga_ref, out_ref, o_sc = refs
    else:
        q_ref, k_ref, ve_ref, vo_ref, oga_ref, out_ref, o_sc = refs
    step = pl.program_id(2)
    lane = lax.broadcasted_iota(jnp.int32, (1, LANES), 1)
    even_lanes = lane < V_DIM
    for g in range(group):
        qrows = slice(g * tq, (g + 1) * tq)
        krows = slice(g * seq, (g + 1) * seq)
        for pp in range(pps):
            psl = slice(pp * LANES, (pp + 1) * LANES)
            acc = None
            inv = []
            for par, (v_ref, vc_ref) in enumerate(((ve_ref, vec_ref if cached else None),
                                                   (vo_ref, voc_ref if cached else None))):
                hd = 2 * pp + par
                hsl = slice(hd * HEAD_PAD, (hd + 1) * HEAD_PAD)
                qh = q_ref[qrows, hsl]
                s = _dot_nt(qh, k_ref[krows, hsl])
                m = jnp.max(s, axis=-1, keepdims=True)
                if cached:
                    sc = _dot_nt(qh, kc_ref[:, hsl])
                    m = jnp.maximum(m, jnp.max(sc, axis=-1, keepdims=True))
                e = jnp.exp2(s - m)
                l = jnp.sum(e, axis=-1, keepdims=True)
                o = _dot(e.astype(BF16), v_ref[krows, psl])
                if cached:
                    ec = jnp.exp2(sc - m)
                    l = l + jnp.sum(ec, axis=-1, keepdims=True)
                    o = o + _dot(ec.astype(BF16), vc_ref[:, psl])
                acc = o if acc is None else acc + o
                inv.append(1.0 / l)
            o_sc[step * pps + pp, qrows, :] = acc * jnp.where(even_lanes, inv[0], inv[1])

    @pl.when(step == N_HEADS // 2 // pps - 1)
    def _():
        blocks = [o_sc[j] for j in range(N_HEADS // 2)]
        ssq = sum(jnp.sum(b * b, axis=-1, keepdims=True) for b in blocks)
        r = lax.rsqrt(ssq * (1.0 / ATTN_W) + EPS)
        for j, b in enumerate(blocks):
            sl = slice(j * LANES, (j + 1) * LANES)
            out_ref[:, sl] = (b * r * oga_ref[:, sl]).astype(BF16)


def _attention(q, k, ve, vo, cache, oga, seq, tq, group, pps):
    n = q.shape[0]
    nq = seq // tq
    n_pairs = N_HEADS // 2
    qtile = lambda b, i, j: (b * nq + i, j)
    kv = lambda b, i, j: (b, j)
    qk_w, v_w = pps * 2 * HEAD_PAD, pps * LANES
    in_specs = [
        pl.BlockSpec((group * tq, qk_w), qtile),
        pl.BlockSpec((group * seq, qk_w), kv),
        pl.BlockSpec((group * seq, v_w), kv),
        pl.BlockSpec((group * seq, v_w), kv),
    ]
    args = [q, k, ve, vo]
    if cache is not None:
        kc, vec, voc = cache
        past = kc.shape[0] // (n // seq)
        in_specs += [pl.BlockSpec((past, qk_w), kv), pl.BlockSpec((past, v_w), kv),
                     pl.BlockSpec((past, v_w), kv)]
        args += [kc, vec, voc]
    in_specs.append(pl.BlockSpec((1, ATTN_W), lambda b, i, j: (0, 0)))
    args.append(oga)
    return pl.pallas_call(
        functools.partial(_attn_kernel, cached=cache is not None, group=group, seq=seq, tq=tq, pps=pps),
        grid=(n // (group * seq), nq, n_pairs // pps),
        in_specs=in_specs,
        out_specs=pl.BlockSpec((group * tq, ATTN_W), lambda b, i, j: (b * nq + i, 0)),
        out_shape=jax.ShapeDtypeStruct((n, ATTN_W), BF16),
        scratch_shapes=[pltpu.VMEM((n_pairs, group * tq, LANES), F32)],
        compiler_params=_params(3),
        name="attn_cached" if cache is not None else "attn",
    )(*args)


def _ffn_kernel(*refs, nseg, seg, tiles_per_seq):
    halo = tiles_per_seq > 1
    if halo:
        (x_ref, an_ref, gm_ref, xp_ref, xn_ref, anp_ref, ann_ref, gmp_ref, gmn_ref, mod_ref, wo_ref, n2g_ref,
         wup_ref, cw_ref, cb_ref, wd_ref, fg_ref, out_ref, lhs_sc, a_sc, g_sc, mix_sc) = refs
    else:
        (x_ref, an_ref, gm_ref, mod_ref, wo_ref, n2g_ref,
         wup_ref, cw_ref, cb_ref, wd_ref, fg_ref, out_ref, lhs_sc, a_sc, g_sc) = refs
    stride = seg + HALO
    rows = HALO + nseg * stride
    gate1 = mod_ref[:, 2 * D_MODEL:3 * D_MODEL]
    shift2 = mod_ref[:, 3 * D_MODEL:4 * D_MODEL]
    scale2 = mod_ref[:, 4 * D_MODEL:5 * D_MODEL]

    def residual_norm(x, y):
        x1 = x + gate1 * y
        return x1, (_rms(x1, n2g_ref[...]) * (1.0 + scale2) + shift2).astype(BF16)

    zeros = jnp.zeros((HALO, D_MODEL), BF16)
    if halo:
        mix_sc[0:HALO, 0:ATTN_W] = anp_ref[...]
        mix_sc[0:HALO, ATTN_W:D_MODEL] = gmp_ref[...]
        mix_sc[HALO:HALO + seg, 0:ATTN_W] = an_ref[...]
        mix_sc[HALO:HALO + seg, ATTN_W:D_MODEL] = gm_ref[...]
        mix_sc[HALO + seg:rows, 0:ATTN_W] = ann_ref[...]
        mix_sc[HALO + seg:rows, ATTN_W:D_MODEL] = gmn_ref[...]
        y = _dot(mix_sc[...], wo_ref[...])
        x1, h2 = residual_norm(x_ref[...], y[HALO:HALO + seg])
        out_ref[...] = x1
        lhs_sc[HALO:HALO + seg, :] = h2
        pos = pl.program_id(0) % tiles_per_seq
        lhs_sc[0:HALO, :] = zeros
        lhs_sc[HALO + seg:rows, :] = zeros

        @pl.when(pos != 0)
        def _():
            lhs_sc[0:HALO, :] = residual_norm(xp_ref[...], y[0:HALO])[1]

        @pl.when(pos != tiles_per_seq - 1)
        def _():
            lhs_sc[HALO + seg:rows, :] = residual_norm(xn_ref[...], y[HALO + seg:rows])[1]
    else:
        y = _dot(an_ref[...], wo_ref[0:ATTN_W, :]) + _dot(gm_ref[...], wo_ref[ATTN_W:D_MODEL, :])
        x1, h2 = residual_norm(x_ref[...], y)
        out_ref[...] = x1
        lhs_sc[0:HALO, :] = zeros
        for s in range(nseg):
            base = HALO + s * stride
            lhs_sc[base:base + seg, :] = h2[s * seg:(s + 1) * seg, :]
            lhs_sc[base + seg:base + stride, :] = zeros

    for c in range(N_FF_CHUNKS):
        slot = c % 2
        for k, col0 in enumerate((c * FF_CHUNK, D_FF + c * FF_CHUNK)):
            a = _dot(lhs_sc[...], wup_ref[:, col0:col0 + FF_CHUNK])
            for j in range(FF_CHUNK // LANES):
                a_sc[slot, 2 * k + j, :, :] = a[8:rows - 8, j * LANES:(j + 1) * LANES]
        for j in range(FF_CHUNK // LANES):
            gl = slice(c * FF_CHUNK + j * LANES, c * FF_CHUNK + (j + 1) * LANES)
            vl = slice(D_FF + gl.start, D_FF + gl.stop)
            for s in range(nseg):
                base = 8 + s * stride

                def conv(k, lanes):
                    acc = cb_ref[:, lanes]
                    for t in range(CONV_W):
                        acc = acc + a_sc[slot, k, base - 1 + t:base - 1 + t + seg, :] * cw_ref[t:t + 1, lanes]
                    return acc

                g = _silu(conv(j, gl)) * conv(2 + j, vl)
                g_sc[s * seg:(s + 1) * seg, gl] = g.astype(BF16)

    x2 = out_ref[...] + mod_ref[:, 5 * D_MODEL:6 * D_MODEL] * _dot(g_sc[...], wd_ref[...])
    out_ref[...] = _rms(x2, fg_ref[...])


def _ffn(x, an, gm, mods3, mod_row, p, seq, tm):
    n = x.shape[0]
    tiles_per_seq = max(seq // tm, 1)
    seg = min(seq, tm)
    nseg = tm // seg
    rows = HALO + nseg * (seg + HALO)
    const = lambda i: (0, 0)
    tile = lambda i: (i, 0)
    resident = dict(pipeline_mode=pl.Buffered(1))
    in_specs = [pl.BlockSpec((tm, D_MODEL), tile), pl.BlockSpec((tm, ATTN_W), tile),
                pl.BlockSpec((tm, GMLP_W), tile)]
    args = [x, an, gm]
    scratch = [pltpu.VMEM((rows, D_MODEL), BF16),
               pltpu.VMEM((2, 2 * FF_CHUNK // LANES, rows - 16, LANES), F32),
               pltpu.VMEM((tm, D_FF), BF16)]
    if tiles_per_seq > 1:
        per = tm // HALO
        last = n // HALO - 1
        prev = lambda i: (jnp.maximum(i * per - 1, 0), 0)
        nxt = lambda i: (jnp.minimum((i + 1) * per, last), 0)
        for arr, width in ((x, D_MODEL), (an, ATTN_W), (gm, GMLP_W)):
            in_specs += [pl.BlockSpec((HALO, width), prev), pl.BlockSpec((HALO, width), nxt)]
            args += [arr, arr]
        scratch.append(pltpu.VMEM((rows, D_MODEL), BF16))
    in_specs += [
        pl.BlockSpec((None, 1, N_MODS * D_MODEL), lambda i: (mod_row(i), 0, 0)),
        pl.BlockSpec((D_MODEL, D_MODEL), const, **resident),
        pl.BlockSpec((1, D_MODEL), const),
        pl.BlockSpec((D_MODEL, 2 * D_FF), const, **resident),
        pl.BlockSpec((CONV_W, 2 * D_FF), const),
        pl.BlockSpec((1, 2 * D_FF), const),
        pl.BlockSpec((D_FF, D_MODEL), const, **resident),
        pl.BlockSpec((1, D_MODEL), const),
    ]
    args += [mods3, p["w_o"], p["norm2_g"], p["w_up"], p["conv_w"], p["conv_b"], p["w_down"], p["final_g"]]
    return pl.pallas_call(
        functools.partial(_ffn_kernel, nseg=nseg, seg=seg, tiles_per_seq=tiles_per_seq),
        grid=(n // tm,),
        in_specs=in_specs,
        out_specs=pl.BlockSpec((tm, D_MODEL), tile),
        out_shape=jax.ShapeDtypeStruct((n, D_MODEL), F32),
        scratch_shapes=scratch,
        compiler_params=_params(1),
        name="ffn_halo" if tiles_per_seq > 1 else "ffn",
    )(*args)


def _rope_tables(length):
    pos = np.arange(length)
    row = (pos // GRID_W).astype(np.float32)
    col = (pos % GRID_W).astype(np.float32)
    n_freq = QK_ROPE // 4
    inv = (np.float32(ROPE_THETA) ** (-(np.arange(n_freq, dtype=np.float32) / np.float32(n_freq)))).astype(np.float32)
    ang_r, ang_c = row[:, None] * inv, col[:, None] * inv
    cos32 = np.concatenate([np.cos(ang_r)] * 2 + [np.cos(ang_c)] * 2, axis=-1)
    sin32 = np.concatenate([-np.sin(ang_r), np.sin(ang_r), -np.sin(ang_c), np.sin(ang_c)], axis=-1)
    tail = LANES - ROPE_LANE0 - QK_ROPE
    cos_q = np.concatenate([np.ones((length, ROPE_LANE0)), cos32, np.zeros((length, tail))], -1)
    sin_q = np.concatenate([np.zeros((length, ROPE_LANE0)), sin32, np.zeros((length, tail))], -1)
    return jnp.asarray(cos_q, F32), jnp.asarray(sin_q, F32)


def _layer_params(l, norm1_g, w_in, q_norm_g, w_uq, kv_norm_g, w_ukv, v_norm_g, w_s, b_s, out_g_attn,
                  out_g_gmlp, w_o, norm2_g, w_up, conv_w, conv_b, w_down, final_g):
    wi = w_in[l]
    kr = wi[:, IN_QKV:IN_QKV + QK_ROPE].astype(BF16)
    z32 = jnp.zeros((D_MODEL, QK_ROPE), BF16)
    w_qkv = wi[:, :IN_QKV].astype(BF16)
    w_uv = wi[:, IN_QKV + QK_ROPE:].astype(BF16)
    w_kr = jnp.concatenate([kr, z32, kr, z32], axis=1)
    w_uq_r = jnp.pad(w_uq[l].reshape(Q_RANK, N_HEADS, QK_NOPE + QK_ROPE),
                     ((0, 0), (0, 0), (0, HEAD_PAD - QK_NOPE - QK_ROPE))).reshape(Q_RANK, QK_W).astype(BF16)
    w_ukv3 = w_ukv[l].reshape(KV_RANK, N_HEADS, QK_NOPE + V_DIM)
    w_k = jnp.pad(w_ukv3[:, :, :QK_NOPE], ((0, 0), (0, 0), (0, HEAD_PAD - QK_NOPE))
                  ).reshape(KV_RANK, QK_W).astype(BF16)
    w_v = w_ukv3[:, :, QK_NOPE:]
    w_ve = jnp.pad(w_v[:, 0::2], ((0, 0), (0, 0), (0, V_DIM))).reshape(KV_RANK, ATTN_W).astype(BF16)
    w_vo = jnp.pad(w_v[:, 1::2], ((0, 0), (0, 0), (V_DIM, 0))).reshape(KV_RANK, ATTN_W).astype(BF16)
    row = lambda a: a.reshape(1, -1)
    return {
        "norm1_g": row(norm1_g[l]), "w_qkv": w_qkv, "w_uv": w_uv, "w_kr": w_kr, "q_norm_g": row(q_norm_g[l]), "w_uq": w_uq_r,
        "kv_norm_g": row(kv_norm_g[l]), "w_k": w_k, "w_ve": w_ve, "w_vo": w_vo,
        "v_norm_g": row(v_norm_g[l]), "w_s": w_s[l].astype(BF16), "b_st": b_s[l].T,
        "out_g_attn": row(out_g_attn[l]), "out_g_gmlp": row(out_g_gmlp[l]), "w_o": w_o[l].astype(BF16),
        "norm2_g": row(norm2_g[l]), "w_up": w_up[l].astype(BF16), "conv_w": conv_w[l],
        "conv_b": row(conv_b[l]), "w_down": w_down[l].astype(BF16), "final_g": row(final_g),
    }


def kernel(x_prompt, x_sample, cache_ckv, cache_krope, c, c_ctx, ada_w, ada_b, norm1_g, w_in, q_norm_g, w_uq, kv_norm_g, w_ukv, v_norm_g, w_s, b_s, out_g_attn, out_g_gmlp, w_o, norm2_g, w_up, conv_w, conv_b, w_down, final_g):
    batch, seq, _ = x_prompt.shape
    dec_batch, dec_seq, _ = x_sample.shape
    depth, past = cache_ckv.shape[1], cache_ckv.shape[2]
    assert depth == 1 and dec_batch + 1 <= MOD_ROWS
    l = 0
    p = _layer_params(l, norm1_g, w_in, q_norm_g, w_uq, kv_norm_g, w_ukv, v_norm_g, w_s, b_s, out_g_attn,
                      out_g_gmlp, w_o, norm2_g, w_up, conv_w, conv_b, w_down, final_g)

    cond = jnp.concatenate([c_ctx[None, :], c, jnp.zeros((MOD_ROWS - 1 - dec_batch, D_MODEL), F32)], axis=0)
    mods3 = _adaln(cond, ada_w[l], ada_b[l].reshape(1, -1))

    tm_pre, tm_ffn = 512, 512
    xp = x_prompt.reshape(batch * seq, D_MODEL)
    qp, kp, vep, vop, gmp, ckv_p, kr_p = _pre(xp, mods3, lambda i: 0, p, None, tm_pre)
    anp = _attention(qp, kp, vep, vop, None, p["out_g_attn"], seq, seq, 4, 4)
    yp = _ffn(xp, anp, gmp, mods3, lambda i: 0, p, seq, tm_ffn)

    xs = x_sample.reshape(dec_batch * dec_seq, D_MODEL)
    qs, ks, ves, vos, gms, _, _ = _pre(xs, mods3, lambda i: 1 + i // (dec_seq // tm_pre), p,
                                       _rope_tables(dec_seq), tm_pre)
    kr_pad = jnp.pad(cache_krope[:, l].reshape(dec_batch * past, QK_ROPE),
                     ((0, 0), (ROPE_LANE0, LANES - ROPE_LANE0 - QK_ROPE)))
    cache = _expand(cache_ckv[:, l].reshape(dec_batch * past, KV_RANK), kr_pad, p, 512)
    ans = _attention(qs, ks, ves, vos, cache, p["out_g_attn"], dec_seq, 512, 1, 4)
    ys = _ffn(xs, ans, gms, mods3, lambda i: 1 + i // (dec_seq // tm_ffn), p, dec_seq, tm_ffn)

    return (yp.reshape(batch, seq, D_MODEL), ys.reshape(dec_batch, dec_seq, D_MODEL),
            ckv_p.reshape(batch, 1, seq, KV_RANK), kr_p.reshape(batch, 1, seq, QK_ROPE))
```

```python
import functools

import numpy as np
import jax
import jax.numpy as jnp
from jax import lax
from jax.experimental import pallas as pl
from jax.experimental.pallas import tpu as pltpu

D_MODEL = 1024
GRID_W = 64
CHUNK = 128
N_HEADS = 8
QK_NOPE = 64
QK_ROPE = 32
V_DIM = 64
Q_RANK = 384
KV_RANK = 256
ATTN_W = N_HEADS * V_DIM
GMLP_W = D_MODEL - ATTN_W
GMLP_GROUPS = 4
GMLP_DG = GMLP_W // GMLP_GROUPS
D_FF = 2816
CONV_W = 3
ROPE_THETA = 10000.0
EPS = 1e-6
SM_SCALE = (QK_NOPE + QK_ROPE) ** -0.5
LOG2_E = 1.4426950408889634

LANES = 128
HEAD_PAD = LANES
ROPE_LANE0 = QK_NOPE
QK_W = N_HEADS * HEAD_PAD
N_MODS = 6
MOD_ROWS = 8
IN_QKV = Q_RANK + KV_RANK
FF_CHUNK = 256
HALO = 16
N_FF_CHUNKS = D_FF // FF_CHUNK
VMEM_LIMIT = 56 * 1024 * 1024

F32 = jnp.float32
BF16 = jnp.bfloat16


def _dot(a, b):
    return jnp.dot(a, b, preferred_element_type=F32)


def _dot_nt(a, b):
    return lax.dot_general(a, b, (((1,), (1,)), ((), ())), preferred_element_type=F32)


def _rms(x, g):
    return x * lax.rsqrt(jnp.mean(x * x, axis=-1, keepdims=True) + EPS) * g


def _gelu(x):
    return 0.5 * x * (1.0 + lax.erf(x * (0.5 ** 0.5)))


def _silu(x):
    return x * (1.0 / (1.0 + jnp.exp(-x)))


def _params(n_axes):
    return pltpu.CompilerParams(dimension_semantics=("arbitrary",) * n_axes,
                                vmem_limit_bytes=VMEM_LIMIT)


def _adaln_kernel(cond_ref, w_ref, b_ref, out_ref):
    s = _silu(cond_ref[...]).astype(BF16)
    out_ref[:, 0, :] = _dot(s, w_ref[...].astype(BF16)) + b_ref[...]


def _adaln(cond, ada_w, ada_b):
    return pl.pallas_call(
        _adaln_kernel,
        grid=(N_MODS,),
        in_specs=[
            pl.BlockSpec((MOD_ROWS, D_MODEL), lambda j: (0, 0)),
            pl.BlockSpec((D_MODEL, D_MODEL), lambda j: (0, j)),
            pl.BlockSpec((1, D_MODEL), lambda j: (0, j)),
        ],
        out_specs=pl.BlockSpec((MOD_ROWS, 1, D_MODEL), lambda j: (0, 0, j)),
        out_shape=jax.ShapeDtypeStruct((MOD_ROWS, 1, N_MODS * D_MODEL), F32),
        compiler_params=_params(1),
        name="adaln",
    )(cond, ada_w, ada_b)


def _pre_kernel(*refs, rope, tm):
    if rope:
        (x_ref, mod_ref, n1g_ref, wqkv_ref, wuv_ref, wkr_ref, qng_ref, wuq_ref, kvg_ref, wk_ref, wve_ref, wvo_ref,
         vng_ref, ws_ref, bst_ref, ogg_ref, cos_ref, sin_ref,
         q_out, k_out, ve_out, vo_out, gm_out, ckv_out, kr_out, go_sc) = refs
    else:
        (x_ref, mod_ref, n1g_ref, wqkv_ref, wuv_ref, wkr_ref, qng_ref, wuq_ref, kvg_ref, wk_ref, wve_ref, wvo_ref,
         vng_ref, ws_ref, bst_ref, ogg_ref,
         q_out, k_out, ve_out, vo_out, gm_out, ckv_out, kr_out, go_sc) = refs
    shift1 = mod_ref[:, 0:D_MODEL]
    scale1 = mod_ref[:, D_MODEL:2 * D_MODEL]
    h = (_rms(x_ref[...], n1g_ref[...]) * (1.0 + scale1) + shift1).astype(BF16)

    lane = lax.broadcasted_iota(jnp.int32, (1, LANES), 1)
    rope_lanes = lane >= ROPE_LANE0
    first_half = (lane % 16) < 8

    def rotate(v, cos):
        swapped = jnp.where(first_half, pltpu.roll(v, LANES - 8, 1), pltpu.roll(v, 8, 1))
        return v * cos + swapped * sin_ref[...]

    qkv = _dot(h, wqkv_ref[...])
    qn = _rms(qkv[:, 0:Q_RANK], qng_ref[...]).astype(BF16)
    ckv = _rms(qkv[:, Q_RANK:IN_QKV], kvg_ref[...])
    ckv_out[...] = ckv
    ckv_b = ckv.astype(BF16)
    q = _dot(qn, wuq_ref[...]) * (SM_SCALE * LOG2_E)
    kr = _dot(h, wkr_ref[...])
    kr_out[...] = kr[:, 0:QK_ROPE]
    if rope:
        cos_q = cos_ref[...]
        kr_k = rotate(kr, jnp.where(rope_lanes, cos_q, 0.0))
    else:
        kr_k = jnp.where(rope_lanes, kr, 0.0)
    kn = _dot(ckv_b, wk_ref[...])
    for hd in range(N_HEADS):
        sl = slice(hd * HEAD_PAD, (hd + 1) * HEAD_PAD)
        qh = q[:, sl]
        if rope:
            qh = rotate(qh, cos_q)
        q_out[:, sl] = qh.astype(BF16)
        k_out[:, sl] = (kn[:, sl] + kr_k).astype(BF16)
    ve_out[...] = _dot(ckv_b, wve_ref[...]).astype(BF16)
    vo_out[...] = _dot(ckv_b, wvo_ref[...]).astype(BF16)

    uv = _dot(h, wuv_ref[...])
    gu = _gelu(uv[:, 0:GMLP_W])
    gv = _gelu(uv[:, GMLP_W:2 * GMLP_W])
    for g in range(GMLP_GROUPS):
        sl = slice(g * GMLP_DG, (g + 1) * GMLP_DG)
        vg = gv[:, sl]
        vg = (vg * lax.rsqrt(jnp.mean(vg * vg, axis=-1, keepdims=True) + EPS) * vng_ref[:, sl]).astype(BF16)
        bias = bst_ref[:, g:g + 1]
        for n in range(tm // CHUNK):
            rows = slice(n * CHUNK, (n + 1) * CHUNK)
            s = _dot(ws_ref[g], vg[rows, :]) + bias
            go_sc[rows, sl] = gu[rows, sl] * s
    gm_out[...] = _rms(go_sc[...], ogg_ref[...]).astype(BF16)


def _pre(x, mods3, mod_row, p, rope_tabs, tm):
    n = x.shape[0]
    rope = rope_tabs is not None
    const = lambda i: (0, 0)
    tile = lambda i: (i, 0)
    in_specs = [
        pl.BlockSpec((tm, D_MODEL), tile),
        pl.BlockSpec((None, 1, N_MODS * D_MODEL), lambda i: (mod_row(i), 0, 0)),
        pl.BlockSpec((1, D_MODEL), const),
        pl.BlockSpec((D_MODEL, IN_QKV), const),
        pl.BlockSpec((D_MODEL, 2 * GMLP_W), const),
        pl.BlockSpec((D_MODEL, LANES), const),
        pl.BlockSpec((1, Q_RANK), const),
        pl.BlockSpec((Q_RANK, QK_W), const),
        pl.BlockSpec((1, KV_RANK), const),
        pl.BlockSpec((KV_RANK, QK_W), const),
        pl.BlockSpec((KV_RANK, ATTN_W), const),
        pl.BlockSpec((KV_RANK, ATTN_W), const),
        pl.BlockSpec((1, GMLP_W), const),
        pl.BlockSpec((GMLP_GROUPS, CHUNK, CHUNK), lambda i: (0, 0, 0)),
        pl.BlockSpec((CHUNK, GMLP_GROUPS), const),
        pl.BlockSpec((1, GMLP_W), const),
    ]
    args = [x, mods3, p["norm1_g"], p["w_qkv"], p["w_uv"], p["w_kr"], p["q_norm_g"], p["w_uq"],
            p["kv_norm_g"], p["w_k"], p["w_ve"], p["w_vo"], p["v_norm_g"], p["w_s"], p["b_st"], p["out_g_gmlp"]]
    if rope:
        cos_q, sin_q = rope_tabs
        per_seq = cos_q.shape[0] // tm
        in_specs += [pl.BlockSpec((tm, LANES), lambda i: (i % per_seq, 0))] * 2
        args += [cos_q, sin_q]
    widths = [(QK_W, BF16), (QK_W, BF16), (ATTN_W, BF16), (ATTN_W, BF16), (GMLP_W, BF16),
              (KV_RANK, F32), (QK_ROPE, F32)]
    return pl.pallas_call(
        functools.partial(_pre_kernel, rope=rope, tm=tm),
        grid=(n // tm,),
        in_specs=in_specs,
        out_specs=[pl.BlockSpec((tm, w), tile) for w, _ in widths],
        out_shape=[jax.ShapeDtypeStruct((n, w), dt) for w, dt in widths],
        scratch_shapes=[pltpu.VMEM((tm, GMLP_W), F32)],
        compiler_params=_params(1),
        name="pre_rope" if rope else "pre",
    )(*args)


def _expand_kernel(ckv_ref, kr_ref, wk_ref, wve_ref, wvo_ref, k_out, ve_out, vo_out):
    ckv_b = ckv_ref[...].astype(BF16)
    kn = _dot(ckv_b, wk_ref[...])
    kr = kr_ref[...]
    for hd in range(N_HEADS):
        sl = slice(hd * HEAD_PAD, (hd + 1) * HEAD_PAD)
        k_out[:, sl] = (kn[:, sl] + kr).astype(BF16)
    ve_out[...] = _dot(ckv_b, wve_ref[...]).astype(BF16)
    vo_out[...] = _dot(ckv_b, wvo_ref[...]).astype(BF16)


def _expand(ckv, kr_pad, p, tm):
    n = ckv.shape[0]
    const = lambda i: (0, 0)
    tile = lambda i: (i, 0)
    widths = [QK_W, ATTN_W, ATTN_W]
    return pl.pallas_call(
        _expand_kernel,
        grid=(n // tm,),
        in_specs=[
            pl.BlockSpec((tm, KV_RANK), tile),
            pl.BlockSpec((tm, LANES), tile),
            pl.BlockSpec((KV_RANK, QK_W), const),
            pl.BlockSpec((KV_RANK, ATTN_W), const),
            pl.BlockSpec((KV_RANK, ATTN_W), const),
        ],
        out_specs=[pl.BlockSpec((tm, w), tile) for w in widths],
        out_shape=[jax.ShapeDtypeStruct((n, w), BF16) for w in widths],
        compiler_params=_params(1),
        name="expand",
    )(ckv, kr_pad, p["w_k"], p["w_ve"], p["w_vo"])


def _attn_kernel(*refs, cached, group, seq, tq, pps, n_casts):
    n_main = 8 if cached else 5
    cast_in = refs[n_main:n_main + n_casts]
    out_ref = refs[n_main + n_casts]
    cast_out = refs[n_main + n_casts + 1:n_main + 2 * n_casts + 1]
    o_sc = refs[-1]
    if cached:
        q_ref, k_ref, ve_ref, vo_ref, kc_ref, vec_ref, voc_ref, oga_ref = refs[:n_main]
    else:
        q_ref, k_ref, ve_ref, vo_ref, oga_ref = refs[:n_main]
    for src, dst in zip(cast_in, cast_out):
        dst[...] = src[...].astype(BF16)
    step = pl.program_id(2)
    lane = lax.broadcasted_iota(jnp.int32, (1, LANES), 1)
    even_lanes = lane < V_DIM
    for g in range(group):
        qrows = slice(g * tq, (g + 1) * tq)
        krows = slice(g * seq, (g + 1) * seq)
        for pp in range(pps):
            psl = slice(pp * LANES, (pp + 1) * LANES)
            acc = None
            inv = []
            for par, (v_ref, vc_ref) in enumerate(((ve_ref, vec_ref if cached else None),
                                                   (vo_ref, voc_ref if cached else None))):
                hd = 2 * pp + par
                hsl = slice(hd * HEAD_PAD, (hd + 1) * HEAD_PAD)
                qh = q_ref[qrows, hsl]
                s = _dot_nt(qh, k_ref[krows, hsl])
                m = jnp.max(s, axis=-1, keepdims=True)
                if cached:
                    sc = _dot_nt(qh, kc_ref[:, hsl])
                    m = jnp.maximum(m, jnp.max(sc, axis=-1, keepdims=True))
                e = jnp.exp2(s - m)
                l = jnp.sum(e, axis=-1, keepdims=True)
                o = _dot(e.astype(BF16), v_ref[krows, psl])
                if cached:
                    ec = jnp.exp2(sc - m)
                    l = l + jnp.sum(ec, axis=-1, keepdims=True)
                    o = o + _dot(ec.astype(BF16), vc_ref[:, psl])
                acc = o if acc is None else acc + o
                inv.append(1.0 / l)
            o_sc[step * pps + pp, qrows, :] = acc * jnp.where(even_lanes, inv[0], inv[1])

    @pl.when(step == N_HEADS // 2 // pps - 1)
    def _():
        blocks = [o_sc[j] for j in range(N_HEADS // 2)]
        ssq = sum(jnp.sum(b * b, axis=-1, keepdims=True) for b in blocks)
        r = lax.rsqrt(ssq * (1.0 / ATTN_W) + EPS)
        for j, b in enumerate(blocks):
            sl = slice(j * LANES, (j + 1) * LANES)
            out_ref[:, sl] = (b * r * oga_ref[:, sl]).astype(BF16)


def _attention(q, k, ve, vo, cache, oga, seq, tq, group, pps, casts):
    n = q.shape[0]
    nq = seq // tq
    n_pairs = N_HEADS // 2
    grid = (n // (group * seq), nq, n_pairs // pps)
    n_steps = grid[0] * grid[1] * grid[2]
    linear = lambda b, i, j: ((b * grid[1] + i) * grid[2] + j, 0)
    cast_specs = [pl.BlockSpec((w.shape[0] // n_steps, w.shape[1]), linear) for w in casts]
    qtile = lambda b, i, j: (b * nq + i, j)
    kv = lambda b, i, j: (b, j)
    qk_w, v_w = pps * 2 * HEAD_PAD, pps * LANES
    in_specs = [
        pl.BlockSpec((group * tq, qk_w), qtile),
        pl.BlockSpec((group * seq, qk_w), kv),
        pl.BlockSpec((group * seq, v_w), kv),
        pl.BlockSpec((group * seq, v_w), kv),
    ]
    args = [q, k, ve, vo]
    if cache is not None:
        kc, vec, voc = cache
        past = kc.shape[0] // (n // seq)
        in_specs += [pl.BlockSpec((past, qk_w), kv), pl.BlockSpec((past, v_w), kv),
                     pl.BlockSpec((past, v_w), kv)]
        args += [kc, vec, voc]
    in_specs.append(pl.BlockSpec((1, ATTN_W), lambda b, i, j: (0, 0)))
    args.append(oga)
    outs = pl.pallas_call(
        functools.partial(_attn_kernel, cached=cache is not None, group=group, seq=seq, tq=tq, pps=pps,
                          n_casts=len(casts)),
        grid=grid,
        in_specs=in_specs + cast_specs,
        out_specs=[pl.BlockSpec((group * tq, ATTN_W), lambda b, i, j: (b * nq + i, 0))] + cast_specs,
        out_shape=[jax.ShapeDtypeStruct((n, ATTN_W), BF16)]
        + [jax.ShapeDtypeStruct(w.shape, BF16) for w in casts],
        scratch_shapes=[pltpu.VMEM((n_pairs, group * tq, LANES), F32)],
        compiler_params=_params(3),
        name="attn_cached" if cache is not None else "attn",
    )(*args, *casts)
    return outs[0], outs[1:]


def _ffn_kernel(*refs, nseg, seg, tiles_per_seq):
    halo = tiles_per_seq > 1
    if halo:
        (x_ref, an_ref, gm_ref, xp_ref, xn_ref, anp_ref, ann_ref, gmp_ref, gmn_ref, mod_ref, wo_ref, n2g_ref,
         wup_ref, cw_ref, cb_ref, wd_ref, fg_ref, out_ref, lhs_sc, a_sc, g_sc, mix_sc) = refs
    else:
        (x_ref, an_ref, gm_ref, mod_ref, wo_ref, n2g_ref,
         wup_ref, cw_ref, cb_ref, wd_ref, fg_ref, out_ref, lhs_sc, a_sc, g_sc) = refs
    stride = seg + HALO
    rows = HALO + nseg * stride
    gate1 = mod_ref[:, 2 * D_MODEL:3 * D_MODEL]
    shift2 = mod_ref[:, 3 * D_MODEL:4 * D_MODEL]
    scale2 = mod_ref[:, 4 * D_MODEL:5 * D_MODEL]

    def residual_norm(x, y):
        x1 = x + gate1 * y
        return x1, (_rms(x1, n2g_ref[...]) * (1.0 + scale2) + shift2).astype(BF16)

    zeros = jnp.zeros((HALO, D_MODEL), BF16)
    if halo:
        mix_sc[0:HALO, 0:ATTN_W] = anp_ref[...]
        mix_sc[0:HALO, ATTN_W:D_MODEL] = gmp_ref[...]
        mix_sc[HALO:HALO + seg, 0:ATTN_W] = an_ref[...]
        mix_sc[HALO:HALO + seg, ATTN_W:D_MODEL] = gm_ref[...]
        mix_sc[HALO + seg:rows, 0:ATTN_W] = ann_ref[...]
        mix_sc[HALO + seg:rows, ATTN_W:D_MODEL] = gmn_ref[...]
        y = _dot(mix_sc[...], wo_ref[...])
        x1, h2 = residual_norm(x_ref[...], y[HALO:HALO + seg])
        out_ref[...] = x1
        lhs_sc[HALO:HALO + seg, :] = h2
        pos = pl.program_id(0) % tiles_per_seq
        lhs_sc[0:HALO, :] = zeros
        lhs_sc[HALO + seg:rows, :] = zeros

        @pl.when(pos != 0)
        def _():
            lhs_sc[0:HALO, :] = residual_norm(xp_ref[...], y[0:HALO])[1]

        @pl.when(pos != tiles_per_seq - 1)
        def _():
            lhs_sc[HALO + seg:rows, :] = residual_norm(xn_ref[...], y[HALO + seg:rows])[1]
    else:
        y = _dot(an_ref[...], wo_ref[0:ATTN_W, :]) + _dot(gm_ref[...], wo_ref[ATTN_W:D_MODEL, :])
        x1, h2 = residual_norm(x_ref[...], y)
        out_ref[...] = x1
        lhs_sc[0:HALO, :] = zeros
        for s in range(nseg):
            base = HALO + s * stride
            lhs_sc[base:base + seg, :] = h2[s * seg:(s + 1) * seg, :]
            lhs_sc[base + seg:base + stride, :] = zeros

    for c in range(N_FF_CHUNKS):
        slot = c % 2
        for k, col0 in enumerate((c * FF_CHUNK, D_FF + c * FF_CHUNK)):
            a = _dot(lhs_sc[...], wup_ref[:, col0:col0 + FF_CHUNK])
            for j in range(FF_CHUNK // LANES):
                a_sc[slot, 2 * k + j, :, :] = a[8:rows - 8, j * LANES:(j + 1) * LANES]
        for j in range(FF_CHUNK // LANES):
            gl = slice(c * FF_CHUNK + j * LANES, c * FF_CHUNK + (j + 1) * LANES)
            vl = slice(D_FF + gl.start, D_FF + gl.stop)
            for s in range(nseg):
                base = 8 + s * stride

                def conv(k, lanes):
                    acc = cb_ref[:, lanes]
                    for t in range(CONV_W):
                        acc = acc + a_sc[slot, k, base - 1 + t:base - 1 + t + seg, :] * cw_ref[t:t + 1, lanes]
                    return acc

                g = _silu(conv(j, gl)) * conv(2 + j, vl)
                g_sc[s * seg:(s + 1) * seg, gl] = g.astype(BF16)

    x2 = out_ref[...] + mod_ref[:, 5 * D_MODEL:6 * D_MODEL] * _dot(g_sc[...], wd_ref[...])
    out_ref[...] = _rms(x2, fg_ref[...])


def _ffn(x, an, gm, mods3, mod_row, p, ffn_w, seq, tm):
    n = x.shape[0]
    tiles_per_seq = max(seq // tm, 1)
    seg = min(seq, tm)
    nseg = tm // seg
    rows = HALO + nseg * (seg + HALO)
    const = lambda i: (0, 0)
    tile = lambda i: (i, 0)
    resident = dict(pipeline_mode=pl.Buffered(1))
    in_specs = [pl.BlockSpec((tm, D_MODEL), tile), pl.BlockSpec((tm, ATTN_W), tile),
                pl.BlockSpec((tm, GMLP_W), tile)]
    args = [x, an, gm]
    scratch = [pltpu.VMEM((rows, D_MODEL), BF16),
               pltpu.VMEM((2, 2 * FF_CHUNK // LANES, rows - 16, LANES), F32),
               pltpu.VMEM((tm, D_FF), BF16)]
    if tiles_per_seq > 1:
        per = tm // HALO
        last = n // HALO - 1
        prev = lambda i: (jnp.maximum(i * per - 1, 0), 0)
        nxt = lambda i: (jnp.minimum((i + 1) * per, last), 0)
        for arr, width in ((x, D_MODEL), (an, ATTN_W), (gm, GMLP_W)):
            in_specs += [pl.BlockSpec((HALO, width), prev), pl.BlockSpec((HALO, width), nxt)]
            args += [arr, arr]
        scratch.append(pltpu.VMEM((rows, D_MODEL), BF16))
    in_specs += [
        pl.BlockSpec((None, 1, N_MODS * D_MODEL), lambda i: (mod_row(i), 0, 0)),
        pl.BlockSpec((D_MODEL, D_MODEL), const, **resident),
        pl.BlockSpec((1, D_MODEL), const),
        pl.BlockSpec((D_MODEL, 2 * D_FF), const, **resident),
        pl.BlockSpec((CONV_W, 2 * D_FF), const),
        pl.BlockSpec((1, 2 * D_FF), const),
        pl.BlockSpec((D_FF, D_MODEL), const, **resident),
        pl.BlockSpec((1, D_MODEL), const),
    ]
    w_o, w_up, w_down = ffn_w
    args += [mods3, w_o, p["norm2_g"], w_up, p["conv_w"], p["conv_b"], w_down, p["final_g"]]
    return pl.pallas_call(
        functools.partial(_ffn_kernel, nseg=nseg, seg=seg, tiles_per_seq=tiles_per_seq),
        grid=(n // tm,),
        in_specs=in_specs,
        out_specs=pl.BlockSpec((tm, D_MODEL), tile),
        out_shape=jax.ShapeDtypeStruct((n, D_MODEL), F32),
        scratch_shapes=scratch,
        compiler_params=_params(1),
        name="ffn_halo" if tiles_per_seq > 1 else "ffn",
    )(*args)


def _rope_tables(length):
    pos = np.arange(length)
    row = (pos // GRID_W).astype(np.float32)
    col = (pos % GRID_W).astype(np.float32)
    n_freq = QK_ROPE // 4
    inv = (np.float32(ROPE_THETA) ** (-(np.arange(n_freq, dtype=np.float32) / np.float32(n_freq)))).astype(np.float32)
    ang_r, ang_c = row[:, None] * inv, col[:, None] * inv
    cos32 = np.concatenate([np.cos(ang_r)] * 2 + [np.cos(ang_c)] * 2, axis=-1)
    sin32 = np.concatenate([-np.sin(ang_r), np.sin(ang_r), -np.sin(ang_c), np.sin(ang_c)], axis=-1)
    tail = LANES - ROPE_LANE0 - QK_ROPE
    cos_q = np.concatenate([np.ones((length, ROPE_LANE0)), cos32, np.zeros((length, tail))], -1)
    sin_q = np.concatenate([np.zeros((length, ROPE_LANE0)), sin32, np.zeros((length, tail))], -1)
    return jnp.asarray(cos_q, F32), jnp.asarray(sin_q, F32)


def _layer_params(l, norm1_g, w_in, q_norm_g, w_uq, kv_norm_g, w_ukv, v_norm_g, w_s, b_s, out_g_attn,
                  out_g_gmlp, norm2_g, conv_w, conv_b, final_g):
    wi = w_in[l]
    kr = wi[:, IN_QKV:IN_QKV + QK_ROPE].astype(BF16)
    z32 = jnp.zeros((D_MODEL, QK_ROPE), BF16)
    w_qkv = wi[:, :IN_QKV].astype(BF16)
    w_uv = wi[:, IN_QKV + QK_ROPE:].astype(BF16)
    w_kr = jnp.concatenate([kr, z32, kr, z32], axis=1)
    w_uq_r = jnp.pad(w_uq[l].reshape(Q_RANK, N_HEADS, QK_NOPE + QK_ROPE),
                     ((0, 0), (0, 0), (0, HEAD_PAD - QK_NOPE - QK_ROPE))).reshape(Q_RANK, QK_W).astype(BF16)
    w_ukv3 = w_ukv[l].reshape(KV_RANK, N_HEADS, QK_NOPE + V_DIM)
    w_k = jnp.pad(w_ukv3[:, :, :QK_NOPE], ((0, 0), (0, 0), (0, HEAD_PAD - QK_NOPE))
                  ).reshape(KV_RANK, QK_W).astype(BF16)
    w_v = w_ukv3[:, :, QK_NOPE:]
    w_ve = jnp.pad(w_v[:, 0::2], ((0, 0), (0, 0), (0, V_DIM))).reshape(KV_RANK, ATTN_W).astype(BF16)
    w_vo = jnp.pad(w_v[:, 1::2], ((0, 0), (0, 0), (V_DIM, 0))).reshape(KV_RANK, ATTN_W).astype(BF16)
    row = lambda a: a.reshape(1, -1)
    return {
        "norm1_g": row(norm1_g[l]), "w_qkv": w_qkv, "w_uv": w_uv, "w_kr": w_kr, "q_norm_g": row(q_norm_g[l]),
        "w_uq": w_uq_r, "kv_norm_g": row(kv_norm_g[l]), "w_k": w_k, "w_ve": w_ve, "w_vo": w_vo,
        "v_norm_g": row(v_norm_g[l]), "w_s": w_s[l].astype(BF16), "b_st": b_s[l].T,
        "out_g_attn": row(out_g_attn[l]), "out_g_gmlp": row(out_g_gmlp[l]), "norm2_g": row(norm2_g[l]),
        "conv_w": conv_w[l], "conv_b": row(conv_b[l]), "final_g": row(final_g),
    }


def kernel(x_prompt, x_sample, cache_ckv, cache_krope, c, c_ctx, ada_w, ada_b, norm1_g, w_in, q_norm_g, w_uq, kv_norm_g, w_ukv, v_norm_g, w_s, b_s, out_g_attn, out_g_gmlp, w_o, norm2_g, w_up, conv_w, conv_b, w_down, final_g):
    batch, seq, _ = x_prompt.shape
    dec_batch, dec_seq, _ = x_sample.shape
    depth, past = cache_ckv.shape[1], cache_ckv.shape[2]
    assert depth == 1 and dec_batch + 1 <= MOD_ROWS
    l = 0
    p = _layer_params(l, norm1_g, w_in, q_norm_g, w_uq, kv_norm_g, w_ukv, v_norm_g, w_s, b_s, out_g_attn,
                      out_g_gmlp, norm2_g, conv_w, conv_b, final_g)

    cond = jnp.concatenate([c_ctx[None, :], c, jnp.zeros((MOD_ROWS - 1 - dec_batch, D_MODEL), F32)], axis=0)
    mods3 = _adaln(cond, ada_w[l], ada_b[l].reshape(1, -1))
    tm_pre, tm_ffn = 512, 512
    ctx_row = lambda i: 0
    lat_row = lambda tm: lambda i: 1 + i // (dec_seq // tm)

    xp = x_prompt.reshape(batch * seq, D_MODEL)
    xs = x_sample.reshape(dec_batch * dec_seq, D_MODEL)
    qp, kp, vep, vop, gmp, ckv_p, kr_p = _pre(xp, mods3, ctx_row, p, None, tm_pre)
    qs, ks, ves, vos, gms, _, _ = _pre(xs, mods3, lat_row(tm_pre), p, _rope_tables(dec_seq), tm_pre)
    kr_pad = jnp.pad(cache_krope[:, l].reshape(dec_batch * past, QK_ROPE),
                     ((0, 0), (ROPE_LANE0, LANES - ROPE_LANE0 - QK_ROPE)))
    cache = _expand(cache_ckv[:, l].reshape(dec_batch * past, KV_RANK), kr_pad, p, 512)

    anp, (w_o_b, w_down_b) = _attention(qp, kp, vep, vop, None, p["out_g_attn"], seq, seq, 4, 4,
                                        [w_o[l], w_down[l]])
    ans, (w_up_b,) = _attention(qs, ks, ves, vos, cache, p["out_g_attn"], dec_seq, 512, 1, 4, [w_up[l]])
    ffn_w = (w_o_b, w_up_b, w_down_b)
    yp = _ffn(xp, anp, gmp, mods3, ctx_row, p, ffn_w, seq, tm_ffn)
    ys = _ffn(xs, ans, gms, mods3, lat_row(tm_ffn), p, ffn_w, dec_seq, tm_ffn)

    return (yp.reshape(batch, seq, D_MODEL), ys.reshape(dec_batch, dec_seq, D_MODEL),
            ckv_p.reshape(batch, 1, seq, KV_RANK), kr_p.reshape(batch, 1, seq, QK_ROPE))
```

```python
import functools

import numpy as np
import jax
import jax.numpy as jnp
from jax import lax
from jax.experimental import pallas as pl
from jax.experimental.pallas import tpu as pltpu

D_MODEL = 1024
GRID_W = 64
CHUNK = 128
N_HEADS = 8
QK_NOPE = 64
QK_ROPE = 32
V_DIM = 64
Q_RANK = 384
KV_RANK = 256
ATTN_W = N_HEADS * V_DIM
GMLP_W = D_MODEL - ATTN_W
GMLP_GROUPS = 4
GMLP_DG = GMLP_W // GMLP_GROUPS
D_FF = 2816
CONV_W = 3
ROPE_THETA = 10000.0
EPS = 1e-6
SM_SCALE = (QK_NOPE + QK_ROPE) ** -0.5
LOG2_E = 1.4426950408889634

LANES = 128
HEAD_PAD = LANES
ROPE_LANE0 = QK_NOPE
QK_W = N_HEADS * HEAD_PAD
N_MODS = 6
MOD_ROWS = 8
IN_QKV = Q_RANK + KV_RANK
FF_CHUNK = 256
HALO = 16
N_FF_CHUNKS = D_FF // FF_CHUNK
VMEM_LIMIT = 56 * 1024 * 1024

F32 = jnp.float32
BF16 = jnp.bfloat16


def _dot(a, b):
    return jnp.dot(a, b, preferred_element_type=F32)


def _dot_nt(a, b):
    return lax.dot_general(a, b, (((1,), (1,)), ((), ())), preferred_element_type=F32)


def _rms(x, g):
    return x * lax.rsqrt(jnp.mean(x * x, axis=-1, keepdims=True) + EPS) * g


def _gelu(x):
    return 0.5 * x * (1.0 + lax.erf(x * (0.5 ** 0.5)))


def _silu(x):
    return x * (1.0 / (1.0 + jnp.exp(-x)))


def _params(n_axes):
    return pltpu.CompilerParams(dimension_semantics=("arbitrary",) * n_axes,
                                vmem_limit_bytes=VMEM_LIMIT)


def _adaln_kernel(cond_ref, w_ref, b_ref, out_ref):
    s = _silu(cond_ref[...]).astype(BF16)
    out_ref[:, 0, :] = _dot(s, w_ref[...].astype(BF16)) + b_ref[...]


def _adaln(cond, ada_w, ada_b):
    return pl.pallas_call(
        _adaln_kernel,
        grid=(N_MODS,),
        in_specs=[
            pl.BlockSpec((MOD_ROWS, D_MODEL), lambda j: (0, 0)),
            pl.BlockSpec((D_MODEL, D_MODEL), lambda j: (0, j)),
            pl.BlockSpec((1, D_MODEL), lambda j: (0, j)),
        ],
        out_specs=pl.BlockSpec((MOD_ROWS, 1, D_MODEL), lambda j: (0, 0, j)),
        out_shape=jax.ShapeDtypeStruct((MOD_ROWS, 1, N_MODS * D_MODEL), F32),
        compiler_params=_params(1),
        name="adaln",
    )(cond, ada_w, ada_b)


def _pre_kernel(*refs, rope, tm):
    if rope:
        (x_ref, mod_ref, n1g_ref, wqkv_ref, wuv_ref, qng_ref, wuq_ref, kvg_ref, wk_ref, wve_ref, wvo_ref,
         vng_ref, ws_ref, bst_ref, ogg_ref, cos_ref, sin_ref,
         q_out, k_out, ve_out, vo_out, gm_out, ckv_out, kr_out, go_sc) = refs
    else:
        (x_ref, mod_ref, n1g_ref, wqkv_ref, wuv_ref, qng_ref, wuq_ref, kvg_ref, wk_ref, wve_ref, wvo_ref,
         vng_ref, ws_ref, bst_ref, ogg_ref,
         q_out, k_out, ve_out, vo_out, gm_out, ckv_out, kr_out, go_sc) = refs
    shift1 = mod_ref[:, 0:D_MODEL]
    scale1 = mod_ref[:, D_MODEL:2 * D_MODEL]
    h = (_rms(x_ref[...], n1g_ref[...]) * (1.0 + scale1) + shift1).astype(BF16)

    lane = lax.broadcasted_iota(jnp.int32, (1, LANES), 1)
    rope_lanes = lane >= ROPE_LANE0
    first_half = (lane % 16) < 8

    def rotate(v, cos):
        swapped = jnp.where(first_half, pltpu.roll(v, LANES - 8, 1), pltpu.roll(v, 8, 1))
        return v * cos + swapped * sin_ref[...]

    qkv = _dot(h, wqkv_ref[...])
    qn = _rms(qkv[:, 0:Q_RANK], qng_ref[...]).astype(BF16)
    ckv = _rms(qkv[:, Q_RANK:IN_QKV], kvg_ref[...])
    ckv_out[...] = ckv
    ckv_b = ckv.astype(BF16)
    q = _dot(qn, wuq_ref[...]) * (SM_SCALE * LOG2_E)
    kr = qkv[:, IN_QKV:IN_QKV + LANES]
    kr_out[...] = kr[:, 0:QK_ROPE]
    if rope:
        cos_q = cos_ref[...]
        kr_k = rotate(kr, jnp.where(rope_lanes, cos_q, 0.0))
    else:
        kr_k = jnp.where(rope_lanes, kr, 0.0)
    kn = _dot(ckv_b, wk_ref[...])
    for hd in range(N_HEADS):
        sl = slice(hd * HEAD_PAD, (hd + 1) * HEAD_PAD)
        qh = q[:, sl]
        if rope:
            qh = rotate(qh, cos_q)
        q_out[:, sl] = qh.astype(BF16)
        k_out[:, sl] = (kn[:, sl] + kr_k).astype(BF16)
    ve_out[...] = _dot(ckv_b, wve_ref[...]).astype(BF16)
    vo_out[...] = _dot(ckv_b, wvo_ref[...]).astype(BF16)

    uv = _dot(h, wuv_ref[...])
    gu = _gelu(uv[:, 0:GMLP_W])
    gv = _gelu(uv[:, GMLP_W:2 * GMLP_W])
    for g in range(GMLP_GROUPS):
        sl = slice(g * GMLP_DG, (g + 1) * GMLP_DG)
        vg = gv[:, sl]
        vg = (vg * lax.rsqrt(jnp.mean(vg * vg, axis=-1, keepdims=True) + EPS) * vng_ref[:, sl]).astype(BF16)
        bias = bst_ref[:, g:g + 1]
        chunks = [vg[n * CHUNK:(n + 1) * CHUNK, :] for n in range(tm // CHUNK)]
        s_all = _dot(ws_ref[g], jnp.concatenate(chunks, axis=1))
        for n in range(tm // CHUNK):
            rows = slice(n * CHUNK, (n + 1) * CHUNK)
            go_sc[rows, sl] = gu[rows, sl] * (s_all[:, n * GMLP_DG:(n + 1) * GMLP_DG] + bias)
    gm_out[...] = _rms(go_sc[...], ogg_ref[...]).astype(BF16)


def _pre(x, mods3, mod_row, p, rope_tabs, tm):
    n = x.shape[0]
    rope = rope_tabs is not None
    const = lambda i: (0, 0)
    tile = lambda i: (i, 0)
    in_specs = [
        pl.BlockSpec((tm, D_MODEL), tile),
        pl.BlockSpec((None, 1, N_MODS * D_MODEL), lambda i: (mod_row(i), 0, 0)),
        pl.BlockSpec((1, D_MODEL), const),
        pl.BlockSpec((D_MODEL, IN_QKV + LANES), const),
        pl.BlockSpec((D_MODEL, 2 * GMLP_W), const),
        pl.BlockSpec((1, Q_RANK), const),
        pl.BlockSpec((Q_RANK, QK_W), const),
        pl.BlockSpec((1, KV_RANK), const),
        pl.BlockSpec((KV_RANK, QK_W), const),
        pl.BlockSpec((KV_RANK, ATTN_W), const),
        pl.BlockSpec((KV_RANK, ATTN_W), const),
        pl.BlockSpec((1, GMLP_W), const),
        pl.BlockSpec((GMLP_GROUPS, CHUNK, CHUNK), lambda i: (0, 0, 0)),
        pl.BlockSpec((CHUNK, GMLP_GROUPS), const),
        pl.BlockSpec((1, GMLP_W), const),
    ]
    args = [x, mods3, p["norm1_g"], p["w_qkv"], p["w_uv"], p["q_norm_g"], p["w_uq"],
            p["kv_norm_g"], p["w_k"], p["w_ve"], p["w_vo"], p["v_norm_g"], p["w_s"], p["b_st"], p["out_g_gmlp"]]
    if rope:
        cos_q, sin_q = rope_tabs
        per_seq = cos_q.shape[0] // tm
        in_specs += [pl.BlockSpec((tm, LANES), lambda i: (i % per_seq, 0))] * 2
        args += [cos_q, sin_q]
    widths = [(QK_W, BF16), (QK_W, BF16), (ATTN_W, BF16), (ATTN_W, BF16), (GMLP_W, BF16),
              (KV_RANK, F32), (QK_ROPE, F32)]
    return pl.pallas_call(
        functools.partial(_pre_kernel, rope=rope, tm=tm),
        grid=(n // tm,),
        in_specs=in_specs,
        out_specs=[pl.BlockSpec((tm, w), tile) for w, _ in widths],
        out_shape=[jax.ShapeDtypeStruct((n, w), dt) for w, dt in widths],
        scratch_shapes=[pltpu.VMEM((tm, GMLP_W), F32)],
        compiler_params=_params(1),
        name="pre_rope" if rope else "pre",
    )(*args)


def _expand_kernel(ckv_ref, kr_ref, wk_ref, wve_ref, wvo_ref, k_out, ve_out, vo_out):
    ckv_b = ckv_ref[...].astype(BF16)
    kn = _dot(ckv_b, wk_ref[...])
    kr = kr_ref[...]
    for hd in range(N_HEADS):
        sl = slice(hd * HEAD_PAD, (hd + 1) * HEAD_PAD)
        k_out[:, sl] = (kn[:, sl] + kr).astype(BF16)
    ve_out[...] = _dot(ckv_b, wve_ref[...]).astype(BF16)
    vo_out[...] = _dot(ckv_b, wvo_ref[...]).astype(BF16)


def _expand(ckv, kr_pad, p, tm):
    n = ckv.shape[0]
    const = lambda i: (0, 0)
    tile = lambda i: (i, 0)
    widths = [QK_W, ATTN_W, ATTN_W]
    return pl.pallas_call(
        _expand_kernel,
        grid=(n // tm,),
        in_specs=[
            pl.BlockSpec((tm, KV_RANK), tile),
            pl.BlockSpec((tm, LANES), tile),
            pl.BlockSpec((KV_RANK, QK_W), const),
            pl.BlockSpec((KV_RANK, ATTN_W), const),
            pl.BlockSpec((KV_RANK, ATTN_W), const),
        ],
        out_specs=[pl.BlockSpec((tm, w), tile) for w in widths],
        out_shape=[jax.ShapeDtypeStruct((n, w), BF16) for w in widths],
        compiler_params=_params(1),
        name="expand",
    )(ckv, kr_pad, p["w_k"], p["w_ve"], p["w_vo"])


def _attn_kernel(*refs, cached, group, seq, tq, pps, n_casts):
    n_main = 8 if cached else 5
    cast_in = refs[n_main:n_main + n_casts]
    out_ref = refs[n_main + n_casts]
    cast_out = refs[n_main + n_casts + 1:n_main + 2 * n_casts + 1]
    o_sc = refs[-1]
    if cached:
        q_ref, k_ref, ve_ref, vo_ref, kc_ref, vec_ref, voc_ref, oga_ref = refs[:n_main]
    else:
        q_ref, k_ref, ve_ref, vo_ref, oga_ref = refs[:n_main]
    for src, dst in zip(cast_in, cast_out):
        dst[...] = src[...].astype(BF16)
    step = pl.program_id(2)
    ones = jnp.ones((seq, LANES), BF16)
    ones_c = jnp.ones((kc_ref.shape[0], LANES), BF16) if cached else None
    for g in range(group):
        qrows = slice(g * tq, (g + 1) * tq)
        krows = slice(g * seq, (g + 1) * seq)
        for pp in range(pps):
            psl = slice(pp * LANES, (pp + 1) * LANES)
            acc = None
            for par, (v_ref, vc_ref) in enumerate(((ve_ref, vec_ref if cached else None),
                                                   (vo_ref, voc_ref if cached else None))):
                hd = 2 * pp + par
                hsl = slice(hd * HEAD_PAD, (hd + 1) * HEAD_PAD)
                qh = q_ref[qrows, hsl]
                s = _dot_nt(qh, k_ref[krows, hsl])
                m = jnp.max(s, axis=-1, keepdims=True)
                if cached:
                    sc = _dot_nt(qh, kc_ref[:, hsl])
                    m = jnp.maximum(m, jnp.max(sc, axis=-1, keepdims=True))
                e = jnp.exp2(s - m).astype(BF16)
                o = _dot(e, jnp.concatenate([v_ref[krows, psl], ones], axis=1))
                if cached:
                    ec = jnp.exp2(sc - m).astype(BF16)
                    o = o + _dot(ec, jnp.concatenate([vc_ref[:, psl], ones_c], axis=1))
                o = o[:, 0:LANES] * (1.0 / o[:, LANES:2 * LANES])
                acc = o if acc is None else acc + o
            o_sc[step * pps + pp, qrows, :] = acc

    @pl.when(step == N_HEADS // 2 // pps - 1)
    def _():
        blocks = [o_sc[j] for j in range(N_HEADS // 2)]
        ssq = sum(jnp.sum(b * b, axis=-1, keepdims=True) for b in blocks)
        r = lax.rsqrt(ssq * (1.0 / ATTN_W) + EPS)
        for j, b in enumerate(blocks):
            sl = slice(j * LANES, (j + 1) * LANES)
            out_ref[:, sl] = (b * r * oga_ref[:, sl]).astype(BF16)


def _attention(q, k, ve, vo, cache, oga, seq, tq, group, pps, casts):
    n = q.shape[0]
    nq = seq // tq
    n_pairs = N_HEADS // 2
    grid = (n // (group * seq), nq, n_pairs // pps)
    n_steps = grid[0] * grid[1] * grid[2]
    linear = lambda b, i, j: ((b * grid[1] + i) * grid[2] + j, 0)
    cast_specs = [pl.BlockSpec((w.shape[0] // n_steps, w.shape[1]), linear) for w in casts]
    qtile = lambda b, i, j: (b * nq + i, j)
    kv = lambda b, i, j: (b, j)
    qk_w, v_w = pps * 2 * HEAD_PAD, pps * LANES
    in_specs = [
        pl.BlockSpec((group * tq, qk_w), qtile),
        pl.BlockSpec((group * seq, qk_w), kv),
        pl.BlockSpec((group * seq, v_w), kv),
        pl.BlockSpec((group * seq, v_w), kv),
    ]
    args = [q, k, ve, vo]
    if cache is not None:
        kc, vec, voc = cache
        past = kc.shape[0] // (n // seq)
        in_specs += [pl.BlockSpec((past, qk_w), kv), pl.BlockSpec((past, v_w), kv),
                     pl.BlockSpec((past, v_w), kv)]
        args += [kc, vec, voc]
    in_specs.append(pl.BlockSpec((1, ATTN_W), lambda b, i, j: (0, 0)))
    args.append(oga)
    outs = pl.pallas_call(
        functools.partial(_attn_kernel, cached=cache is not None, group=group, seq=seq, tq=tq, pps=pps,
                          n_casts=len(casts)),
        grid=grid,
        in_specs=in_specs + cast_specs,
        out_specs=[pl.BlockSpec((group * tq, ATTN_W), lambda b, i, j: (b * nq + i, 0))] + cast_specs,
        out_shape=[jax.ShapeDtypeStruct((n, ATTN_W), BF16)]
        + [jax.ShapeDtypeStruct(w.shape, BF16) for w in casts],
        scratch_shapes=[pltpu.VMEM((n_pairs, group * tq, LANES), F32)],
        compiler_params=_params(3),
        name="attn_cached" if cache is not None else "attn",
    )(*args, *casts)
    return outs[0], outs[1:]


def _ffn_kernel(*refs, nseg, seg, tiles_per_seq):
    halo = tiles_per_seq > 1
    if halo:
        (x_ref, an_ref, gm_ref, xp_ref, xn_ref, anp_ref, ann_ref, gmp_ref, gmn_ref, mod_ref, wo_ref, n2g_ref,
         wup_ref, cw_ref, cb_ref, wd_ref, fg_ref, out_ref, lhs_sc, a_sc, g_sc, mix_sc) = refs
    else:
        (x_ref, an_ref, gm_ref, mod_ref, wo_ref, n2g_ref,
         wup_ref, cw_ref, cb_ref, wd_ref, fg_ref, out_ref, lhs_sc, a_sc, g_sc) = refs
    stride = seg + HALO
    rows = HALO + nseg * stride
    gate1 = mod_ref[:, 2 * D_MODEL:3 * D_MODEL]
    shift2 = mod_ref[:, 3 * D_MODEL:4 * D_MODEL]
    scale2 = mod_ref[:, 4 * D_MODEL:5 * D_MODEL]

    def residual_norm(x, y):
        x1 = x + gate1 * y
        return x1, (_rms(x1, n2g_ref[...]) * (1.0 + scale2) + shift2).astype(BF16)

    zeros = jnp.zeros((HALO, D_MODEL), BF16)
    if halo:
        mix_sc[0:HALO, 0:ATTN_W] = anp_ref[...]
        mix_sc[0:HALO, ATTN_W:D_MODEL] = gmp_ref[...]
        mix_sc[HALO:HALO + seg, 0:ATTN_W] = an_ref[...]
        mix_sc[HALO:HALO + seg, ATTN_W:D_MODEL] = gm_ref[...]
        mix_sc[HALO + seg:rows, 0:ATTN_W] = ann_ref[...]
        mix_sc[HALO + seg:rows, ATTN_W:D_MODEL] = gmn_ref[...]
        y = _dot(mix_sc[...], wo_ref[...])
        x1, h2 = residual_norm(x_ref[...], y[HALO:HALO + seg])
        out_ref[...] = x1
        lhs_sc[HALO:HALO + seg, :] = h2
        pos = pl.program_id(0) % tiles_per_seq
        lhs_sc[0:HALO, :] = zeros
        lhs_sc[HALO + seg:rows, :] = zeros

        @pl.when(pos != 0)
        def _():
            lhs_sc[0:HALO, :] = residual_norm(xp_ref[...], y[0:HALO])[1]

        @pl.when(pos != tiles_per_seq - 1)
        def _():
            lhs_sc[HALO + seg:rows, :] = residual_norm(xn_ref[...], y[HALO + seg:rows])[1]
    else:
        y = _dot(an_ref[...], wo_ref[0:ATTN_W, :]) + _dot(gm_ref[...], wo_ref[ATTN_W:D_MODEL, :])
        x1, h2 = residual_norm(x_ref[...], y)
        out_ref[...] = x1
        lhs_sc[0:HALO, :] = zeros
        for s in range(nseg):
            base = HALO + s * stride
            lhs_sc[base:base + seg, :] = h2[s * seg:(s + 1) * seg, :]
            lhs_sc[base + seg:base + stride, :] = zeros

    for c in range(N_FF_CHUNKS):
        slot = c % 2
        for k, col0 in enumerate((c * FF_CHUNK, D_FF + c * FF_CHUNK)):
            a = _dot(lhs_sc[...], wup_ref[:, col0:col0 + FF_CHUNK])
            for j in range(FF_CHUNK // LANES):
                a_sc[slot, 2 * k + j, :, :] = a[8:rows - 8, j * LANES:(j + 1) * LANES]
        for j in range(FF_CHUNK // LANES):
            gl = slice(c * FF_CHUNK + j * LANES, c * FF_CHUNK + (j + 1) * LANES)
            vl = slice(D_FF + gl.start, D_FF + gl.stop)
            for s in range(nseg):
                base = 8 + s * stride

                def conv(k, lanes):
                    acc = cb_ref[:, lanes]
                    for t in range(CONV_W):
                        acc = acc + a_sc[slot, k, base - 1 + t:base - 1 + t + seg, :] * cw_ref[t:t + 1, lanes]
                    return acc

                g = _silu(conv(j, gl)) * conv(2 + j, vl)
                g_sc[s * seg:(s + 1) * seg, gl] = g.astype(BF16)

    x2 = out_ref[...] + mod_ref[:, 5 * D_MODEL:6 * D_MODEL] * _dot(g_sc[...], wd_ref[...])
    out_ref[...] = _rms(x2, fg_ref[...])


def _ffn(x, an, gm, mods3, mod_row, p, ffn_w, seq, tm):
    n = x.shape[0]
    tiles_per_seq = max(seq // tm, 1)
    seg = min(seq, tm)
    nseg = tm // seg
    rows = HALO + nseg * (seg + HALO)
    const = lambda i: (0, 0)
    tile = lambda i: (i, 0)
    resident = dict(pipeline_mode=pl.Buffered(1))
    in_specs = [pl.BlockSpec((tm, D_MODEL), tile), pl.BlockSpec((tm, ATTN_W), tile),
                pl.BlockSpec((tm, GMLP_W), tile)]
    args = [x, an, gm]
    scratch = [pltpu.VMEM((rows, D_MODEL), BF16),
               pltpu.VMEM((2, 2 * FF_CHUNK // LANES, rows - 16, LANES), F32),
               pltpu.VMEM((tm, D_FF), BF16)]
    if tiles_per_seq > 1:
        per = tm // HALO
        last = n // HALO - 1
        prev = lambda i: (jnp.maximum(i * per - 1, 0), 0)
        nxt = lambda i: (jnp.minimum((i + 1) * per, last), 0)
        for arr, width in ((x, D_MODEL), (an, ATTN_W), (gm, GMLP_W)):
            in_specs += [pl.BlockSpec((HALO, width), prev), pl.BlockSpec((HALO, width), nxt)]
            args += [arr, arr]
        scratch.append(pltpu.VMEM((rows, D_MODEL), BF16))
    in_specs += [
        pl.BlockSpec((None, 1, N_MODS * D_MODEL), lambda i: (mod_row(i), 0, 0)),
        pl.BlockSpec((D_MODEL, D_MODEL), const, **resident),
        pl.BlockSpec((1, D_MODEL), const),
        pl.BlockSpec((D_MODEL, 2 * D_FF), const, **resident),
        pl.BlockSpec((CONV_W, 2 * D_FF), const),
        pl.BlockSpec((1, 2 * D_FF), const),
        pl.BlockSpec((D_FF, D_MODEL), const, **resident),
        pl.BlockSpec((1, D_MODEL), const),
    ]
    w_o, w_up, w_down = ffn_w
    args += [mods3, w_o, p["norm2_g"], w_up, p["conv_w"], p["conv_b"], w_down, p["final_g"]]
    return pl.pallas_call(
        functools.partial(_ffn_kernel, nseg=nseg, seg=seg, tiles_per_seq=tiles_per_seq),
        grid=(n // tm,),
        in_specs=in_specs,
        out_specs=pl.BlockSpec((tm, D_MODEL), tile),
        out_shape=jax.ShapeDtypeStruct((n, D_MODEL), F32),
        scratch_shapes=scratch,
        compiler_params=_params(1),
        name="ffn_halo" if tiles_per_seq > 1 else "ffn",
    )(*args)


def _rope_tables(length):
    pos = np.arange(length)
    row = (pos // GRID_W).astype(np.float32)
    col = (pos % GRID_W).astype(np.float32)
    n_freq = QK_ROPE // 4
    inv = (np.float32(ROPE_THETA) ** (-(np.arange(n_freq, dtype=np.float32) / np.float32(n_freq)))).astype(np.float32)
    ang_r, ang_c = row[:, None] * inv, col[:, None] * inv
    cos32 = np.concatenate([np.cos(ang_r)] * 2 + [np.cos(ang_c)] * 2, axis=-1)
    sin32 = np.concatenate([-np.sin(ang_r), np.sin(ang_r), -np.sin(ang_c), np.sin(ang_c)], axis=-1)
    tail = LANES - ROPE_LANE0 - QK_ROPE
    cos_q = np.concatenate([np.ones((length, ROPE_LANE0)), cos32, np.zeros((length, tail))], -1)
    sin_q = np.concatenate([np.zeros((length, ROPE_LANE0)), sin32, np.zeros((length, tail))], -1)
    return jnp.asarray(cos_q, F32), jnp.asarray(sin_q, F32)


def _layer_params(l, norm1_g, w_in, q_norm_g, w_uq, kv_norm_g, w_ukv, v_norm_g, w_s, b_s, out_g_attn,
                  out_g_gmlp, norm2_g, conv_w, conv_b, final_g):
    wi = w_in[l]
    kr = wi[:, IN_QKV:IN_QKV + QK_ROPE].astype(BF16)
    z32 = jnp.zeros((D_MODEL, QK_ROPE), BF16)
    w_qkv = jnp.concatenate([wi[:, :IN_QKV].astype(BF16), kr, z32, kr, z32], axis=1)
    w_uv = wi[:, IN_QKV + QK_ROPE:].astype(BF16)
    w_uq_r = jnp.pad(w_uq[l].reshape(Q_RANK, N_HEADS, QK_NOPE + QK_ROPE),
                     ((0, 0), (0, 0), (0, HEAD_PAD - QK_NOPE - QK_ROPE))).reshape(Q_RANK, QK_W).astype(BF16)
    w_ukv3 = w_ukv[l].reshape(KV_RANK, N_HEADS, QK_NOPE + V_DIM)
    w_k = jnp.pad(w_ukv3[:, :, :QK_NOPE], ((0, 0), (0, 0), (0, HEAD_PAD - QK_NOPE))
                  ).reshape(KV_RANK, QK_W).astype(BF16)
    w_v = w_ukv3[:, :, QK_NOPE:]
    w_ve = jnp.pad(w_v[:, 0::2], ((0, 0), (0, 0), (0, V_DIM))).reshape(KV_RANK, ATTN_W).astype(BF16)
    w_vo = jnp.pad(w_v[:, 1::2], ((0, 0), (0, 0), (V_DIM, 0))).reshape(KV_RANK, ATTN_W).astype(BF16)
    row = lambda a: a.reshape(1, -1)
    return {
        "norm1_g": row(norm1_g[l]), "w_qkv": w_qkv, "w_uv": w_uv, "q_norm_g": row(q_norm_g[l]),
        "w_uq": w_uq_r, "kv_norm_g": row(kv_norm_g[l]), "w_k": w_k, "w_ve": w_ve, "w_vo": w_vo,
        "v_norm_g": row(v_norm_g[l]), "w_s": w_s[l].astype(BF16), "b_st": b_s[l].T,
        "out_g_attn": row(out_g_attn[l]), "out_g_gmlp": row(out_g_gmlp[l]), "norm2_g": row(norm2_g[l]),
        "conv_w": conv_w[l], "conv_b": row(conv_b[l]), "final_g": row(final_g),
    }


def kernel(x_prompt, x_sample, cache_ckv, cache_krope, c, c_ctx, ada_w, ada_b, norm1_g, w_in, q_norm_g, w_uq, kv_norm_g, w_ukv, v_norm_g, w_s, b_s, out_g_attn, out_g_gmlp, w_o, norm2_g, w_up, conv_w, conv_b, w_down, final_g):
    batch, seq, _ = x_prompt.shape
    dec_batch, dec_seq, _ = x_sample.shape
    depth, past = cache_ckv.shape[1], cache_ckv.shape[2]
    assert depth == 1 and dec_batch + 1 <= MOD_ROWS
    l = 0
    p = _layer_params(l, norm1_g, w_in, q_norm_g, w_uq, kv_norm_g, w_ukv, v_norm_g, w_s, b_s, out_g_attn,
                      out_g_gmlp, norm2_g, conv_w, conv_b, final_g)

    cond = jnp.concatenate([c_ctx[None, :], c, jnp.zeros((MOD_ROWS - 1 - dec_batch, D_MODEL), F32)], axis=0)
    mods3 = _adaln(cond, ada_w[l], ada_b[l].reshape(1, -1))
    tm_pre, tm_ffn = 512, 512
    ctx_row = lambda i: 0
    lat_row = lambda tm: lambda i: 1 + i // (dec_seq // tm)

    xp = x_prompt.reshape(batch * seq, D_MODEL)
    xs = x_sample.reshape(dec_batch * dec_seq, D_MODEL)
    qp, kp, vep, vop, gmp, ckv_p, kr_p = _pre(xp, mods3, ctx_row, p, None, tm_pre)
    qs, ks, ves, vos, gms, _, _ = _pre(xs, mods3, lat_row(tm_pre), p, _rope_tables(dec_seq), tm_pre)
    kr_pad = jnp.pad(cache_krope[:, l].reshape(dec_batch * past, QK_ROPE),
                     ((0, 0), (ROPE_LANE0, LANES - ROPE_LANE0 - QK_ROPE)))
    cache = _expand(cache_ckv[:, l].reshape(dec_batch * past, KV_RANK), kr_pad, p, 512)

    anp, (w_o_b, w_down_b) = _attention(qp, kp, vep, vop, None, p["out_g_attn"], seq, seq, 4, 4,
                                        [w_o[l], w_down[l]])
    ans, (w_up_b,) = _attention(qs, ks, ves, vos, cache, p["out_g_attn"], dec_seq, 512, 1, 4, [w_up[l]])
    ffn_w = (w_o_b, w_up_b, w_down_b)
    yp = _ffn(xp, anp, gmp, mods3, ctx_row, p, ffn_w, seq, tm_ffn)
    ys = _ffn(xs, ans, gms, mods3, lat_row(tm_ffn), p, ffn_w, dec_seq, tm_ffn)

    return (yp.reshape(batch, seq, D_MODEL), ys.reshape(dec_batch, dec_seq, D_MODEL),
            ckv_p.reshape(batch, 1, seq, KV_RANK), kr_p.reshape(batch, 1, seq, QK_ROPE))
```

```python
import functools

import numpy as np
import jax
import jax.numpy as jnp
from jax import lax
from jax.experimental import pallas as pl
from jax.experimental.pallas import tpu as pltpu

D_MODEL = 1024
GRID_W = 64
CHUNK = 128
N_HEADS = 8
QK_NOPE = 64
QK_ROPE = 32
V_DIM = 64
Q_RANK = 384
KV_RANK = 256
ATTN_W = N_HEADS * V_DIM
GMLP_W = D_MODEL - ATTN_W
GMLP_GROUPS = 4
GMLP_DG = GMLP_W // GMLP_GROUPS
D_FF = 2816
CONV_W = 3
ROPE_THETA = 10000.0
EPS = 1e-6
SM_SCALE = (QK_NOPE + QK_ROPE) ** -0.5
LOG2_E = 1.4426950408889634

LANES = 128
HEAD_PAD = LANES
ROPE_LANE0 = QK_NOPE
QK_W = N_HEADS * HEAD_PAD
N_MODS = 6
MOD_ROWS = 8
ADALN_STEPS = 8
IN_QKV = Q_RANK + KV_RANK
FF_CHUNK = 256
HALO = 16
N_FF_CHUNKS = D_FF // FF_CHUNK
VMEM_LIMIT = 56 * 1024 * 1024

F32 = jnp.float32
BF16 = jnp.bfloat16


def _dot(a, b):
    return jnp.dot(a, b, preferred_element_type=F32)


def _dot_nt(a, b):
    return lax.dot_general(a, b, (((1,), (1,)), ((), ())), preferred_element_type=F32)


def _rms(x, g):
    return x * lax.rsqrt(jnp.mean(x * x, axis=-1, keepdims=True) + EPS) * g


def _gelu(x):
    return 0.5 * x * (1.0 + lax.erf(x * (0.5 ** 0.5)))


def _silu(x):
    return x * (1.0 / (1.0 + jnp.exp(-x)))


def _params(n_axes):
    return pltpu.CompilerParams(dimension_semantics=("arbitrary",) * n_axes,
                                vmem_limit_bytes=VMEM_LIMIT)


def _adaln_kernel(cond_ref, w_ref, b_ref, win_ref, wuq_ref, wukv_ref, ws_ref,
                  out_ref, wqkv_out, wuv_out, wuq_out, wk_out, wve_out, wvo_out, ws_out):
    s = _silu(cond_ref[...]).astype(BF16)
    out_ref[:, 0, :] = _dot(s, w_ref[...].astype(BF16)) + b_ref[...]
    w = win_ref[...]
    kr = w[:, IN_QKV:IN_QKV + QK_ROPE]
    z = jnp.zeros_like(kr)
    wqkv_out[...] = jnp.concatenate([w[:, 0:IN_QKV], kr, z, kr, z], axis=1).astype(BF16)
    wuv_out[...] = w[:, IN_QKV + QK_ROPE:].astype(BF16)
    w = wuq_ref[...]
    hw = QK_NOPE + QK_ROPE
    zq = jnp.zeros((w.shape[0], HEAD_PAD - hw), F32)
    wuq_out[...] = jnp.concatenate([blk for hd in range(N_HEADS) for blk in (w[:, hd * hw:(hd + 1) * hw], zq)],
                                   axis=1).astype(BF16)
    w = wukv_ref[...]
    zv = jnp.zeros((w.shape[0], V_DIM), F32)
    v_of = lambda hd: w[:, hd * HEAD_PAD + QK_NOPE:(hd + 1) * HEAD_PAD]
    wk_out[...] = jnp.concatenate([blk for hd in range(N_HEADS)
                                   for blk in (w[:, hd * HEAD_PAD:hd * HEAD_PAD + QK_NOPE], zv)], axis=1).astype(BF16)
    wve_out[...] = jnp.concatenate([blk for hd in range(0, N_HEADS, 2) for blk in (v_of(hd), zv)],
                                   axis=1).astype(BF16)
    wvo_out[...] = jnp.concatenate([blk for hd in range(1, N_HEADS, 2) for blk in (zv, v_of(hd))],
                                   axis=1).astype(BF16)
    ws_out[...] = ws_ref[...].astype(BF16)


def _adaln(cond, ada_w, ada_b, w_in, w_uq, w_ukv, w_s):
    cols = N_MODS * D_MODEL // ADALN_STEPS
    row_block = lambda a, width=None: pl.BlockSpec((a.shape[0] // ADALN_STEPS, width or a.shape[1]),
                                                   lambda j: (j, 0))
    return pl.pallas_call(
        _adaln_kernel,
        grid=(ADALN_STEPS,),
        in_specs=[
            pl.BlockSpec((MOD_ROWS, D_MODEL), lambda j: (0, 0)),
            pl.BlockSpec((D_MODEL, cols), lambda j: (0, j)),
            pl.BlockSpec((1, cols), lambda j: (0, j)),
            row_block(w_in), row_block(w_uq), row_block(w_ukv), row_block(w_s),
        ],
        out_specs=[pl.BlockSpec((MOD_ROWS, 1, cols), lambda j: (0, 0, j)),
                   row_block(w_in, IN_QKV + LANES), row_block(w_in, 2 * GMLP_W),
                   row_block(w_uq, QK_W), row_block(w_ukv, QK_W), row_block(w_ukv, ATTN_W),
                   row_block(w_ukv, ATTN_W), row_block(w_s)],
        out_shape=[jax.ShapeDtypeStruct((MOD_ROWS, 1, N_MODS * D_MODEL), F32),
                   jax.ShapeDtypeStruct((D_MODEL, IN_QKV + LANES), BF16),
                   jax.ShapeDtypeStruct((D_MODEL, 2 * GMLP_W), BF16),
                   jax.ShapeDtypeStruct((Q_RANK, QK_W), BF16),
                   jax.ShapeDtypeStruct((KV_RANK, QK_W), BF16),
                   jax.ShapeDtypeStruct((KV_RANK, ATTN_W), BF16),
                   jax.ShapeDtypeStruct((KV_RANK, ATTN_W), BF16),
                   jax.ShapeDtypeStruct(w_s.shape, BF16)],
        compiler_params=_params(1),
        name="adaln",
    )(cond, ada_w, ada_b, w_in, w_uq, w_ukv, w_s)


def _pre_kernel(*refs, rope, tm, n_casts):
    n_main = 17 if rope else 15
    cast_in = refs[n_main:n_main + n_casts]
    cast_out = refs[n_main + n_casts + 7:n_main + 2 * n_casts + 7]
    q_out, k_out, ve_out, vo_out, gm_out, ckv_out, kr_out = refs[n_main + n_casts:n_main + n_casts + 7]
    go_sc = refs[-1]
    (x_ref, mod_ref, n1g_ref, wqkv_ref, wuv_ref, qng_ref, wuq_ref, kvg_ref, wk_ref, wve_ref, wvo_ref,
     vng_ref, ws_ref, bst_ref, ogg_ref) = refs[:15]
    if rope:
        cos_ref, sin_ref = refs[15:17]
    for src, dst in zip(cast_in, cast_out):
        dst[...] = src[...].astype(BF16)
    shift1 = mod_ref[:, 0:D_MODEL]
    scale1 = mod_ref[:, D_MODEL:2 * D_MODEL]
    h = (_rms(x_ref[...], n1g_ref[...]) * (1.0 + scale1) + shift1).astype(BF16)

    lane = lax.broadcasted_iota(jnp.int32, (1, LANES), 1)
    rope_lanes = lane >= ROPE_LANE0
    first_half = (lane % 16) < 8

    def rotate(v, cos):
        swapped = jnp.where(first_half, pltpu.roll(v, LANES - 8, 1), pltpu.roll(v, 8, 1))
        return v * cos + swapped * sin_ref[...]

    qkv = _dot(h, wqkv_ref[...])
    qn = _rms(qkv[:, 0:Q_RANK], qng_ref[...]).astype(BF16)
    ckv = _rms(qkv[:, Q_RANK:IN_QKV], kvg_ref[...])
    ckv_out[...] = ckv
    ckv_b = ckv.astype(BF16)
    q = _dot(qn, wuq_ref[...]) * (SM_SCALE * LOG2_E)
    kr = qkv[:, IN_QKV:IN_QKV + LANES]
    kr_out[...] = kr[:, 0:QK_ROPE]
    if rope:
        cos_q = cos_ref[...]
        kr_k = rotate(kr, jnp.where(rope_lanes, cos_q, 0.0))
    else:
        kr_k = jnp.where(rope_lanes, kr, 0.0)
    kn = _dot(ckv_b, wk_ref[...])
    for hd in range(N_HEADS):
        sl = slice(hd * HEAD_PAD, (hd + 1) * HEAD_PAD)
        qh = q[:, sl]
        if rope:
            qh = rotate(qh, cos_q)
        q_out[:, sl] = qh.astype(BF16)
        k_out[:, sl] = (kn[:, sl] + kr_k).astype(BF16)
    ve_out[...] = _dot(ckv_b, wve_ref[...]).astype(BF16)
    vo_out[...] = _dot(ckv_b, wvo_ref[...]).astype(BF16)

    uv = _dot(h, wuv_ref[...])
    gu = _gelu(uv[:, 0:GMLP_W])
    gv = _gelu(uv[:, GMLP_W:2 * GMLP_W])
    for g in range(GMLP_GROUPS):
        sl = slice(g * GMLP_DG, (g + 1) * GMLP_DG)
        vg = gv[:, sl]
        vg = (vg * lax.rsqrt(jnp.mean(vg * vg, axis=-1, keepdims=True) + EPS) * vng_ref[:, sl]).astype(BF16)
        bias = bst_ref[:, g:g + 1]
        chunks = [vg[n * CHUNK:(n + 1) * CHUNK, :] for n in range(tm // CHUNK)]
        s_all = _dot(ws_ref[g], jnp.concatenate(chunks, axis=1))
        for n in range(tm // CHUNK):
            rows = slice(n * CHUNK, (n + 1) * CHUNK)
            go_sc[rows, sl] = gu[rows, sl] * (s_all[:, n * GMLP_DG:(n + 1) * GMLP_DG] + bias)
    gm_out[...] = _rms(go_sc[...], ogg_ref[...]).astype(BF16)


def _pre(x, mods3, mod_row, p, rope_tabs, tm, casts):
    n = x.shape[0]
    rope = rope_tabs is not None
    const = lambda i: (0, 0)
    tile = lambda i: (i, 0)
    in_specs = [
        pl.BlockSpec((tm, D_MODEL), tile),
        pl.BlockSpec((None, 1, N_MODS * D_MODEL), lambda i: (mod_row(i), 0, 0)),
        pl.BlockSpec((1, D_MODEL), const),
        pl.BlockSpec((D_MODEL, IN_QKV + LANES), const),
        pl.BlockSpec((D_MODEL, 2 * GMLP_W), const),
        pl.BlockSpec((1, Q_RANK), const),
        pl.BlockSpec((Q_RANK, QK_W), const),
        pl.BlockSpec((1, KV_RANK), const),
        pl.BlockSpec((KV_RANK, QK_W), const),
        pl.BlockSpec((KV_RANK, ATTN_W), const),
        pl.BlockSpec((KV_RANK, ATTN_W), const),
        pl.BlockSpec((1, GMLP_W), const),
        pl.BlockSpec((GMLP_GROUPS, CHUNK, CHUNK), lambda i: (0, 0, 0)),
        pl.BlockSpec((CHUNK, GMLP_GROUPS), const),
        pl.BlockSpec((1, GMLP_W), const),
    ]
    args = [x, mods3, p["norm1_g"], p["w_qkv"], p["w_uv"], p["q_norm_g"], p["w_uq"],
            p["kv_norm_g"], p["w_k"], p["w_ve"], p["w_vo"], p["v_norm_g"], p["w_s"], p["b_st"], p["out_g_gmlp"]]
    if rope:
        cos_q, sin_q = rope_tabs
        per_seq = cos_q.shape[0] // tm
        in_specs += [pl.BlockSpec((tm, LANES), lambda i: (i % per_seq, 0))] * 2
        args += [cos_q, sin_q]
    widths = [(QK_W, BF16), (QK_W, BF16), (ATTN_W, BF16), (ATTN_W, BF16), (GMLP_W, BF16),
              (KV_RANK, F32), (QK_ROPE, F32)]
    cast_specs = [pl.BlockSpec((w.shape[0] // (n // tm), w.shape[1]), tile) for w in casts]
    outs = pl.pallas_call(
        functools.partial(_pre_kernel, rope=rope, tm=tm, n_casts=len(casts)),
        grid=(n // tm,),
        in_specs=in_specs + cast_specs,
        out_specs=[pl.BlockSpec((tm, w), tile) for w, _ in widths] + cast_specs,
        out_shape=[jax.ShapeDtypeStruct((n, w), dt) for w, dt in widths]
        + [jax.ShapeDtypeStruct(w.shape, BF16) for w in casts],
        scratch_shapes=[pltpu.VMEM((tm, GMLP_W), F32)],
        compiler_params=_params(1),
        name="pre_rope" if rope else "pre",
    )(*args, *casts)
    return outs[:7], outs[7:]


def _expand_kernel(ckv_ref, kr_ref, wk_ref, wve_ref, wvo_ref, k_out, ve_out, vo_out):
    ckv_b = ckv_ref[...].astype(BF16)
    kn = _dot(ckv_b, wk_ref[...])
    kr = kr_ref[...]
    for hd in range(N_HEADS):
        sl = slice(hd * HEAD_PAD, (hd + 1) * HEAD_PAD)
        k_out[:, sl] = (kn[:, sl] + kr).astype(BF16)
    ve_out[...] = _dot(ckv_b, wve_ref[...]).astype(BF16)
    vo_out[...] = _dot(ckv_b, wvo_ref[...]).astype(BF16)


def _expand(ckv, kr_pad, p, tm):
    n = ckv.shape[0]
    const = lambda i: (0, 0)
    tile = lambda i: (i, 0)
    widths = [QK_W, ATTN_W, ATTN_W]
    return pl.pallas_call(
        _expand_kernel,
        grid=(n // tm,),
        in_specs=[
            pl.BlockSpec((tm, KV_RANK), tile),
            pl.BlockSpec((tm, LANES), tile),
            pl.BlockSpec((KV_RANK, QK_W), const),
            pl.BlockSpec((KV_RANK, ATTN_W), const),
            pl.BlockSpec((KV_RANK, ATTN_W), const),
        ],
        out_specs=[pl.BlockSpec((tm, w), tile) for w in widths],
        out_shape=[jax.ShapeDtypeStruct((n, w), BF16) for w in widths],
        compiler_params=_params(1),
        name="expand",
    )(ckv, kr_pad, p["w_k"], p["w_ve"], p["w_vo"])


def _attn_kernel(*refs, cached, group, seq, tq, pps, n_casts):
    n_main = 8 if cached else 5
    cast_in = refs[n_main:n_main + n_casts]
    out_ref = refs[n_main + n_casts]
    cast_out = refs[n_main + n_casts + 1:n_main + 2 * n_casts + 1]
    o_sc = refs[-1]
    if cached:
        q_ref, k_ref, ve_ref, vo_ref, kc_ref, vec_ref, voc_ref, oga_ref = refs[:n_main]
    else:
        q_ref, k_ref, ve_ref, vo_ref, oga_ref = refs[:n_main]
    for src, dst in zip(cast_in, cast_out):
        dst[...] = src[...].astype(BF16)
    step = pl.program_id(2)
    ones = jnp.ones((seq, LANES), BF16)
    ones_c = jnp.ones((kc_ref.shape[0], LANES), BF16) if cached else None
    for g in range(group):
        qrows = slice(g * tq, (g + 1) * tq)
        krows = slice(g * seq, (g + 1) * seq)
        for pp in range(pps):
            psl = slice(pp * LANES, (pp + 1) * LANES)
            acc = None
            for par, (v_ref, vc_ref) in enumerate(((ve_ref, vec_ref if cached else None),
                                                   (vo_ref, voc_ref if cached else None))):
                hd = 2 * pp + par
                hsl = slice(hd * HEAD_PAD, (hd + 1) * HEAD_PAD)
                qh = q_ref[qrows, hsl]
                s = _dot_nt(qh, k_ref[krows, hsl])
                m = jnp.max(s, axis=-1, keepdims=True)
                if cached:
                    sc = _dot_nt(qh, kc_ref[:, hsl])
                    m = jnp.maximum(m, jnp.max(sc, axis=-1, keepdims=True))
                e = jnp.exp2(s - m).astype(BF16)
                o = _dot(e, jnp.concatenate([v_ref[krows, psl], ones], axis=1))
                if cached:
                    ec = jnp.exp2(sc - m).astype(BF16)
                    o = o + _dot(ec, jnp.concatenate([vc_ref[:, psl], ones_c], axis=1))
                o = o[:, 0:LANES] * (1.0 / o[:, LANES:2 * LANES])
                acc = o if acc is None else acc + o
            o_sc[step * pps + pp, qrows, :] = acc

    @pl.when(step == N_HEADS // 2 // pps - 1)
    def _():
        blocks = [o_sc[j] for j in range(N_HEADS // 2)]
        ssq = sum(jnp.sum(b * b, axis=-1, keepdims=True) for b in blocks)
        r = lax.rsqrt(ssq * (1.0 / ATTN_W) + EPS)
        for j, b in enumerate(blocks):
            sl = slice(j * LANES, (j + 1) * LANES)
            out_ref[:, sl] = (b * r * oga_ref[:, sl]).astype(BF16)


def _attention(q, k, ve, vo, cache, oga, seq, tq, group, pps, casts):
    n = q.shape[0]
    nq = seq // tq
    n_pairs = N_HEADS // 2
    grid = (n // (group * seq), nq, n_pairs // pps)
    n_steps = grid[0] * grid[1] * grid[2]
    linear = lambda b, i, j: ((b * grid[1] + i) * grid[2] + j, 0)
    cast_specs = [pl.BlockSpec((w.shape[0] // n_steps, w.shape[1]), linear) for w in casts]
    qtile = lambda b, i, j: (b * nq + i, j)
    kv = lambda b, i, j: (b, j)
    qk_w, v_w = pps * 2 * HEAD_PAD, pps * LANES
    in_specs = [
        pl.BlockSpec((group * tq, qk_w), qtile),
        pl.BlockSpec((group * seq, qk_w), kv),
        pl.BlockSpec((group * seq, v_w), kv),
        pl.BlockSpec((group * seq, v_w), kv),
    ]
    args = [q, k, ve, vo]
    if cache is not None:
        kc, vec, voc = cache
        past = kc.shape[0] // (n // seq)
        in_specs += [pl.BlockSpec((past, qk_w), kv), pl.BlockSpec((past, v_w), kv),
                     pl.BlockSpec((past, v_w), kv)]
        args += [kc, vec, voc]
    in_specs.append(pl.BlockSpec((1, ATTN_W), lambda b, i, j: (0, 0)))
    args.append(oga)
    outs = pl.pallas_call(
        functools.partial(_attn_kernel, cached=cache is not None, group=group, seq=seq, tq=tq, pps=pps,
                          n_casts=len(casts)),
        grid=grid,
        in_specs=in_specs + cast_specs,
        out_specs=[pl.BlockSpec((group * tq, ATTN_W), lambda b, i, j: (b * nq + i, 0))] + cast_specs,
        out_shape=[jax.ShapeDtypeStruct((n, ATTN_W), BF16)]
        + [jax.ShapeDtypeStruct(w.shape, BF16) for w in casts],
        scratch_shapes=[pltpu.VMEM((n_pairs, group * tq, LANES), F32)],
        compiler_params=_params(3),
        name="attn_cached" if cache is not None else "attn",
    )(*args, *casts)
    return outs[0], outs[1:]


def _ffn_kernel(*refs, nseg, seg, tiles_per_seq):
    halo = tiles_per_seq > 1
    if halo:
        (x_ref, an_ref, gm_ref, xp_ref, xn_ref, anp_ref, ann_ref, gmp_ref, gmn_ref, mod_ref, wo_ref, n2g_ref,
         wup_ref, cw_ref, cb_ref, wd_ref, fg_ref, out_ref, lhs_sc, a_sc, g_sc, mix_sc) = refs
    else:
        (x_ref, an_ref, gm_ref, mod_ref, wo_ref, n2g_ref,
         wup_ref, cw_ref, cb_ref, wd_ref, fg_ref, out_ref, lhs_sc, a_sc, g_sc) = refs
    stride = seg + HALO
    rows = HALO + nseg * stride
    gate1 = mod_ref[:, 2 * D_MODEL:3 * D_MODEL]
    shift2 = mod_ref[:, 3 * D_MODEL:4 * D_MODEL]
    scale2 = mod_ref[:, 4 * D_MODEL:5 * D_MODEL]

    def residual_norm(x, y):
        x1 = x + gate1 * y
        return x1, (_rms(x1, n2g_ref[...]) * (1.0 + scale2) + shift2).astype(BF16)

    zeros = jnp.zeros((HALO, D_MODEL), BF16)
    if halo:
        mix_sc[0:HALO, 0:ATTN_W] = anp_ref[...]
        mix_sc[0:HALO, ATTN_W:D_MODEL] = gmp_ref[...]
        mix_sc[HALO:HALO + seg, 0:ATTN_W] = an_ref[...]
        mix_sc[HALO:HALO + seg, ATTN_W:D_MODEL] = gm_ref[...]
        mix_sc[HALO + seg:rows, 0:ATTN_W] = ann_ref[...]
        mix_sc[HALO + seg:rows, ATTN_W:D_MODEL] = gmn_ref[...]
        y = _dot(mix_sc[...], wo_ref[...])
        x1, h2 = residual_norm(x_ref[...], y[HALO:HALO + seg])
        out_ref[...] = x1
        lhs_sc[HALO:HALO + seg, :] = h2
        pos = pl.program_id(0) % tiles_per_seq
        lhs_sc[0:HALO, :] = zeros
        lhs_sc[HALO + seg:rows, :] = zeros

        @pl.when(pos != 0)
        def _():
            lhs_sc[0:HALO, :] = residual_norm(xp_ref[...], y[0:HALO])[1]

        @pl.when(pos != tiles_per_seq - 1)
        def _():
            lhs_sc[HALO + seg:rows, :] = residual_norm(xn_ref[...], y[HALO + seg:rows])[1]
    else:
        y = _dot(an_ref[...], wo_ref[0:ATTN_W, :]) + _dot(gm_ref[...], wo_ref[ATTN_W:D_MODEL, :])
        x1, h2 = residual_norm(x_ref[...], y)
        out_ref[...] = x1
        lhs_sc[0:HALO, :] = zeros
        for s in range(nseg):
            base = HALO + s * stride
            lhs_sc[base:base + seg, :] = h2[s * seg:(s + 1) * seg, :]
            lhs_sc[base + seg:base + stride, :] = zeros

    for c in range(N_FF_CHUNKS):
        slot = c % 2
        for k, col0 in enumerate((c * FF_CHUNK, D_FF + c * FF_CHUNK)):
            a = _dot(lhs_sc[...], wup_ref[:, col0:col0 + FF_CHUNK])
            for j in range(FF_CHUNK // LANES):
                a_sc[slot, 2 * k + j, :, :] = a[8:rows - 8, j * LANES:(j + 1) * LANES]
        for j in range(FF_CHUNK // LANES):
            gl = slice(c * FF_CHUNK + j * LANES, c * FF_CHUNK + (j + 1) * LANES)
            vl = slice(D_FF + gl.start, D_FF + gl.stop)
            for s in range(nseg):
                base = 8 + s * stride

                def conv(k, lanes):
                    acc = cb_ref[:, lanes]
                    for t in range(CONV_W):
                        acc = acc + a_sc[slot, k, base - 1 + t:base - 1 + t + seg, :] * cw_ref[t:t + 1, lanes]
                    return acc

                g = _silu(conv(j, gl)) * conv(2 + j, vl)
                g_sc[s * seg:(s + 1) * seg, gl] = g.astype(BF16)

    x2 = out_ref[...] + mod_ref[:, 5 * D_MODEL:6 * D_MODEL] * _dot(g_sc[...], wd_ref[...])
    out_ref[...] = _rms(x2, fg_ref[...])


def _ffn(x, an, gm, mods3, mod_row, p, ffn_w, seq, tm):
    n = x.shape[0]
    tiles_per_seq = max(seq // tm, 1)
    seg = min(seq, tm)
    nseg = tm // seg
    rows = HALO + nseg * (seg + HALO)
    const = lambda i: (0, 0)
    tile = lambda i: (i, 0)
    resident = dict(pipeline_mode=pl.Buffered(1))
    in_specs = [pl.BlockSpec((tm, D_MODEL), tile), pl.BlockSpec((tm, ATTN_W), tile),
                pl.BlockSpec((tm, GMLP_W), tile)]
    args = [x, an, gm]
    scratch = [pltpu.VMEM((rows, D_MODEL), BF16),
               pltpu.VMEM((2, 2 * FF_CHUNK // LANES, rows - 16, LANES), F32),
               pltpu.VMEM((tm, D_FF), BF16)]
    if tiles_per_seq > 1:
        per = tm // HALO
        last = n // HALO - 1
        prev = lambda i: (jnp.maximum(i * per - 1, 0), 0)
        nxt = lambda i: (jnp.minimum((i + 1) * per, last), 0)
        for arr, width in ((x, D_MODEL), (an, ATTN_W), (gm, GMLP_W)):
            in_specs += [pl.BlockSpec((HALO, width), prev), pl.BlockSpec((HALO, width), nxt)]
            args += [arr, arr]
        scratch.append(pltpu.VMEM((rows, D_MODEL), BF16))
    in_specs += [
        pl.BlockSpec((None, 1, N_MODS * D_MODEL), lambda i: (mod_row(i), 0, 0)),
        pl.BlockSpec((D_MODEL, D_MODEL), const, **resident),
        pl.BlockSpec((1, D_MODEL), const),
        pl.BlockSpec((D_MODEL, 2 * D_FF), const, **resident),
        pl.BlockSpec((CONV_W, 2 * D_FF), const),
        pl.BlockSpec((1, 2 * D_FF), const),
        pl.BlockSpec((D_FF, D_MODEL), const, **resident),
        pl.BlockSpec((1, D_MODEL), const),
    ]
    w_o, w_up, w_down = ffn_w
    args += [mods3, w_o, p["norm2_g"], w_up, p["conv_w"], p["conv_b"], w_down, p["final_g"]]
    return pl.pallas_call(
        functools.partial(_ffn_kernel, nseg=nseg, seg=seg, tiles_per_seq=tiles_per_seq),
        grid=(n // tm,),
        in_specs=in_specs,
        out_specs=pl.BlockSpec((tm, D_MODEL), tile),
        out_shape=jax.ShapeDtypeStruct((n, D_MODEL), F32),
        scratch_shapes=scratch,
        compiler_params=_params(1),
        name="ffn_halo" if tiles_per_seq > 1 else "ffn",
    )(*args)


def _rope_tables(length):
    pos = np.arange(length)
    row = (pos // GRID_W).astype(np.float32)
    col = (pos % GRID_W).astype(np.float32)
    n_freq = QK_ROPE // 4
    inv = (np.float32(ROPE_THETA) ** (-(np.arange(n_freq, dtype=np.float32) / np.float32(n_freq)))).astype(np.float32)
    ang_r, ang_c = row[:, None] * inv, col[:, None] * inv
    cos32 = np.concatenate([np.cos(ang_r)] * 2 + [np.cos(ang_c)] * 2, axis=-1)
    sin32 = np.concatenate([-np.sin(ang_r), np.sin(ang_r), -np.sin(ang_c), np.sin(ang_c)], axis=-1)
    tail = LANES - ROPE_LANE0 - QK_ROPE
    cos_q = np.concatenate([np.ones((length, ROPE_LANE0)), cos32, np.zeros((length, tail))], -1)
    sin_q = np.concatenate([np.zeros((length, ROPE_LANE0)), sin32, np.zeros((length, tail))], -1)
    return jnp.asarray(cos_q, F32), jnp.asarray(sin_q, F32)


def _layer_params(l, norm1_g, q_norm_g, kv_norm_g, v_norm_g, b_s, out_g_attn,
                  out_g_gmlp, norm2_g, conv_w, conv_b, final_g):
    row = lambda a: a.reshape(1, -1)
    return {
        "norm1_g": row(norm1_g[l]), "q_norm_g": row(q_norm_g[l]),
        "kv_norm_g": row(kv_norm_g[l]), "v_norm_g": row(v_norm_g[l]), "b_st": b_s[l].T,
        "out_g_attn": row(out_g_attn[l]), "out_g_gmlp": row(out_g_gmlp[l]), "norm2_g": row(norm2_g[l]),
        "conv_w": conv_w[l], "conv_b": row(conv_b[l]), "final_g": row(final_g),
    }


def kernel(x_prompt, x_sample, cache_ckv, cache_krope, c, c_ctx, ada_w, ada_b, norm1_g, w_in, q_norm_g, w_uq, kv_norm_g, w_ukv, v_norm_g, w_s, b_s, out_g_attn, out_g_gmlp, w_o, norm2_g, w_up, conv_w, conv_b, w_down, final_g):
    batch, seq, _ = x_prompt.shape
    dec_batch, dec_seq, _ = x_sample.shape
    depth, past = cache_ckv.shape[1], cache_ckv.shape[2]
    assert depth == 1 and dec_batch + 1 <= MOD_ROWS
    l = 0
    p = _layer_params(l, norm1_g, q_norm_g, kv_norm_g, v_norm_g, b_s, out_g_attn,
                      out_g_gmlp, norm2_g, conv_w, conv_b, final_g)

    cond = jnp.concatenate([c_ctx[None, :], c, jnp.zeros((MOD_ROWS - 1 - dec_batch, D_MODEL), F32)], axis=0)
    mods3, p["w_qkv"], p["w_uv"], p["w_uq"], p["w_k"], p["w_ve"], p["w_vo"], w_s_b = _adaln(
        cond, ada_w[l], ada_b[l].reshape(1, -1), w_in[l], w_uq[l], w_ukv[l],
        w_s[l].reshape(GMLP_GROUPS * CHUNK, CHUNK))
    p["w_s"] = w_s_b.reshape(GMLP_GROUPS, CHUNK, CHUNK)
    tm_pre, tm_ffn = 512, 512
    ctx_row = lambda i: 0
    lat_row = lambda tm: lambda i: 1 + i // (dec_seq // tm)

    xp = x_prompt.reshape(batch * seq, D_MODEL)
    xs = x_sample.reshape(dec_batch * dec_seq, D_MODEL)
    (qp, kp, vep, vop, gmp, ckv_p, kr_p), (w_down_b,) = _pre(xp, mods3, ctx_row, p, None, tm_pre, [w_down[l]])
    (qs, ks, ves, vos, gms, _, _), (w_o_b,) = _pre(xs, mods3, lat_row(tm_pre), p, _rope_tables(dec_seq), tm_pre,
                                                   [w_o[l]])
    kr_pad = jnp.pad(cache_krope[:, l].reshape(dec_batch * past, QK_ROPE),
                     ((0, 0), (ROPE_LANE0, LANES - ROPE_LANE0 - QK_ROPE)))
    cache = _expand(cache_ckv[:, l].reshape(dec_batch * past, KV_RANK), kr_pad, p, 512)

    anp, _ = _attention(qp, kp, vep, vop, None, p["out_g_attn"], seq, seq, 4, 4, [])
    ans, (w_up_b,) = _attention(qs, ks, ves, vos, cache, p["out_g_attn"], dec_seq, 512, 1, 4, [w_up[l]])
    ffn_w = (w_o_b, w_up_b, w_down_b)
    yp = _ffn(xp, anp, gmp, mods3, ctx_row, p, ffn_w, seq, tm_ffn)
    ys = _ffn(xs, ans, gms, mods3, lat_row(tm_ffn), p, ffn_w, dec_seq, tm_ffn)

    return (yp.reshape(batch, seq, D_MODEL), ys.reshape(dec_batch, dec_seq, D_MODEL),
            ckv_p.reshape(batch, 1, seq, KV_RANK), kr_p.reshape(batch, 1, seq, QK_ROPE))
```

```python
import functools

import numpy as np
import jax
import jax.numpy as jnp
from jax import lax
from jax.experimental import pallas as pl
from jax.experimental.pallas import tpu as pltpu

D_MODEL = 1024
GRID_W = 64
CHUNK = 128
N_HEADS = 8
QK_NOPE = 64
QK_ROPE = 32
V_DIM = 64
Q_RANK = 384
KV_RANK = 256
ATTN_W = N_HEADS * V_DIM
GMLP_W = D_MODEL - ATTN_W
GMLP_GROUPS = 4
GMLP_DG = GMLP_W // GMLP_GROUPS
D_FF = 2816
CONV_W = 3
ROPE_THETA = 10000.0
EPS = 1e-6
SM_SCALE = (QK_NOPE + QK_ROPE) ** -0.5
LOG2_E = 1.4426950408889634

LANES = 128
HEAD_PAD = LANES
ROPE_LANE0 = QK_NOPE
QK_W = N_HEADS * HEAD_PAD
N_MODS = 6
MOD_ROWS = 8
ADALN_STEPS = 8
IN_QKV = Q_RANK + KV_RANK
FF_CHUNK = 256
HALO = 16
N_FF_CHUNKS = D_FF // FF_CHUNK
VMEM_LIMIT = 56 * 1024 * 1024

F32 = jnp.float32
BF16 = jnp.bfloat16


def _dot(a, b):
    return jnp.dot(a, b, preferred_element_type=F32)


def _dot_nt(a, b):
    return lax.dot_general(a, b, (((1,), (1,)), ((), ())), preferred_element_type=F32)


def _rms(x, g):
    return x * lax.rsqrt(jnp.mean(x * x, axis=-1, keepdims=True) + EPS) * g


def _gelu(x):
    return 0.5 * x * (1.0 + lax.erf(x * (0.5 ** 0.5)))


def _silu(x):
    return x * (1.0 / (1.0 + jnp.exp(-x)))


def _params(n_axes):
    return pltpu.CompilerParams(dimension_semantics=("arbitrary",) * n_axes,
                                vmem_limit_bytes=VMEM_LIMIT)


def _adaln_kernel(cctx_ref, c_ref, w_ref, b_ref, win_ref, wuq_ref, wukv_ref, ws_ref,
                  out_ref, wqkv_out, wuv_out, wuq_out, wk_out, wve_out, wvo_out, ws_out):
    pad = jnp.zeros((MOD_ROWS - 1 - c_ref.shape[0], D_MODEL), F32)
    cond = jnp.concatenate([cctx_ref[...], c_ref[...], pad], axis=0)
    s = _silu(cond).astype(BF16)
    out_ref[:, 0, :] = _dot(s, w_ref[...].astype(BF16)) + b_ref[...]
    @pl.when(pl.program_id(0) == 0)
    def _():
        kr = win_ref[IN_QKV:IN_QKV + QK_ROPE, :].astype(BF16)
        z = jnp.zeros_like(kr)
        wqkv_out[0:IN_QKV, :] = win_ref[0:IN_QKV, :].astype(BF16)
        for t, blk in enumerate((kr, z, kr, z)):
            wqkv_out[IN_QKV + t * QK_ROPE:IN_QKV + (t + 1) * QK_ROPE, :] = blk
        wuv_out[...] = win_ref[IN_QKV + QK_ROPE:, :].astype(BF16)

    w = wuq_ref[...]
    hw = QK_NOPE + QK_ROPE
    zq = jnp.zeros((w.shape[0], HEAD_PAD - hw), F32)
    wuq_out[...] = jnp.concatenate([blk for hd in range(N_HEADS) for blk in (w[:, hd * hw:(hd + 1) * hw], zq)],
                                   axis=1).astype(BF16)
    w = wukv_ref[...]
    zv = jnp.zeros((w.shape[0], V_DIM), F32)
    v_of = lambda hd: w[:, hd * HEAD_PAD + QK_NOPE:(hd + 1) * HEAD_PAD]
    wk_out[...] = jnp.concatenate([blk for hd in range(N_HEADS)
                                   for blk in (w[:, hd * HEAD_PAD:hd * HEAD_PAD + QK_NOPE], zv)], axis=1).astype(BF16)
    wve_out[...] = jnp.concatenate([blk for hd in range(0, N_HEADS, 2) for blk in (v_of(hd), zv)],
                                   axis=1).astype(BF16)
    wvo_out[...] = jnp.concatenate([blk for hd in range(1, N_HEADS, 2) for blk in (zv, v_of(hd))],
                                   axis=1).astype(BF16)
    ws_out[...] = ws_ref[...].astype(BF16)


def _adaln(c_ctx, c, ada_w, ada_b, w_in_t, w_uq, w_ukv, w_s):
    cols = N_MODS * D_MODEL // ADALN_STEPS
    row_block = lambda a, width=None: pl.BlockSpec((a.shape[0] // ADALN_STEPS, width or a.shape[1]),
                                                   lambda j: (j, 0))
    whole = lambda shape: pl.BlockSpec(shape, lambda j: (0, 0), pipeline_mode=pl.Buffered(1))
    return pl.pallas_call(
        _adaln_kernel,
        grid=(ADALN_STEPS,),
        in_specs=[
            pl.BlockSpec(c_ctx.shape, lambda j: (0, 0)),
            pl.BlockSpec(c.shape, lambda j: (0, 0)),
            pl.BlockSpec((D_MODEL, cols), lambda j: (0, j)),
            pl.BlockSpec((1, cols), lambda j: (0, j)),
            whole(w_in_t.shape), row_block(w_uq), row_block(w_ukv), row_block(w_s),
        ],
        out_specs=[pl.BlockSpec((MOD_ROWS, 1, cols), lambda j: (0, 0, j)),
                   whole((IN_QKV + LANES, D_MODEL)), whole((2 * GMLP_W, D_MODEL)),
                   row_block(w_uq, QK_W), row_block(w_ukv, QK_W), row_block(w_ukv, ATTN_W),
                   row_block(w_ukv, ATTN_W), row_block(w_s)],
        out_shape=[jax.ShapeDtypeStruct((MOD_ROWS, 1, N_MODS * D_MODEL), F32),
                   jax.ShapeDtypeStruct((IN_QKV + LANES, D_MODEL), BF16),
                   jax.ShapeDtypeStruct((2 * GMLP_W, D_MODEL), BF16),
                   jax.ShapeDtypeStruct((Q_RANK, QK_W), BF16),
                   jax.ShapeDtypeStruct((KV_RANK, QK_W), BF16),
                   jax.ShapeDtypeStruct((KV_RANK, ATTN_W), BF16),
                   jax.ShapeDtypeStruct((KV_RANK, ATTN_W), BF16),
                   jax.ShapeDtypeStruct(w_s.shape, BF16)],
        compiler_params=_params(1),
        name="adaln",
    )(c_ctx, c, ada_w, ada_b, w_in_t, w_uq, w_ukv, w_s)


def _pre_kernel(*refs, rope, tm, n_casts):
    n_main = 17 if rope else 15
    cast_in = refs[n_main:n_main + n_casts]
    cast_out = refs[n_main + n_casts + 7:n_main + 2 * n_casts + 7]
    q_out, k_out, ve_out, vo_out, gm_out, ckv_out, kr_out = refs[n_main + n_casts:n_main + n_casts + 7]
    go_sc = refs[-1]
    (x_ref, mod_ref, n1g_ref, wqkv_ref, wuv_ref, qng_ref, wuq_ref, kvg_ref, wk_ref, wve_ref, wvo_ref,
     vng_ref, ws_ref, bst_ref, ogg_ref) = refs[:15]
    if rope:
        cos_ref, sin_ref = refs[15:17]
    for src, dst in zip(cast_in, cast_out):
        dst[...] = src[...].astype(BF16)
    shift1 = mod_ref[:, 0:D_MODEL]
    scale1 = mod_ref[:, D_MODEL:2 * D_MODEL]
    h = (_rms(x_ref[...], n1g_ref[...]) * (1.0 + scale1) + shift1).astype(BF16)

    lane = lax.broadcasted_iota(jnp.int32, (1, LANES), 1)
    rope_lanes = lane >= ROPE_LANE0
    first_half = (lane % 16) < 8

    def rotate(v, cos):
        swapped = jnp.where(first_half, pltpu.roll(v, LANES - 8, 1), pltpu.roll(v, 8, 1))
        return v * cos + swapped * sin_ref[...]

    qkv = _dot_nt(h, wqkv_ref[...])
    qn = _rms(qkv[:, 0:Q_RANK], qng_ref[...]).astype(BF16)
    ckv = _rms(qkv[:, Q_RANK:IN_QKV], kvg_ref[...])
    ckv_out[...] = ckv
    ckv_b = ckv.astype(BF16)
    q = _dot(qn, wuq_ref[...]) * (SM_SCALE * LOG2_E)
    kr = qkv[:, IN_QKV:IN_QKV + LANES]
    kr_out[...] = kr[:, 0:QK_ROPE]
    if rope:
        cos_q = cos_ref[...]
        kr_k = rotate(kr, jnp.where(rope_lanes, cos_q, 0.0))
    else:
        kr_k = jnp.where(rope_lanes, kr, 0.0)
    kn = _dot(ckv_b, wk_ref[...])
    for hd in range(N_HEADS):
        sl = slice(hd * HEAD_PAD, (hd + 1) * HEAD_PAD)
        qh = q[:, sl]
        if rope:
            qh = rotate(qh, cos_q)
        q_out[:, sl] = qh.astype(BF16)
        k_out[:, sl] = (kn[:, sl] + kr_k).astype(BF16)
    ve_out[...] = _dot(ckv_b, wve_ref[...]).astype(BF16)
    vo_out[...] = _dot(ckv_b, wvo_ref[...]).astype(BF16)

    uv = _dot_nt(h, wuv_ref[...])
    gu = _gelu(uv[:, 0:GMLP_W])
    gv = _gelu(uv[:, GMLP_W:2 * GMLP_W])
    for g in range(GMLP_GROUPS):
        sl = slice(g * GMLP_DG, (g + 1) * GMLP_DG)
        vg = gv[:, sl]
        vg = (vg * lax.rsqrt(jnp.mean(vg * vg, axis=-1, keepdims=True) + EPS) * vng_ref[:, sl]).astype(BF16)
        bias = bst_ref[:, g:g + 1]
        chunks = [vg[n * CHUNK:(n + 1) * CHUNK, :] for n in range(tm // CHUNK)]
        s_all = _dot(ws_ref[g], jnp.concatenate(chunks, axis=1))
        for n in range(tm // CHUNK):
            rows = slice(n * CHUNK, (n + 1) * CHUNK)
            go_sc[rows, sl] = gu[rows, sl] * (s_all[:, n * GMLP_DG:(n + 1) * GMLP_DG] + bias)
    gm_out[...] = _rms(go_sc[...], ogg_ref[...]).astype(BF16)


def _pre(x, mods3, mod_row, p, rope_tabs, tm, casts):
    n = x.shape[0]
    rope = rope_tabs is not None
    const = lambda i: (0, 0)
    tile = lambda i: (i, 0)
    in_specs = [
        pl.BlockSpec((tm, D_MODEL), tile),
        pl.BlockSpec((None, 1, N_MODS * D_MODEL), lambda i: (mod_row(i), 0, 0)),
        pl.BlockSpec((1, D_MODEL), const),
        pl.BlockSpec((IN_QKV + LANES, D_MODEL), const),
        pl.BlockSpec((2 * GMLP_W, D_MODEL), const),
        pl.BlockSpec((1, Q_RANK), const),
        pl.BlockSpec((Q_RANK, QK_W), const),
        pl.BlockSpec((1, KV_RANK), const),
        pl.BlockSpec((KV_RANK, QK_W), const),
        pl.BlockSpec((KV_RANK, ATTN_W), const),
        pl.BlockSpec((KV_RANK, ATTN_W), const),
        pl.BlockSpec((1, GMLP_W), const),
        pl.BlockSpec((GMLP_GROUPS, CHUNK, CHUNK), lambda i: (0, 0, 0)),
        pl.BlockSpec((CHUNK, GMLP_GROUPS), const),
        pl.BlockSpec((1, GMLP_W), const),
    ]
    args = [x, mods3, p["norm1_g"], p["w_qkv"], p["w_uv"], p["q_norm_g"], p["w_uq"],
            p["kv_norm_g"], p["w_k"], p["w_ve"], p["w_vo"], p["v_norm_g"], p["w_s"], p["b_st"], p["out_g_gmlp"]]
    if rope:
        cos_q, sin_q = rope_tabs
        per_seq = cos_q.shape[0] // tm
        in_specs += [pl.BlockSpec((tm, LANES), lambda i: (i % per_seq, 0))] * 2
        args += [cos_q, sin_q]
    widths = [(QK_W, BF16), (QK_W, BF16), (ATTN_W, BF16), (ATTN_W, BF16), (GMLP_W, BF16),
              (KV_RANK, F32), (QK_ROPE, F32)]
    cast_specs = [pl.BlockSpec((w.shape[0] // (n // tm), w.shape[1]), tile) for w in casts]
    outs = pl.pallas_call(
        functools.partial(_pre_kernel, rope=rope, tm=tm, n_casts=len(casts)),
        grid=(n // tm,),
        in_specs=in_specs + cast_specs,
        out_specs=[pl.BlockSpec((tm, w), tile) for w, _ in widths] + cast_specs,
        out_shape=[jax.ShapeDtypeStruct((n, w), dt) for w, dt in widths]
        + [jax.ShapeDtypeStruct(w.shape, BF16) for w in casts],
        scratch_shapes=[pltpu.VMEM((tm, GMLP_W), F32)],
        compiler_params=_params(1),
        name="pre_rope" if rope else "pre",
    )(*args, *casts)
    return outs[:7], outs[7:]


def _expand_kernel(ckv_ref, kr_ref, wk_ref, wve_ref, wvo_ref, k_out, ve_out, vo_out):
    ckv_b = ckv_ref[...].astype(BF16)
    kn = _dot(ckv_b, wk_ref[...])
    kr = kr_ref[...]
    for hd in range(N_HEADS):
        sl = slice(hd * HEAD_PAD, (hd + 1) * HEAD_PAD)
        k_out[:, sl] = (kn[:, sl] + kr).astype(BF16)
    ve_out[...] = _dot(ckv_b, wve_ref[...]).astype(BF16)
    vo_out[...] = _dot(ckv_b, wvo_ref[...]).astype(BF16)


def _expand(ckv, kr_pad, p, tm):
    n = ckv.shape[0]
    const = lambda i: (0, 0)
    tile = lambda i: (i, 0)
    widths = [QK_W, ATTN_W, ATTN_W]
    return pl.pallas_call(
        _expand_kernel,
        grid=(n // tm,),
        in_specs=[
            pl.BlockSpec((tm, KV_RANK), tile),
            pl.BlockSpec((tm, LANES), tile),
            pl.BlockSpec((KV_RANK, QK_W), const),
            pl.BlockSpec((KV_RANK, ATTN_W), const),
            pl.BlockSpec((KV_RANK, ATTN_W), const),
        ],
        out_specs=[pl.BlockSpec((tm, w), tile) for w in widths],
        out_shape=[jax.ShapeDtypeStruct((n, w), BF16) for w in widths],
        compiler_params=_params(1),
        name="expand",
    )(ckv, kr_pad, p["w_k"], p["w_ve"], p["w_vo"])


def _attn_kernel(*refs, cached, group, seq, tq, pps, n_casts):
    n_main = 8 if cached else 5
    cast_in = refs[n_main:n_main + n_casts]
    out_ref = refs[n_main + n_casts]
    cast_out = refs[n_main + n_casts + 1:n_main + 2 * n_casts + 1]
    o_sc = refs[-1]
    if cached:
        q_ref, k_ref, ve_ref, vo_ref, kc_ref, vec_ref, voc_ref, oga_ref = refs[:n_main]
    else:
        q_ref, k_ref, ve_ref, vo_ref, oga_ref = refs[:n_main]
    for src, dst in zip(cast_in, cast_out):
        dst[...] = src[...].astype(BF16)
    step = pl.program_id(2)
    ones = jnp.ones((seq, LANES), BF16)
    ones_c = jnp.ones((kc_ref.shape[0], LANES), BF16) if cached else None
    for g in range(group):
        qrows = slice(g * tq, (g + 1) * tq)
        krows = slice(g * seq, (g + 1) * seq)
        for pp in range(pps):
            psl = slice(pp * LANES, (pp + 1) * LANES)
            acc = None
            for par, (v_ref, vc_ref) in enumerate(((ve_ref, vec_ref if cached else None),
                                                   (vo_ref, voc_ref if cached else None))):
                hd = 2 * pp + par
                hsl = slice(hd * HEAD_PAD, (hd + 1) * HEAD_PAD)
                qh = q_ref[qrows, hsl]
                s = _dot_nt(qh, k_ref[krows, hsl])
                m = jnp.max(s, axis=-1, keepdims=True)
                if cached:
                    sc = _dot_nt(qh, kc_ref[:, hsl])
                    m = jnp.maximum(m, jnp.max(sc, axis=-1, keepdims=True))
                e = jnp.exp2(s - m).astype(BF16)
                o = _dot(e, jnp.concatenate([v_ref[krows, psl], ones], axis=1))
                if cached:
                    ec = jnp.exp2(sc - m).astype(BF16)
                    o = o + _dot(ec, jnp.concatenate([vc_ref[:, psl], ones_c], axis=1))
                o = o[:, 0:LANES] * (1.0 / o[:, LANES:2 * LANES])
                acc = o if acc is None else acc + o
            o_sc[step * pps + pp, qrows, :] = acc

    @pl.when(step == N_HEADS // 2 // pps - 1)
    def _():
        blocks = [o_sc[j] for j in range(N_HEADS // 2)]
        ssq = sum(jnp.sum(b * b, axis=-1, keepdims=True) for b in blocks)
        r = lax.rsqrt(ssq * (1.0 / ATTN_W) + EPS)
        for j, b in enumerate(blocks):
            sl = slice(j * LANES, (j + 1) * LANES)
            out_ref[:, sl] = (b * r * oga_ref[:, sl]).astype(BF16)


def _attention(q, k, ve, vo, cache, oga, seq, tq, group, pps, casts):
    n = q.shape[0]
    nq = seq // tq
    n_pairs = N_HEADS // 2
    grid = (n // (group * seq), nq, n_pairs // pps)
    n_steps = grid[0] * grid[1] * grid[2]
    linear = lambda b, i, j: ((b * grid[1] + i) * grid[2] + j, 0)
    cast_specs = [pl.BlockSpec((w.shape[0] // n_steps, w.shape[1]), linear) for w in casts]
    qtile = lambda b, i, j: (b * nq + i, j)
    kv = lambda b, i, j: (b, j)
    qk_w, v_w = pps * 2 * HEAD_PAD, pps * LANES
    in_specs = [
        pl.BlockSpec((group * tq, qk_w), qtile),
        pl.BlockSpec((group * seq, qk_w), kv),
        pl.BlockSpec((group * seq, v_w), kv),
        pl.BlockSpec((group * seq, v_w), kv),
    ]
    args = [q, k, ve, vo]
    if cache is not None:
        kc, vec, voc = cache
        past = kc.shape[0] // (n // seq)
        in_specs += [pl.BlockSpec((past, qk_w), kv), pl.BlockSpec((past, v_w), kv),
                     pl.BlockSpec((past, v_w), kv)]
        args += [kc, vec, voc]
    in_specs.append(pl.BlockSpec((1, ATTN_W), lambda b, i, j: (0, 0)))
    args.append(oga)
    outs = pl.pallas_call(
        functools.partial(_attn_kernel, cached=cache is not None, group=group, seq=seq, tq=tq, pps=pps,
                          n_casts=len(casts)),
        grid=grid,
        in_specs=in_specs + cast_specs,
        out_specs=[pl.BlockSpec((group * tq, ATTN_W), lambda b, i, j: (b * nq + i, 0))] + cast_specs,
        out_shape=[jax.ShapeDtypeStruct((n, ATTN_W), BF16)]
        + [jax.ShapeDtypeStruct(w.shape, BF16) for w in casts],
        scratch_shapes=[pltpu.VMEM((n_pairs, group * tq, LANES), F32)],
        compiler_params=_params(3),
        name="attn_cached" if cache is not None else "attn",
    )(*args, *casts)
    return outs[0], outs[1:]


def _ffn_kernel(*refs, nseg, seg, tiles_per_seq):
    halo = tiles_per_seq > 1
    if halo:
        (x_ref, an_ref, gm_ref, xp_ref, xn_ref, anp_ref, ann_ref, gmp_ref, gmn_ref, mod_ref, wo_ref, n2g_ref,
         wup_ref, cw_ref, cb_ref, wd_ref, fg_ref, out_ref, lhs_sc, a_sc, g_sc, mix_sc) = refs
    else:
        (x_ref, an_ref, gm_ref, mod_ref, wo_ref, n2g_ref,
         wup_ref, cw_ref, cb_ref, wd_ref, fg_ref, out_ref, lhs_sc, a_sc, g_sc) = refs
    stride = seg + HALO
    rows = HALO + nseg * stride
    gate1 = mod_ref[:, 2 * D_MODEL:3 * D_MODEL]
    shift2 = mod_ref[:, 3 * D_MODEL:4 * D_MODEL]
    scale2 = mod_ref[:, 4 * D_MODEL:5 * D_MODEL]

    def residual_norm(x, y):
        x1 = x + gate1 * y
        return x1, (_rms(x1, n2g_ref[...]) * (1.0 + scale2) + shift2).astype(BF16)

    zeros = jnp.zeros((HALO, D_MODEL), BF16)
    if halo:
        mix_sc[0:HALO, 0:ATTN_W] = anp_ref[...]
        mix_sc[0:HALO, ATTN_W:D_MODEL] = gmp_ref[...]
        mix_sc[HALO:HALO + seg, 0:ATTN_W] = an_ref[...]
        mix_sc[HALO:HALO + seg, ATTN_W:D_MODEL] = gm_ref[...]
        mix_sc[HALO + seg:rows, 0:ATTN_W] = ann_ref[...]
        mix_sc[HALO + seg:rows, ATTN_W:D_MODEL] = gmn_ref[...]
        y = _dot(mix_sc[...], wo_ref[...])
        x1, h2 = residual_norm(x_ref[...], y[HALO:HALO + seg])
        out_ref[...] = x1
        lhs_sc[HALO:HALO + seg, :] = h2
        pos = pl.program_id(0) % tiles_per_seq
        lhs_sc[0:HALO, :] = zeros
        lhs_sc[HALO + seg:rows, :] = zeros

        @pl.when(pos != 0)
        def _():
            lhs_sc[0:HALO, :] = residual_norm(xp_ref[...], y[0:HALO])[1]

        @pl.when(pos != tiles_per_seq - 1)
        def _():
            lhs_sc[HALO + seg:rows, :] = residual_norm(xn_ref[...], y[HALO + seg:rows])[1]
    else:
        y = _dot(an_ref[...], wo_ref[0:ATTN_W, :]) + _dot(gm_ref[...], wo_ref[ATTN_W:D_MODEL, :])
        x1, h2 = residual_norm(x_ref[...], y)
        out_ref[...] = x1
        lhs_sc[0:HALO, :] = zeros
        for s in range(nseg):
            base = HALO + s * stride
            lhs_sc[base:base + seg, :] = h2[s * seg:(s + 1) * seg, :]
            lhs_sc[base + seg:base + stride, :] = zeros

    for c in range(N_FF_CHUNKS):
        slot = c % 2
        for k, col0 in enumerate((c * FF_CHUNK, D_FF + c * FF_CHUNK)):
            a = _dot(lhs_sc[...], wup_ref[:, col0:col0 + FF_CHUNK])
            for j in range(FF_CHUNK // LANES):
                a_sc[slot, 2 * k + j, :, :] = a[8:rows - 8, j * LANES:(j + 1) * LANES]
        for j in range(FF_CHUNK // LANES):
            gl = slice(c * FF_CHUNK + j * LANES, c * FF_CHUNK + (j + 1) * LANES)
            vl = slice(D_FF + gl.start, D_FF + gl.stop)
            for s in range(nseg):
                base = 8 + s * stride

                def conv(k, lanes):
                    acc = cb_ref[:, lanes]
                    for t in range(CONV_W):
                        tap = slice(t * 2 * D_FF + lanes.start, t * 2 * D_FF + lanes.stop)
                        acc = acc + a_sc[slot, k, base - 1 + t:base - 1 + t + seg, :] * cw_ref[:, tap]
                    return acc

                g = _silu(conv(j, gl)) * conv(2 + j, vl)
                g_sc[s * seg:(s + 1) * seg, gl] = g.astype(BF16)

    x2 = out_ref[...] + mod_ref[:, 5 * D_MODEL:6 * D_MODEL] * _dot(g_sc[...], wd_ref[...])
    out_ref[...] = _rms(x2, fg_ref[...])


def _ffn(x, an, gm, mods3, mod_row, p, ffn_w, seq, tm):
    n = x.shape[0]
    tiles_per_seq = max(seq // tm, 1)
    seg = min(seq, tm)
    nseg = tm // seg
    rows = HALO + nseg * (seg + HALO)
    const = lambda i: (0, 0)
    tile = lambda i: (i, 0)
    resident = dict(pipeline_mode=pl.Buffered(1))
    in_specs = [pl.BlockSpec((tm, D_MODEL), tile), pl.BlockSpec((tm, ATTN_W), tile),
                pl.BlockSpec((tm, GMLP_W), tile)]
    args = [x, an, gm]
    scratch = [pltpu.VMEM((rows, D_MODEL), BF16),
               pltpu.VMEM((2, 2 * FF_CHUNK // LANES, rows - 16, LANES), F32),
               pltpu.VMEM((tm, D_FF), BF16)]
    if tiles_per_seq > 1:
        per = tm // HALO
        last = n // HALO - 1
        prev = lambda i: (jnp.maximum(i * per - 1, 0), 0)
        nxt = lambda i: (jnp.minimum((i + 1) * per, last), 0)
        for arr, width in ((x, D_MODEL), (an, ATTN_W), (gm, GMLP_W)):
            in_specs += [pl.BlockSpec((HALO, width), prev), pl.BlockSpec((HALO, width), nxt)]
            args += [arr, arr]
        scratch.append(pltpu.VMEM((rows, D_MODEL), BF16))
    in_specs += [
        pl.BlockSpec((None, 1, N_MODS * D_MODEL), lambda i: (mod_row(i), 0, 0)),
        pl.BlockSpec((D_MODEL, D_MODEL), const, **resident),
        pl.BlockSpec((1, D_MODEL), const),
        pl.BlockSpec((D_MODEL, 2 * D_FF), const, **resident),
        pl.BlockSpec((1, CONV_W * 2 * D_FF), const),
        pl.BlockSpec((1, 2 * D_FF), const),
        pl.BlockSpec((D_FF, D_MODEL), const, **resident),
        pl.BlockSpec((1, D_MODEL), const),
    ]
    w_o, w_up, w_down = ffn_w
    args += [mods3, w_o, p["norm2_g"], w_up, p["conv_w"], p["conv_b"], w_down, p["final_g"]]
    return pl.pallas_call(
        functools.partial(_ffn_kernel, nseg=nseg, seg=seg, tiles_per_seq=tiles_per_seq),
        grid=(n // tm,),
        in_specs=in_specs,
        out_specs=pl.BlockSpec((tm, D_MODEL), tile),
        out_shape=jax.ShapeDtypeStruct((n, D_MODEL), F32),
        scratch_shapes=scratch,
        compiler_params=_params(1),
        name="ffn_halo" if tiles_per_seq > 1 else "ffn",
    )(*args)


def _rope_tables(length):
    pos = np.arange(length)
    row = (pos // GRID_W).astype(np.float32)
    col = (pos % GRID_W).astype(np.float32)
    n_freq = QK_ROPE // 4
    inv = (np.float32(ROPE_THETA) ** (-(np.arange(n_freq, dtype=np.float32) / np.float32(n_freq)))).astype(np.float32)
    ang_r, ang_c = row[:, None] * inv, col[:, None] * inv
    cos32 = np.concatenate([np.cos(ang_r)] * 2 + [np.cos(ang_c)] * 2, axis=-1)
    sin32 = np.concatenate([-np.sin(ang_r), np.sin(ang_r), -np.sin(ang_c), np.sin(ang_c)], axis=-1)
    tail = LANES - ROPE_LANE0 - QK_ROPE
    cos_q = np.concatenate([np.ones((length, ROPE_LANE0)), cos32, np.zeros((length, tail))], -1)
    sin_q = np.concatenate([np.zeros((length, ROPE_LANE0)), sin32, np.zeros((length, tail))], -1)
    return jnp.asarray(cos_q, F32), jnp.asarray(sin_q, F32)


def _layer_params(l, norm1_g, q_norm_g, kv_norm_g, v_norm_g, b_s, out_g_attn,
                  out_g_gmlp, norm2_g, conv_w, conv_b, final_g):
    row = lambda a: a.reshape(1, -1)
    return {
        "norm1_g": row(norm1_g[l]), "q_norm_g": row(q_norm_g[l]),
        "kv_norm_g": row(kv_norm_g[l]), "v_norm_g": row(v_norm_g[l]), "b_st": b_s[l].T,
        "out_g_attn": row(out_g_attn[l]), "out_g_gmlp": row(out_g_gmlp[l]), "norm2_g": row(norm2_g[l]),
        "conv_w": conv_w[l].reshape(1, CONV_W * 2 * D_FF), "conv_b": row(conv_b[l]), "final_g": row(final_g),
    }


def kernel(x_prompt, x_sample, cache_ckv, cache_krope, c, c_ctx, ada_w, ada_b, norm1_g, w_in, q_norm_g, w_uq, kv_norm_g, w_ukv, v_norm_g, w_s, b_s, out_g_attn, out_g_gmlp, w_o, norm2_g, w_up, conv_w, conv_b, w_down, final_g):
    batch, seq, _ = x_prompt.shape
    dec_batch, dec_seq, _ = x_sample.shape
    depth, past = cache_ckv.shape[1], cache_ckv.shape[2]
    assert depth == 1 and dec_batch + 1 <= MOD_ROWS
    l = 0
    p = _layer_params(l, norm1_g, q_norm_g, kv_norm_g, v_norm_g, b_s, out_g_attn,
                      out_g_gmlp, norm2_g, conv_w, conv_b, final_g)

    mods3, p["w_qkv"], p["w_uv"], p["w_uq"], p["w_k"], p["w_ve"], p["w_vo"], w_s_b = _adaln(
        c_ctx.reshape(1, D_MODEL), c, ada_w[l], ada_b[l].reshape(1, -1), w_in[l].T, w_uq[l], w_ukv[l],
        w_s[l].reshape(GMLP_GROUPS * CHUNK, CHUNK))
    p["w_s"] = w_s_b.reshape(GMLP_GROUPS, CHUNK, CHUNK)
    tm_pre, tm_ffn = 512, 512
    ctx_row = lambda i: 0
    lat_row = lambda tm: lambda i: 1 + i // (dec_seq // tm)

    xp = x_prompt.reshape(batch * seq, D_MODEL)
    xs = x_sample.reshape(dec_batch * dec_seq, D_MODEL)
    (qp, kp, vep, vop, gmp, ckv_p, kr_p), (w_down_b,) = _pre(xp, mods3, ctx_row, p, None, tm_pre, [w_down[l]])
    (qs, ks, ves, vos, gms, _, _), (w_o_b,) = _pre(xs, mods3, lat_row(tm_pre), p, _rope_tables(dec_seq), tm_pre,
                                                   [w_o[l]])
    kr_pad = jnp.pad(cache_krope[:, l].reshape(dec_batch * past, QK_ROPE),
                     ((0, 0), (ROPE_LANE0, LANES - ROPE_LANE0 - QK_ROPE)))
    cache = _expand(cache_ckv[:, l].reshape(dec_batch * past, KV_RANK), kr_pad, p, 512)

    anp, _ = _attention(qp, kp, vep, vop, None, p["out_g_attn"], seq, seq, 4, 4, [])
    ans, (w_up_b,) = _attention(qs, ks, ves, vos, cache, p["out_g_attn"], dec_seq, 512, 1, 4, [w_up[l]])
    ffn_w = (w_o_b, w_up_b, w_down_b)
    yp = _ffn(xp, anp, gmp, mods3, ctx_row, p, ffn_w, seq, tm_ffn)
    ys = _ffn(xs, ans, gms, mods3, lat_row(tm_ffn), p, ffn_w, dec_seq, tm_ffn)

    return (yp.reshape(batch, seq, D_MODEL), ys.reshape(dec_batch, dec_seq, D_MODEL),
            ckv_p.reshape(batch, 1, seq, KV_RANK), kr_p.reshape(batch, 1, seq, QK_ROPE))
```

```python
import functools

import numpy as np
import jax
import jax.numpy as jnp
from jax import lax
from jax.experimental import pallas as pl
from jax.experimental.pallas import tpu as pltpu

D_MODEL = 1024
GRID_W = 64
CHUNK = 128
N_HEADS = 8
QK_NOPE = 64
QK_ROPE = 32
V_DIM = 64
Q_RANK = 384
KV_RANK = 256
ATTN_W = N_HEADS * V_DIM
GMLP_W = D_MODEL - ATTN_W
GMLP_GROUPS = 4
GMLP_DG = GMLP_W // GMLP_GROUPS
D_FF = 2816
CONV_W = 3
ROPE_THETA = 10000.0
EPS = 1e-6
SM_SCALE = (QK_NOPE + QK_ROPE) ** -0.5
LOG2_E = 1.4426950408889634

LANES = 128
HEAD_PAD = LANES
ROPE_LANE0 = QK_NOPE
QK_W = N_HEADS * HEAD_PAD
N_MODS = 6
MOD_ROWS = 8
ADALN_STEPS = 8
IN_QKV = Q_RANK + KV_RANK
FF_CHUNK = 256
HALO = 16
N_FF_CHUNKS = D_FF // FF_CHUNK
A_SLOTS = 11
VMEM_LIMIT = 56 * 1024 * 1024

F32 = jnp.float32
BF16 = jnp.bfloat16


def _dot(a, b):
    return jnp.dot(a, b, preferred_element_type=F32)


def _dot_nt(a, b):
    return lax.dot_general(a, b, (((1,), (1,)), ((), ())), preferred_element_type=F32)


def _rms(x, g):
    return x * lax.rsqrt(jnp.mean(x * x, axis=-1, keepdims=True) + EPS) * g


def _gelu(x):
    return 0.5 * x * (1.0 + lax.erf(x * (0.5 ** 0.5)))


def _silu(x):
    return x * (1.0 / (1.0 + jnp.exp(-x)))


def _params(n_axes):
    return pltpu.CompilerParams(dimension_semantics=("arbitrary",) * n_axes,
                                vmem_limit_bytes=VMEM_LIMIT)


def _adaln_kernel(cctx_ref, c_ref, w_ref, b_ref, win_ref, wuq_ref, wukv_ref, ws_ref,
                  out_ref, wqkv_out, wuv_out, wuq_out, wk_out, wve_out, wvo_out, ws_out):
    pad = jnp.zeros((MOD_ROWS - 1 - c_ref.shape[0], D_MODEL), F32)
    cond = jnp.concatenate([cctx_ref[...], c_ref[...], pad], axis=0)
    s = _silu(cond).astype(BF16)
    out_ref[:, 0, :] = _dot(s, w_ref[...].astype(BF16)) + b_ref[...]
    @pl.when(pl.program_id(0) == 0)
    def _():
        kr = win_ref[IN_QKV:IN_QKV + QK_ROPE, :].astype(BF16)
        z = jnp.zeros_like(kr)
        wqkv_out[0:IN_QKV, :] = win_ref[0:IN_QKV, :].astype(BF16)
        for t, blk in enumerate((kr, z, kr, z)):
            wqkv_out[IN_QKV + t * QK_ROPE:IN_QKV + (t + 1) * QK_ROPE, :] = blk
        wuv_out[...] = win_ref[IN_QKV + QK_ROPE:, :].astype(BF16)

    w = wuq_ref[...]
    hw = QK_NOPE + QK_ROPE
    zq = jnp.zeros((w.shape[0], HEAD_PAD - hw), F32)
    wuq_out[...] = jnp.concatenate([blk for hd in range(N_HEADS) for blk in (w[:, hd * hw:(hd + 1) * hw], zq)],
                                   axis=1).astype(BF16)
    w = wukv_ref[...]
    zv = jnp.zeros((w.shape[0], V_DIM), F32)
    v_of = lambda hd: w[:, hd * HEAD_PAD + QK_NOPE:(hd + 1) * HEAD_PAD]
    wk_out[...] = jnp.concatenate([blk for hd in range(N_HEADS)
                                   for blk in (w[:, hd * HEAD_PAD:hd * HEAD_PAD + QK_NOPE], zv)], axis=1).astype(BF16)
    wve_out[...] = jnp.concatenate([blk for hd in range(0, N_HEADS, 2) for blk in (v_of(hd), zv)],
                                   axis=1).astype(BF16)
    wvo_out[...] = jnp.concatenate([blk for hd in range(1, N_HEADS, 2) for blk in (zv, v_of(hd))],
                                   axis=1).astype(BF16)
    ws_out[...] = ws_ref[...].astype(BF16)


def _adaln(c_ctx, c, ada_w, ada_b, w_in_t, w_uq, w_ukv, w_s):
    cols = N_MODS * D_MODEL // ADALN_STEPS
    row_block = lambda a, width=None: pl.BlockSpec((a.shape[0] // ADALN_STEPS, width or a.shape[1]),
                                                   lambda j: (j, 0))
    whole = lambda shape: pl.BlockSpec(shape, lambda j: (0, 0), pipeline_mode=pl.Buffered(1))
    return pl.pallas_call(
        _adaln_kernel,
        grid=(ADALN_STEPS,),
        in_specs=[
            pl.BlockSpec(c_ctx.shape, lambda j: (0, 0)),
            pl.BlockSpec(c.shape, lambda j: (0, 0)),
            pl.BlockSpec((D_MODEL, cols), lambda j: (0, j)),
            pl.BlockSpec((1, cols), lambda j: (0, j)),
            whole(w_in_t.shape), row_block(w_uq), row_block(w_ukv), row_block(w_s),
        ],
        out_specs=[pl.BlockSpec((MOD_ROWS, 1, cols), lambda j: (0, 0, j)),
                   whole((IN_QKV + LANES, D_MODEL)), whole((2 * GMLP_W, D_MODEL)),
                   row_block(w_uq, QK_W), row_block(w_ukv, QK_W), row_block(w_ukv, ATTN_W),
                   row_block(w_ukv, ATTN_W), row_block(w_s)],
        out_shape=[jax.ShapeDtypeStruct((MOD_ROWS, 1, N_MODS * D_MODEL), F32),
                   jax.ShapeDtypeStruct((IN_QKV + LANES, D_MODEL), BF16),
                   jax.ShapeDtypeStruct((2 * GMLP_W, D_MODEL), BF16),
                   jax.ShapeDtypeStruct((Q_RANK, QK_W), BF16),
                   jax.ShapeDtypeStruct((KV_RANK, QK_W), BF16),
                   jax.ShapeDtypeStruct((KV_RANK, ATTN_W), BF16),
                   jax.ShapeDtypeStruct((KV_RANK, ATTN_W), BF16),
                   jax.ShapeDtypeStruct(w_s.shape, BF16)],
        compiler_params=_params(1),
        name="adaln",
    )(c_ctx, c, ada_w, ada_b, w_in_t, w_uq, w_ukv, w_s)


def _pre_kernel(*refs, rope, tm, n_casts):
    n_main = 17 if rope else 15
    cast_in = refs[n_main:n_main + n_casts]
    cast_out = refs[n_main + n_casts + 7:n_main + 2 * n_casts + 7]
    q_out, k_out, ve_out, vo_out, gm_out, ckv_out, kr_out = refs[n_main + n_casts:n_main + n_casts + 7]
    go_sc = refs[-1]
    (x_ref, mod_ref, n1g_ref, wqkv_ref, wuv_ref, qng_ref, wuq_ref, kvg_ref, wk_ref, wve_ref, wvo_ref,
     vng_ref, ws_ref, bst_ref, ogg_ref) = refs[:15]
    if rope:
        cos_ref, sin_ref = refs[15:17]
    for src, dst in zip(cast_in, cast_out):
        dst[...] = src[...].astype(BF16)
    shift1 = mod_ref[:, 0:D_MODEL]
    scale1 = mod_ref[:, D_MODEL:2 * D_MODEL]
    h = (_rms(x_ref[...], n1g_ref[...]) * (1.0 + scale1) + shift1).astype(BF16)

    lane = lax.broadcasted_iota(jnp.int32, (1, LANES), 1)
    rope_lanes = lane >= ROPE_LANE0
    first_half = (lane % 16) < 8

    def rotate(v, cos):
        swapped = jnp.where(first_half, pltpu.roll(v, LANES - 8, 1), pltpu.roll(v, 8, 1))
        return v * cos + swapped * sin_ref[...]

    qkv = _dot_nt(h, wqkv_ref[...])
    qn = _rms(qkv[:, 0:Q_RANK], qng_ref[...]).astype(BF16)
    ckv = _rms(qkv[:, Q_RANK:IN_QKV], kvg_ref[...])
    ckv_out[...] = ckv
    ckv_b = ckv.astype(BF16)
    q = _dot(qn, wuq_ref[...]) * (SM_SCALE * LOG2_E)
    kr = qkv[:, IN_QKV:IN_QKV + LANES]
    kr_out[...] = kr[:, 0:QK_ROPE]
    if rope:
        cos_q = cos_ref[...]
        kr_k = rotate(kr, jnp.where(rope_lanes, cos_q, 0.0))
    else:
        kr_k = jnp.where(rope_lanes, kr, 0.0)
    kn = _dot(ckv_b, wk_ref[...])
    for hd in range(N_HEADS):
        sl = slice(hd * HEAD_PAD, (hd + 1) * HEAD_PAD)
        qh = q[:, sl]
        if rope:
            qh = rotate(qh, cos_q)
        q_out[:, sl] = qh.astype(BF16)
        k_out[:, sl] = (kn[:, sl] + kr_k).astype(BF16)
    ve_out[...] = _dot(ckv_b, wve_ref[...]).astype(BF16)
    vo_out[...] = _dot(ckv_b, wvo_ref[...]).astype(BF16)

    uv = _dot_nt(h, wuv_ref[...])
    gu = _gelu(uv[:, 0:GMLP_W])
    gv = _gelu(uv[:, GMLP_W:2 * GMLP_W])
    for g in range(GMLP_GROUPS):
        sl = slice(g * GMLP_DG, (g + 1) * GMLP_DG)
        vg = gv[:, sl]
        vg = (vg * lax.rsqrt(jnp.mean(vg * vg, axis=-1, keepdims=True) + EPS) * vng_ref[:, sl]).astype(BF16)
        bias = bst_ref[:, g:g + 1]
        chunks = [vg[n * CHUNK:(n + 1) * CHUNK, :] for n in range(tm // CHUNK)]
        s_all = _dot(ws_ref[g], jnp.concatenate(chunks, axis=1))
        for n in range(tm // CHUNK):
            rows = slice(n * CHUNK, (n + 1) * CHUNK)
            go_sc[rows, sl] = gu[rows, sl] * (s_all[:, n * GMLP_DG:(n + 1) * GMLP_DG] + bias)
    gm_out[...] = _rms(go_sc[...], ogg_ref[...]).astype(BF16)


def _pre(x, mods3, mod_row, p, rope_tabs, tm, casts):
    n = x.shape[0]
    rope = rope_tabs is not None
    const = lambda i: (0, 0)
    tile = lambda i: (i, 0)
    in_specs = [
        pl.BlockSpec((tm, D_MODEL), tile),
        pl.BlockSpec((None, 1, N_MODS * D_MODEL), lambda i: (mod_row(i), 0, 0)),
        pl.BlockSpec((1, D_MODEL), const),
        pl.BlockSpec((IN_QKV + LANES, D_MODEL), const),
        pl.BlockSpec((2 * GMLP_W, D_MODEL), const),
        pl.BlockSpec((1, Q_RANK), const),
        pl.BlockSpec((Q_RANK, QK_W), const),
        pl.BlockSpec((1, KV_RANK), const),
        pl.BlockSpec((KV_RANK, QK_W), const),
        pl.BlockSpec((KV_RANK, ATTN_W), const),
        pl.BlockSpec((KV_RANK, ATTN_W), const),
        pl.BlockSpec((1, GMLP_W), const),
        pl.BlockSpec((GMLP_GROUPS, CHUNK, CHUNK), lambda i: (0, 0, 0)),
        pl.BlockSpec((CHUNK, GMLP_GROUPS), const),
        pl.BlockSpec((1, GMLP_W), const),
    ]
    args = [x, mods3, p["norm1_g"], p["w_qkv"], p["w_uv"], p["q_norm_g"], p["w_uq"],
            p["kv_norm_g"], p["w_k"], p["w_ve"], p["w_vo"], p["v_norm_g"], p["w_s"], p["b_st"], p["out_g_gmlp"]]
    if rope:
        cos_q, sin_q = rope_tabs
        per_seq = cos_q.shape[0] // tm
        in_specs += [pl.BlockSpec((tm, LANES), lambda i: (i % per_seq, 0))] * 2
        args += [cos_q, sin_q]
    widths = [(QK_W, BF16), (QK_W, BF16), (ATTN_W, BF16), (ATTN_W, BF16), (GMLP_W, BF16),
              (KV_RANK, F32), (QK_ROPE, F32)]
    cast_specs = [pl.BlockSpec((w.shape[0] // (n // tm), w.shape[1]), tile) for w in casts]
    outs = pl.pallas_call(
        functools.partial(_pre_kernel, rope=rope, tm=tm, n_casts=len(casts)),
        grid=(n // tm,),
        in_specs=in_specs + cast_specs,
        out_specs=[pl.BlockSpec((tm, w), tile) for w, _ in widths] + cast_specs,
        out_shape=[jax.ShapeDtypeStruct((n, w), dt) for w, dt in widths]
        + [jax.ShapeDtypeStruct(w.shape, BF16) for w in casts],
        scratch_shapes=[pltpu.VMEM((tm, GMLP_W), F32)],
        compiler_params=_params(1),
        name="pre_rope" if rope else "pre",
    )(*args, *casts)
    return outs[:7], outs[7:]


def _expand_kernel(ckv_ref, kr_ref, wk_ref, wve_ref, wvo_ref, k_out, ve_out, vo_out):
    ckv_b = ckv_ref[...].astype(BF16)
    kn = _dot(ckv_b, wk_ref[...])
    kr = kr_ref[...]
    for hd in range(N_HEADS):
        sl = slice(hd * HEAD_PAD, (hd + 1) * HEAD_PAD)
        k_out[:, sl] = (kn[:, sl] + kr).astype(BF16)
    ve_out[...] = _dot(ckv_b, wve_ref[...]).astype(BF16)
    vo_out[...] = _dot(ckv_b, wvo_ref[...]).astype(BF16)


def _expand(ckv, kr_pad, p, tm):
    n = ckv.shape[0]
    const = lambda i: (0, 0)
    tile = lambda i: (i, 0)
    widths = [QK_W, ATTN_W, ATTN_W]
    return pl.pallas_call(
        _expand_kernel,
        grid=(n // tm,),
        in_specs=[
            pl.BlockSpec((tm, KV_RANK), tile),
            pl.BlockSpec((tm, LANES), tile),
            pl.BlockSpec((KV_RANK, QK_W), const),
            pl.BlockSpec((KV_RANK, ATTN_W), const),
            pl.BlockSpec((KV_RANK, ATTN_W), const),
        ],
        out_specs=[pl.BlockSpec((tm, w), tile) for w in widths],
        out_shape=[jax.ShapeDtypeStruct((n, w), BF16) for w in widths],
        compiler_params=_params(1),
        name="expand",
    )(ckv, kr_pad, p["w_k"], p["w_ve"], p["w_vo"])


def _attn_kernel(*refs, cached, group, seq, tq, pps, n_casts):
    n_main = 8 if cached else 5
    cast_in = refs[n_main:n_main + n_casts]
    out_ref = refs[n_main + n_casts]
    cast_out = refs[n_main + n_casts + 1:n_main + 2 * n_casts + 1]
    o_sc = refs[-1]
    if cached:
        q_ref, k_ref, ve_ref, vo_ref, kc_ref, vec_ref, voc_ref, oga_ref = refs[:n_main]
    else:
        q_ref, k_ref, ve_ref, vo_ref, oga_ref = refs[:n_main]
    for src, dst in zip(cast_in, cast_out):
        dst[...] = src[...].astype(BF16)
    step = pl.program_id(2)
    ones = jnp.ones((seq, LANES), BF16)
    ones_c = jnp.ones((kc_ref.shape[0], LANES), BF16) if cached else None
    for g in range(group):
        qrows = slice(g * tq, (g + 1) * tq)
        krows = slice(g * seq, (g + 1) * seq)
        for pp in range(pps):
            psl = slice(pp * LANES, (pp + 1) * LANES)
            acc = None
            for par, (v_ref, vc_ref) in enumerate(((ve_ref, vec_ref if cached else None),
                                                   (vo_ref, voc_ref if cached else None))):
                hd = 2 * pp + par
                hsl = slice(hd * HEAD_PAD, (hd + 1) * HEAD_PAD)
                qh = q_ref[qrows, hsl]
                s = _dot_nt(qh, k_ref[krows, hsl])
                m = jnp.max(s, axis=-1, keepdims=True)
                if cached:
                    sc = _dot_nt(qh, kc_ref[:, hsl])
                    m = jnp.maximum(m, jnp.max(sc, axis=-1, keepdims=True))
                e = jnp.exp2(s - m).astype(BF16)
                o = _dot(e, jnp.concatenate([v_ref[krows, psl], ones], axis=1))
                if cached:
                    ec = jnp.exp2(sc - m).astype(BF16)
                    o = o + _dot(ec, jnp.concatenate([vc_ref[:, psl], ones_c], axis=1))
                o = o[:, 0:LANES] * (1.0 / o[:, LANES:2 * LANES])
                acc = o if acc is None else acc + o
            o_sc[step * pps + pp, qrows, :] = acc

    @pl.when(step == N_HEADS // 2 // pps - 1)
    def _():
        blocks = [o_sc[j] for j in range(N_HEADS // 2)]
        ssq = sum(jnp.sum(b * b, axis=-1, keepdims=True) for b in blocks)
        r = lax.rsqrt(ssq * (1.0 / ATTN_W) + EPS)
        for j, b in enumerate(blocks):
            sl = slice(j * LANES, (j + 1) * LANES)
            out_ref[:, sl] = (b * r * oga_ref[:, sl]).astype(BF16)


def _attention(q, k, ve, vo, cache, oga, seq, tq, group, pps, casts):
    n = q.shape[0]
    nq = seq // tq
    n_pairs = N_HEADS // 2
    grid = (n // (group * seq), nq, n_pairs // pps)
    n_steps = grid[0] * grid[1] * grid[2]
    linear = lambda b, i, j: ((b * grid[1] + i) * grid[2] + j, 0)
    cast_specs = [pl.BlockSpec((w.shape[0] // n_steps, w.shape[1]), linear) for w in casts]
    qtile = lambda b, i, j: (b * nq + i, j)
    kv = lambda b, i, j: (b, j)
    qk_w, v_w = pps * 2 * HEAD_PAD, pps * LANES
    in_specs = [
        pl.BlockSpec((group * tq, qk_w), qtile),
        pl.BlockSpec((group * seq, qk_w), kv),
        pl.BlockSpec((group * seq, v_w), kv),
        pl.BlockSpec((group * seq, v_w), kv),
    ]
    args = [q, k, ve, vo]
    if cache is not None:
        kc, vec, voc = cache
        past = kc.shape[0] // (n // seq)
        in_specs += [pl.BlockSpec((past, qk_w), kv), pl.BlockSpec((past, v_w), kv),
                     pl.BlockSpec((past, v_w), kv)]
        args += [kc, vec, voc]
    in_specs.append(pl.BlockSpec((1, ATTN_W), lambda b, i, j: (0, 0)))
    args.append(oga)
    outs = pl.pallas_call(
        functools.partial(_attn_kernel, cached=cache is not None, group=group, seq=seq, tq=tq, pps=pps,
                          n_casts=len(casts)),
        grid=grid,
        in_specs=in_specs + cast_specs,
        out_specs=[pl.BlockSpec((group * tq, ATTN_W), lambda b, i, j: (b * nq + i, 0))] + cast_specs,
        out_shape=[jax.ShapeDtypeStruct((n, ATTN_W), BF16)]
        + [jax.ShapeDtypeStruct(w.shape, BF16) for w in casts],
        scratch_shapes=[pltpu.VMEM((n_pairs, group * tq, LANES), F32)],
        compiler_params=_params(3),
        name="attn_cached" if cache is not None else "attn",
    )(*args, *casts)
    return outs[0], outs[1:]


def _ffn_kernel(*refs, nseg, seg, tiles_per_seq):
    halo = tiles_per_seq > 1
    if halo:
        (x_ref, an_ref, gm_ref, xp_ref, xn_ref, anp_ref, ann_ref, gmp_ref, gmn_ref, mod_ref, wo_ref, n2g_ref,
         wup_ref, cw_ref, cb_ref, wd_ref, fg_ref, out_ref, lhs_sc, a_sc, g_sc, mix_sc) = refs
    else:
        (x_ref, an_ref, gm_ref, mod_ref, wo_ref, n2g_ref,
         wup_ref, cw_ref, cb_ref, wd_ref, fg_ref, out_ref, lhs_sc, a_sc, g_sc) = refs
    stride = seg + HALO
    rows = HALO + nseg * stride
    gate1 = mod_ref[:, 2 * D_MODEL:3 * D_MODEL]
    shift2 = mod_ref[:, 3 * D_MODEL:4 * D_MODEL]
    scale2 = mod_ref[:, 4 * D_MODEL:5 * D_MODEL]

    def residual_norm(x, y):
        x1 = x + gate1 * y
        return x1, (_rms(x1, n2g_ref[...]) * (1.0 + scale2) + shift2).astype(BF16)

    zeros = jnp.zeros((HALO, D_MODEL), BF16)
    if halo:
        mix_sc[0:HALO, 0:ATTN_W] = anp_ref[...]
        mix_sc[0:HALO, ATTN_W:D_MODEL] = gmp_ref[...]
        mix_sc[HALO:HALO + seg, 0:ATTN_W] = an_ref[...]
        mix_sc[HALO:HALO + seg, ATTN_W:D_MODEL] = gm_ref[...]
        mix_sc[HALO + seg:rows, 0:ATTN_W] = ann_ref[...]
        mix_sc[HALO + seg:rows, ATTN_W:D_MODEL] = gmn_ref[...]
        y = _dot(mix_sc[...], wo_ref[...])
        x1, h2 = residual_norm(x_ref[...], y[HALO:HALO + seg])
        out_ref[...] = x1
        lhs_sc[HALO:HALO + seg, :] = h2
        pos = pl.program_id(0) % tiles_per_seq
        lhs_sc[0:HALO, :] = zeros
        lhs_sc[HALO + seg:rows, :] = zeros

        @pl.when(pos != 0)
        def _():
            lhs_sc[0:HALO, :] = residual_norm(xp_ref[...], y[0:HALO])[1]

        @pl.when(pos != tiles_per_seq - 1)
        def _():
            lhs_sc[HALO + seg:rows, :] = residual_norm(xn_ref[...], y[HALO + seg:rows])[1]
    else:
        y = _dot(an_ref[...], wo_ref[0:ATTN_W, :]) + _dot(gm_ref[...], wo_ref[ATTN_W:D_MODEL, :])
        x1, h2 = residual_norm(x_ref[...], y)
        out_ref[...] = x1
        lhs_sc[0:HALO, :] = zeros
        for s in range(nseg):
            base = HALO + s * stride
            lhs_sc[base:base + seg, :] = h2[s * seg:(s + 1) * seg, :]
            lhs_sc[base + seg:base + stride, :] = zeros

    for c in range(N_FF_CHUNKS):
        slot = c % A_SLOTS
        for k, col0 in enumerate((c * FF_CHUNK, D_FF + c * FF_CHUNK)):
            a = _dot(lhs_sc[...], wup_ref[:, col0:col0 + FF_CHUNK])
            for j in range(FF_CHUNK // LANES):
                a_sc[slot, 2 * k + j, :, :] = a[8:rows - 8, j * LANES:(j + 1) * LANES]
        for j in range(FF_CHUNK // LANES):
            gl = slice(c * FF_CHUNK + j * LANES, c * FF_CHUNK + (j + 1) * LANES)
            vl = slice(D_FF + gl.start, D_FF + gl.stop)
            for s in range(nseg):
                base = 8 + s * stride

                def conv(k, lanes):
                    acc = cb_ref[:, lanes]
                    for t in range(CONV_W):
                        tap = slice(t * 2 * D_FF + lanes.start, t * 2 * D_FF + lanes.stop)
                        acc = acc + a_sc[slot, k, base - 1 + t:base - 1 + t + seg, :] * cw_ref[:, tap]
                    return acc

                g = _silu(conv(j, gl)) * conv(2 + j, vl)
                g_sc[s * seg:(s + 1) * seg, gl] = g.astype(BF16)

    x2 = out_ref[...] + mod_ref[:, 5 * D_MODEL:6 * D_MODEL] * _dot(g_sc[...], wd_ref[...])
    out_ref[...] = _rms(x2, fg_ref[...])


def _ffn(x, an, gm, mods3, mod_row, p, ffn_w, seq, tm):
    n = x.shape[0]
    tiles_per_seq = max(seq // tm, 1)
    seg = min(seq, tm)
    nseg = tm // seg
    rows = HALO + nseg * (seg + HALO)
    const = lambda i: (0, 0)
    tile = lambda i: (i, 0)
    resident = dict(pipeline_mode=pl.Buffered(1))
    in_specs = [pl.BlockSpec((tm, D_MODEL), tile), pl.BlockSpec((tm, ATTN_W), tile),
                pl.BlockSpec((tm, GMLP_W), tile)]
    args = [x, an, gm]
    scratch = [pltpu.VMEM((rows, D_MODEL), BF16),
               pltpu.VMEM((A_SLOTS, 2 * FF_CHUNK // LANES, rows - 16, LANES), F32),
               pltpu.VMEM((tm, D_FF), BF16)]
    if tiles_per_seq > 1:
        per = tm // HALO
        last = n // HALO - 1
        prev = lambda i: (jnp.maximum(i * per - 1, 0), 0)
        nxt = lambda i: (jnp.minimum((i + 1) * per, last), 0)
        for arr, width in ((x, D_MODEL), (an, ATTN_W), (gm, GMLP_W)):
            in_specs += [pl.BlockSpec((HALO, width), prev), pl.BlockSpec((HALO, width), nxt)]
            args += [arr, arr]
        scratch.append(pltpu.VMEM((rows, D_MODEL), BF16))
    in_specs += [
        pl.BlockSpec((None, 1, N_MODS * D_MODEL), lambda i: (mod_row(i), 0, 0)),
        pl.BlockSpec((D_MODEL, D_MODEL), const, **resident),
        pl.BlockSpec((1, D_MODEL), const),
        pl.BlockSpec((D_MODEL, 2 * D_FF), const, **resident),
        pl.BlockSpec((1, CONV_W * 2 * D_FF), const),
        pl.BlockSpec((1, 2 * D_FF), const),
        pl.BlockSpec((D_FF, D_MODEL), const, **resident),
        pl.BlockSpec((1, D_MODEL), const),
    ]
    w_o, w_up, w_down = ffn_w
    args += [mods3, w_o, p["norm2_g"], w_up, p["conv_w"], p["conv_b"], w_down, p["final_g"]]
    return pl.pallas_call(
        functools.partial(_ffn_kernel, nseg=nseg, seg=seg, tiles_per_seq=tiles_per_seq),
        grid=(n // tm,),
        in_specs=in_specs,
        out_specs=pl.BlockSpec((tm, D_MODEL), tile),
        out_shape=jax.ShapeDtypeStruct((n, D_MODEL), F32),
        scratch_shapes=scratch,
        compiler_params=_params(1),
        name="ffn_halo" if tiles_per_seq > 1 else "ffn",
    )(*args)


def _rope_tables(length):
    pos = np.arange(length)
    row = (pos // GRID_W).astype(np.float32)
    col = (pos % GRID_W).astype(np.float32)
    n_freq = QK_ROPE // 4
    inv = (np.float32(ROPE_THETA) ** (-(np.arange(n_freq, dtype=np.float32) / np.float32(n_freq)))).astype(np.float32)
    ang_r, ang_c = row[:, None] * inv, col[:, None] * inv
    cos32 = np.concatenate([np.cos(ang_r)] * 2 + [np.cos(ang_c)] * 2, axis=-1)
    sin32 = np.concatenate([-np.sin(ang_r), np.sin(ang_r), -np.sin(ang_c), np.sin(ang_c)], axis=-1)
    tail = LANES - ROPE_LANE0 - QK_ROPE
    cos_q = np.concatenate([np.ones((length, ROPE_LANE0)), cos32, np.zeros((length, tail))], -1)
    sin_q = np.concatenate([np.zeros((length, ROPE_LANE0)), sin32, np.zeros((length, tail))], -1)
    return jnp.asarray(cos_q, F32), jnp.asarray(sin_q, F32)


def _layer_params(l, norm1_g, q_norm_g, kv_norm_g, v_norm_g, b_s, out_g_attn,
                  out_g_gmlp, norm2_g, conv_w, conv_b, final_g):
    row = lambda a: a.reshape(1, -1)
    return {
        "norm1_g": row(norm1_g[l]), "q_norm_g": row(q_norm_g[l]),
        "kv_norm_g": row(kv_norm_g[l]), "v_norm_g": row(v_norm_g[l]), "b_st": b_s[l].T,
        "out_g_attn": row(out_g_attn[l]), "out_g_gmlp": row(out_g_gmlp[l]), "norm2_g": row(norm2_g[l]),
        "conv_w": conv_w[l].reshape(1, CONV_W * 2 * D_FF), "conv_b": row(conv_b[l]), "final_g": row(final_g),
    }


def kernel(x_prompt, x_sample, cache_ckv, cache_krope, c, c_ctx, ada_w, ada_b, norm1_g, w_in, q_norm_g, w_uq, kv_norm_g, w_ukv, v_norm_g, w_s, b_s, out_g_attn, out_g_gmlp, w_o, norm2_g, w_up, conv_w, conv_b, w_down, final_g):
    batch, seq, _ = x_prompt.shape
    dec_batch, dec_seq, _ = x_sample.shape
    depth, past = cache_ckv.shape[1], cache_ckv.shape[2]
    assert depth == 1 and dec_batch + 1 <= MOD_ROWS
    l = 0
    p = _layer_params(l, norm1_g, q_norm_g, kv_norm_g, v_norm_g, b_s, out_g_attn,
                      out_g_gmlp, norm2_g, conv_w, conv_b, final_g)

    mods3, p["w_qkv"], p["w_uv"], p["w_uq"], p["w_k"], p["w_ve"], p["w_vo"], w_s_b = _adaln(
        c_ctx.reshape(1, D_MODEL), c, ada_w[l], ada_b[l].reshape(1, -1), w_in[l].T, w_uq[l], w_ukv[l],
        w_s[l].reshape(GMLP_GROUPS * CHUNK, CHUNK))
    p["w_s"] = w_s_b.reshape(GMLP_GROUPS, CHUNK, CHUNK)
    tm_pre, tm_ffn = 1024, 512
    ctx_row = lambda i: 0
    lat_row = lambda tm: lambda i: 1 + i // (dec_seq // tm)

    xp = x_prompt.reshape(batch * seq, D_MODEL)
    xs = x_sample.reshape(dec_batch * dec_seq, D_MODEL)
    (qp, kp, vep, vop, gmp, ckv_p, kr_p), (w_down_b,) = _pre(xp, mods3, ctx_row, p, None, tm_pre, [w_down[l]])
    (qs, ks, ves, vos, gms, _, _), (w_o_b,) = _pre(xs, mods3, lat_row(tm_pre), p, _rope_tables(dec_seq), tm_pre,
                                                   [w_o[l]])
    kr_pad = jnp.pad(cache_krope[:, l].reshape(dec_batch * past, QK_ROPE),
                     ((0, 0), (ROPE_LANE0, LANES - ROPE_LANE0 - QK_ROPE)))
    cache = _expand(cache_ckv[:, l].reshape(dec_batch * past, KV_RANK), kr_pad, p, 512)

    anp, _ = _attention(qp, kp, vep, vop, None, p["out_g_attn"], seq, seq, 4, 4, [])
    ans, (w_up_b,) = _attention(qs, ks, ves, vos, cache, p["out_g_attn"], dec_seq, 512, 1, 4, [w_up[l]])
    ffn_w = (w_o_b, w_up_b, w_down_b)
    yp = _ffn(xp, anp, gmp, mods3, ctx_row, p, ffn_w, seq, tm_ffn)
    ys = _ffn(xs, ans, gms, mods3, lat_row(tm_ffn), p, ffn_w, dec_seq, tm_ffn)

    return (yp.reshape(batch, seq, D_MODEL), ys.reshape(dec_batch, dec_seq, D_MODEL),
            ckv_p.reshape(batch, 1, seq, KV_RANK), kr_p.reshape(batch, 1, seq, QK_ROPE))
```

```python
import functools

import numpy as np
import jax
import jax.numpy as jnp
from jax import lax
from jax.experimental import pallas as pl
from jax.experimental.pallas import tpu as pltpu

D_MODEL = 1024
GRID_W = 64
CHUNK = 128
N_HEADS = 8
QK_NOPE = 64
QK_ROPE = 32
V_DIM = 64
Q_RANK = 384
KV_RANK = 256
ATTN_W = N_HEADS * V_DIM
GMLP_W = D_MODEL - ATTN_W
GMLP_GROUPS = 4
GMLP_DG = GMLP_W // GMLP_GROUPS
D_FF = 2816
CONV_W = 3
ROPE_THETA = 10000.0
EPS = 1e-6
SM_SCALE = (QK_NOPE + QK_ROPE) ** -0.5
LOG2_E = 1.4426950408889634

LANES = 128
HEAD_PAD = LANES
ROPE_LANE0 = QK_NOPE
QK_W = N_HEADS * HEAD_PAD
N_MODS = 6
MOD_ROWS = 8
ADALN_STEPS = 8
IN_QKV = Q_RANK + KV_RANK
FF_CHUNK = 256
HALO = 16
N_FF_CHUNKS = D_FF // FF_CHUNK
A_SLOTS = 4
VMEM_LIMIT = 56 * 1024 * 1024

F32 = jnp.float32
BF16 = jnp.bfloat16


def _dot(a, b):
    return jnp.dot(a, b, preferred_element_type=F32)


def _dot_nt(a, b):
    return lax.dot_general(a, b, (((1,), (1,)), ((), ())), preferred_element_type=F32)


def _rms(x, g):
    return x * lax.rsqrt(jnp.mean(x * x, axis=-1, keepdims=True) + EPS) * g


def _gelu(x):
    return 0.5 * x * (1.0 + lax.erf(x * (0.5 ** 0.5)))


def _silu(x):
    return x * (1.0 / (1.0 + jnp.exp(-x)))


def _params(n_axes):
    return pltpu.CompilerParams(dimension_semantics=("arbitrary",) * n_axes,
                                vmem_limit_bytes=VMEM_LIMIT)


def _adaln_kernel(cctx_ref, c_ref, w_ref, b_ref, win_ref, wuq_ref, wukv_ref, ws_ref,
                  out_ref, wqkv_out, wuv_out, wuq_out, wk_out, wve_out, wvo_out, ws_out):
    pad = jnp.zeros((MOD_ROWS - 1 - c_ref.shape[0], D_MODEL), F32)
    cond = jnp.concatenate([cctx_ref[...], c_ref[...], pad], axis=0)
    s = _silu(cond).astype(BF16)
    out_ref[:, 0, :] = _dot(s, w_ref[...].astype(BF16)) + b_ref[...]
    @pl.when(pl.program_id(0) == 0)
    def _():
        kr = win_ref[IN_QKV:IN_QKV + QK_ROPE, :].astype(BF16)
        z = jnp.zeros_like(kr)
        wqkv_out[0:IN_QKV, :] = win_ref[0:IN_QKV, :].astype(BF16)
        for t, blk in enumerate((kr, z, kr, z)):
            wqkv_out[IN_QKV + t * QK_ROPE:IN_QKV + (t + 1) * QK_ROPE, :] = blk
        wuv_out[...] = win_ref[IN_QKV + QK_ROPE:, :].astype(BF16)

    w = wuq_ref[...]
    hw = QK_NOPE + QK_ROPE
    zq = jnp.zeros((w.shape[0], HEAD_PAD - hw), F32)
    wuq_out[...] = jnp.concatenate([blk for hd in range(N_HEADS) for blk in (w[:, hd * hw:(hd + 1) * hw], zq)],
                                   axis=1).astype(BF16)
    w = wukv_ref[...]
    zv = jnp.zeros((w.shape[0], V_DIM), F32)
    v_of = lambda hd: w[:, hd * HEAD_PAD + QK_NOPE:(hd + 1) * HEAD_PAD]
    wk_out[...] = jnp.concatenate([blk for hd in range(N_HEADS)
                                   for blk in (w[:, hd * HEAD_PAD:hd * HEAD_PAD + QK_NOPE], zv)], axis=1).astype(BF16)
    wve_out[...] = jnp.concatenate([blk for hd in range(0, N_HEADS, 2) for blk in (v_of(hd), zv)],
                                   axis=1).astype(BF16)
    wvo_out[...] = jnp.concatenate([blk for hd in range(1, N_HEADS, 2) for blk in (zv, v_of(hd))],
                                   axis=1).astype(BF16)
    ws_out[...] = ws_ref[...].astype(BF16)


def _adaln(c_ctx, c, ada_w, ada_b, w_in_t, w_uq, w_ukv, w_s):
    cols = N_MODS * D_MODEL // ADALN_STEPS
    row_block = lambda a, width=None: pl.BlockSpec((a.shape[0] // ADALN_STEPS, width or a.shape[1]),
                                                   lambda j: (j, 0))
    whole = lambda shape: pl.BlockSpec(shape, lambda j: (0, 0), pipeline_mode=pl.Buffered(1))
    return pl.pallas_call(
        _adaln_kernel,
        grid=(ADALN_STEPS,),
        in_specs=[
            pl.BlockSpec(c_ctx.shape, lambda j: (0, 0)),
            pl.BlockSpec(c.shape, lambda j: (0, 0)),
            pl.BlockSpec((D_MODEL, cols), lambda j: (0, j)),
            pl.BlockSpec((1, cols), lambda j: (0, j)),
            whole(w_in_t.shape), row_block(w_uq), row_block(w_ukv), row_block(w_s),
        ],
        out_specs=[pl.BlockSpec((MOD_ROWS, 1, cols), lambda j: (0, 0, j)),
                   whole((IN_QKV + LANES, D_MODEL)), whole((2 * GMLP_W, D_MODEL)),
                   row_block(w_uq, QK_W), row_block(w_ukv, QK_W), row_block(w_ukv, ATTN_W),
                   row_block(w_ukv, ATTN_W), row_block(w_s)],
        out_shape=[jax.ShapeDtypeStruct((MOD_ROWS, 1, N_MODS * D_MODEL), F32),
                   jax.ShapeDtypeStruct((IN_QKV + LANES, D_MODEL), BF16),
                   jax.ShapeDtypeStruct((2 * GMLP_W, D_MODEL), BF16),
                   jax.ShapeDtypeStruct((Q_RANK, QK_W), BF16),
                   jax.ShapeDtypeStruct((KV_RANK, QK_W), BF16),
                   jax.ShapeDtypeStruct((KV_RANK, ATTN_W), BF16),
                   jax.ShapeDtypeStruct((KV_RANK, ATTN_W), BF16),
                   jax.ShapeDtypeStruct(w_s.shape, BF16)],
        compiler_params=_params(1),
        name="adaln",
    )(c_ctx, c, ada_w, ada_b, w_in_t, w_uq, w_ukv, w_s)


def _pre_kernel(*refs, rope, tm, n_casts):
    n_main = 17 if rope else 15
    cast_in = refs[n_main:n_main + n_casts]
    cast_out = refs[n_main + n_casts + 7:n_main + 2 * n_casts + 7]
    q_out, k_out, ve_out, vo_out, gm_out, ckv_out, kr_out = refs[n_main + n_casts:n_main + n_casts + 7]
    go_sc = refs[-1]
    (x_ref, mod_ref, n1g_ref, wqkv_ref, wuv_ref, qng_ref, wuq_ref, kvg_ref, wk_ref, wve_ref, wvo_ref,
     vng_ref, ws_ref, bst_ref, ogg_ref) = refs[:15]
    if rope:
        cos_ref, sin_ref = refs[15:17]
    for src, dst in zip(cast_in, cast_out):
        dst[...] = src[...].astype(BF16)
    shift1 = mod_ref[:, 0:D_MODEL]
    scale1 = mod_ref[:, D_MODEL:2 * D_MODEL]
    h = (_rms(x_ref[...], n1g_ref[...]) * (1.0 + scale1) + shift1).astype(BF16)

    lane = lax.broadcasted_iota(jnp.int32, (1, LANES), 1)
    rope_lanes = lane >= ROPE_LANE0
    first_half = (lane % 16) < 8

    def rotate(v, cos):
        swapped = jnp.where(first_half, pltpu.roll(v, LANES - 8, 1), pltpu.roll(v, 8, 1))
        return v * cos + swapped * sin_ref[...]

    qkv = _dot_nt(h, wqkv_ref[...])
    qn = _rms(qkv[:, 0:Q_RANK], qng_ref[...]).astype(BF16)
    ckv = _rms(qkv[:, Q_RANK:IN_QKV], kvg_ref[...])
    ckv_out[...] = ckv
    ckv_b = ckv.astype(BF16)
    q = _dot(qn, wuq_ref[...]) * (SM_SCALE * LOG2_E)
    kr = qkv[:, IN_QKV:IN_QKV + LANES]
    kr_out[...] = kr[:, 0:QK_ROPE]
    if rope:
        cos_q = cos_ref[...]
        kr_k = rotate(kr, jnp.where(rope_lanes, cos_q, 0.0))
    else:
        kr_k = jnp.where(rope_lanes, kr, 0.0)
    kn = _dot(ckv_b, wk_ref[...])
    for hd in range(N_HEADS):
        sl = slice(hd * HEAD_PAD, (hd + 1) * HEAD_PAD)
        qh = q[:, sl]
        if rope:
            qh = rotate(qh, cos_q)
        q_out[:, sl] = qh.astype(BF16)
        k_out[:, sl] = (kn[:, sl] + kr_k).astype(BF16)
    ve_out[...] = _dot(ckv_b, wve_ref[...]).astype(BF16)
    vo_out[...] = _dot(ckv_b, wvo_ref[...]).astype(BF16)

    uv = _dot_nt(h, wuv_ref[...])
    gu = _gelu(uv[:, 0:GMLP_W])
    gv = _gelu(uv[:, GMLP_W:2 * GMLP_W])
    for g in range(GMLP_GROUPS):
        sl = slice(g * GMLP_DG, (g + 1) * GMLP_DG)
        vg = gv[:, sl]
        vg = (vg * lax.rsqrt(jnp.mean(vg * vg, axis=-1, keepdims=True) + EPS) * vng_ref[:, sl]).astype(BF16)
        bias = bst_ref[:, g:g + 1]
        chunks = [vg[n * CHUNK:(n + 1) * CHUNK, :] for n in range(tm // CHUNK)]
        s_all = _dot(ws_ref[g], jnp.concatenate(chunks, axis=1))
        for n in range(tm // CHUNK):
            rows = slice(n * CHUNK, (n + 1) * CHUNK)
            go_sc[rows, sl] = gu[rows, sl] * (s_all[:, n * GMLP_DG:(n + 1) * GMLP_DG] + bias)
    gm_out[...] = _rms(go_sc[...], ogg_ref[...]).astype(BF16)


def _pre(x, mods3, mod_row, p, rope_tabs, tm, casts):
    n = x.shape[0]
    rope = rope_tabs is not None
    const = lambda i: (0, 0)
    tile = lambda i: (i, 0)
    in_specs = [
        pl.BlockSpec((tm, D_MODEL), tile),
        pl.BlockSpec((None, 1, N_MODS * D_MODEL), lambda i: (mod_row(i), 0, 0)),
        pl.BlockSpec((1, D_MODEL), const),
        pl.BlockSpec((IN_QKV + LANES, D_MODEL), const),
        pl.BlockSpec((2 * GMLP_W, D_MODEL), const),
        pl.BlockSpec((1, Q_RANK), const),
        pl.BlockSpec((Q_RANK, QK_W), const),
        pl.BlockSpec((1, KV_RANK), const),
        pl.BlockSpec((KV_RANK, QK_W), const),
        pl.BlockSpec((KV_RANK, ATTN_W), const),
        pl.BlockSpec((KV_RANK, ATTN_W), const),
        pl.BlockSpec((1, GMLP_W), const),
        pl.BlockSpec((GMLP_GROUPS, CHUNK, CHUNK), lambda i: (0, 0, 0)),
        pl.BlockSpec((CHUNK, GMLP_GROUPS), const),
        pl.BlockSpec((1, GMLP_W), const),
    ]
    args = [x, mods3, p["norm1_g"], p["w_qkv"], p["w_uv"], p["q_norm_g"], p["w_uq"],
            p["kv_norm_g"], p["w_k"], p["w_ve"], p["w_vo"], p["v_norm_g"], p["w_s"], p["b_st"], p["out_g_gmlp"]]
    if rope:
        cos_q, sin_q = rope_tabs
        per_seq = cos_q.shape[0] // tm
        in_specs += [pl.BlockSpec((tm, LANES), lambda i: (i % per_seq, 0))] * 2
        args += [cos_q, sin_q]
    widths = [(QK_W, BF16), (QK_W, BF16), (ATTN_W, BF16), (ATTN_W, BF16), (GMLP_W, BF16),
              (KV_RANK, F32), (QK_ROPE, F32)]
    cast_specs = [pl.BlockSpec((w.shape[0] // (n // tm), w.shape[1]), tile) for w in casts]
    outs = pl.pallas_call(
        functools.partial(_pre_kernel, rope=rope, tm=tm, n_casts=len(casts)),
        grid=(n // tm,),
        in_specs=in_specs + cast_specs,
        out_specs=[pl.BlockSpec((tm, w), tile) for w, _ in widths] + cast_specs,
        out_shape=[jax.ShapeDtypeStruct((n, w), dt) for w, dt in widths]
        + [jax.ShapeDtypeStruct(w.shape, BF16) for w in casts],
        scratch_shapes=[pltpu.VMEM((tm, GMLP_W), F32)],
        compiler_params=_params(1),
        name="pre_rope" if rope else "pre",
    )(*args, *casts)
    return outs[:7], outs[7:]


def _expand_kernel(ckv_ref, kr_ref, wk_ref, wve_ref, wvo_ref, k_out, ve_out, vo_out):
    ckv_b = ckv_ref[...].astype(BF16)
    kn = _dot(ckv_b, wk_ref[...])
    kr = kr_ref[...]
    for hd in range(N_HEADS):
        sl = slice(hd * HEAD_PAD, (hd + 1) * HEAD_PAD)
        k_out[:, sl] = (kn[:, sl] + kr).astype(BF16)
    ve_out[...] = _dot(ckv_b, wve_ref[...]).astype(BF16)
    vo_out[...] = _dot(ckv_b, wvo_ref[...]).astype(BF16)


def _expand(ckv, kr_pad, p, tm):
    n = ckv.shape[0]
    const = lambda i: (0, 0)
    tile = lambda i: (i, 0)
    widths = [QK_W, ATTN_W, ATTN_W]
    return pl.pallas_call(
        _expand_kernel,
        grid=(n // tm,),
        in_specs=[
            pl.BlockSpec((tm, KV_RANK), tile),
            pl.BlockSpec((tm, LANES), tile),
            pl.BlockSpec((KV_RANK, QK_W), const),
            pl.BlockSpec((KV_RANK, ATTN_W), const),
            pl.BlockSpec((KV_RANK, ATTN_W), const),
        ],
        out_specs=[pl.BlockSpec((tm, w), tile) for w in widths],
        out_shape=[jax.ShapeDtypeStruct((n, w), BF16) for w in widths],
        compiler_params=_params(1),
        name="expand",
    )(ckv, kr_pad, p["w_k"], p["w_ve"], p["w_vo"])


def _attn_kernel(*refs, cached, group, seq, tq, pps, n_casts):
    n_main = 8 if cached else 5
    cast_in = refs[n_main:n_main + n_casts]
    out_ref = refs[n_main + n_casts]
    cast_out = refs[n_main + n_casts + 1:n_main + 2 * n_casts + 1]
    o_sc = refs[-1]
    if cached:
        q_ref, k_ref, ve_ref, vo_ref, kc_ref, vec_ref, voc_ref, oga_ref = refs[:n_main]
    else:
        q_ref, k_ref, ve_ref, vo_ref, oga_ref = refs[:n_main]
    for src, dst in zip(cast_in, cast_out):
        dst[...] = src[...].astype(BF16)
    step = pl.program_id(2)
    ones = jnp.ones((seq, LANES), BF16)
    ones_c = jnp.ones((kc_ref.shape[0], LANES), BF16) if cached else None
    for g in range(group):
        qrows = slice(g * tq, (g + 1) * tq)
        krows = slice(g * seq, (g + 1) * seq)
        for pp in range(pps):
            psl = slice(pp * LANES, (pp + 1) * LANES)
            acc = None
            for par, (v_ref, vc_ref) in enumerate(((ve_ref, vec_ref if cached else None),
                                                   (vo_ref, voc_ref if cached else None))):
                hd = 2 * pp + par
                hsl = slice(hd * HEAD_PAD, (hd + 1) * HEAD_PAD)
                qh = q_ref[qrows, hsl]
                s = _dot_nt(qh, k_ref[krows, hsl])
                m = jnp.max(s, axis=-1, keepdims=True)
                if cached:
                    sc = _dot_nt(qh, kc_ref[:, hsl])
                    m = jnp.maximum(m, jnp.max(sc, axis=-1, keepdims=True))
                e = jnp.exp2(s - m).astype(BF16)
                o = _dot(e, jnp.concatenate([v_ref[krows, psl], ones], axis=1))
                if cached:
                    ec = jnp.exp2(sc - m).astype(BF16)
                    o = o + _dot(ec, jnp.concatenate([vc_ref[:, psl], ones_c], axis=1))
                o = o[:, 0:LANES] * (1.0 / o[:, LANES:2 * LANES])
                acc = o if acc is None else acc + o
            o_sc[step * pps + pp, qrows, :] = acc

    @pl.when(step == N_HEADS // 2 // pps - 1)
    def _():
        blocks = [o_sc[j] for j in range(N_HEADS // 2)]
        ssq = sum(jnp.sum(b * b, axis=-1, keepdims=True) for b in blocks)
        r = lax.rsqrt(ssq * (1.0 / ATTN_W) + EPS)
        for j, b in enumerate(blocks):
            sl = slice(j * LANES, (j + 1) * LANES)
            out_ref[:, sl] = (b * r * oga_ref[:, sl]).astype(BF16)


def _attention(q, k, ve, vo, cache, oga, seq, tq, group, pps, casts):
    n = q.shape[0]
    nq = seq // tq
    n_pairs = N_HEADS // 2
    grid = (n // (group * seq), nq, n_pairs // pps)
    n_steps = grid[0] * grid[1] * grid[2]
    linear = lambda b, i, j: ((b * grid[1] + i) * grid[2] + j, 0)
    cast_specs = [pl.BlockSpec((w.shape[0] // n_steps, w.shape[1]), linear) for w in casts]
    qtile = lambda b, i, j: (b * nq + i, j)
    kv = lambda b, i, j: (b, j)
    qk_w, v_w = pps * 2 * HEAD_PAD, pps * LANES
    in_specs = [
        pl.BlockSpec((group * tq, qk_w), qtile),
        pl.BlockSpec((group * seq, qk_w), kv),
        pl.BlockSpec((group * seq, v_w), kv),
        pl.BlockSpec((group * seq, v_w), kv),
    ]
    args = [q, k, ve, vo]
    if cache is not None:
        kc, vec, voc = cache
        past = kc.shape[0] // (n // seq)
        in_specs += [pl.BlockSpec((past, qk_w), kv), pl.BlockSpec((past, v_w), kv),
                     pl.BlockSpec((past, v_w), kv)]
        args += [kc, vec, voc]
    in_specs.append(pl.BlockSpec((1, ATTN_W), lambda b, i, j: (0, 0)))
    args.append(oga)
    outs = pl.pallas_call(
        functools.partial(_attn_kernel, cached=cache is not None, group=group, seq=seq, tq=tq, pps=pps,
                          n_casts=len(casts)),
        grid=grid,
        in_specs=in_specs + cast_specs,
        out_specs=[pl.BlockSpec((group * tq, ATTN_W), lambda b, i, j: (b * nq + i, 0))] + cast_specs,
        out_shape=[jax.ShapeDtypeStruct((n, ATTN_W), BF16)]
        + [jax.ShapeDtypeStruct(w.shape, BF16) for w in casts],
        scratch_shapes=[pltpu.VMEM((n_pairs, group * tq, LANES), F32)],
        compiler_params=_params(3),
        name="attn_cached" if cache is not None else "attn",
    )(*args, *casts)
    return outs[0], outs[1:]


def _ffn_kernel(*refs, nseg, seg, tiles_per_seq):
    halo = tiles_per_seq > 1
    if halo:
        (x_ref, an_ref, gm_ref, xp_ref, xn_ref, anp_ref, ann_ref, gmp_ref, gmn_ref, mod_ref, wo_ref, n2g_ref,
         wup_ref, cw_ref, cb_ref, wd_ref, fg_ref, out_ref, lhs_sc, a_sc, g_sc, mix_sc) = refs
    else:
        (x_ref, an_ref, gm_ref, mod_ref, wo_ref, n2g_ref,
         wup_ref, cw_ref, cb_ref, wd_ref, fg_ref, out_ref, lhs_sc, a_sc, g_sc) = refs
    stride = seg + HALO
    rows = HALO + nseg * stride
    gate1 = mod_ref[:, 2 * D_MODEL:3 * D_MODEL]
    shift2 = mod_ref[:, 3 * D_MODEL:4 * D_MODEL]
    scale2 = mod_ref[:, 4 * D_MODEL:5 * D_MODEL]

    def residual_norm(x, y):
        x1 = x + gate1 * y
        return x1, (_rms(x1, n2g_ref[...]) * (1.0 + scale2) + shift2).astype(BF16)

    zeros = jnp.zeros((HALO, D_MODEL), BF16)
    if halo:
        mix_sc[0:HALO, 0:ATTN_W] = anp_ref[...]
        mix_sc[0:HALO, ATTN_W:D_MODEL] = gmp_ref[...]
        mix_sc[HALO:HALO + seg, 0:ATTN_W] = an_ref[...]
        mix_sc[HALO:HALO + seg, ATTN_W:D_MODEL] = gm_ref[...]
        mix_sc[HALO + seg:rows, 0:ATTN_W] = ann_ref[...]
        mix_sc[HALO + seg:rows, ATTN_W:D_MODEL] = gmn_ref[...]
        y = _dot(mix_sc[...], wo_ref[...])
        x1, h2 = residual_norm(x_ref[...], y[HALO:HALO + seg])
        out_ref[...] = x1
        lhs_sc[HALO:HALO + seg, :] = h2
        pos = pl.program_id(0) % tiles_per_seq
        lhs_sc[0:HALO, :] = zeros
        lhs_sc[HALO + seg:rows, :] = zeros

        @pl.when(pos != 0)
        def _():
            lhs_sc[0:HALO, :] = residual_norm(xp_ref[...], y[0:HALO])[1]

        @pl.when(pos != tiles_per_seq - 1)
        def _():
            lhs_sc[HALO + seg:rows, :] = residual_norm(xn_ref[...], y[HALO + seg:rows])[1]
    else:
        y = _dot(an_ref[...], wo_ref[0:ATTN_W, :]) + _dot(gm_ref[...], wo_ref[ATTN_W:D_MODEL, :])
        x1, h2 = residual_norm(x_ref[...], y)
        out_ref[...] = x1
        lhs_sc[0:HALO, :] = zeros
        for s in range(nseg):
            base = HALO + s * stride
            lhs_sc[base:base + seg, :] = h2[s * seg:(s + 1) * seg, :]
            lhs_sc[base + seg:base + stride, :] = zeros

    for c in range(N_FF_CHUNKS):
        slot = c % A_SLOTS
        for k, col0 in enumerate((c * FF_CHUNK, D_FF + c * FF_CHUNK)):
            a = _dot(lhs_sc[...], wup_ref[:, col0:col0 + FF_CHUNK])
            for j in range(FF_CHUNK // LANES):
                a_sc[slot, 2 * k + j, :, :] = a[8:rows - 8, j * LANES:(j + 1) * LANES]
        for j in range(FF_CHUNK // LANES):
            gl = slice(c * FF_CHUNK + j * LANES, c * FF_CHUNK + (j + 1) * LANES)
            vl = slice(D_FF + gl.start, D_FF + gl.stop)
            for s in range(nseg):
                base = 8 + s * stride

                def conv(k, lanes):
                    acc = cb_ref[:, lanes]
                    for t in range(CONV_W):
                        tap = slice(t * 2 * D_FF + lanes.start, t * 2 * D_FF + lanes.stop)
                        acc = acc + a_sc[slot, k, base - 1 + t:base - 1 + t + seg, :] * cw_ref[:, tap]
                    return acc

                g = _silu(conv(j, gl)) * conv(2 + j, vl)
                g_sc[s * seg:(s + 1) * seg, gl] = g.astype(BF16)

    x2 = out_ref[...] + mod_ref[:, 5 * D_MODEL:6 * D_MODEL] * _dot(g_sc[...], wd_ref[...])
    out_ref[...] = _rms(x2, fg_ref[...])


def _ffn(x, an, gm, mods3, mod_row, p, ffn_w, seq, tm):
    n = x.shape[0]
    tiles_per_seq = max(seq // tm, 1)
    seg = min(seq, tm)
    nseg = tm // seg
    rows = HALO + nseg * (seg + HALO)
    const = lambda i: (0, 0)
    tile = lambda i: (i, 0)
    resident = dict(pipeline_mode=pl.Buffered(1))
    in_specs = [pl.BlockSpec((tm, D_MODEL), tile), pl.BlockSpec((tm, ATTN_W), tile),
                pl.BlockSpec((tm, GMLP_W), tile)]
    args = [x, an, gm]
    scratch = [pltpu.VMEM((rows, D_MODEL), BF16),
               pltpu.VMEM((A_SLOTS, 2 * FF_CHUNK // LANES, rows - 16, LANES), F32),
               pltpu.VMEM((tm, D_FF), BF16)]
    if tiles_per_seq > 1:
        per = tm // HALO
        last = n // HALO - 1
        prev = lambda i: (jnp.maximum(i * per - 1, 0), 0)
        nxt = lambda i: (jnp.minimum((i + 1) * per, last), 0)
        for arr, width in ((x, D_MODEL), (an, ATTN_W), (gm, GMLP_W)):
            in_specs += [pl.BlockSpec((HALO, width), prev), pl.BlockSpec((HALO, width), nxt)]
            args += [arr, arr]
        scratch.append(pltpu.VMEM((rows, D_MODEL), BF16))
    in_specs += [
        pl.BlockSpec((None, 1, N_MODS * D_MODEL), lambda i: (mod_row(i), 0, 0)),
        pl.BlockSpec((D_MODEL, D_MODEL), const, **resident),
        pl.BlockSpec((1, D_MODEL), const),
        pl.BlockSpec((D_MODEL, 2 * D_FF), const, **resident),
        pl.BlockSpec((1, CONV_W * 2 * D_FF), const),
        pl.BlockSpec((1, 2 * D_FF), const),
        pl.BlockSpec((D_FF, D_MODEL), const, **resident),
        pl.BlockSpec((1, D_MODEL), const),
    ]
    w_o, w_up, w_down = ffn_w
    args += [mods3, w_o, p["norm2_g"], w_up, p["conv_w"], p["conv_b"], w_down, p["final_g"]]
    return pl.pallas_call(
        functools.partial(_ffn_kernel, nseg=nseg, seg=seg, tiles_per_seq=tiles_per_seq),
        grid=(n // tm,),
        in_specs=in_specs,
        out_specs=pl.BlockSpec((tm, D_MODEL), tile),
        out_shape=jax.ShapeDtypeStruct((n, D_MODEL), F32),
        scratch_shapes=scratch,
        compiler_params=_params(1),
        name="ffn_halo" if tiles_per_seq > 1 else "ffn",
    )(*args)


def _rope_tables(length):
    pos = np.arange(length)
    row = (pos // GRID_W).astype(np.float32)
    col = (pos % GRID_W).astype(np.float32)
    n_freq = QK_ROPE // 4
    inv = (np.float32(ROPE_THETA) ** (-(np.arange(n_freq, dtype=np.float32) / np.float32(n_freq)))).astype(np.float32)
    ang_r, ang_c = row[:, None] * inv, col[:, None] * inv
    cos32 = np.concatenate([np.cos(ang_r)] * 2 + [np.cos(ang_c)] * 2, axis=-1)
    sin32 = np.concatenate([-np.sin(ang_r), np.sin(ang_r), -np.sin(ang_c), np.sin(ang_c)], axis=-1)
    tail = LANES - ROPE_LANE0 - QK_ROPE
    cos_q = np.concatenate([np.ones((length, ROPE_LANE0)), cos32, np.zeros((length, tail))], -1)
    sin_q = np.concatenate([np.zeros((length, ROPE_LANE0)), sin32, np.zeros((length, tail))], -1)
    return jnp.asarray(cos_q, F32), jnp.asarray(sin_q, F32)


def _layer_params(l, norm1_g, q_norm_g, kv_norm_g, v_norm_g, b_s, out_g_attn,
                  out_g_gmlp, norm2_g, conv_w, conv_b, final_g):
    row = lambda a: a.reshape(1, -1)
    return {
        "norm1_g": row(norm1_g[l]), "q_norm_g": row(q_norm_g[l]),
        "kv_norm_g": row(kv_norm_g[l]), "v_norm_g": row(v_norm_g[l]), "b_st": b_s[l].T,
        "out_g_attn": row(out_g_attn[l]), "out_g_gmlp": row(out_g_gmlp[l]), "norm2_g": row(norm2_g[l]),
        "conv_w": conv_w[l].reshape(1, CONV_W * 2 * D_FF), "conv_b": row(conv_b[l]), "final_g": row(final_g),
    }


def kernel(x_prompt, x_sample, cache_ckv, cache_krope, c, c_ctx, ada_w, ada_b, norm1_g, w_in, q_norm_g, w_uq, kv_norm_g, w_ukv, v_norm_g, w_s, b_s, out_g_attn, out_g_gmlp, w_o, norm2_g, w_up, conv_w, conv_b, w_down, final_g):
    batch, seq, _ = x_prompt.shape
    dec_batch, dec_seq, _ = x_sample.shape
    depth, past = cache_ckv.shape[1], cache_ckv.shape[2]
    assert depth == 1 and dec_batch + 1 <= MOD_ROWS
    l = 0
    p = _layer_params(l, norm1_g, q_norm_g, kv_norm_g, v_norm_g, b_s, out_g_attn,
                      out_g_gmlp, norm2_g, conv_w, conv_b, final_g)

    mods3, p["w_qkv"], p["w_uv"], p["w_uq"], p["w_k"], p["w_ve"], p["w_vo"], w_s_b = _adaln(
        c_ctx.reshape(1, D_MODEL), c, ada_w[l], ada_b[l].reshape(1, -1), w_in[l].T, w_uq[l], w_ukv[l],
        w_s[l].reshape(GMLP_GROUPS * CHUNK, CHUNK))
    p["w_s"] = w_s_b.reshape(GMLP_GROUPS, CHUNK, CHUNK)
    tm_pre, tm_ffn = 1024, 512
    ctx_row = lambda i: 0
    lat_row = lambda tm: lambda i: 1 + i // (dec_seq // tm)

    xp = x_prompt.reshape(batch * seq, D_MODEL)
    xs = x_sample.reshape(dec_batch * dec_seq, D_MODEL)
    (qp, kp, vep, vop, gmp, ckv_p, kr_p), (w_down_b,) = _pre(xp, mods3, ctx_row, p, None, tm_pre, [w_down[l]])
    (qs, ks, ves, vos, gms, _, _), (w_o_b,) = _pre(xs, mods3, lat_row(tm_pre), p, _rope_tables(dec_seq), tm_pre,
                                                   [w_o[l]])
    kr_pad = jnp.pad(cache_krope[:, l].reshape(dec_batch * past, QK_ROPE),
                     ((0, 0), (ROPE_LANE0, LANES - ROPE_LANE0 - QK_ROPE)))
    cache = _expand(cache_ckv[:, l].reshape(dec_batch * past, KV_RANK), kr_pad, p, 512)

    anp, _ = _attention(qp, kp, vep, vop, None, p["out_g_attn"], seq, seq, 4, 4, [])
    ans, (w_up_b,) = _attention(qs, ks, ves, vos, cache, p["out_g_attn"], dec_seq, 512, 1, 4, [w_up[l]])
    ffn_w = (w_o_b, w_up_b, w_down_b)
    yp = _ffn(xp, anp, gmp, mods3, ctx_row, p, ffn_w, seq, tm_ffn)
    ys = _ffn(xs, ans, gms, mods3, lat_row(tm_ffn), p, ffn_w, dec_seq, tm_ffn)

    return (yp.reshape(batch, seq, D_MODEL), ys.reshape(dec_batch, dec_seq, D_MODEL),
            ckv_p.reshape(batch, 1, seq, KV_RANK), kr_p.reshape(batch, 1, seq, QK_ROPE))
```

```python
import functools

import numpy as np
import jax
import jax.numpy as jnp
from jax import lax
from jax.experimental import pallas as pl
from jax.experimental.pallas import tpu as pltpu

D_MODEL = 1024
GRID_W = 64
CHUNK = 128
N_HEADS = 8
QK_NOPE = 64
QK_ROPE = 32
V_DIM = 64
Q_RANK = 384
KV_RANK = 256
ATTN_W = N_HEADS * V_DIM
GMLP_W = D_MODEL - ATTN_W
GMLP_GROUPS = 4
GMLP_DG = GMLP_W // GMLP_GROUPS
D_FF = 2816
CONV_W = 3
ROPE_THETA = 10000.0
EPS = 1e-6
SM_SCALE = (QK_NOPE + QK_ROPE) ** -0.5
LOG2_E = 1.4426950408889634

LANES = 128
HEAD_PAD = LANES
ROPE_LANE0 = QK_NOPE
QK_W = N_HEADS * HEAD_PAD
N_MODS = 6
MOD_ROWS = 8
ADALN_STEPS = 8
IN_QKV = Q_RANK + KV_RANK
FF_CHUNK = 256
HALO = 16
N_FF_CHUNKS = D_FF // FF_CHUNK
A_SLOTS = 6
VMEM_LIMIT = 56 * 1024 * 1024

F32 = jnp.float32
BF16 = jnp.bfloat16


def _dot(a, b):
    return jnp.dot(a, b, preferred_element_type=F32)


def _dot_nt(a, b):
    return lax.dot_general(a, b, (((1,), (1,)), ((), ())), preferred_element_type=F32)


def _rms(x, g):
    return x * lax.rsqrt(jnp.mean(x * x, axis=-1, keepdims=True) + EPS) * g


def _gelu(x):
    return 0.5 * x * (1.0 + lax.erf(x * (0.5 ** 0.5)))


def _silu(x):
    return x * (1.0 / (1.0 + jnp.exp(-x)))


def _params(n_axes):
    return pltpu.CompilerParams(dimension_semantics=("arbitrary",) * n_axes,
                                vmem_limit_bytes=VMEM_LIMIT)


def _adaln_kernel(cctx_ref, c_ref, w_ref, b_ref, win_ref, wuq_ref, wukv_ref, ws_ref,
                  out_ref, wqkv_out, wuv_out, wuq_out, wk_out, wve_out, wvo_out, ws_out):
    pad = jnp.zeros((MOD_ROWS - 1 - c_ref.shape[0], D_MODEL), F32)
    cond = jnp.concatenate([cctx_ref[...], c_ref[...], pad], axis=0)
    s = _silu(cond).astype(BF16)
    out_ref[:, 0, :] = _dot(s, w_ref[...].astype(BF16)) + b_ref[...]
    @pl.when(pl.program_id(0) == 0)
    def _():
        kr = win_ref[IN_QKV:IN_QKV + QK_ROPE, :].astype(BF16)
        z = jnp.zeros_like(kr)
        wqkv_out[0:IN_QKV, :] = win_ref[0:IN_QKV, :].astype(BF16)
        for t, blk in enumerate((kr, z, kr, z)):
            wqkv_out[IN_QKV + t * QK_ROPE:IN_QKV + (t + 1) * QK_ROPE, :] = blk
        wuv_out[...] = win_ref[IN_QKV + QK_ROPE:, :].astype(BF16)

    w = wuq_ref[...]
    hw = QK_NOPE + QK_ROPE
    zq = jnp.zeros((w.shape[0], HEAD_PAD - hw), F32)
    wuq_out[...] = jnp.concatenate([blk for hd in range(N_HEADS) for blk in (w[:, hd * hw:(hd + 1) * hw], zq)],
                                   axis=1).astype(BF16)
    w = wukv_ref[...]
    zv = jnp.zeros((w.shape[0], V_DIM), F32)
    v_of = lambda hd: w[:, hd * HEAD_PAD + QK_NOPE:(hd + 1) * HEAD_PAD]
    wk_out[...] = jnp.concatenate([blk for hd in range(N_HEADS)
                                   for blk in (w[:, hd * HEAD_PAD:hd * HEAD_PAD + QK_NOPE], zv)], axis=1).astype(BF16)
    wve_out[...] = jnp.concatenate([blk for hd in range(0, N_HEADS, 2) for blk in (v_of(hd), zv)],
                                   axis=1).astype(BF16)
    wvo_out[...] = jnp.concatenate([blk for hd in range(1, N_HEADS, 2) for blk in (zv, v_of(hd))],
                                   axis=1).astype(BF16)
    ws_out[...] = ws_ref[...].astype(BF16)


def _adaln(c_ctx, c, ada_w, ada_b, w_in_t, w_uq, w_ukv, w_s):
    cols = N_MODS * D_MODEL // ADALN_STEPS
    row_block = lambda a, width=None: pl.BlockSpec((a.shape[0] // ADALN_STEPS, width or a.shape[1]),
                                                   lambda j: (j, 0))
    whole = lambda shape: pl.BlockSpec(shape, lambda j: (0, 0), pipeline_mode=pl.Buffered(1))
    return pl.pallas_call(
        _adaln_kernel,
        grid=(ADALN_STEPS,),
        in_specs=[
            pl.BlockSpec(c_ctx.shape, lambda j: (0, 0)),
            pl.BlockSpec(c.shape, lambda j: (0, 0)),
            pl.BlockSpec((D_MODEL, cols), lambda j: (0, j)),
            pl.BlockSpec((1, cols), lambda j: (0, j)),
            whole(w_in_t.shape), row_block(w_uq), row_block(w_ukv), row_block(w_s),
        ],
        out_specs=[pl.BlockSpec((MOD_ROWS, 1, cols), lambda j: (0, 0, j)),
                   whole((IN_QKV + LANES, D_MODEL)), whole((2 * GMLP_W, D_MODEL)),
                   row_block(w_uq, QK_W), row_block(w_ukv, QK_W), row_block(w_ukv, ATTN_W),
                   row_block(w_ukv, ATTN_W), row_block(w_s)],
        out_shape=[jax.ShapeDtypeStruct((MOD_ROWS, 1, N_MODS * D_MODEL), F32),
                   jax.ShapeDtypeStruct((IN_QKV + LANES, D_MODEL), BF16),
                   jax.ShapeDtypeStruct((2 * GMLP_W, D_MODEL), BF16),
                   jax.ShapeDtypeStruct((Q_RANK, QK_W), BF16),
                   jax.ShapeDtypeStruct((KV_RANK, QK_W), BF16),
                   jax.ShapeDtypeStruct((KV_RANK, ATTN_W), BF16),
                   jax.ShapeDtypeStruct((KV_RANK, ATTN_W), BF16),
                   jax.ShapeDtypeStruct(w_s.shape, BF16)],
        compiler_params=_params(1),
        name="adaln",
    )(c_ctx, c, ada_w, ada_b, w_in_t, w_uq, w_ukv, w_s)


def _pre_kernel(*refs, rope, tm, n_casts):
    n_main = 17 if rope else 15
    cast_in = refs[n_main:n_main + n_casts]
    cast_out = refs[n_main + n_casts + 7:n_main + 2 * n_casts + 7]
    q_out, k_out, ve_out, vo_out, gm_out, ckv_out, kr_out = refs[n_main + n_casts:n_main + n_casts + 7]
    go_sc = refs[-1]
    (x_ref, mod_ref, n1g_ref, wqkv_ref, wuv_ref, qng_ref, wuq_ref, kvg_ref, wk_ref, wve_ref, wvo_ref,
     vng_ref, ws_ref, bst_ref, ogg_ref) = refs[:15]
    if rope:
        cos_ref, sin_ref = refs[15:17]
    for src, dst in zip(cast_in, cast_out):
        dst[...] = src[...].astype(BF16)
    shift1 = mod_ref[:, 0:D_MODEL]
    scale1 = mod_ref[:, D_MODEL:2 * D_MODEL]
    h = (_rms(x_ref[...], n1g_ref[...]) * (1.0 + scale1) + shift1).astype(BF16)

    lane = lax.broadcasted_iota(jnp.int32, (1, LANES), 1)
    rope_lanes = lane >= ROPE_LANE0
    first_half = (lane % 16) < 8

    def rotate(v, cos):
        swapped = jnp.where(first_half, pltpu.roll(v, LANES - 8, 1), pltpu.roll(v, 8, 1))
        return v * cos + swapped * sin_ref[...]

    qkv = _dot_nt(h, wqkv_ref[...])
    qn = _rms(qkv[:, 0:Q_RANK], qng_ref[...]).astype(BF16)
    ckv = _rms(qkv[:, Q_RANK:IN_QKV], kvg_ref[...])
    ckv_out[...] = ckv
    ckv_b = ckv.astype(BF16)
    q = _dot(qn, wuq_ref[...]) * (SM_SCALE * LOG2_E)
    kr = qkv[:, IN_QKV:IN_QKV + LANES]
    kr_out[...] = kr[:, 0:QK_ROPE]
    if rope:
        cos_q = cos_ref[...]
        kr_k = rotate(kr, jnp.where(rope_lanes, cos_q, 0.0))
    else:
        kr_k = jnp.where(rope_lanes, kr, 0.0)
    kn = _dot(ckv_b, wk_ref[...])
    for hd in range(N_HEADS):
        sl = slice(hd * HEAD_PAD, (hd + 1) * HEAD_PAD)
        qh = q[:, sl]
        if rope:
            qh = rotate(qh, cos_q)
        q_out[:, sl] = qh.astype(BF16)
        k_out[:, sl] = (kn[:, sl] + kr_k).astype(BF16)
    ve_out[...] = _dot(ckv_b, wve_ref[...]).astype(BF16)
    vo_out[...] = _dot(ckv_b, wvo_ref[...]).astype(BF16)

    uv = _dot_nt(h, wuv_ref[...])
    gu = _gelu(uv[:, 0:GMLP_W])
    gv = _gelu(uv[:, GMLP_W:2 * GMLP_W])
    for g in range(GMLP_GROUPS):
        sl = slice(g * GMLP_DG, (g + 1) * GMLP_DG)
        vg = gv[:, sl]
        vg = (vg * lax.rsqrt(jnp.mean(vg * vg, axis=-1, keepdims=True) + EPS) * vng_ref[:, sl]).astype(BF16)
        bias = bst_ref[:, g:g + 1]
        chunks = [vg[n * CHUNK:(n + 1) * CHUNK, :] for n in range(tm // CHUNK)]
        s_all = _dot(ws_ref[g], jnp.concatenate(chunks, axis=1))
        for n in range(tm // CHUNK):
            rows = slice(n * CHUNK, (n + 1) * CHUNK)
            go_sc[rows, sl] = gu[rows, sl] * (s_all[:, n * GMLP_DG:(n + 1) * GMLP_DG] + bias)
    gm_out[...] = _rms(go_sc[...], ogg_ref[...]).astype(BF16)


def _pre(x, mods3, mod_row, p, rope_tabs, tm, casts):
    n = x.shape[0]
    rope = rope_tabs is not None
    const = lambda i: (0, 0)
    tile = lambda i: (i, 0)
    in_specs = [
        pl.BlockSpec((tm, D_MODEL), tile),
        pl.BlockSpec((None, 1, N_MODS * D_MODEL), lambda i: (mod_row(i), 0, 0)),
        pl.BlockSpec((1, D_MODEL), const),
        pl.BlockSpec((IN_QKV + LANES, D_MODEL), const),
        pl.BlockSpec((2 * GMLP_W, D_MODEL), const),
        pl.BlockSpec((1, Q_RANK), const),
        pl.BlockSpec((Q_RANK, QK_W), const),
        pl.BlockSpec((1, KV_RANK), const),
        pl.BlockSpec((KV_RANK, QK_W), const),
        pl.BlockSpec((KV_RANK, ATTN_W), const),
        pl.BlockSpec((KV_RANK, ATTN_W), const),
        pl.BlockSpec((1, GMLP_W), const),
        pl.BlockSpec((GMLP_GROUPS, CHUNK, CHUNK), lambda i: (0, 0, 0)),
        pl.BlockSpec((CHUNK, GMLP_GROUPS), const),
        pl.BlockSpec((1, GMLP_W), const),
    ]
    args = [x, mods3, p["norm1_g"], p["w_qkv"], p["w_uv"], p["q_norm_g"], p["w_uq"],
            p["kv_norm_g"], p["w_k"], p["w_ve"], p["w_vo"], p["v_norm_g"], p["w_s"], p["b_st"], p["out_g_gmlp"]]
    if rope:
        cos_q, sin_q = rope_tabs
        per_seq = cos_q.shape[0] // tm
        in_specs += [pl.BlockSpec((tm, LANES), lambda i: (i % per_seq, 0))] * 2
        args += [cos_q, sin_q]
    widths = [(QK_W, BF16), (QK_W, BF16), (ATTN_W, BF16), (ATTN_W, BF16), (GMLP_W, BF16),
              (KV_RANK, F32), (QK_ROPE, F32)]
    cast_specs = [pl.BlockSpec((w.shape[0] // (n // tm), w.shape[1]), tile) for w in casts]
    outs = pl.pallas_call(
        functools.partial(_pre_kernel, rope=rope, tm=tm, n_casts=len(casts)),
        grid=(n // tm,),
        in_specs=in_specs + cast_specs,
        out_specs=[pl.BlockSpec((tm, w), tile) for w, _ in widths] + cast_specs,
        out_shape=[jax.ShapeDtypeStruct((n, w), dt) for w, dt in widths]
        + [jax.ShapeDtypeStruct(w.shape, BF16) for w in casts],
        scratch_shapes=[pltpu.VMEM((tm, GMLP_W), F32)],
        compiler_params=_params(1),
        name="pre_rope" if rope else "pre",
    )(*args, *casts)
    return outs[:7], outs[7:]


def _expand_kernel(ckv_ref, kr_ref, wk_ref, wve_ref, wvo_ref, k_out, ve_out, vo_out):
    ckv_b = ckv_ref[...].astype(BF16)
    kn = _dot(ckv_b, wk_ref[...])
    kr = kr_ref[...]
    for hd in range(N_HEADS):
        sl = slice(hd * HEAD_PAD, (hd + 1) * HEAD_PAD)
        k_out[:, sl] = (kn[:, sl] + kr).astype(BF16)
    ve_out[...] = _dot(ckv_b, wve_ref[...]).astype(BF16)
    vo_out[...] = _dot(ckv_b, wvo_ref[...]).astype(BF16)


def _expand(ckv, kr_pad, p, tm):
    n = ckv.shape[0]
    const = lambda i: (0, 0)
    tile = lambda i: (i, 0)
    widths = [QK_W, ATTN_W, ATTN_W]
    return pl.pallas_call(
        _expand_kernel,
        grid=(n // tm,),
        in_specs=[
            pl.BlockSpec((tm, KV_RANK), tile),
            pl.BlockSpec((tm, LANES), tile),
            pl.BlockSpec((KV_RANK, QK_W), const),
            pl.BlockSpec((KV_RANK, ATTN_W), const),
            pl.BlockSpec((KV_RANK, ATTN_W), const),
        ],
        out_specs=[pl.BlockSpec((tm, w), tile) for w in widths],
        out_shape=[jax.ShapeDtypeStruct((n, w), BF16) for w in widths],
        compiler_params=_params(1),
        name="expand",
    )(ckv, kr_pad, p["w_k"], p["w_ve"], p["w_vo"])


def _attn_kernel(*refs, cached, group, seq, tq, pps, n_casts):
    n_main = 8 if cached else 5
    cast_in = refs[n_main:n_main + n_casts]
    out_ref = refs[n_main + n_casts]
    cast_out = refs[n_main + n_casts + 1:n_main + 2 * n_casts + 1]
    o_sc = refs[-1]
    if cached:
        q_ref, k_ref, ve_ref, vo_ref, kc_ref, vec_ref, voc_ref, oga_ref = refs[:n_main]
    else:
        q_ref, k_ref, ve_ref, vo_ref, oga_ref = refs[:n_main]
    for src, dst in zip(cast_in, cast_out):
        dst[...] = src[...].astype(BF16)
    step = pl.program_id(2)
    ones = jnp.ones((seq, LANES), BF16)
    ones_c = jnp.ones((kc_ref.shape[0], LANES), BF16) if cached else None
    for g in range(group):
        qrows = slice(g * tq, (g + 1) * tq)
        krows = slice(g * seq, (g + 1) * seq)
        for pp in range(pps):
            psl = slice(pp * LANES, (pp + 1) * LANES)
            acc = None
            for par, (v_ref, vc_ref) in enumerate(((ve_ref, vec_ref if cached else None),
                                                   (vo_ref, voc_ref if cached else None))):
                hd = 2 * pp + par
                hsl = slice(hd * HEAD_PAD, (hd + 1) * HEAD_PAD)
                qh = q_ref[qrows, hsl]
                s = _dot_nt(qh, k_ref[krows, hsl])
                m = jnp.max(s, axis=-1, keepdims=True)
                if cached:
                    sc = _dot_nt(qh, kc_ref[:, hsl])
                    m = jnp.maximum(m, jnp.max(sc, axis=-1, keepdims=True))
                e = jnp.exp2(s - m).astype(BF16)
                o = _dot(e, jnp.concatenate([v_ref[krows, psl], ones], axis=1))
                if cached:
                    ec = jnp.exp2(sc - m).astype(BF16)
                    o = o + _dot(ec, jnp.concatenate([vc_ref[:, psl], ones_c], axis=1))
                o = o[:, 0:LANES] * (1.0 / o[:, LANES:2 * LANES])
                acc = o if acc is None else acc + o
            o_sc[step * pps + pp, qrows, :] = acc

    @pl.when(step == N_HEADS // 2 // pps - 1)
    def _():
        blocks = [o_sc[j] for j in range(N_HEADS // 2)]
        ssq = sum(jnp.sum(b * b, axis=-1, keepdims=True) for b in blocks)
        r = lax.rsqrt(ssq * (1.0 / ATTN_W) + EPS)
        for j, b in enumerate(blocks):
            sl = slice(j * LANES, (j + 1) * LANES)
            out_ref[:, sl] = (b * r * oga_ref[:, sl]).astype(BF16)


def _attention(q, k, ve, vo, cache, oga, seq, tq, group, pps, casts):
    n = q.shape[0]
    nq = seq // tq
    n_pairs = N_HEADS // 2
    grid = (n // (group * seq), nq, n_pairs // pps)
    n_steps = grid[0] * grid[1] * grid[2]
    linear = lambda b, i, j: ((b * grid[1] + i) * grid[2] + j, 0)
    cast_specs = [pl.BlockSpec((w.shape[0] // n_steps, w.shape[1]), linear) for w in casts]
    qtile = lambda b, i, j: (b * nq + i, j)
    kv = lambda b, i, j: (b, j)
    qk_w, v_w = pps * 2 * HEAD_PAD, pps * LANES
    in_specs = [
        pl.BlockSpec((group * tq, qk_w), qtile),
        pl.BlockSpec((group * seq, qk_w), kv),
        pl.BlockSpec((group * seq, v_w), kv),
        pl.BlockSpec((group * seq, v_w), kv),
    ]
    args = [q, k, ve, vo]
    if cache is not None:
        kc, vec, voc = cache
        past = kc.shape[0] // (n // seq)
        in_specs += [pl.BlockSpec((past, qk_w), kv), pl.BlockSpec((past, v_w), kv),
                     pl.BlockSpec((past, v_w), kv)]
        args += [kc, vec, voc]
    in_specs.append(pl.BlockSpec((1, ATTN_W), lambda b, i, j: (0, 0)))
    args.append(oga)
    outs = pl.pallas_call(
        functools.partial(_attn_kernel, cached=cache is not None, group=group, seq=seq, tq=tq, pps=pps,
                          n_casts=len(casts)),
        grid=grid,
        in_specs=in_specs + cast_specs,
        out_specs=[pl.BlockSpec((group * tq, ATTN_W), lambda b, i, j: (b * nq + i, 0))] + cast_specs,
        out_shape=[jax.ShapeDtypeStruct((n, ATTN_W), BF16)]
        + [jax.ShapeDtypeStruct(w.shape, BF16) for w in casts],
        scratch_shapes=[pltpu.VMEM((n_pairs, group * tq, LANES), F32)],
        compiler_params=_params(3),
        name="attn_cached" if cache is not None else "attn",
    )(*args, *casts)
    return outs[0], outs[1:]


def _ffn_kernel(*refs, nseg, seg, tiles_per_seq):
    halo = tiles_per_seq > 1
    if halo:
        (x_ref, an_ref, gm_ref, xp_ref, xn_ref, anp_ref, ann_ref, gmp_ref, gmn_ref, mod_ref, wo_ref, n2g_ref,
         wup_ref, cw_ref, cb_ref, wd_ref, fg_ref, out_ref, lhs_sc, a_sc, g_sc, mix_sc) = refs
    else:
        (x_ref, an_ref, gm_ref, mod_ref, wo_ref, n2g_ref,
         wup_ref, cw_ref, cb_ref, wd_ref, fg_ref, out_ref, lhs_sc, a_sc, g_sc) = refs
    stride = seg + HALO
    rows = HALO + nseg * stride
    gate1 = mod_ref[:, 2 * D_MODEL:3 * D_MODEL]
    shift2 = mod_ref[:, 3 * D_MODEL:4 * D_MODEL]
    scale2 = mod_ref[:, 4 * D_MODEL:5 * D_MODEL]

    def residual_norm(x, y):
        x1 = x + gate1 * y
        return x1, (_rms(x1, n2g_ref[...]) * (1.0 + scale2) + shift2).astype(BF16)

    zeros = jnp.zeros((HALO, D_MODEL), BF16)
    if halo:
        mix_sc[0:HALO, 0:ATTN_W] = anp_ref[...]
        mix_sc[0:HALO, ATTN_W:D_MODEL] = gmp_ref[...]
        mix_sc[HALO:HALO + seg, 0:ATTN_W] = an_ref[...]
        mix_sc[HALO:HALO + seg, ATTN_W:D_MODEL] = gm_ref[...]
        mix_sc[HALO + seg:rows, 0:ATTN_W] = ann_ref[...]
        mix_sc[HALO + seg:rows, ATTN_W:D_MODEL] = gmn_ref[...]
        y = _dot(mix_sc[...], wo_ref[...])
        x1, h2 = residual_norm(x_ref[...], y[HALO:HALO + seg])
        out_ref[...] = x1
        lhs_sc[HALO:HALO + seg, :] = h2
        pos = pl.program_id(0) % tiles_per_seq
        lhs_sc[0:HALO, :] = zeros
        lhs_sc[HALO + seg:rows, :] = zeros

        @pl.when(pos != 0)
        def _():
            lhs_sc[0:HALO, :] = residual_norm(xp_ref[...], y[0:HALO])[1]

        @pl.when(pos != tiles_per_seq - 1)
        def _():
            lhs_sc[HALO + seg:rows, :] = residual_norm(xn_ref[...], y[HALO + seg:rows])[1]
    else:
        y = _dot(an_ref[...], wo_ref[0:ATTN_W, :]) + _dot(gm_ref[...], wo_ref[ATTN_W:D_MODEL, :])
        x1, h2 = residual_norm(x_ref[...], y)
        out_ref[...] = x1
        lhs_sc[0:HALO, :] = zeros
        for s in range(nseg):
            base = HALO + s * stride
            lhs_sc[base:base + seg, :] = h2[s * seg:(s + 1) * seg, :]
            lhs_sc[base + seg:base + stride, :] = zeros

    for c in range(N_FF_CHUNKS):
        slot = c % A_SLOTS
        for k, col0 in enumerate((c * FF_CHUNK, D_FF + c * FF_CHUNK)):
            a = _dot(lhs_sc[...], wup_ref[:, col0:col0 + FF_CHUNK])
            for j in range(FF_CHUNK // LANES):
                a_sc[slot, 2 * k + j, :, :] = a[8:rows - 8, j * LANES:(j + 1) * LANES]
        for j in range(FF_CHUNK // LANES):
            gl = slice(c * FF_CHUNK + j * LANES, c * FF_CHUNK + (j + 1) * LANES)
            vl = slice(D_FF + gl.start, D_FF + gl.stop)
            for s in range(nseg):
                base = 8 + s * stride

                def conv(k, lanes):
                    acc = cb_ref[:, lanes]
                    for t in range(CONV_W):
                        tap = slice(t * 2 * D_FF + lanes.start, t * 2 * D_FF + lanes.stop)
                        acc = acc + a_sc[slot, k, base - 1 + t:base - 1 + t + seg, :] * cw_ref[:, tap]
                    return acc

                g = _silu(conv(j, gl)) * conv(2 + j, vl)
                g_sc[s * seg:(s + 1) * seg, gl] = g.astype(BF16)

    x2 = out_ref[...] + mod_ref[:, 5 * D_MODEL:6 * D_MODEL] * _dot(g_sc[...], wd_ref[...])
    out_ref[...] = _rms(x2, fg_ref[...])


def _ffn(x, an, gm, mods3, mod_row, p, ffn_w, seq, tm):
    n = x.shape[0]
    tiles_per_seq = max(seq // tm, 1)
    seg = min(seq, tm)
    nseg = tm // seg
    rows = HALO + nseg * (seg + HALO)
    const = lambda i: (0, 0)
    tile = lambda i: (i, 0)
    resident = dict(pipeline_mode=pl.Buffered(1))
    in_specs = [pl.BlockSpec((tm, D_MODEL), tile), pl.BlockSpec((tm, ATTN_W), tile),
                pl.BlockSpec((tm, GMLP_W), tile)]
    args = [x, an, gm]
    scratch = [pltpu.VMEM((rows, D_MODEL), BF16),
               pltpu.VMEM((A_SLOTS, 2 * FF_CHUNK // LANES, rows - 16, LANES), F32),
               pltpu.VMEM((tm, D_FF), BF16)]
    if tiles_per_seq > 1:
        per = tm // HALO
        last = n // HALO - 1
        prev = lambda i: (jnp.maximum(i * per - 1, 0), 0)
        nxt = lambda i: (jnp.minimum((i + 1) * per, last), 0)
        for arr, width in ((x, D_MODEL), (an, ATTN_W), (gm, GMLP_W)):
            in_specs += [pl.BlockSpec((HALO, width), prev), pl.BlockSpec((HALO, width), nxt)]
            args += [arr, arr]
        scratch.append(pltpu.VMEM((rows, D_MODEL), BF16))
    in_specs += [
        pl.BlockSpec((None, 1, N_MODS * D_MODEL), lambda i: (mod_row(i), 0, 0)),
        pl.BlockSpec((D_MODEL, D_MODEL), const, **resident),
        pl.BlockSpec((1, D_MODEL), const),
        pl.BlockSpec((D_MODEL, 2 * D_FF), const, **resident),
        pl.BlockSpec((1, CONV_W * 2 * D_FF), const),
        pl.BlockSpec((1, 2 * D_FF), const),
        pl.BlockSpec((D_FF, D_MODEL), const, **resident),
        pl.BlockSpec((1, D_MODEL), const),
    ]
    w_o, w_up, w_down = ffn_w
    args += [mods3, w_o, p["norm2_g"], w_up, p["conv_w"], p["conv_b"], w_down, p["final_g"]]
    return pl.pallas_call(
        functools.partial(_ffn_kernel, nseg=nseg, seg=seg, tiles_per_seq=tiles_per_seq),
        grid=(n // tm,),
        in_specs=in_specs,
        out_specs=pl.BlockSpec((tm, D_MODEL), tile),
        out_shape=jax.ShapeDtypeStruct((n, D_MODEL), F32),
        scratch_shapes=scratch,
        compiler_params=_params(1),
        name="ffn_halo" if tiles_per_seq > 1 else "ffn",
    )(*args)


def _rope_tables(length):
    pos = np.arange(length)
    row = (pos // GRID_W).astype(np.float32)
    col = (pos % GRID_W).astype(np.float32)
    n_freq = QK_ROPE // 4
    inv = (np.float32(ROPE_THETA) ** (-(np.arange(n_freq, dtype=np.float32) / np.float32(n_freq)))).astype(np.float32)
    ang_r, ang_c = row[:, None] * inv, col[:, None] * inv
    cos32 = np.concatenate([np.cos(ang_r)] * 2 + [np.cos(ang_c)] * 2, axis=-1)
    sin32 = np.concatenate([-np.sin(ang_r), np.sin(ang_r), -np.sin(ang_c), np.sin(ang_c)], axis=-1)
    tail = LANES - ROPE_LANE0 - QK_ROPE
    cos_q = np.concatenate([np.ones((length, ROPE_LANE0)), cos32, np.zeros((length, tail))], -1)
    sin_q = np.concatenate([np.zeros((length, ROPE_LANE0)), sin32, np.zeros((length, tail))], -1)
    return jnp.asarray(cos_q, F32), jnp.asarray(sin_q, F32)


def _layer_params(l, norm1_g, q_norm_g, kv_norm_g, v_norm_g, b_s, out_g_attn,
                  out_g_gmlp, norm2_g, conv_w, conv_b, final_g):
    row = lambda a: a.reshape(1, -1)
    return {
        "norm1_g": row(norm1_g[l]), "q_norm_g": row(q_norm_g[l]),
        "kv_norm_g": row(kv_norm_g[l]), "v_norm_g": row(v_norm_g[l]), "b_st": b_s[l].T,
        "out_g_attn": row(out_g_attn[l]), "out_g_gmlp": row(out_g_gmlp[l]), "norm2_g": row(norm2_g[l]),
        "conv_w": conv_w[l].reshape(1, CONV_W * 2 * D_FF), "conv_b": row(conv_b[l]), "final_g": row(final_g),
    }


def kernel(x_prompt, x_sample, cache_ckv, cache_krope, c, c_ctx, ada_w, ada_b, norm1_g, w_in, q_norm_g, w_uq, kv_norm_g, w_ukv, v_norm_g, w_s, b_s, out_g_attn, out_g_gmlp, w_o, norm2_g, w_up, conv_w, conv_b, w_down, final_g):
    batch, seq, _ = x_prompt.shape
    dec_batch, dec_seq, _ = x_sample.shape
    depth, past = cache_ckv.shape[1], cache_ckv.shape[2]
    assert depth == 1 and dec_batch + 1 <= MOD_ROWS
    l = 0
    p = _layer_params(l, norm1_g, q_norm_g, kv_norm_g, v_norm_g, b_s, out_g_attn,
                      out_g_gmlp, norm2_g, conv_w, conv_b, final_g)

    mods3, p["w_qkv"], p["w_uv"], p["w_uq"], p["w_k"], p["w_ve"], p["w_vo"], w_s_b = _adaln(
        c_ctx.reshape(1, D_MODEL), c, ada_w[l], ada_b[l].reshape(1, -1), w_in[l].T, w_uq[l], w_ukv[l],
        w_s[l].reshape(GMLP_GROUPS * CHUNK, CHUNK))
    p["w_s"] = w_s_b.reshape(GMLP_GROUPS, CHUNK, CHUNK)
    tm_pre, tm_ffn = 1024, 512
    ctx_row = lambda i: 0
    lat_row = lambda tm: lambda i: 1 + i // (dec_seq // tm)

    xp = x_prompt.reshape(batch * seq, D_MODEL)
    xs = x_sample.reshape(dec_batch * dec_seq, D_MODEL)
    (qp, kp, vep, vop, gmp, ckv_p, kr_p), (w_down_b,) = _pre(xp, mods3, ctx_row, p, None, tm_pre, [w_down[l]])
    (qs, ks, ves, vos, gms, _, _), (w_o_b,) = _pre(xs, mods3, lat_row(tm_pre), p, _rope_tables(dec_seq), tm_pre,
                                                   [w_o[l]])
    kr_pad = jnp.pad(cache_krope[:, l].reshape(dec_batch * past, QK_ROPE),
                     ((0, 0), (ROPE_LANE0, LANES - ROPE_LANE0 - QK_ROPE)))
    cache = _expand(cache_ckv[:, l].reshape(dec_batch * past, KV_RANK), kr_pad, p, 512)

    anp, _ = _attention(qp, kp, vep, vop, None, p["out_g_attn"], seq, seq, 4, 4, [])
    ans, (w_up_b,) = _attention(qs, ks, ves, vos, cache, p["out_g_attn"], dec_seq, 512, 1, 4, [w_up[l]])
    ffn_w = (w_o_b, w_up_b, w_down_b)
    yp = _ffn(xp, anp, gmp, mods3, ctx_row, p, ffn_w, seq, tm_ffn)
    ys = _ffn(xs, ans, gms, mods3, lat_row(tm_ffn), p, ffn_w, dec_seq, tm_ffn)

    return (yp.reshape(batch, seq, D_MODEL), ys.reshape(dec_batch, dec_seq, D_MODEL),
            ckv_p.reshape(batch, 1, seq, KV_RANK), kr_p.reshape(batch, 1, seq, QK_ROPE))
```

```python
import functools

import numpy as np
import jax
import jax.numpy as jnp
from jax import lax
from jax.experimental import pallas as pl
from jax.experimental.pallas import tpu as pltpu

D_MODEL = 1024
GRID_W = 64
CHUNK = 128
N_HEADS = 8
QK_NOPE = 64
QK_ROPE = 32
V_DIM = 64
Q_RANK = 384
KV_RANK = 256
ATTN_W = N_HEADS * V_DIM
GMLP_W = D_MODEL - ATTN_W
GMLP_GROUPS = 4
GMLP_DG = GMLP_W // GMLP_GROUPS
D_FF = 2816
CONV_W = 3
ROPE_THETA = 10000.0
EPS = 1e-6
SM_SCALE = (QK_NOPE + QK_ROPE) ** -0.5
LOG2_E = 1.4426950408889634

LANES = 128
HEAD_PAD = LANES
ROPE_LANE0 = QK_NOPE
QK_W = N_HEADS * HEAD_PAD
N_MODS = 6
MOD_ROWS = 8
ADALN_STEPS = 8
IN_QKV = Q_RANK + KV_RANK
FF_CHUNK = 256
HALO = 16
N_FF_CHUNKS = D_FF // FF_CHUNK
A_SLOTS = 4
VMEM_LIMIT = 56 * 1024 * 1024

F32 = jnp.float32
BF16 = jnp.bfloat16


def _dot(a, b):
    return jnp.dot(a, b, preferred_element_type=F32)


def _dot_nt(a, b):
    return lax.dot_general(a, b, (((1,), (1,)), ((), ())), preferred_element_type=F32)


def _rms(x, g):
    return x * lax.rsqrt(jnp.mean(x * x, axis=-1, keepdims=True) + EPS) * g


def _gelu(x):
    return 0.5 * x * (1.0 + lax.erf(x * (0.5 ** 0.5)))


def _silu(x):
    return x * (1.0 / (1.0 + jnp.exp(-x)))


def _params(n_axes):
    return pltpu.CompilerParams(dimension_semantics=("arbitrary",) * n_axes,
                                vmem_limit_bytes=VMEM_LIMIT)


def _adaln_kernel(cctx_ref, c_ref, w_ref, b_ref, win_ref, wuq_ref, wukv_ref, ws_ref,
                  out_ref, wqkv_out, wuv_out, wuq_out, wk_out, wve_out, wvo_out, ws_out):
    pad = jnp.zeros((MOD_ROWS - 1 - c_ref.shape[0], D_MODEL), F32)
    cond = jnp.concatenate([cctx_ref[...], c_ref[...], pad], axis=0)
    s = _silu(cond).astype(BF16)
    out_ref[:, 0, :] = _dot(s, w_ref[...].astype(BF16)) + b_ref[...]
    @pl.when(pl.program_id(0) == 0)
    def _():
        kr = win_ref[IN_QKV:IN_QKV + QK_ROPE, :].astype(BF16)
        z = jnp.zeros_like(kr)
        wqkv_out[0:IN_QKV, :] = win_ref[0:IN_QKV, :].astype(BF16)
        for t, blk in enumerate((kr, z, kr, z)):
            wqkv_out[IN_QKV + t * QK_ROPE:IN_QKV + (t + 1) * QK_ROPE, :] = blk
        wuv_out[...] = win_ref[IN_QKV + QK_ROPE:, :].astype(BF16)

    w = wuq_ref[...]
    hw = QK_NOPE + QK_ROPE
    zq = jnp.zeros((w.shape[0], HEAD_PAD - hw), F32)
    wuq_out[...] = jnp.concatenate([blk for hd in range(N_HEADS) for blk in (w[:, hd * hw:(hd + 1) * hw], zq)],
                                   axis=1).astype(BF16)
    w = wukv_ref[...]
    zv = jnp.zeros((w.shape[0], V_DIM), F32)
    v_of = lambda hd: w[:, hd * HEAD_PAD + QK_NOPE:(hd + 1) * HEAD_PAD]
    wk_out[...] = jnp.concatenate([blk for hd in range(N_HEADS)
                                   for blk in (w[:, hd * HEAD_PAD:hd * HEAD_PAD + QK_NOPE], zv)], axis=1).astype(BF16)
    wve_out[...] = jnp.concatenate([blk for hd in range(0, N_HEADS, 2) for blk in (v_of(hd), zv)],
                                   axis=1).astype(BF16)
    wvo_out[...] = jnp.concatenate([blk for hd in range(1, N_HEADS, 2) for blk in (zv, v_of(hd))],
                                   axis=1).astype(BF16)
    ws_out[...] = ws_ref[...].astype(BF16)


def _adaln(c_ctx, c, ada_w, ada_b, w_in_t, w_uq, w_ukv, w_s):
    cols = N_MODS * D_MODEL // ADALN_STEPS
    row_block = lambda a, width=None: pl.BlockSpec((a.shape[0] // ADALN_STEPS, width or a.shape[1]),
                                                   lambda j: (j, 0))
    whole = lambda shape: pl.BlockSpec(shape, lambda j: (0, 0), pipeline_mode=pl.Buffered(1))
    return pl.pallas_call(
        _adaln_kernel,
        grid=(ADALN_STEPS,),
        in_specs=[
            pl.BlockSpec(c_ctx.shape, lambda j: (0, 0)),
            pl.BlockSpec(c.shape, lambda j: (0, 0)),
            pl.BlockSpec((D_MODEL, cols), lambda j: (0, j)),
            pl.BlockSpec((1, cols), lambda j: (0, j)),
            whole(w_in_t.shape), row_block(w_uq), row_block(w_ukv), row_block(w_s),
        ],
        out_specs=[pl.BlockSpec((MOD_ROWS, 1, cols), lambda j: (0, 0, j)),
                   whole((IN_QKV + LANES, D_MODEL)), whole((2 * GMLP_W, D_MODEL)),
                   row_block(w_uq, QK_W), row_block(w_ukv, QK_W), row_block(w_ukv, ATTN_W),
                   row_block(w_ukv, ATTN_W), row_block(w_s)],
        out_shape=[jax.ShapeDtypeStruct((MOD_ROWS, 1, N_MODS * D_MODEL), F32),
                   jax.ShapeDtypeStruct((IN_QKV + LANES, D_MODEL), BF16),
                   jax.ShapeDtypeStruct((2 * GMLP_W, D_MODEL), BF16),
                   jax.ShapeDtypeStruct((Q_RANK, QK_W), BF16),
                   jax.ShapeDtypeStruct((KV_RANK, QK_W), BF16),
                   jax.ShapeDtypeStruct((KV_RANK, ATTN_W), BF16),
                   jax.ShapeDtypeStruct((KV_RANK, ATTN_W), BF16),
                   jax.ShapeDtypeStruct(w_s.shape, BF16)],
        compiler_params=_params(1),
        name="adaln",
    )(c_ctx, c, ada_w, ada_b, w_in_t, w_uq, w_ukv, w_s)


def _pre_kernel(*refs, rope, tm, n_casts):
    n_main = 17 if rope else 15
    cast_in = refs[n_main:n_main + n_casts]
    cast_out = refs[n_main + n_casts + 7:n_main + 2 * n_casts + 7]
    q_out, k_out, ve_out, vo_out, gm_out, ckv_out, kr_out = refs[n_main + n_casts:n_main + n_casts + 7]
    go_sc = refs[-1]
    (x_ref, mod_ref, n1g_ref, wqkv_ref, wuv_ref, qng_ref, wuq_ref, kvg_ref, wk_ref, wve_ref, wvo_ref,
     vng_ref, ws_ref, bst_ref, ogg_ref) = refs[:15]
    if rope:
        cos_ref, sin_ref = refs[15:17]
    for src, dst in zip(cast_in, cast_out):
        dst[...] = src[...].astype(BF16)
    shift1 = mod_ref[:, 0:D_MODEL]
    scale1 = mod_ref[:, D_MODEL:2 * D_MODEL]
    h = (_rms(x_ref[...], n1g_ref[...]) * (1.0 + scale1) + shift1).astype(BF16)

    lane = lax.broadcasted_iota(jnp.int32, (1, LANES), 1)
    rope_lanes = lane >= ROPE_LANE0
    first_half = (lane % 16) < 8

    def rotate(v, cos):
        swapped = jnp.where(first_half, pltpu.roll(v, LANES - 8, 1), pltpu.roll(v, 8, 1))
        return v * cos + swapped * sin_ref[...]

    qkv = _dot_nt(h, wqkv_ref[...])
    qn = _rms(qkv[:, 0:Q_RANK], qng_ref[...]).astype(BF16)
    ckv = _rms(qkv[:, Q_RANK:IN_QKV], kvg_ref[...])
    ckv_out[...] = ckv
    ckv_b = ckv.astype(BF16)
    q = _dot(qn, wuq_ref[...]) * (SM_SCALE * LOG2_E)
    kr = qkv[:, IN_QKV:IN_QKV + LANES]
    kr_out[...] = kr[:, 0:QK_ROPE]
    if rope:
        cos_q = cos_ref[...]
        kr_k = rotate(kr, jnp.where(rope_lanes, cos_q, 0.0))
    else:
        kr_k = jnp.where(rope_lanes, kr, 0.0)
    kn = _dot(ckv_b, wk_ref[...])
    for hd in range(N_HEADS):
        sl = slice(hd * HEAD_PAD, (hd + 1) * HEAD_PAD)
        qh = q[:, sl]
        if rope:
            qh = rotate(qh, cos_q)
        q_out[:, sl] = qh.astype(BF16)
        k_out[:, sl] = (kn[:, sl] + kr_k).astype(BF16)
    ve_out[...] = _dot(ckv_b, wve_ref[...]).astype(BF16)
    vo_out[...] = _dot(ckv_b, wvo_ref[...]).astype(BF16)

    uv = _dot_nt(h, wuv_ref[...])
    gu = _gelu(uv[:, 0:GMLP_W])
    gv = _gelu(uv[:, GMLP_W:2 * GMLP_W])
    for g in range(GMLP_GROUPS):
        sl = slice(g * GMLP_DG, (g + 1) * GMLP_DG)
        vg = gv[:, sl]
        vg = (vg * lax.rsqrt(jnp.mean(vg * vg, axis=-1, keepdims=True) + EPS) * vng_ref[:, sl]).astype(BF16)
        bias = bst_ref[:, g:g + 1]
        chunks = [vg[n * CHUNK:(n + 1) * CHUNK, :] for n in range(tm // CHUNK)]
        s_all = _dot(ws_ref[g], jnp.concatenate(chunks, axis=1))
        for n in range(tm // CHUNK):
            rows = slice(n * CHUNK, (n + 1) * CHUNK)
            go_sc[rows, sl] = gu[rows, sl] * (s_all[:, n * GMLP_DG:(n + 1) * GMLP_DG] + bias)
    gm_out[...] = _rms(go_sc[...], ogg_ref[...]).astype(BF16)


def _pre(x, mods3, mod_row, p, rope_tabs, tm, casts):
    n = x.shape[0]
    rope = rope_tabs is not None
    const = lambda i: (0, 0)
    tile = lambda i: (i, 0)
    in_specs = [
        pl.BlockSpec((tm, D_MODEL), tile),
        pl.BlockSpec((None, 1, N_MODS * D_MODEL), lambda i: (mod_row(i), 0, 0)),
        pl.BlockSpec((1, D_MODEL), const),
        pl.BlockSpec((IN_QKV + LANES, D_MODEL), const),
        pl.BlockSpec((2 * GMLP_W, D_MODEL), const),
        pl.BlockSpec((1, Q_RANK), const),
        pl.BlockSpec((Q_RANK, QK_W), const),
        pl.BlockSpec((1, KV_RANK), const),
        pl.BlockSpec((KV_RANK, QK_W), const),
        pl.BlockSpec((KV_RANK, ATTN_W), const),
        pl.BlockSpec((KV_RANK, ATTN_W), const),
        pl.BlockSpec((1, GMLP_W), const),
        pl.BlockSpec((GMLP_GROUPS, CHUNK, CHUNK), lambda i: (0, 0, 0)),
        pl.BlockSpec((CHUNK, GMLP_GROUPS), const),
        pl.BlockSpec((1, GMLP_W), const),
    ]
    args = [x, mods3, p["norm1_g"], p["w_qkv"], p["w_uv"], p["q_norm_g"], p["w_uq"],
            p["kv_norm_g"], p["w_k"], p["w_ve"], p["w_vo"], p["v_norm_g"], p["w_s"], p["b_st"], p["out_g_gmlp"]]
    if rope:
        cos_q, sin_q = rope_tabs
        per_seq = cos_q.shape[0] // tm
        in_specs += [pl.BlockSpec((tm, LANES), lambda i: (i % per_seq, 0))] * 2
        args += [cos_q, sin_q]
    widths = [(QK_W, BF16), (QK_W, BF16), (ATTN_W, BF16), (ATTN_W, BF16), (GMLP_W, BF16),
              (KV_RANK, F32), (QK_ROPE, F32)]
    cast_specs = [pl.BlockSpec((w.shape[0] // (n // tm), w.shape[1]), tile) for w in casts]
    outs = pl.pallas_call(
        functools.partial(_pre_kernel, rope=rope, tm=tm, n_casts=len(casts)),
        grid=(n // tm,),
        in_specs=in_specs + cast_specs,
        out_specs=[pl.BlockSpec((tm, w), tile) for w, _ in widths] + cast_specs,
        out_shape=[jax.ShapeDtypeStruct((n, w), dt) for w, dt in widths]
        + [jax.ShapeDtypeStruct(w.shape, BF16) for w in casts],
        scratch_shapes=[pltpu.VMEM((tm, GMLP_W), F32)],
        compiler_params=_params(1),
        name="pre_rope" if rope else "pre",
    )(*args, *casts)
    return outs[:7], outs[7:]


def _expand_kernel(ckv_ref, kr_ref, wk_ref, wve_ref, wvo_ref, k_out, ve_out, vo_out):
    ckv_b = ckv_ref[...].astype(BF16)
    kn = _dot(ckv_b, wk_ref[...])
    kr = kr_ref[...]
    for hd in range(N_HEADS):
        sl = slice(hd * HEAD_PAD, (hd + 1) * HEAD_PAD)
        k_out[:, sl] = (kn[:, sl] + kr).astype(BF16)
    ve_out[...] = _dot(ckv_b, wve_ref[...]).astype(BF16)
    vo_out[...] = _dot(ckv_b, wvo_ref[...]).astype(BF16)


def _expand(ckv, kr_pad, p, tm):
    n = ckv.shape[0]
    const = lambda i: (0, 0)
    tile = lambda i: (i, 0)
    widths = [QK_W, ATTN_W, ATTN_W]
    return pl.pallas_call(
        _expand_kernel,
        grid=(n // tm,),
        in_specs=[
            pl.BlockSpec((tm, KV_RANK), tile),
            pl.BlockSpec((tm, LANES), tile),
            pl.BlockSpec((KV_RANK, QK_W), const),
            pl.BlockSpec((KV_RANK, ATTN_W), const),
            pl.BlockSpec((KV_RANK, ATTN_W), const),
        ],
        out_specs=[pl.BlockSpec((tm, w), tile) for w in widths],
        out_shape=[jax.ShapeDtypeStruct((n, w), BF16) for w in widths],
        compiler_params=_params(1),
        name="expand",
    )(ckv, kr_pad, p["w_k"], p["w_ve"], p["w_vo"])


def _attn_kernel(*refs, cached, group, seq, tq, pps, n_casts):
    n_main = 8 if cached else 5
    cast_in = refs[n_main:n_main + n_casts]
    out_ref = refs[n_main + n_casts]
    cast_out = refs[n_main + n_casts + 1:n_main + 2 * n_casts + 1]
    o_sc = refs[-1]
    if cached:
        q_ref, k_ref, ve_ref, vo_ref, kc_ref, vec_ref, voc_ref, oga_ref = refs[:n_main]
    else:
        q_ref, k_ref, ve_ref, vo_ref, oga_ref = refs[:n_main]
    for src, dst in zip(cast_in, cast_out):
        dst[...] = src[...].astype(BF16)
    step = pl.program_id(2)
    ones = jnp.ones((seq, LANES), BF16)
    ones_c = jnp.ones((kc_ref.shape[0], LANES), BF16) if cached else None
    for g in range(group):
        qrows = slice(g * tq, (g + 1) * tq)
        krows = slice(g * seq, (g + 1) * seq)
        for pp in range(pps):
            psl = slice(pp * LANES, (pp + 1) * LANES)
            acc = None
            for par, (v_ref, vc_ref) in enumerate(((ve_ref, vec_ref if cached else None),
                                                   (vo_ref, voc_ref if cached else None))):
                hd = 2 * pp + par
                hsl = slice(hd * HEAD_PAD, (hd + 1) * HEAD_PAD)
                qh = q_ref[qrows, hsl]
                s = _dot_nt(qh, k_ref[krows, hsl])
                m = jnp.max(s, axis=-1, keepdims=True)
                if cached:
                    sc = _dot_nt(qh, kc_ref[:, hsl])
                    m = jnp.maximum(m, jnp.max(sc, axis=-1, keepdims=True))
                e = jnp.exp2(s - m).astype(BF16)
                o = _dot(e, jnp.concatenate([v_ref[krows, psl], ones], axis=1))
                if cached:
                    ec = jnp.exp2(sc - m).astype(BF16)
                    o = o + _dot(ec, jnp.concatenate([vc_ref[:, psl], ones_c], axis=1))
                o = o[:, 0:LANES] * (1.0 / o[:, LANES:2 * LANES])
                acc = o if acc is None else acc + o
            o_sc[step * pps + pp, qrows, :] = acc

    @pl.when(step == N_HEADS // 2 // pps - 1)
    def _():
        blocks = [o_sc[j] for j in range(N_HEADS // 2)]
        ssq = sum(jnp.sum(b * b, axis=-1, keepdims=True) for b in blocks)
        r = lax.rsqrt(ssq * (1.0 / ATTN_W) + EPS)
        for j, b in enumerate(blocks):
            sl = slice(j * LANES, (j + 1) * LANES)
            out_ref[:, sl] = (b * r * oga_ref[:, sl]).astype(BF16)


def _attention(q, k, ve, vo, cache, oga, seq, tq, group, pps, casts):
    n = q.shape[0]
    nq = seq // tq
    n_pairs = N_HEADS // 2
    grid = (n // (group * seq), nq, n_pairs // pps)
    n_steps = grid[0] * grid[1] * grid[2]
    linear = lambda b, i, j: ((b * grid[1] + i) * grid[2] + j, 0)
    cast_specs = [pl.BlockSpec((w.shape[0] // n_steps, w.shape[1]), linear) for w in casts]
    qtile = lambda b, i, j: (b * nq + i, j)
    kv = lambda b, i, j: (b, j)
    qk_w, v_w = pps * 2 * HEAD_PAD, pps * LANES
    in_specs = [
        pl.BlockSpec((group * tq, qk_w), qtile),
        pl.BlockSpec((group * seq, qk_w), kv),
        pl.BlockSpec((group * seq, v_w), kv),
        pl.BlockSpec((group * seq, v_w), kv),
    ]
    args = [q, k, ve, vo]
    if cache is not None:
        kc, vec, voc = cache
        past = kc.shape[0] // (n // seq)
        in_specs += [pl.BlockSpec((past, qk_w), kv), pl.BlockSpec((past, v_w), kv),
                     pl.BlockSpec((past, v_w), kv)]
        args += [kc, vec, voc]
    in_specs.append(pl.BlockSpec((1, ATTN_W), lambda b, i, j: (0, 0)))
    args.append(oga)
    outs = pl.pallas_call(
        functools.partial(_attn_kernel, cached=cache is not None, group=group, seq=seq, tq=tq, pps=pps,
                          n_casts=len(casts)),
        grid=grid,
        in_specs=in_specs + cast_specs,
        out_specs=[pl.BlockSpec((group * tq, ATTN_W), lambda b, i, j: (b * nq + i, 0))] + cast_specs,
        out_shape=[jax.ShapeDtypeStruct((n, ATTN_W), BF16)]
        + [jax.ShapeDtypeStruct(w.shape, BF16) for w in casts],
        scratch_shapes=[pltpu.VMEM((n_pairs, group * tq, LANES), F32)],
        compiler_params=_params(3),
        name="attn_cached" if cache is not None else "attn",
    )(*args, *casts)
    return outs[0], outs[1:]


def _ffn_kernel(*refs, nseg, seg, tiles_per_seq):
    halo = tiles_per_seq > 1
    if halo:
        (x_ref, an_ref, gm_ref, xp_ref, xn_ref, anp_ref, ann_ref, gmp_ref, gmn_ref, mod_ref, wo_ref, n2g_ref,
         wup_ref, cw_ref, cb_ref, wd_ref, fg_ref, out_ref, lhs_sc, a_sc, g_sc, mix_sc) = refs
    else:
        (x_ref, an_ref, gm_ref, mod_ref, wo_ref, n2g_ref,
         wup_ref, cw_ref, cb_ref, wd_ref, fg_ref, out_ref, lhs_sc, a_sc, g_sc) = refs
    tm = nseg * seg
    stride = seg + HALO
    edge = HALO // 2
    gate1 = mod_ref[:, 2 * D_MODEL:3 * D_MODEL]
    shift2 = mod_ref[:, 3 * D_MODEL:4 * D_MODEL]
    scale2 = mod_ref[:, 4 * D_MODEL:5 * D_MODEL]

    def residual_norm(x, y, keep=None):
        x1 = x + gate1 * y
        h2 = _rms(x1, n2g_ref[...]) * (1.0 + scale2) + shift2
        if keep is not None:
            h2 = jnp.where(keep, h2, 0.0)
        return x1, h2.astype(BF16)

    if halo:
        def neighbours(prev_ref, next_ref):
            return jnp.concatenate([prev_ref[...].astype(F32)[edge:HALO, :],
                                    next_ref[...].astype(F32)[0:edge, :]], axis=0)

        mix_sc[0:tm, 0:ATTN_W] = an_ref[...]
        mix_sc[0:tm, ATTN_W:D_MODEL] = gm_ref[...]
        mix_sc[tm:tm + HALO, 0:ATTN_W] = neighbours(anp_ref, ann_ref).astype(BF16)
        mix_sc[tm:tm + HALO, ATTN_W:D_MODEL] = neighbours(gmp_ref, gmn_ref).astype(BF16)
        y = _dot(mix_sc[...], wo_ref[...])
        x1, h2 = residual_norm(x_ref[...], y[0:tm])
        out_ref[...] = x1
        lhs_sc[0:tm, :] = h2
        pos = pl.program_id(0) % tiles_per_seq
        is_prev = lax.broadcasted_iota(jnp.int32, (HALO, 1), 0) < edge
        has_prev = (pos != 0).astype(jnp.int32)
        has_next = (pos != tiles_per_seq - 1).astype(jnp.int32)
        keep = jnp.where(is_prev, has_prev, has_next) > 0
        lhs_sc[tm:tm + HALO, :] = residual_norm(neighbours(xp_ref, xn_ref), y[tm:tm + HALO], keep)[1]
    else:
        y = _dot(an_ref[...], wo_ref[0:ATTN_W, :]) + _dot(gm_ref[...], wo_ref[ATTN_W:D_MODEL, :])
        x1, h2 = residual_norm(x_ref[...], y)
        out_ref[...] = x1
        lhs_sc[...] = h2

        @pl.when(pl.program_id(0) == 0)
        def _():
            zeros = jnp.zeros((A_SLOTS, 2 * FF_CHUNK // LANES, HALO, LANES), F32)
            a_sc[:, :, 0:edge, :] = zeros[:, :, 0:edge, :]
            for s in range(1, nseg):
                a_sc[:, :, s * stride - edge:s * stride + edge, :] = zeros
            a_sc[:, :, nseg * stride - edge:nseg * stride, :] = zeros[:, :, 0:edge, :]

    for c in range(N_FF_CHUNKS):
        slot = c % A_SLOTS
        for k, col0 in enumerate((c * FF_CHUNK, D_FF + c * FF_CHUNK)):
            a = _dot(lhs_sc[...], wup_ref[:, col0:col0 + FF_CHUNK])
            for j in range(FF_CHUNK // LANES):
                lanes = slice(j * LANES, (j + 1) * LANES)
                for s in range(nseg):
                    row0 = edge + s * stride
                    a_sc[slot, 2 * k + j, row0:row0 + seg, :] = a[s * seg:(s + 1) * seg, lanes]
                if halo:
                    a_sc[slot, 2 * k + j, 0:edge, :] = a[tm:tm + edge, lanes]
                    a_sc[slot, 2 * k + j, edge + seg:HALO + seg, :] = a[tm + edge:tm + HALO, lanes]
        for j in range(FF_CHUNK // LANES):
            gl = slice(c * FF_CHUNK + j * LANES, c * FF_CHUNK + (j + 1) * LANES)
            vl = slice(D_FF + gl.start, D_FF + gl.stop)
            for s in range(nseg):
                base = edge + s * stride

                def conv(k, lanes):
                    acc = cb_ref[:, lanes]
                    for t in range(CONV_W):
                        tap = slice(t * 2 * D_FF + lanes.start, t * 2 * D_FF + lanes.stop)
                        acc = acc + a_sc[slot, k, base - 1 + t:base - 1 + t + seg, :] * cw_ref[:, tap]
                    return acc

                g = _silu(conv(j, gl)) * conv(2 + j, vl)
                g_sc[s * seg:(s + 1) * seg, gl] = g.astype(BF16)

    x2 = out_ref[...] + mod_ref[:, 5 * D_MODEL:6 * D_MODEL] * _dot(g_sc[...], wd_ref[...])
    out_ref[...] = _rms(x2, fg_ref[...])


def _ffn(x, an, gm, mods3, mod_row, p, ffn_w, seq, tm):
    n = x.shape[0]
    tiles_per_seq = max(seq // tm, 1)
    seg = min(seq, tm)
    nseg = tm // seg
    rows = tm + (HALO if tiles_per_seq > 1 else 0)
    a_rows = nseg * (seg + HALO)
    const = lambda i: (0, 0)
    tile = lambda i: (i, 0)
    resident = dict(pipeline_mode=pl.Buffered(1))
    in_specs = [pl.BlockSpec((tm, D_MODEL), tile), pl.BlockSpec((tm, ATTN_W), tile),
                pl.BlockSpec((tm, GMLP_W), tile)]
    args = [x, an, gm]
    scratch = [pltpu.VMEM((rows, D_MODEL), BF16),
               pltpu.VMEM((A_SLOTS, 2 * FF_CHUNK // LANES, a_rows, LANES), F32),
               pltpu.VMEM((tm, D_FF), BF16)]
    if tiles_per_seq > 1:
        per = tm // HALO
        last = n // HALO - 1
        prev = lambda i: (jnp.maximum(i * per - 1, 0), 0)
        nxt = lambda i: (jnp.minimum((i + 1) * per, last), 0)
        for arr, width in ((x, D_MODEL), (an, ATTN_W), (gm, GMLP_W)):
            in_specs += [pl.BlockSpec((HALO, width), prev), pl.BlockSpec((HALO, width), nxt)]
            args += [arr, arr]
        scratch.append(pltpu.VMEM((rows, D_MODEL), BF16))
    in_specs += [
        pl.BlockSpec((None, 1, N_MODS * D_MODEL), lambda i: (mod_row(i), 0, 0)),
        pl.BlockSpec((D_MODEL, D_MODEL), const, **resident),
        pl.BlockSpec((1, D_MODEL), const),
        pl.BlockSpec((D_MODEL, 2 * D_FF), const, **resident),
        pl.BlockSpec((1, CONV_W * 2 * D_FF), const),
        pl.BlockSpec((1, 2 * D_FF), const),
        pl.BlockSpec((D_FF, D_MODEL), const, **resident),
        pl.BlockSpec((1, D_MODEL), const),
    ]
    w_o, w_up, w_down = ffn_w
    args += [mods3, w_o, p["norm2_g"], w_up, p["conv_w"], p["conv_b"], w_down, p["final_g"]]
    return pl.pallas_call(
        functools.partial(_ffn_kernel, nseg=nseg, seg=seg, tiles_per_seq=tiles_per_seq),
        grid=(n // tm,),
        in_specs=in_specs,
        out_specs=pl.BlockSpec((tm, D_MODEL), tile),
        out_shape=jax.ShapeDtypeStruct((n, D_MODEL), F32),
        scratch_shapes=scratch,
        compiler_params=_params(1),
        name="ffn_halo" if tiles_per_seq > 1 else "ffn",
    )(*args)


def _rope_tables(length):
    pos = np.arange(length)
    row = (pos // GRID_W).astype(np.float32)
    col = (pos % GRID_W).astype(np.float32)
    n_freq = QK_ROPE // 4
    inv = (np.float32(ROPE_THETA) ** (-(np.arange(n_freq, dtype=np.float32) / np.float32(n_freq)))).astype(np.float32)
    ang_r, ang_c = row[:, None] * inv, col[:, None] * inv
    cos32 = np.concatenate([np.cos(ang_r)] * 2 + [np.cos(ang_c)] * 2, axis=-1)
    sin32 = np.concatenate([-np.sin(ang_r), np.sin(ang_r), -np.sin(ang_c), np.sin(ang_c)], axis=-1)
    tail = LANES - ROPE_LANE0 - QK_ROPE
    cos_q = np.concatenate([np.ones((length, ROPE_LANE0)), cos32, np.zeros((length, tail))], -1)
    sin_q = np.concatenate([np.zeros((length, ROPE_LANE0)), sin32, np.zeros((length, tail))], -1)
    return jnp.asarray(cos_q, F32), jnp.asarray(sin_q, F32)


def _layer_params(l, norm1_g, q_norm_g, kv_norm_g, v_norm_g, b_s, out_g_attn,
                  out_g_gmlp, norm2_g, conv_w, conv_b, final_g):
    row = lambda a: a.reshape(1, -1)
    return {
        "norm1_g": row(norm1_g[l]), "q_norm_g": row(q_norm_g[l]),
        "kv_norm_g": row(kv_norm_g[l]), "v_norm_g": row(v_norm_g[l]), "b_st": b_s[l].T,
        "out_g_attn": row(out_g_attn[l]), "out_g_gmlp": row(out_g_gmlp[l]), "norm2_g": row(norm2_g[l]),
        "conv_w": conv_w[l].reshape(1, CONV_W * 2 * D_FF), "conv_b": row(conv_b[l]), "final_g": row(final_g),
    }


def kernel(x_prompt, x_sample, cache_ckv, cache_krope, c, c_ctx, ada_w, ada_b, norm1_g, w_in, q_norm_g, w_uq, kv_norm_g, w_ukv, v_norm_g, w_s, b_s, out_g_attn, out_g_gmlp, w_o, norm2_g, w_up, conv_w, conv_b, w_down, final_g):
    batch, seq, _ = x_prompt.shape
    dec_batch, dec_seq, _ = x_sample.shape
    depth, past = cache_ckv.shape[1], cache_ckv.shape[2]
    assert depth == 1 and dec_batch + 1 <= MOD_ROWS
    l = 0
    p = _layer_params(l, norm1_g, q_norm_g, kv_norm_g, v_norm_g, b_s, out_g_attn,
                      out_g_gmlp, norm2_g, conv_w, conv_b, final_g)

    mods3, p["w_qkv"], p["w_uv"], p["w_uq"], p["w_k"], p["w_ve"], p["w_vo"], w_s_b = _adaln(
        c_ctx.reshape(1, D_MODEL), c, ada_w[l], ada_b[l].reshape(1, -1), w_in[l].T, w_uq[l], w_ukv[l],
        w_s[l].reshape(GMLP_GROUPS * CHUNK, CHUNK))
    p["w_s"] = w_s_b.reshape(GMLP_GROUPS, CHUNK, CHUNK)
    tm_pre, tm_ffn = 1024, 512
    ctx_row = lambda i: 0
    lat_row = lambda tm: lambda i: 1 + i // (dec_seq // tm)

    xp = x_prompt.reshape(batch * seq, D_MODEL)
    xs = x_sample.reshape(dec_batch * dec_seq, D_MODEL)
    (qp, kp, vep, vop, gmp, ckv_p, kr_p), (w_down_b,) = _pre(xp, mods3, ctx_row, p, None, tm_pre, [w_down[l]])
    (qs, ks, ves, vos, gms, _, _), (w_o_b,) = _pre(xs, mods3, lat_row(tm_pre), p, _rope_tables(dec_seq), tm_pre,
                                                   [w_o[l]])
    kr_pad = jnp.pad(cache_krope[:, l].reshape(dec_batch * past, QK_ROPE),
                     ((0, 0), (ROPE_LANE0, LANES - ROPE_LANE0 - QK_ROPE)))
    cache = _expand(cache_ckv[:, l].reshape(dec_batch * past, KV_RANK), kr_pad, p, 512)

    anp, _ = _attention(qp, kp, vep, vop, None, p["out_g_attn"], seq, seq, 4, 4, [])
    ans, (w_up_b,) = _attention(qs, ks, ves, vos, cache, p["out_g_attn"], dec_seq, 512, 1, 4, [w_up[l]])
    ffn_w = (w_o_b, w_up_b, w_down_b)
    yp = _ffn(xp, anp, gmp, mods3, ctx_row, p, ffn_w, seq, tm_ffn)
    ys = _ffn(xs, ans, gms, mods3, lat_row(tm_ffn), p, ffn_w, dec_seq, tm_ffn)

    return (yp.reshape(batch, seq, D_MODEL), ys.reshape(dec_batch, dec_seq, D_MODEL),
            ckv_p.reshape(batch, 1, seq, KV_RANK), kr_p.reshape(batch, 1, seq, QK_ROPE))
```

```python
import functools

import numpy as np
import jax
import jax.numpy as jnp
from jax import lax
from jax.experimental import pallas as pl
from jax.experimental.pallas import tpu as pltpu

D_MODEL = 1024
GRID_W = 64
CHUNK = 128
N_HEADS = 8
QK_NOPE = 64
QK_ROPE = 32
V_DIM = 64
Q_RANK = 384
KV_RANK = 256
ATTN_W = N_HEADS * V_DIM
GMLP_W = D_MODEL - ATTN_W
GMLP_GROUPS = 4
GMLP_DG = GMLP_W // GMLP_GROUPS
D_FF = 2816
CONV_W = 3
ROPE_THETA = 10000.0
EPS = 1e-6
SM_SCALE = (QK_NOPE + QK_ROPE) ** -0.5
LOG2_E = 1.4426950408889634

LANES = 128
HEAD_PAD = LANES
ROPE_LANE0 = QK_NOPE
QK_W = N_HEADS * HEAD_PAD
N_MODS = 6
MOD_ROWS = 8
ADALN_STEPS = 8
IN_QKV = Q_RANK + KV_RANK
FF_CHUNK = 256
HALO = 16
N_FF_CHUNKS = D_FF // FF_CHUNK
A_SLOTS = 4
VMEM_LIMIT = 56 * 1024 * 1024

F32 = jnp.float32
BF16 = jnp.bfloat16


def _dot(a, b):
    return jnp.dot(a, b, preferred_element_type=F32)


def _dot_nt(a, b):
    return lax.dot_general(a, b, (((1,), (1,)), ((), ())), preferred_element_type=F32)


def _rms(x, g):
    return x * lax.rsqrt(jnp.mean(x * x, axis=-1, keepdims=True) + EPS) * g


def _gelu(x):
    return 0.5 * x * (1.0 + lax.erf(x * (0.5 ** 0.5)))


def _silu(x):
    return x * (1.0 / (1.0 + jnp.exp(-x)))


def _params(n_axes):
    return pltpu.CompilerParams(dimension_semantics=("arbitrary",) * n_axes,
                                vmem_limit_bytes=VMEM_LIMIT)


def _adaln_kernel(cctx_ref, c_ref, w_ref, b_ref, win_ref, wuq_ref, wukv_ref, ws_ref,
                  out_ref, wqkv_out, wuv_out, wuq_out, wk_out, wve_out, wvo_out, ws_out):
    pad = jnp.zeros((MOD_ROWS - 1 - c_ref.shape[0], D_MODEL), F32)
    cond = jnp.concatenate([cctx_ref[...], c_ref[...], pad], axis=0)
    s = _silu(cond).astype(BF16)
    out_ref[:, 0, :] = _dot(s, w_ref[...].astype(BF16)) + b_ref[...]
    @pl.when(pl.program_id(0) == 0)
    def _():
        kr = win_ref[IN_QKV:IN_QKV + QK_ROPE, :].astype(BF16)
        z = jnp.zeros_like(kr)
        wqkv_out[0:IN_QKV, :] = win_ref[0:IN_QKV, :].astype(BF16)
        for t, blk in enumerate((kr, z, kr, z)):
            wqkv_out[IN_QKV + t * QK_ROPE:IN_QKV + (t + 1) * QK_ROPE, :] = blk
        wuv_out[...] = win_ref[IN_QKV + QK_ROPE:, :].astype(BF16)

    w = wuq_ref[...]
    hw = QK_NOPE + QK_ROPE
    zq = jnp.zeros((w.shape[0], HEAD_PAD - hw), F32)
    wuq_out[...] = jnp.concatenate([blk for hd in range(N_HEADS) for blk in (w[:, hd * hw:(hd + 1) * hw], zq)],
                                   axis=1).astype(BF16)
    w = wukv_ref[...]
    zv = jnp.zeros((w.shape[0], V_DIM), F32)
    v_of = lambda hd: w[:, hd * HEAD_PAD + QK_NOPE:(hd + 1) * HEAD_PAD]
    wk_out[...] = jnp.concatenate([blk for hd in range(N_HEADS)
                                   for blk in (w[:, hd * HEAD_PAD:hd * HEAD_PAD + QK_NOPE], zv)], axis=1).astype(BF16)
    wve_out[...] = jnp.concatenate([blk for hd in range(0, N_HEADS, 2) for blk in (v_of(hd), zv)],
                                   axis=1).astype(BF16)
    wvo_out[...] = jnp.concatenate([blk for hd in range(1, N_HEADS, 2) for blk in (zv, v_of(hd))],
                                   axis=1).astype(BF16)
    ws_out[...] = ws_ref[...].astype(BF16)


def _adaln(c_ctx, c, ada_w, ada_b, w_in_t, w_uq, w_ukv, w_s):
    cols = N_MODS * D_MODEL // ADALN_STEPS
    row_block = lambda a, width=None: pl.BlockSpec((a.shape[0] // ADALN_STEPS, width or a.shape[1]),
                                                   lambda j: (j, 0))
    whole = lambda shape: pl.BlockSpec(shape, lambda j: (0, 0), pipeline_mode=pl.Buffered(1))
    return pl.pallas_call(
        _adaln_kernel,
        grid=(ADALN_STEPS,),
        in_specs=[
            pl.BlockSpec(c_ctx.shape, lambda j: (0, 0)),
            pl.BlockSpec(c.shape, lambda j: (0, 0)),
            pl.BlockSpec((D_MODEL, cols), lambda j: (0, j)),
            pl.BlockSpec((1, cols), lambda j: (0, j)),
            whole(w_in_t.shape), row_block(w_uq), row_block(w_ukv), row_block(w_s),
        ],
        out_specs=[pl.BlockSpec((MOD_ROWS, 1, cols), lambda j: (0, 0, j)),
                   whole((IN_QKV + LANES, D_MODEL)), whole((2 * GMLP_W, D_MODEL)),
                   row_block(w_uq, QK_W), row_block(w_ukv, QK_W), row_block(w_ukv, ATTN_W),
                   row_block(w_ukv, ATTN_W), row_block(w_s)],
        out_shape=[jax.ShapeDtypeStruct((MOD_ROWS, 1, N_MODS * D_MODEL), F32),
                   jax.ShapeDtypeStruct((IN_QKV + LANES, D_MODEL), BF16),
                   jax.ShapeDtypeStruct((2 * GMLP_W, D_MODEL), BF16),
                   jax.ShapeDtypeStruct((Q_RANK, QK_W), BF16),
                   jax.ShapeDtypeStruct((KV_RANK, QK_W), BF16),
                   jax.ShapeDtypeStruct((KV_RANK, ATTN_W), BF16),
                   jax.ShapeDtypeStruct((KV_RANK, ATTN_W), BF16),
                   jax.ShapeDtypeStruct(w_s.shape, BF16)],
        compiler_params=_params(1),
        name="adaln",
    )(c_ctx, c, ada_w, ada_b, w_in_t, w_uq, w_ukv, w_s)


def _pre_kernel(*refs, latent, tm, seq, n_casts):
    (x_ref, mod_ref, n1g_ref, wqkv_ref, wuv_ref, qng_ref, wuq_ref, kvg_ref, wk_ref, wve_ref, wvo_ref,
     vng_ref, ws_ref, bs_ref, ogg_ref) = refs[:15]
    n_in = 15 + (4 if latent else 0)
    n_out = 5 + (3 if latent else 2)
    cast_in = refs[n_in:n_in + n_casts]
    outs = refs[n_in + n_casts:n_in + n_casts + n_out]
    cast_out = refs[n_in + n_casts + n_out:n_in + 2 * n_casts + n_out]
    go_sc = refs[-1]
    q_out, k_out, ve_out, vo_out, gm_out = outs[:5]
    for src, dst in zip(cast_in, cast_out):
        dst[...] = src[...].astype(BF16)
    shift1 = mod_ref[:, 0:D_MODEL]
    scale1 = mod_ref[:, D_MODEL:2 * D_MODEL]
    h = (_rms(x_ref[...], n1g_ref[...]) * (1.0 + scale1) + shift1).astype(BF16)

    lane = lax.broadcasted_iota(jnp.int32, (1, LANES), 1)
    rope_lanes = lane >= ROPE_LANE0
    first_half = (lane % 16) < 8

    def expand(ckv_b, kr_k, k_dst, ve_dst, vo_dst):
        kn = _dot(ckv_b, wk_ref[...])
        for hd in range(N_HEADS):
            sl = slice(hd * HEAD_PAD, (hd + 1) * HEAD_PAD)
            k_dst[:, sl] = (kn[:, sl] + kr_k).astype(BF16)
        ve_dst[...] = _dot(ckv_b, wve_ref[...]).astype(BF16)
        vo_dst[...] = _dot(ckv_b, wvo_ref[...]).astype(BF16)

    qkv = _dot_nt(h, wqkv_ref[...])
    qn = _rms(qkv[:, 0:Q_RANK], qng_ref[...]).astype(BF16)
    ckv = _rms(qkv[:, Q_RANK:IN_QKV], kvg_ref[...])
    q = _dot(qn, wuq_ref[...]) * (SM_SCALE * LOG2_E)
    kr = qkv[:, IN_QKV:IN_QKV + LANES]
    if latent:
        cos_ref, sin_ref, cckv_ref, ckrt_ref = refs[15:19]
        kc_out, vec_out, voc_out = outs[5:8]
        cos_q = cos_ref[...]

        def rotate(v, cos):
            swapped = jnp.where(first_half, pltpu.roll(v, LANES - 8, 1), pltpu.roll(v, 8, 1))
            return v * cos + swapped * sin_ref[...]

        kr_k = rotate(kr, jnp.where(rope_lanes, cos_q, 0.0))
        ckrt = ckrt_ref[...]
        padded = jnp.concatenate([ckrt, jnp.zeros((LANES - QK_ROPE, ckrt.shape[1]), F32)], axis=0)
        expand(cckv_ref[...].astype(BF16), pltpu.roll(padded.T, ROPE_LANE0, 1), kc_out, vec_out, voc_out)
    else:
        ckv_out, krt_out = outs[5:7]
        ckv_out[...] = ckv
        krt = kr.T
        for s in range(tm // seq):
            krt_out[s] = krt[0:QK_ROPE, s * seq:(s + 1) * seq]
        kr_k = jnp.where(rope_lanes, kr, 0.0)
    expand(ckv.astype(BF16), kr_k, k_out, ve_out, vo_out)
    for hd in range(N_HEADS):
        sl = slice(hd * HEAD_PAD, (hd + 1) * HEAD_PAD)
        qh = q[:, sl]
        if latent:
            qh = rotate(qh, cos_q)
        q_out[:, sl] = qh.astype(BF16)

    uv = _dot_nt(h, wuv_ref[...])
    gu = _gelu(uv[:, 0:GMLP_W])
    gv = _gelu(uv[:, GMLP_W:2 * GMLP_W])
    eye = (lax.broadcasted_iota(jnp.int32, (CHUNK, CHUNK), 0)
           == lax.broadcasted_iota(jnp.int32, (CHUNK, CHUNK), 1))
    for g in range(GMLP_GROUPS):
        sl = slice(g * GMLP_DG, (g + 1) * GMLP_DG)
        vg = gv[:, sl]
        vg = (vg * lax.rsqrt(jnp.mean(vg * vg, axis=-1, keepdims=True) + EPS) * vng_ref[:, sl]).astype(BF16)
        bias = jnp.sum(jnp.where(eye, bs_ref[g:g + 1, :], 0.0), axis=1, keepdims=True)
        chunks = [vg[n * CHUNK:(n + 1) * CHUNK, :] for n in range(tm // CHUNK)]
        s_all = _dot(ws_ref[g], jnp.concatenate(chunks, axis=1))
        for n in range(tm // CHUNK):
            rows = slice(n * CHUNK, (n + 1) * CHUNK)
            go_sc[rows, sl] = gu[rows, sl] * (s_all[:, n * GMLP_DG:(n + 1) * GMLP_DG] + bias)
    gm_out[...] = _rms(go_sc[...], ogg_ref[...]).astype(BF16)


def _pre(x, mods3, mod_row, p, seq, tm, casts, latent=None):
    n = x.shape[0]
    const = lambda i: (0, 0)
    tile = lambda i: (i, 0)
    in_specs = [
        pl.BlockSpec((tm, D_MODEL), tile),
        pl.BlockSpec((None, 1, N_MODS * D_MODEL), lambda i: (mod_row(i), 0, 0)),
        pl.BlockSpec((1, D_MODEL), const),
        pl.BlockSpec((IN_QKV + LANES, D_MODEL), const),
        pl.BlockSpec((2 * GMLP_W, D_MODEL), const),
        pl.BlockSpec((1, Q_RANK), const),
        pl.BlockSpec((Q_RANK, QK_W), const),
        pl.BlockSpec((1, KV_RANK), const),
        pl.BlockSpec((KV_RANK, QK_W), const),
        pl.BlockSpec((KV_RANK, ATTN_W), const),
        pl.BlockSpec((KV_RANK, ATTN_W), const),
        pl.BlockSpec((1, GMLP_W), const),
        pl.BlockSpec((GMLP_GROUPS, CHUNK, CHUNK), lambda i: (0, 0, 0)),
        pl.BlockSpec((GMLP_GROUPS, CHUNK), const),
        pl.BlockSpec((1, GMLP_W), const),
    ]
    args = [x, mods3, p["norm1_g"], p["w_qkv"], p["w_uv"], p["q_norm_g"], p["w_uq"],
            p["kv_norm_g"], p["w_k"], p["w_ve"], p["w_vo"], p["v_norm_g"], p["w_s"], p["b_s"], p["out_g_gmlp"]]
    out_specs = [pl.BlockSpec((tm, w), tile) for w in (QK_W, QK_W, ATTN_W, ATTN_W, GMLP_W)]
    out_shape = [jax.ShapeDtypeStruct((n, w), BF16) for w in (QK_W, QK_W, ATTN_W, ATTN_W, GMLP_W)]
    if latent is not None:
        cos_q, sin_q, cache_ckv, cache_krt = latent
        assert tm == seq == cos_q.shape[0]
        past = cache_krt.shape[2]
        in_specs += [pl.BlockSpec((tm, LANES), const), pl.BlockSpec((tm, LANES), const),
                     pl.BlockSpec((past, KV_RANK), tile),
                     pl.BlockSpec((None, QK_ROPE, past), lambda i: (i, 0, 0))]
        args += [cos_q, sin_q, cache_ckv, cache_krt]
        out_specs += [pl.BlockSpec((past, w), tile) for w in (QK_W, ATTN_W, ATTN_W)]
        out_shape += [jax.ShapeDtypeStruct((cache_ckv.shape[0], w), BF16) for w in (QK_W, ATTN_W, ATTN_W)]
    else:
        out_specs += [pl.BlockSpec((tm, KV_RANK), tile),
                      pl.BlockSpec((tm // seq, QK_ROPE, seq), lambda i: (i, 0, 0))]
        out_shape += [jax.ShapeDtypeStruct((n, KV_RANK), F32),
                      jax.ShapeDtypeStruct((n // seq, QK_ROPE, seq), F32)]
    n_out = len(out_specs)
    cast_specs = [pl.BlockSpec((w.shape[0] // (n // tm), w.shape[1]), tile) for w in casts]
    outs = pl.pallas_call(
        functools.partial(_pre_kernel, latent=latent is not None, tm=tm, seq=seq, n_casts=len(casts)),
        grid=(n // tm,),
        in_specs=in_specs + cast_specs,
        out_specs=out_specs + cast_specs,
        out_shape=out_shape + [jax.ShapeDtypeStruct(w.shape, BF16) for w in casts],
        scratch_shapes=[pltpu.VMEM((tm, GMLP_W), F32)],
        compiler_params=_params(1),
        name="pre_latent" if latent is not None else "pre",
    )(*args, *casts)
    return outs[:n_out], outs[n_out:]


def _attn_kernel(*refs, cached, group, seq, tq, pps, n_casts):
    n_main = 8 if cached else 5
    cast_in = refs[n_main:n_main + n_casts]
    out_ref = refs[n_main + n_casts]
    cast_out = refs[n_main + n_casts + 1:n_main + 2 * n_casts + 1]
    o_sc = refs[-1]
    if cached:
        q_ref, k_ref, ve_ref, vo_ref, kc_ref, vec_ref, voc_ref, oga_ref = refs[:n_main]
    else:
        q_ref, k_ref, ve_ref, vo_ref, oga_ref = refs[:n_main]
    for src, dst in zip(cast_in, cast_out):
        dst[...] = src[...].astype(BF16)
    step = pl.program_id(2)
    ones = jnp.ones((seq, LANES), BF16)
    ones_c = jnp.ones((kc_ref.shape[0], LANES), BF16) if cached else None
    for g in range(group):
        qrows = slice(g * tq, (g + 1) * tq)
        krows = slice(g * seq, (g + 1) * seq)
        for pp in range(pps):
            psl = slice(pp * LANES, (pp + 1) * LANES)
            acc = None
            for par, (v_ref, vc_ref) in enumerate(((ve_ref, vec_ref if cached else None),
                                                   (vo_ref, voc_ref if cached else None))):
                hd = 2 * pp + par
                hsl = slice(hd * HEAD_PAD, (hd + 1) * HEAD_PAD)
                qh = q_ref[qrows, hsl]
                s = _dot_nt(qh, k_ref[krows, hsl])
                m = jnp.max(s, axis=-1, keepdims=True)
                if cached:
                    sc = _dot_nt(qh, kc_ref[:, hsl])
                    m = jnp.maximum(m, jnp.max(sc, axis=-1, keepdims=True))
                e = jnp.exp2(s - m).astype(BF16)
                o = _dot(e, jnp.concatenate([v_ref[krows, psl], ones], axis=1))
                if cached:
                    ec = jnp.exp2(sc - m).astype(BF16)
                    o = o + _dot(ec, jnp.concatenate([vc_ref[:, psl], ones_c], axis=1))
                o = o[:, 0:LANES] * (1.0 / o[:, LANES:2 * LANES])
                acc = o if acc is None else acc + o
            o_sc[step * pps + pp, qrows, :] = acc

    @pl.when(step == N_HEADS // 2 // pps - 1)
    def _():
        blocks = [o_sc[j] for j in range(N_HEADS // 2)]
        ssq = sum(jnp.sum(b * b, axis=-1, keepdims=True) for b in blocks)
        r = lax.rsqrt(ssq * (1.0 / ATTN_W) + EPS)
        for j, b in enumerate(blocks):
            sl = slice(j * LANES, (j + 1) * LANES)
            out_ref[:, sl] = (b * r * oga_ref[:, sl]).astype(BF16)


def _attention(q, k, ve, vo, cache, oga, seq, tq, group, pps, casts):
    n = q.shape[0]
    nq = seq // tq
    n_pairs = N_HEADS // 2
    grid = (n // (group * seq), nq, n_pairs // pps)
    n_steps = grid[0] * grid[1] * grid[2]
    linear = lambda b, i, j: ((b * grid[1] + i) * grid[2] + j, 0)
    cast_specs = [pl.BlockSpec((w.shape[0] // n_steps, w.shape[1]), linear) for w in casts]
    qtile = lambda b, i, j: (b * nq + i, j)
    kv = lambda b, i, j: (b, j)
    qk_w, v_w = pps * 2 * HEAD_PAD, pps * LANES
    in_specs = [
        pl.BlockSpec((group * tq, qk_w), qtile),
        pl.BlockSpec((group * seq, qk_w), kv),
        pl.BlockSpec((group * seq, v_w), kv),
        pl.BlockSpec((group * seq, v_w), kv),
    ]
    args = [q, k, ve, vo]
    if cache is not None:
        kc, vec, voc = cache
        past = kc.shape[0] // (n // seq)
        in_specs += [pl.BlockSpec((past, qk_w), kv), pl.BlockSpec((past, v_w), kv),
                     pl.BlockSpec((past, v_w), kv)]
        args += [kc, vec, voc]
    in_specs.append(pl.BlockSpec((1, ATTN_W), lambda b, i, j: (0, 0)))
    args.append(oga)
    outs = pl.pallas_call(
        functools.partial(_attn_kernel, cached=cache is not None, group=group, seq=seq, tq=tq, pps=pps,
                          n_casts=len(casts)),
        grid=grid,
        in_specs=in_specs + cast_specs,
        out_specs=[pl.BlockSpec((group * tq, ATTN_W), lambda b, i, j: (b * nq + i, 0))] + cast_specs,
        out_shape=[jax.ShapeDtypeStruct((n, ATTN_W), BF16)]
        + [jax.ShapeDtypeStruct(w.shape, BF16) for w in casts],
        scratch_shapes=[pltpu.VMEM((n_pairs, group * tq, LANES), F32)],
        compiler_params=_params(3),
        name="attn_cached" if cache is not None else "attn",
    )(*args, *casts)
    return outs[0], outs[1:]


def _ffn_pass(refs, tile, nseg, seg, tiles_per_seq):
    halo = tiles_per_seq > 1
    if halo:
        (x_ref, an_ref, gm_ref, xp_ref, xn_ref, anp_ref, ann_ref, gmp_ref, gmn_ref, mod_ref, wo_ref, n2g_ref,
         wup_ref, cw_ref, cb_ref, wd_ref, fg_ref, out_ref, lhs_sc, a_sc, g_sc, mix_sc) = refs
    else:
        (x_ref, an_ref, gm_ref, mod_ref, wo_ref, n2g_ref,
         wup_ref, cw_ref, cb_ref, wd_ref, fg_ref, out_ref, lhs_sc, a_sc, g_sc) = refs
    tm = nseg * seg
    rows = tm + (HALO if halo else 0)
    stride = seg + HALO
    edge = HALO // 2
    gate1 = mod_ref[:, 2 * D_MODEL:3 * D_MODEL]
    shift2 = mod_ref[:, 3 * D_MODEL:4 * D_MODEL]
    scale2 = mod_ref[:, 4 * D_MODEL:5 * D_MODEL]

    def residual_norm(x, y, keep=None):
        x1 = x + gate1 * y
        h2 = _rms(x1, n2g_ref[...]) * (1.0 + scale2) + shift2
        if keep is not None:
            h2 = jnp.where(keep, h2, 0.0)
        return x1, h2.astype(BF16)

    if halo:
        def neighbours(prev_ref, next_ref):
            return jnp.concatenate([prev_ref[...].astype(F32)[edge:HALO, :],
                                    next_ref[...].astype(F32)[0:edge, :]], axis=0)

        mix_sc[0:tm, 0:ATTN_W] = an_ref[...]
        mix_sc[0:tm, ATTN_W:D_MODEL] = gm_ref[...]
        mix_sc[tm:tm + HALO, 0:ATTN_W] = neighbours(anp_ref, ann_ref).astype(BF16)
        mix_sc[tm:tm + HALO, ATTN_W:D_MODEL] = neighbours(gmp_ref, gmn_ref).astype(BF16)
        y = _dot(mix_sc[...], wo_ref[...])
        x1, h2 = residual_norm(x_ref[...], y[0:tm])
        out_ref[...] = x1
        lhs_sc[0:tm, :] = h2
        pos = tile % tiles_per_seq
        is_prev = lax.broadcasted_iota(jnp.int32, (HALO, 1), 0) < edge
        has_prev = (pos != 0).astype(jnp.int32)
        has_next = (pos != tiles_per_seq - 1).astype(jnp.int32)
        keep = jnp.where(is_prev, has_prev, has_next) > 0
        lhs_sc[tm:tm + HALO, :] = residual_norm(neighbours(xp_ref, xn_ref), y[tm:tm + HALO], keep)[1]
    else:
        y = _dot(an_ref[...], wo_ref[0:ATTN_W, :]) + _dot(gm_ref[...], wo_ref[ATTN_W:D_MODEL, :])
        x1, h2 = residual_norm(x_ref[...], y)
        out_ref[...] = x1
        lhs_sc[0:tm, :] = h2

        @pl.when(tile == 0)
        def _():
            zeros = jnp.zeros((A_SLOTS, 2 * FF_CHUNK // LANES, HALO, LANES), F32)
            a_sc[:, :, 0:edge, :] = zeros[:, :, 0:edge, :]
            for s in range(1, nseg):
                a_sc[:, :, s * stride - edge:s * stride + edge, :] = zeros
            a_sc[:, :, nseg * stride - edge:nseg * stride, :] = zeros[:, :, 0:edge, :]

    for c in range(N_FF_CHUNKS):
        slot = c % A_SLOTS
        for k, col0 in enumerate((c * FF_CHUNK, D_FF + c * FF_CHUNK)):
            a = _dot(lhs_sc[0:rows, :], wup_ref[:, col0:col0 + FF_CHUNK])
            for j in range(FF_CHUNK // LANES):
                lanes = slice(j * LANES, (j + 1) * LANES)
                for s in range(nseg):
                    row0 = edge + s * stride
                    a_sc[slot, 2 * k + j, row0:row0 + seg, :] = a[s * seg:(s + 1) * seg, lanes]
                if halo:
                    a_sc[slot, 2 * k + j, 0:edge, :] = a[tm:tm + edge, lanes]
                    a_sc[slot, 2 * k + j, edge + seg:HALO + seg, :] = a[tm + edge:tm + HALO, lanes]
        for j in range(FF_CHUNK // LANES):
            gl = slice(c * FF_CHUNK + j * LANES, c * FF_CHUNK + (j + 1) * LANES)
            vl = slice(D_FF + gl.start, D_FF + gl.stop)
            for s in range(nseg):
                base = edge + s * stride

                def conv(k, lanes):
                    acc = cb_ref[:, lanes]
                    for t in range(CONV_W):
                        tap = slice(t * 2 * D_FF + lanes.start, t * 2 * D_FF + lanes.stop)
                        acc = acc + a_sc[slot, k, base - 1 + t:base - 1 + t + seg, :] * cw_ref[:, tap]
                    return acc

                g = _silu(conv(j, gl)) * conv(2 + j, vl)
                g_sc[s * seg:(s + 1) * seg, gl] = g.astype(BF16)

    x2 = out_ref[...] + mod_ref[:, 5 * D_MODEL:6 * D_MODEL] * _dot(g_sc[...], wd_ref[...])
    out_ref[...] = _rms(x2, fg_ref[...])


def _ffn_kernel(*refs, passes):
    n_tile_refs = [9 if tps > 1 else 3 for _, _, tps, _ in passes]
    n_in = sum(n_tile_refs)
    mod_refs = refs[n_in:n_in + len(passes)]
    shared = refs[n_in + len(passes):n_in + len(passes) + 7]
    out_refs = refs[n_in + len(passes) + 7:n_in + 2 * len(passes) + 7]
    lhs_sc, a_sc, g_sc, mix_sc = refs[n_in + 2 * len(passes) + 7:]
    step = pl.program_id(0)
    first, at = 0, 0
    for k, (nseg, seg, tps, n_tiles) in enumerate(passes):
        tile_refs = refs[at:at + n_tile_refs[k]]
        at += n_tile_refs[k]

        @pl.when((step >= first) & (step < first + n_tiles))
        def _(k=k, tile_refs=tile_refs, first=first, nseg=nseg, seg=seg, tps=tps):
            scratch = (lhs_sc, a_sc, g_sc) + ((mix_sc,) if tps > 1 else ())
            _ffn_pass((*tile_refs, mod_refs[k], *shared, out_refs[k], *scratch), step - first, nseg, seg, tps)

        first += n_tiles


def _ffn(pass_args, mods3, p, ffn_w, tm):
    const = lambda i: (0, 0)
    resident = dict(pipeline_mode=pl.Buffered(1))
    in_specs, args, mod_specs, out_specs, out_shape, passes = [], [], [], [], [], []
    first = 0
    for x, an, gm, mod_row, seq in pass_args:
        n = x.shape[0]
        n_tiles = n // tm
        tiles_per_seq = max(seq // tm, 1)
        seg = min(seq, tm)
        passes.append((tm // seg, seg, tiles_per_seq, n_tiles))
        t = lambda i, first=first, n_tiles=n_tiles: jnp.clip(i - first, 0, n_tiles - 1)
        tile = lambda i, t=t: (t(i), 0)
        in_specs += [pl.BlockSpec((tm, D_MODEL), tile), pl.BlockSpec((tm, ATTN_W), tile),
                     pl.BlockSpec((tm, GMLP_W), tile)]
        args += [x, an, gm]
        if tiles_per_seq > 1:
            per = tm // HALO
            last = n // HALO - 1
            prev = lambda i, t=t, per=per: (jnp.maximum(t(i) * per - 1, 0), 0)
            nxt = lambda i, t=t, per=per, last=last: (jnp.minimum((t(i) + 1) * per, last), 0)
            for arr, width in ((x, D_MODEL), (an, ATTN_W), (gm, GMLP_W)):
                in_specs += [pl.BlockSpec((HALO, width), prev), pl.BlockSpec((HALO, width), nxt)]
                args += [arr, arr]
        mod_specs.append(pl.BlockSpec((None, 1, N_MODS * D_MODEL),
                                      lambda i, t=t, mod_row=mod_row: (mod_row(t(i)), 0, 0)))
        out_specs.append(pl.BlockSpec((tm, D_MODEL), tile))
        out_shape.append(jax.ShapeDtypeStruct((n, D_MODEL), F32))
        first += n_tiles
    in_specs += mod_specs + [
        pl.BlockSpec((D_MODEL, D_MODEL), const, **resident),
        pl.BlockSpec((1, D_MODEL), const),
        pl.BlockSpec((D_MODEL, 2 * D_FF), const, **resident),
        pl.BlockSpec((1, CONV_W * 2 * D_FF), const),
        pl.BlockSpec((1, 2 * D_FF), const),
        pl.BlockSpec((D_FF, D_MODEL), const, **resident),
        pl.BlockSpec((1, D_MODEL), const),
    ]
    w_o, w_up, w_down = ffn_w
    args += [mods3] * len(pass_args) + [w_o, p["norm2_g"], w_up, p["conv_w"], p["conv_b"], w_down, p["final_g"]]
    rows = tm + HALO
    a_rows = max(nseg * (seg + HALO) for nseg, seg, _, _ in passes)
    return pl.pallas_call(
        functools.partial(_ffn_kernel, passes=tuple(passes)),
        grid=(first,),
        in_specs=in_specs,
        out_specs=out_specs,
        out_shape=out_shape,
        scratch_shapes=[pltpu.VMEM((rows, D_MODEL), BF16),
                        pltpu.VMEM((A_SLOTS, 2 * FF_CHUNK // LANES, a_rows, LANES), F32),
                        pltpu.VMEM((tm, D_FF), BF16),
                        pltpu.VMEM((rows, D_MODEL), BF16)],
        compiler_params=_params(1),
        name="ffn",
    )(*args)


def _rope_tables(length):
    pos = np.arange(length)
    row = (pos // GRID_W).astype(np.float32)
    col = (pos % GRID_W).astype(np.float32)
    n_freq = QK_ROPE // 4
    inv = (np.float32(ROPE_THETA) ** (-(np.arange(n_freq, dtype=np.float32) / np.float32(n_freq)))).astype(np.float32)
    ang_r, ang_c = row[:, None] * inv, col[:, None] * inv
    cos32 = np.concatenate([np.cos(ang_r)] * 2 + [np.cos(ang_c)] * 2, axis=-1)
    sin32 = np.concatenate([-np.sin(ang_r), np.sin(ang_r), -np.sin(ang_c), np.sin(ang_c)], axis=-1)
    tail = LANES - ROPE_LANE0 - QK_ROPE
    cos_q = np.concatenate([np.ones((length, ROPE_LANE0)), cos32, np.zeros((length, tail))], -1)
    sin_q = np.concatenate([np.zeros((length, ROPE_LANE0)), sin32, np.zeros((length, tail))], -1)
    return jnp.asarray(cos_q, F32), jnp.asarray(sin_q, F32)


def _layer_params(l, norm1_g, q_norm_g, kv_norm_g, v_norm_g, b_s, out_g_attn,
                  out_g_gmlp, norm2_g, conv_w, conv_b, final_g):
    row = lambda a: a.reshape(1, -1)
    return {
        "norm1_g": row(norm1_g[l]), "q_norm_g": row(q_norm_g[l]),
        "kv_norm_g": row(kv_norm_g[l]), "v_norm_g": row(v_norm_g[l]), "b_s": b_s[l],
        "out_g_attn": row(out_g_attn[l]), "out_g_gmlp": row(out_g_gmlp[l]), "norm2_g": row(norm2_g[l]),
        "conv_w": conv_w[l].reshape(1, CONV_W * 2 * D_FF), "conv_b": row(conv_b[l]), "final_g": row(final_g),
    }


def kernel(x_prompt, x_sample, cache_ckv, cache_krope, c, c_ctx, ada_w, ada_b, norm1_g, w_in, q_norm_g, w_uq, kv_norm_g, w_ukv, v_norm_g, w_s, b_s, out_g_attn, out_g_gmlp, w_o, norm2_g, w_up, conv_w, conv_b, w_down, final_g):
    batch, seq, _ = x_prompt.shape
    dec_batch, dec_seq, _ = x_sample.shape
    depth, past = cache_ckv.shape[1], cache_ckv.shape[2]
    assert depth == 1 and dec_batch + 1 <= MOD_ROWS
    l = 0
    p = _layer_params(l, norm1_g, q_norm_g, kv_norm_g, v_norm_g, b_s, out_g_attn,
                      out_g_gmlp, norm2_g, conv_w, conv_b, final_g)

    mods3, p["w_qkv"], p["w_uv"], p["w_uq"], p["w_k"], p["w_ve"], p["w_vo"], w_s_b = _adaln(
        c_ctx.reshape(1, D_MODEL), c, ada_w[l], ada_b[l].reshape(1, -1), w_in[l].T, w_uq[l], w_ukv[l],
        w_s[l].reshape(GMLP_GROUPS * CHUNK, CHUNK))
    p["w_s"] = w_s_b.reshape(GMLP_GROUPS, CHUNK, CHUNK)
    tm_pre, tm_ffn = 1024, 512
    ctx_row = lambda i: 0
    lat_row = lambda tm: lambda i: 1 + i // (dec_seq // tm)

    xp = x_prompt.reshape(batch * seq, D_MODEL)
    xs = x_sample.reshape(dec_batch * dec_seq, D_MODEL)
    (qp, kp, vep, vop, gmp, ckv_p, krt_p), (w_down_b,) = _pre(xp, mods3, ctx_row, p, seq, tm_pre, [w_down[l]])
    cache_in = (*_rope_tables(dec_seq), cache_ckv[:, l].reshape(dec_batch * past, KV_RANK),
                jnp.swapaxes(cache_krope[:, l], 1, 2))
    (qs, ks, ves, vos, gms, *cache), (w_o_b,) = _pre(xs, mods3, lat_row(tm_pre), p, dec_seq, tm_pre, [w_o[l]],
                                                     latent=cache_in)

    anp, _ = _attention(qp, kp, vep, vop, None, p["out_g_attn"], seq, seq, 4, 4, [])
    ans, (w_up_b,) = _attention(qs, ks, ves, vos, cache, p["out_g_attn"], dec_seq, 512, 1, 4, [w_up[l]])
    ffn_w = (w_o_b, w_up_b, w_down_b)
    yp, ys = _ffn([(xp, anp, gmp, ctx_row, seq), (xs, ans, gms, lat_row(tm_ffn), dec_seq)], mods3, p, ffn_w, tm_ffn)

    return (yp.reshape(batch, seq, D_MODEL), ys.reshape(dec_batch, dec_seq, D_MODEL),
            ckv_p.reshape(batch, 1, seq, KV_RANK), jnp.swapaxes(krt_p, 1, 2).reshape(batch, 1, seq, QK_ROPE))
```

```python
import functools

import numpy as np
import jax
import jax.numpy as jnp
from jax import lax
from jax.experimental import pallas as pl
from jax.experimental.pallas import tpu as pltpu

D_MODEL = 1024
GRID_W = 64
CHUNK = 128
N_HEADS = 8
QK_NOPE = 64
QK_ROPE = 32
V_DIM = 64
Q_RANK = 384
KV_RANK = 256
ATTN_W = N_HEADS * V_DIM
GMLP_W = D_MODEL - ATTN_W
GMLP_GROUPS = 4
GMLP_DG = GMLP_W // GMLP_GROUPS
D_FF = 2816
CONV_W = 3
ROPE_THETA = 10000.0
EPS = 1e-6
SM_SCALE = (QK_NOPE + QK_ROPE) ** -0.5
LOG2_E = 1.4426950408889634

LANES = 128
HEAD_PAD = LANES
ROPE_LANE0 = QK_NOPE
QK_W = N_HEADS * HEAD_PAD
N_MODS = 6
MOD_ROWS = 8
ADALN_STEPS = 4
IN_QKV = Q_RANK + KV_RANK
FF_CHUNK = 256
HALO = 16
N_FF_CHUNKS = D_FF // FF_CHUNK
A_SLOTS = 4
VMEM_LIMIT = 56 * 1024 * 1024

F32 = jnp.float32
BF16 = jnp.bfloat16


def _dot(a, b):
    return jnp.dot(a, b, preferred_element_type=F32)


def _dot_nt(a, b):
    return lax.dot_general(a, b, (((1,), (1,)), ((), ())), preferred_element_type=F32)


def _rms(x, g):
    return x * lax.rsqrt(jnp.mean(x * x, axis=-1, keepdims=True) + EPS) * g


def _gelu(x):
    return 0.5 * x * (1.0 + lax.erf(x * (0.5 ** 0.5)))


def _silu(x):
    return x * (1.0 / (1.0 + jnp.exp(-x)))


def _params(n_axes):
    return pltpu.CompilerParams(dimension_semantics=("arbitrary",) * n_axes,
                                vmem_limit_bytes=VMEM_LIMIT)


def _adaln_kernel(cctx_ref, c_ref, w_ref, b_ref, win_ref, wuq_ref, wukv_ref, ws_ref,
                  out_ref, wqkv_out, wuv_out, wuq_out, wk_out, wve_out, wvo_out, ws_out):
    pad = jnp.zeros((MOD_ROWS - 1 - c_ref.shape[0], D_MODEL), F32)
    cond = jnp.concatenate([cctx_ref[...], c_ref[...], pad], axis=0)
    s = _silu(cond).astype(BF16)
    out_ref[:, 0, :] = _dot(s, w_ref[...].astype(BF16)) + b_ref[...]
    @pl.when(pl.program_id(0) == 0)
    def _():
        kr = win_ref[IN_QKV:IN_QKV + QK_ROPE, :].astype(BF16)
        z = jnp.zeros_like(kr)
        wqkv_out[0:IN_QKV, :] = win_ref[0:IN_QKV, :].astype(BF16)
        for t, blk in enumerate((kr, z, kr, z)):
            wqkv_out[IN_QKV + t * QK_ROPE:IN_QKV + (t + 1) * QK_ROPE, :] = blk
        wuv_out[...] = win_ref[IN_QKV + QK_ROPE:, :].astype(BF16)

    w = wuq_ref[...]
    hw = QK_NOPE + QK_ROPE
    zq = jnp.zeros((w.shape[0], HEAD_PAD - hw), F32)
    wuq_out[...] = jnp.concatenate([blk for hd in range(N_HEADS) for blk in (w[:, hd * hw:(hd + 1) * hw], zq)],
                                   axis=1).astype(BF16)
    w = wukv_ref[...]
    zv = jnp.zeros((w.shape[0], V_DIM), F32)
    v_of = lambda hd: w[:, hd * HEAD_PAD + QK_NOPE:(hd + 1) * HEAD_PAD]
    wk_out[...] = jnp.concatenate([blk for hd in range(N_HEADS)
                                   for blk in (w[:, hd * HEAD_PAD:hd * HEAD_PAD + QK_NOPE], zv)], axis=1).astype(BF16)
    wve_out[...] = jnp.concatenate([blk for hd in range(0, N_HEADS, 2) for blk in (v_of(hd), zv)],
                                   axis=1).astype(BF16)
    wvo_out[...] = jnp.concatenate([blk for hd in range(1, N_HEADS, 2) for blk in (zv, v_of(hd))],
                                   axis=1).astype(BF16)
    ws_out[...] = ws_ref[...].astype(BF16)


def _adaln(c_ctx, c, ada_w, ada_b, w_in_t, w_uq, w_ukv, w_s):
    cols = N_MODS * D_MODEL // ADALN_STEPS
    row_block = lambda a, width=None: pl.BlockSpec((a.shape[0] // ADALN_STEPS, width or a.shape[1]),
                                                   lambda j: (j, 0))
    whole = lambda shape: pl.BlockSpec(shape, lambda j: (0, 0), pipeline_mode=pl.Buffered(1))
    return pl.pallas_call(
        _adaln_kernel,
        grid=(ADALN_STEPS,),
        in_specs=[
            pl.BlockSpec(c_ctx.shape, lambda j: (0, 0)),
            pl.BlockSpec(c.shape, lambda j: (0, 0)),
            pl.BlockSpec((D_MODEL, cols), lambda j: (0, j)),
            pl.BlockSpec((1, cols), lambda j: (0, j)),
            whole(w_in_t.shape), row_block(w_uq), row_block(w_ukv), row_block(w_s),
        ],
        out_specs=[pl.BlockSpec((MOD_ROWS, 1, cols), lambda j: (0, 0, j)),
                   whole((IN_QKV + LANES, D_MODEL)), whole((2 * GMLP_W, D_MODEL)),
                   row_block(w_uq, QK_W), row_block(w_ukv, QK_W), row_block(w_ukv, ATTN_W),
                   row_block(w_ukv, ATTN_W), row_block(w_s)],
        out_shape=[jax.ShapeDtypeStruct((MOD_ROWS, 1, N_MODS * D_MODEL), F32),
                   jax.ShapeDtypeStruct((IN_QKV + LANES, D_MODEL), BF16),
                   jax.ShapeDtypeStruct((2 * GMLP_W, D_MODEL), BF16),
                   jax.ShapeDtypeStruct((Q_RANK, QK_W), BF16),
                   jax.ShapeDtypeStruct((KV_RANK, QK_W), BF16),
                   jax.ShapeDtypeStruct((KV_RANK, ATTN_W), BF16),
                   jax.ShapeDtypeStruct((KV_RANK, ATTN_W), BF16),
                   jax.ShapeDtypeStruct(w_s.shape, BF16)],
        compiler_params=_params(1),
        name="adaln",
    )(c_ctx, c, ada_w, ada_b, w_in_t, w_uq, w_ukv, w_s)


def _pre_kernel(*refs, latent, tm, seq, n_casts):
    (x_ref, mod_ref, n1g_ref, wqkv_ref, wuv_ref, qng_ref, wuq_ref, kvg_ref, wk_ref, wve_ref, wvo_ref,
     vng_ref, ws_ref, bs_ref, ogg_ref) = refs[:15]
    n_in = 15 + (4 if latent else 0)
    n_out = 5 + (0 if latent else 2)
    cast_in = refs[n_in:n_in + n_casts]
    outs = refs[n_in + n_casts:n_in + n_casts + n_out]
    cast_out = refs[n_in + n_casts + n_out:n_in + 2 * n_casts + n_out]
    go_sc = refs[-1]
    q_out, k_out, ve_out, vo_out, gm_out = outs[:5]
    for src, dst in zip(cast_in, cast_out):
        dst[...] = src[...].astype(BF16)
    shift1 = mod_ref[:, 0:D_MODEL]
    scale1 = mod_ref[:, D_MODEL:2 * D_MODEL]
    h = (_rms(x_ref[...], n1g_ref[...]) * (1.0 + scale1) + shift1).astype(BF16)

    lane = lax.broadcasted_iota(jnp.int32, (1, LANES), 1)
    rope_lanes = lane >= ROPE_LANE0
    first_half = (lane % 16) < 8

    def expand(ckv_b, kr_k, rows):
        kn = _dot(ckv_b, wk_ref[...])
        for hd in range(N_HEADS):
            sl = slice(hd * HEAD_PAD, (hd + 1) * HEAD_PAD)
            k_out[rows, sl] = (kn[:, sl] + kr_k).astype(BF16)
        ve_out[rows, :] = _dot(ckv_b, wve_ref[...]).astype(BF16)
        vo_out[rows, :] = _dot(ckv_b, wvo_ref[...]).astype(BF16)

    qkv = _dot_nt(h, wqkv_ref[...])
    qn = _rms(qkv[:, 0:Q_RANK], qng_ref[...]).astype(BF16)
    ckv = _rms(qkv[:, Q_RANK:IN_QKV], kvg_ref[...])
    q = _dot(qn, wuq_ref[...]) * (SM_SCALE * LOG2_E)
    kr = qkv[:, IN_QKV:IN_QKV + LANES]
    if latent:
        cos_ref, sin_ref, cckv_ref, ckrt_ref = refs[15:19]
        cos_q = cos_ref[...]

        def rotate(v, cos):
            swapped = jnp.where(first_half, pltpu.roll(v, LANES - 8, 1), pltpu.roll(v, 8, 1))
            return v * cos + swapped * sin_ref[...]

        kr_k = rotate(kr, jnp.where(rope_lanes, cos_q, 0.0))
        ckrt = ckrt_ref[...]
        past = ckrt.shape[1]
        padded = jnp.concatenate([ckrt, jnp.zeros((LANES - QK_ROPE, past), F32)], axis=0)
        expand(cckv_ref[...].astype(BF16), pltpu.roll(padded.T, ROPE_LANE0, 1), slice(0, past))
        own_rows = slice(past, past + tm)
    else:
        ckv_out, krt_out = outs[5:7]
        ckv_out[...] = ckv
        krt = kr.T
        for s in range(tm // seq):
            krt_out[s] = krt[0:QK_ROPE, s * seq:(s + 1) * seq]
        kr_k = jnp.where(rope_lanes, kr, 0.0)
        own_rows = slice(0, tm)
    expand(ckv.astype(BF16), kr_k, own_rows)
    for hd in range(N_HEADS):
        sl = slice(hd * HEAD_PAD, (hd + 1) * HEAD_PAD)
        qh = q[:, sl]
        if latent:
            qh = rotate(qh, cos_q)
        q_out[:, sl] = qh.astype(BF16)

    uv = _dot_nt(h, wuv_ref[...])
    gu = _gelu(uv[:, 0:GMLP_W])
    gv = _gelu(uv[:, GMLP_W:2 * GMLP_W])
    eye = (lax.broadcasted_iota(jnp.int32, (CHUNK, CHUNK), 0)
           == lax.broadcasted_iota(jnp.int32, (CHUNK, CHUNK), 1))
    for g in range(GMLP_GROUPS):
        sl = slice(g * GMLP_DG, (g + 1) * GMLP_DG)
        vg = gv[:, sl]
        vg = (vg * lax.rsqrt(jnp.mean(vg * vg, axis=-1, keepdims=True) + EPS) * vng_ref[:, sl]).astype(BF16)
        bias = jnp.sum(jnp.where(eye, bs_ref[g:g + 1, :], 0.0), axis=1, keepdims=True)
        chunks = [vg[n * CHUNK:(n + 1) * CHUNK, :] for n in range(tm // CHUNK)]
        s_all = _dot(ws_ref[g], jnp.concatenate(chunks, axis=1))
        for n in range(tm // CHUNK):
            rows = slice(n * CHUNK, (n + 1) * CHUNK)
            go_sc[rows, sl] = gu[rows, sl] * (s_all[:, n * GMLP_DG:(n + 1) * GMLP_DG] + bias)
    gm_out[...] = _rms(go_sc[...], ogg_ref[...]).astype(BF16)


def _pre(x, mods3, mod_row, p, seq, tm, casts, latent=None):
    n = x.shape[0]
    const = lambda i: (0, 0)
    tile = lambda i: (i, 0)
    in_specs = [
        pl.BlockSpec((tm, D_MODEL), tile),
        pl.BlockSpec((None, 1, N_MODS * D_MODEL), lambda i: (mod_row(i), 0, 0)),
        pl.BlockSpec((1, D_MODEL), const),
        pl.BlockSpec((IN_QKV + LANES, D_MODEL), const),
        pl.BlockSpec((2 * GMLP_W, D_MODEL), const),
        pl.BlockSpec((1, Q_RANK), const),
        pl.BlockSpec((Q_RANK, QK_W), const),
        pl.BlockSpec((1, KV_RANK), const),
        pl.BlockSpec((KV_RANK, QK_W), const),
        pl.BlockSpec((KV_RANK, ATTN_W), const),
        pl.BlockSpec((KV_RANK, ATTN_W), const),
        pl.BlockSpec((1, GMLP_W), const),
        pl.BlockSpec((GMLP_GROUPS, CHUNK, CHUNK), lambda i: (0, 0, 0)),
        pl.BlockSpec((GMLP_GROUPS, CHUNK), const),
        pl.BlockSpec((1, GMLP_W), const),
    ]
    args = [x, mods3, p["norm1_g"], p["w_qkv"], p["w_uv"], p["q_norm_g"], p["w_uq"],
            p["kv_norm_g"], p["w_k"], p["w_ve"], p["w_vo"], p["v_norm_g"], p["w_s"], p["b_s"], p["out_g_gmlp"]]
    past = 0 if latent is None else latent[3].shape[2]
    kv_rows = tm + past
    out_rows = (tm, kv_rows, kv_rows, kv_rows, tm)
    widths = (QK_W, QK_W, ATTN_W, ATTN_W, GMLP_W)
    out_specs = [pl.BlockSpec((r, w), tile) for r, w in zip(out_rows, widths)]
    out_shape = [jax.ShapeDtypeStruct((n // tm * r, w), BF16) for r, w in zip(out_rows, widths)]
    if latent is not None:
        cos_q, sin_q, cache_ckv, cache_krt = latent
        assert tm == seq == cos_q.shape[0]
        in_specs += [pl.BlockSpec((tm, LANES), const), pl.BlockSpec((tm, LANES), const),
                     pl.BlockSpec((past, KV_RANK), tile),
                     pl.BlockSpec((None, QK_ROPE, past), lambda i: (i, 0, 0))]
        args += [cos_q, sin_q, cache_ckv, cache_krt]
    else:
        out_specs += [pl.BlockSpec((tm, KV_RANK), tile),
                      pl.BlockSpec((tm // seq, QK_ROPE, seq), lambda i: (i, 0, 0))]
        out_shape += [jax.ShapeDtypeStruct((n, KV_RANK), F32),
                      jax.ShapeDtypeStruct((n // seq, QK_ROPE, seq), F32)]
    n_out = len(out_specs)
    cast_specs = [pl.BlockSpec((w.shape[0] // (n // tm), w.shape[1]), tile) for w in casts]
    outs = pl.pallas_call(
        functools.partial(_pre_kernel, latent=latent is not None, tm=tm, seq=seq, n_casts=len(casts)),
        grid=(n // tm,),
        in_specs=in_specs + cast_specs,
        out_specs=out_specs + cast_specs,
        out_shape=out_shape + [jax.ShapeDtypeStruct(w.shape, BF16) for w in casts],
        scratch_shapes=[pltpu.VMEM((tm, GMLP_W), F32)],
        compiler_params=_params(1),
        name="pre_latent" if latent is not None else "pre",
    )(*args, *casts)
    return outs[:n_out], outs[n_out:]


def _attn_kernel(*refs, group, kv_len, tq, pps, n_casts):
    q_ref, k_ref, ve_ref, vo_ref, oga_ref = refs[:5]
    cast_in = refs[5:5 + n_casts]
    out_ref = refs[5 + n_casts]
    cast_out = refs[6 + n_casts:6 + 2 * n_casts]
    o_sc = refs[-1]
    for src, dst in zip(cast_in, cast_out):
        dst[...] = src[...].astype(BF16)
    step = pl.program_id(2)
    ones = jnp.ones((kv_len, LANES), BF16)
    for g in range(group):
        qrows = slice(g * tq, (g + 1) * tq)
        krows = slice(g * kv_len, (g + 1) * kv_len)
        for pp in range(pps):
            psl = slice(pp * LANES, (pp + 1) * LANES)
            acc = None
            for par, v_ref in enumerate((ve_ref, vo_ref)):
                hd = 2 * pp + par
                hsl = slice(hd * HEAD_PAD, (hd + 1) * HEAD_PAD)
                s = _dot_nt(q_ref[qrows, hsl], k_ref[krows, hsl])
                e = jnp.exp2(s - jnp.max(s, axis=-1, keepdims=True)).astype(BF16)
                o = _dot(e, jnp.concatenate([v_ref[krows, psl], ones], axis=1))
                o = o[:, 0:LANES] * (1.0 / o[:, LANES:2 * LANES])
                acc = o if acc is None else acc + o
            o_sc[step * pps + pp, qrows, :] = acc

    @pl.when(step == N_HEADS // 2 // pps - 1)
    def _():
        blocks = [o_sc[j] for j in range(N_HEADS // 2)]
        ssq = sum(jnp.sum(b * b, axis=-1, keepdims=True) for b in blocks)
        r = lax.rsqrt(ssq * (1.0 / ATTN_W) + EPS)
        for j, b in enumerate(blocks):
            sl = slice(j * LANES, (j + 1) * LANES)
            out_ref[:, sl] = (b * r * oga_ref[:, sl]).astype(BF16)


def _attention(q, k, ve, vo, oga, seq, kv_len, tq, group, pps, casts):
    n = q.shape[0]
    nq = seq // tq
    n_pairs = N_HEADS // 2
    grid = (n // (group * seq), nq, n_pairs // pps)
    n_steps = grid[0] * grid[1] * grid[2]
    linear = lambda b, i, j: ((b * grid[1] + i) * grid[2] + j, 0)
    cast_specs = [pl.BlockSpec((w.shape[0] // n_steps, w.shape[1]), linear) for w in casts]
    qtile = lambda b, i, j: (b * nq + i, j)
    kv = lambda b, i, j: (b, j)
    qk_w, v_w = pps * 2 * HEAD_PAD, pps * LANES
    in_specs = [
        pl.BlockSpec((group * tq, qk_w), qtile),
        pl.BlockSpec((group * kv_len, qk_w), kv),
        pl.BlockSpec((group * kv_len, v_w), kv),
        pl.BlockSpec((group * kv_len, v_w), kv),
        pl.BlockSpec((1, ATTN_W), lambda b, i, j: (0, 0)),
    ]
    outs = pl.pallas_call(
        functools.partial(_attn_kernel, group=group, kv_len=kv_len, tq=tq, pps=pps, n_casts=len(casts)),
        grid=grid,
        in_specs=in_specs + cast_specs,
        out_specs=[pl.BlockSpec((group * tq, ATTN_W), lambda b, i, j: (b * nq + i, 0))] + cast_specs,
        out_shape=[jax.ShapeDtypeStruct((n, ATTN_W), BF16)]
        + [jax.ShapeDtypeStruct(w.shape, BF16) for w in casts],
        scratch_shapes=[pltpu.VMEM((n_pairs, group * tq, LANES), F32)],
        compiler_params=_params(3),
        name="attn",
    )(q, k, ve, vo, oga, *casts)
    return outs[0], outs[1:]


def _ffn_kernel(*refs, nseg, seg, tiles_per_seq):
    halo = tiles_per_seq > 1
    if halo:
        (x_ref, an_ref, gm_ref, xp_ref, xn_ref, anp_ref, ann_ref, gmp_ref, gmn_ref, mod_ref, wo_ref, n2g_ref,
         wup_ref, cw_ref, cb_ref, wd_ref, fg_ref, out_ref, lhs_sc, a_sc, g_sc, mix_sc) = refs
    else:
        (x_ref, an_ref, gm_ref, mod_ref, wo_ref, n2g_ref,
         wup_ref, cw_ref, cb_ref, wd_ref, fg_ref, out_ref, lhs_sc, a_sc, g_sc) = refs
    tm = nseg * seg
    stride = seg + HALO
    edge = HALO // 2
    gate1 = mod_ref[:, 2 * D_MODEL:3 * D_MODEL]
    shift2 = mod_ref[:, 3 * D_MODEL:4 * D_MODEL]
    scale2 = mod_ref[:, 4 * D_MODEL:5 * D_MODEL]

    def residual_norm(x, y, keep=None):
        x1 = x + gate1 * y
        h2 = _rms(x1, n2g_ref[...]) * (1.0 + scale2) + shift2
        if keep is not None:
            h2 = jnp.where(keep, h2, 0.0)
        return x1, h2.astype(BF16)

    if halo:
        def neighbours(prev_ref, next_ref):
            return jnp.concatenate([prev_ref[...].astype(F32)[edge:HALO, :],
                                    next_ref[...].astype(F32)[0:edge, :]], axis=0)

        mix_sc[0:tm, 0:ATTN_W] = an_ref[...]
        mix_sc[0:tm, ATTN_W:D_MODEL] = gm_ref[...]
        mix_sc[tm:tm + HALO, 0:ATTN_W] = neighbours(anp_ref, ann_ref).astype(BF16)
        mix_sc[tm:tm + HALO, ATTN_W:D_MODEL] = neighbours(gmp_ref, gmn_ref).astype(BF16)
        y = _dot(mix_sc[...], wo_ref[...])
        x1, h2 = residual_norm(x_ref[...], y[0:tm])
        out_ref[...] = x1
        lhs_sc[0:tm, :] = h2
        pos = pl.program_id(0) % tiles_per_seq
        is_prev = lax.broadcasted_iota(jnp.int32, (HALO, 1), 0) < edge
        has_prev = (pos != 0).astype(jnp.int32)
        has_next = (pos != tiles_per_seq - 1).astype(jnp.int32)
        keep = jnp.where(is_prev, has_prev, has_next) > 0
        lhs_sc[tm:tm + HALO, :] = residual_norm(neighbours(xp_ref, xn_ref), y[tm:tm + HALO], keep)[1]
    else:
        y = _dot(an_ref[...], wo_ref[0:ATTN_W, :]) + _dot(gm_ref[...], wo_ref[ATTN_W:D_MODEL, :])
        x1, h2 = residual_norm(x_ref[...], y)
        out_ref[...] = x1
        lhs_sc[...] = h2

        @pl.when(pl.program_id(0) == 0)
        def _():
            zeros = jnp.zeros((A_SLOTS, 2 * FF_CHUNK // LANES, HALO, LANES), F32)
            a_sc[:, :, 0:edge, :] = zeros[:, :, 0:edge, :]
            for s in range(1, nseg):
                a_sc[:, :, s * stride - edge:s * stride + edge, :] = zeros
            a_sc[:, :, nseg * stride - edge:nseg * stride, :] = zeros[:, :, 0:edge, :]

    for c in range(N_FF_CHUNKS):
        slot = c % A_SLOTS
        for k, col0 in enumerate((c * FF_CHUNK, D_FF + c * FF_CHUNK)):
            a = _dot(lhs_sc[...], wup_ref[:, col0:col0 + FF_CHUNK])
            for j in range(FF_CHUNK // LANES):
                lanes = slice(j * LANES, (j + 1) * LANES)
                for s in range(nseg):
                    row0 = edge + s * stride
                    a_sc[slot, 2 * k + j, row0:row0 + seg, :] = a[s * seg:(s + 1) * seg, lanes]
                if halo:
                    a_sc[slot, 2 * k + j, 0:edge, :] = a[tm:tm + edge, lanes]
                    a_sc[slot, 2 * k + j, edge + seg:HALO + seg, :] = a[tm + edge:tm + HALO, lanes]
        for j in range(FF_CHUNK // LANES):
            gl = slice(c * FF_CHUNK + j * LANES, c * FF_CHUNK + (j + 1) * LANES)
            vl = slice(D_FF + gl.start, D_FF + gl.stop)
            for s in range(nseg):
                base = edge + s * stride

                def conv(k, lanes):
                    acc = cb_ref[:, lanes]
                    for t in range(CONV_W):
                        tap = slice(t * 2 * D_FF + lanes.start, t * 2 * D_FF + lanes.stop)
                        acc = acc + a_sc[slot, k, base - 1 + t:base - 1 + t + seg, :] * cw_ref[:, tap]
                    return acc

                g = _silu(conv(j, gl)) * conv(2 + j, vl)
                g_sc[s * seg:(s + 1) * seg, gl] = g.astype(BF16)

    x2 = out_ref[...] + mod_ref[:, 5 * D_MODEL:6 * D_MODEL] * _dot(g_sc[...], wd_ref[...])
    out_ref[...] = _rms(x2, fg_ref[...])


def _ffn(x, an, gm, mods3, mod_row, p, ffn_w, seq, tm):
    n = x.shape[0]
    tiles_per_seq = max(seq // tm, 1)
    seg = min(seq, tm)
    nseg = tm // seg
    rows = tm + (HALO if tiles_per_seq > 1 else 0)
    a_rows = nseg * (seg + HALO)
    const = lambda i: (0, 0)
    tile = lambda i: (i, 0)
    resident = dict(pipeline_mode=pl.Buffered(1))
    in_specs = [pl.BlockSpec((tm, D_MODEL), tile), pl.BlockSpec((tm, ATTN_W), tile),
                pl.BlockSpec((tm, GMLP_W), tile)]
    args = [x, an, gm]
    scratch = [pltpu.VMEM((rows, D_MODEL), BF16),
               pltpu.VMEM((A_SLOTS, 2 * FF_CHUNK // LANES, a_rows, LANES), F32),
               pltpu.VMEM((tm, D_FF), BF16)]
    if tiles_per_seq > 1:
        per = tm // HALO
        last = n // HALO - 1
        prev = lambda i: (jnp.maximum(i * per - 1, 0), 0)
        nxt = lambda i: (jnp.minimum((i + 1) * per, last), 0)
        for arr, width in ((x, D_MODEL), (an, ATTN_W), (gm, GMLP_W)):
            in_specs += [pl.BlockSpec((HALO, width), prev), pl.BlockSpec((HALO, width), nxt)]
            args += [arr, arr]
        scratch.append(pltpu.VMEM((rows, D_MODEL), BF16))
    in_specs += [
        pl.BlockSpec((None, 1, N_MODS * D_MODEL), lambda i: (mod_row(i), 0, 0)),
        pl.BlockSpec((D_MODEL, D_MODEL), const, **resident),
        pl.BlockSpec((1, D_MODEL), const),
        pl.BlockSpec((D_MODEL, 2 * D_FF), const, **resident),
        pl.BlockSpec((1, CONV_W * 2 * D_FF), const),
        pl.BlockSpec((1, 2 * D_FF), const),
        pl.BlockSpec((D_FF, D_MODEL), const, **resident),
        pl.BlockSpec((1, D_MODEL), const),
    ]
    w_o, w_up, w_down = ffn_w
    args += [mods3, w_o, p["norm2_g"], w_up, p["conv_w"], p["conv_b"], w_down, p["final_g"]]
    return pl.pallas_call(
        functools.partial(_ffn_kernel, nseg=nseg, seg=seg, tiles_per_seq=tiles_per_seq),
        grid=(n // tm,),
        in_specs=in_specs,
        out_specs=pl.BlockSpec((tm, D_MODEL), tile),
        out_shape=jax.ShapeDtypeStruct((n, D_MODEL), F32),
        scratch_shapes=scratch,
        compiler_params=_params(1),
        name="ffn_halo" if tiles_per_seq > 1 else "ffn",
    )(*args)


def _rope_tables(length):
    pos = np.arange(length)
    row = (pos // GRID_W).astype(np.float32)
    col = (pos % GRID_W).astype(np.float32)
    n_freq = QK_ROPE // 4
    inv = (np.float32(ROPE_THETA) ** (-(np.arange(n_freq, dtype=np.float32) / np.float32(n_freq)))).astype(np.float32)
    ang_r, ang_c = row[:, None] * inv, col[:, None] * inv
    cos32 = np.concatenate([np.cos(ang_r)] * 2 + [np.cos(ang_c)] * 2, axis=-1)
    sin32 = np.concatenate([-np.sin(ang_r), np.sin(ang_r), -np.sin(ang_c), np.sin(ang_c)], axis=-1)
    tail = LANES - ROPE_LANE0 - QK_ROPE
    cos_q = np.concatenate([np.ones((length, ROPE_LANE0)), cos32, np.zeros((length, tail))], -1)
    sin_q = np.concatenate([np.zeros((length, ROPE_LANE0)), sin32, np.zeros((length, tail))], -1)
    return jnp.asarray(cos_q, F32), jnp.asarray(sin_q, F32)


def _layer_params(l, norm1_g, q_norm_g, kv_norm_g, v_norm_g, b_s, out_g_attn,
                  out_g_gmlp, norm2_g, conv_w, conv_b, final_g):
    row = lambda a: a.reshape(1, -1)
    return {
        "norm1_g": row(norm1_g[l]), "q_norm_g": row(q_norm_g[l]),
        "kv_norm_g": row(kv_norm_g[l]), "v_norm_g": row(v_norm_g[l]), "b_s": b_s[l],
        "out_g_attn": row(out_g_attn[l]), "out_g_gmlp": row(out_g_gmlp[l]), "norm2_g": row(norm2_g[l]),
        "conv_w": conv_w[l].reshape(1, CONV_W * 2 * D_FF), "conv_b": row(conv_b[l]), "final_g": row(final_g),
    }


def kernel(x_prompt, x_sample, cache_ckv, cache_krope, c, c_ctx, ada_w, ada_b, norm1_g, w_in, q_norm_g, w_uq, kv_norm_g, w_ukv, v_norm_g, w_s, b_s, out_g_attn, out_g_gmlp, w_o, norm2_g, w_up, conv_w, conv_b, w_down, final_g):
    batch, seq, _ = x_prompt.shape
    dec_batch, dec_seq, _ = x_sample.shape
    depth, past = cache_ckv.shape[1], cache_ckv.shape[2]
    assert depth == 1 and dec_batch + 1 <= MOD_ROWS
    l = 0
    p = _layer_params(l, norm1_g, q_norm_g, kv_norm_g, v_norm_g, b_s, out_g_attn,
                      out_g_gmlp, norm2_g, conv_w, conv_b, final_g)

    mods3, p["w_qkv"], p["w_uv"], p["w_uq"], p["w_k"], p["w_ve"], p["w_vo"], w_s_b = _adaln(
        c_ctx.reshape(1, D_MODEL), c, ada_w[l], ada_b[l].reshape(1, -1), w_in[l].T, w_uq[l], w_ukv[l],
        w_s[l].reshape(GMLP_GROUPS * CHUNK, CHUNK))
    p["w_s"] = w_s_b.reshape(GMLP_GROUPS, CHUNK, CHUNK)
    tm_pre, tm_ffn = 1024, 512
    ctx_row = lambda i: 0
    lat_row = lambda tm: lambda i: 1 + i // (dec_seq // tm)

    xp = x_prompt.reshape(batch * seq, D_MODEL)
    xs = x_sample.reshape(dec_batch * dec_seq, D_MODEL)
    (qp, kp, vep, vop, gmp, ckv_p, krt_p), (w_down_b,) = _pre(xp, mods3, ctx_row, p, seq, tm_pre, [w_down[l]])
    cache_in = (*_rope_tables(dec_seq), cache_ckv[:, l].reshape(dec_batch * past, KV_RANK),
                jnp.swapaxes(cache_krope[:, l], 1, 2))
    (qs, ks, ves, vos, gms), (w_o_b,) = _pre(xs, mods3, lat_row(tm_pre), p, dec_seq, tm_pre, [w_o[l]],
                                             latent=cache_in)

    anp, _ = _attention(qp, kp, vep, vop, p["out_g_attn"], seq, seq, seq, 8, 4, [])
    ans, (w_up_b,) = _attention(qs, ks, ves, vos, p["out_g_attn"], dec_seq, past + dec_seq, 1024, 1, 4, [w_up[l]])
    ffn_w = (w_o_b, w_up_b, w_down_b)
    yp = _ffn(xp, anp, gmp, mods3, ctx_row, p, ffn_w, seq, tm_ffn)
    ys = _ffn(xs, ans, gms, mods3, lat_row(tm_ffn), p, ffn_w, dec_seq, tm_ffn)

    return (yp.reshape(batch, seq, D_MODEL), ys.reshape(dec_batch, dec_seq, D_MODEL),
            ckv_p.reshape(batch, 1, seq, KV_RANK), jnp.swapaxes(krt_p, 1, 2).reshape(batch, 1, seq, QK_ROPE))
```

```python
import functools

import numpy as np
import jax
import jax.numpy as jnp
from jax import lax
from jax.experimental import pallas as pl
from jax.experimental.pallas import tpu as pltpu

D_MODEL = 1024
GRID_W = 64
CHUNK = 128
N_HEADS = 8
QK_NOPE = 64
QK_ROPE = 32
V_DIM = 64
Q_RANK = 384
KV_RANK = 256
ATTN_W = N_HEADS * V_DIM
GMLP_W = D_MODEL - ATTN_W
GMLP_GROUPS = 4
GMLP_DG = GMLP_W // GMLP_GROUPS
D_FF = 2816
CONV_W = 3
ROPE_THETA = 10000.0
EPS = 1e-6
SM_SCALE = (QK_NOPE + QK_ROPE) ** -0.5
LOG2_E = 1.4426950408889634

LANES = 128
HEAD_PAD = LANES
ROPE_LANE0 = QK_NOPE
QK_W = N_HEADS * HEAD_PAD
N_MODS = 6
MOD_ROWS = 8
ADALN_STEPS = 8
IN_QKV = Q_RANK + KV_RANK
FF_CHUNK = 256
HALO = 16
N_FF_CHUNKS = D_FF // FF_CHUNK
A_SLOTS = 4
V7X_VMEM_BYTES = 64 * 1024 * 1024
VMEM_LIMIT = V7X_VMEM_BYTES - 8 * 1024 * 1024

F32 = jnp.float32
BF16 = jnp.bfloat16


def _dot(a, b):
    return jnp.dot(a, b, preferred_element_type=F32)


def _dot_nt(a, b):
    return lax.dot_general(a, b, (((1,), (1,)), ((), ())), preferred_element_type=F32)


def _rms(x, g):
    return x * lax.rsqrt(jnp.mean(x * x, axis=-1, keepdims=True) + EPS) * g


def _gelu(x):
    return 0.5 * x * (1.0 + lax.erf(x * (0.5 ** 0.5)))


def _silu(x):
    return x * (1.0 / (1.0 + jnp.exp(-x)))


def _params(n_axes):
    return pltpu.CompilerParams(dimension_semantics=("arbitrary",) * n_axes,
                                vmem_limit_bytes=VMEM_LIMIT)


def _adaln_kernel(cctx_ref, c_ref, w_ref, b_ref, win_ref, wuq_ref, wukv_ref, ws_ref,
                  out_ref, wqkv_out, wuv_out, wuq_out, wk_out, wve_out, wvo_out, ws_out):
    pad = jnp.zeros((MOD_ROWS - 1 - c_ref.shape[0], D_MODEL), F32)
    cond = jnp.concatenate([cctx_ref[...], c_ref[...], pad], axis=0)
    s = _silu(cond).astype(BF16)
    out_ref[:, 0, :] = _dot(s, w_ref[...].astype(BF16)) + b_ref[...]
    @pl.when(pl.program_id(0) == 0)
    def _():
        kr = win_ref[IN_QKV:IN_QKV + QK_ROPE, :].astype(BF16)
        z = jnp.zeros_like(kr)
        wqkv_out[0:IN_QKV, :] = win_ref[0:IN_QKV, :].astype(BF16)
        for t, blk in enumerate((kr, z, kr, z)):
            wqkv_out[IN_QKV + t * QK_ROPE:IN_QKV + (t + 1) * QK_ROPE, :] = blk
        wuv_out[...] = win_ref[IN_QKV + QK_ROPE:, :].astype(BF16)

    w = wuq_ref[...]
    hw = QK_NOPE + QK_ROPE
    zq = jnp.zeros((w.shape[0], HEAD_PAD - hw), F32)
    wuq_out[...] = jnp.concatenate([blk for hd in range(N_HEADS) for blk in (w[:, hd * hw:(hd + 1) * hw], zq)],
                                   axis=1).astype(BF16)
    w = wukv_ref[...]
    zv = jnp.zeros((w.shape[0], V_DIM), F32)
    v_of = lambda hd: w[:, hd * HEAD_PAD + QK_NOPE:(hd + 1) * HEAD_PAD]
    wk_out[...] = jnp.concatenate([blk for hd in range(N_HEADS)
                                   for blk in (w[:, hd * HEAD_PAD:hd * HEAD_PAD + QK_NOPE], zv)], axis=1).astype(BF16)
    wve_out[...] = jnp.concatenate([blk for hd in range(0, N_HEADS, 2) for blk in (v_of(hd), zv)],
                                   axis=1).astype(BF16)
    wvo_out[...] = jnp.concatenate([blk for hd in range(1, N_HEADS, 2) for blk in (zv, v_of(hd))],
                                   axis=1).astype(BF16)
    ws_out[...] = ws_ref[...].astype(BF16)


def _adaln(c_ctx, c, ada_w, ada_b, w_in_t, w_uq, w_ukv, w_s):
    cols = N_MODS * D_MODEL // ADALN_STEPS
    row_block = lambda a, width=None: pl.BlockSpec((a.shape[0] // ADALN_STEPS, width or a.shape[1]),
                                                   lambda j: (j, 0))
    whole = lambda shape: pl.BlockSpec(shape, lambda j: (0, 0), pipeline_mode=pl.Buffered(1))
    return pl.pallas_call(
        _adaln_kernel,
        grid=(ADALN_STEPS,),
        in_specs=[
            pl.BlockSpec(c_ctx.shape, lambda j: (0, 0)),
            pl.BlockSpec(c.shape, lambda j: (0, 0)),
            pl.BlockSpec((D_MODEL, cols), lambda j: (0, j)),
            pl.BlockSpec((1, cols), lambda j: (0, j)),
            whole(w_in_t.shape), row_block(w_uq), row_block(w_ukv), row_block(w_s),
        ],
        out_specs=[pl.BlockSpec((MOD_ROWS, 1, cols), lambda j: (0, 0, j)),
                   whole((IN_QKV + LANES, D_MODEL)), whole((2 * GMLP_W, D_MODEL)),
                   row_block(w_uq, QK_W), row_block(w_ukv, QK_W), row_block(w_ukv, ATTN_W),
                   row_block(w_ukv, ATTN_W), row_block(w_s)],
        out_shape=[jax.ShapeDtypeStruct((MOD_ROWS, 1, N_MODS * D_MODEL), F32),
                   jax.ShapeDtypeStruct((IN_QKV + LANES, D_MODEL), BF16),
                   jax.ShapeDtypeStruct((2 * GMLP_W, D_MODEL), BF16),
                   jax.ShapeDtypeStruct((Q_RANK, QK_W), BF16),
                   jax.ShapeDtypeStruct((KV_RANK, QK_W), BF16),
                   jax.ShapeDtypeStruct((KV_RANK, ATTN_W), BF16),
                   jax.ShapeDtypeStruct((KV_RANK, ATTN_W), BF16),
                   jax.ShapeDtypeStruct(w_s.shape, BF16)],
        compiler_params=_params(1),
        name="adaln",
    )(c_ctx, c, ada_w, ada_b, w_in_t, w_uq, w_ukv, w_s)


def _pre_kernel(*refs, latent, tm, seq, n_casts):
    (x_ref, mod_ref, n1g_ref, wqkv_ref, wuv_ref, qng_ref, wuq_ref, kvg_ref, wk_ref, wve_ref, wvo_ref,
     vng_ref, ws_ref, bs_ref, ogg_ref) = refs[:15]
    n_in = 15 + (4 if latent else 0)
    n_out = 5 + (0 if latent else 2)
    cast_in = refs[n_in:n_in + n_casts]
    outs = refs[n_in + n_casts:n_in + n_casts + n_out]
    cast_out = refs[n_in + n_casts + n_out:n_in + 2 * n_casts + n_out]
    go_sc = refs[-1]
    q_out, k_out, ve_out, vo_out, gm_out = outs[:5]
    for src, dst in zip(cast_in, cast_out):
        dst[...] = src[...].astype(BF16)
    shift1 = mod_ref[:, 0:D_MODEL]
    scale1 = mod_ref[:, D_MODEL:2 * D_MODEL]
    h = (_rms(x_ref[...], n1g_ref[...]) * (1.0 + scale1) + shift1).astype(BF16)

    lane = lax.broadcasted_iota(jnp.int32, (1, LANES), 1)
    rope_lanes = lane >= ROPE_LANE0
    first_half = (lane % 16) < 8

    def expand(ckv_b, kr_k, rows):
        kn = _dot(ckv_b, wk_ref[...])
        for hd in range(N_HEADS):
            sl = slice(hd * HEAD_PAD, (hd + 1) * HEAD_PAD)
            k_out[rows, sl] = (kn[:, sl] + kr_k).astype(BF16)
        ve_out[rows, :] = _dot(ckv_b, wve_ref[...]).astype(BF16)
        vo_out[rows, :] = _dot(ckv_b, wvo_ref[...]).astype(BF16)

    qkv = _dot_nt(h, wqkv_ref[...])
    qn = _rms(qkv[:, 0:Q_RANK], qng_ref[...]).astype(BF16)
    ckv = _rms(qkv[:, Q_RANK:IN_QKV], kvg_ref[...])
    q = _dot(qn, wuq_ref[...]) * (SM_SCALE * LOG2_E)
    kr = qkv[:, IN_QKV:IN_QKV + LANES]
    if latent:
        cos_ref, sin_ref, cckv_ref, ckrt_ref = refs[15:19]
        cos_q = cos_ref[...]

        def rotate(v, cos):
            swapped = jnp.where(first_half, pltpu.roll(v, LANES - 8, 1), pltpu.roll(v, 8, 1))
            return v * cos + swapped * sin_ref[...]

        kr_k = rotate(kr, jnp.where(rope_lanes, cos_q, 0.0))
        ckrt = ckrt_ref[...]
        past = ckrt.shape[1]
        padded = jnp.concatenate([ckrt, jnp.zeros((LANES - QK_ROPE, past), F32)], axis=0)
        expand(cckv_ref[...].astype(BF16), pltpu.roll(padded.T, ROPE_LANE0, 1), slice(0, past))
        own_rows = slice(past, past + tm)
    else:
        ckv_out, krt_out = outs[5:7]
        ckv_out[...] = ckv
        krt = kr.T
        for s in range(tm // seq):
            krt_out[s] = krt[0:QK_ROPE, s * seq:(s + 1) * seq]
        kr_k = jnp.where(rope_lanes, kr, 0.0)
        own_rows = slice(0, tm)
    expand(ckv.astype(BF16), kr_k, own_rows)
    for hd in range(N_HEADS):
        sl = slice(hd * HEAD_PAD, (hd + 1) * HEAD_PAD)
        qh = q[:, sl]
        if latent:
            qh = rotate(qh, cos_q)
        q_out[:, sl] = qh.astype(BF16)

    uv = _dot_nt(h, wuv_ref[...])
    gu = _gelu(uv[:, 0:GMLP_W])
    gv = _gelu(uv[:, GMLP_W:2 * GMLP_W])
    eye = (lax.broadcasted_iota(jnp.int32, (CHUNK, CHUNK), 0)
           == lax.broadcasted_iota(jnp.int32, (CHUNK, CHUNK), 1))
    for g in range(GMLP_GROUPS):
        sl = slice(g * GMLP_DG, (g + 1) * GMLP_DG)
        vg = gv[:, sl]
        vg = (vg * lax.rsqrt(jnp.mean(vg * vg, axis=-1, keepdims=True) + EPS) * vng_ref[:, sl]).astype(BF16)
        bias = jnp.sum(jnp.where(eye, bs_ref[g:g + 1, :], 0.0), axis=1, keepdims=True)
        chunks = [vg[n * CHUNK:(n + 1) * CHUNK, :] for n in range(tm // CHUNK)]
        s_all = _dot(ws_ref[g], jnp.concatenate(chunks, axis=1))
        for n in range(tm // CHUNK):
            rows = slice(n * CHUNK, (n + 1) * CHUNK)
            go_sc[rows, sl] = gu[rows, sl] * (s_all[:, n * GMLP_DG:(n + 1) * GMLP_DG] + bias)
    gm_out[...] = _rms(go_sc[...], ogg_ref[...]).astype(BF16)


def _pre(x, mods3, mod_row, p, seq, tm, casts, latent=None):
    n = x.shape[0]
    const = lambda i: (0, 0)
    tile = lambda i: (i, 0)
    in_specs = [
        pl.BlockSpec((tm, D_MODEL), tile),
        pl.BlockSpec((None, 1, N_MODS * D_MODEL), lambda i: (mod_row(i), 0, 0)),
        pl.BlockSpec((1, D_MODEL), const),
        pl.BlockSpec((IN_QKV + LANES, D_MODEL), const),
        pl.BlockSpec((2 * GMLP_W, D_MODEL), const),
        pl.BlockSpec((1, Q_RANK), const),
        pl.BlockSpec((Q_RANK, QK_W), const),
        pl.BlockSpec((1, KV_RANK), const),
        pl.BlockSpec((KV_RANK, QK_W), const),
        pl.BlockSpec((KV_RANK, ATTN_W), const),
        pl.BlockSpec((KV_RANK, ATTN_W), const),
        pl.BlockSpec((1, GMLP_W), const),
        pl.BlockSpec((GMLP_GROUPS, CHUNK, CHUNK), lambda i: (0, 0, 0)),
        pl.BlockSpec((GMLP_GROUPS, CHUNK), const),
        pl.BlockSpec((1, GMLP_W), const),
    ]
    args = [x, mods3, p["norm1_g"], p["w_qkv"], p["w_uv"], p["q_norm_g"], p["w_uq"],
            p["kv_norm_g"], p["w_k"], p["w_ve"], p["w_vo"], p["v_norm_g"], p["w_s"], p["b_s"], p["out_g_gmlp"]]
    past = 0 if latent is None else latent[3].shape[2]
    kv_rows = tm + past
    out_rows = (tm, kv_rows, kv_rows, kv_rows, tm)
    widths = (QK_W, QK_W, ATTN_W, ATTN_W, GMLP_W)
    out_specs = [pl.BlockSpec((r, w), tile) for r, w in zip(out_rows, widths)]
    out_shape = [jax.ShapeDtypeStruct((n // tm * r, w), BF16) for r, w in zip(out_rows, widths)]
    if latent is not None:
        cos_q, sin_q, cache_ckv, cache_krt = latent
        assert tm == seq == cos_q.shape[0]
        in_specs += [pl.BlockSpec((tm, LANES), const), pl.BlockSpec((tm, LANES), const),
                     pl.BlockSpec((past, KV_RANK), tile),
                     pl.BlockSpec((None, QK_ROPE, past), lambda i: (i, 0, 0))]
        args += [cos_q, sin_q, cache_ckv, cache_krt]
    else:
        out_specs += [pl.BlockSpec((tm, KV_RANK), tile),
                      pl.BlockSpec((tm // seq, QK_ROPE, seq), lambda i: (i, 0, 0))]
        out_shape += [jax.ShapeDtypeStruct((n, KV_RANK), F32),
                      jax.ShapeDtypeStruct((n // seq, QK_ROPE, seq), F32)]
    n_out = len(out_specs)
    cast_specs = [pl.BlockSpec((w.shape[0] // (n // tm), w.shape[1]), tile) for w in casts]
    outs = pl.pallas_call(
        functools.partial(_pre_kernel, latent=latent is not None, tm=tm, seq=seq, n_casts=len(casts)),
        grid=(n // tm,),
        in_specs=in_specs + cast_specs,
        out_specs=out_specs + cast_specs,
        out_shape=out_shape + [jax.ShapeDtypeStruct(w.shape, BF16) for w in casts],
        scratch_shapes=[pltpu.VMEM((tm, GMLP_W), F32)],
        compiler_params=_params(1),
        name="pre_latent" if latent is not None else "pre",
    )(*args, *casts)
    return outs[:n_out], outs[n_out:]


def _attn_kernel(*refs, group, kv_len, tq, pps, n_casts):
    q_ref, k_ref, ve_ref, vo_ref, oga_ref = refs[:5]
    cast_in = refs[5:5 + n_casts]
    out_ref = refs[5 + n_casts]
    cast_out = refs[6 + n_casts:6 + 2 * n_casts]
    o_sc = refs[-1]
    for src, dst in zip(cast_in, cast_out):
        dst[...] = src[...].astype(BF16)
    step = pl.program_id(2)
    ones = jnp.ones((kv_len, LANES), BF16)
    for g in range(group):
        qrows = slice(g * tq, (g + 1) * tq)
        krows = slice(g * kv_len, (g + 1) * kv_len)
        for pp in range(pps):
            psl = slice(pp * LANES, (pp + 1) * LANES)
            acc = None
            for par, v_ref in enumerate((ve_ref, vo_ref)):
                hd = 2 * pp + par
                hsl = slice(hd * HEAD_PAD, (hd + 1) * HEAD_PAD)
                s = _dot_nt(q_ref[qrows, hsl], k_ref[krows, hsl])
                e = jnp.exp2(s - jnp.max(s, axis=-1, keepdims=True)).astype(BF16)
                o = _dot(e, jnp.concatenate([v_ref[krows, psl], ones], axis=1))
                o = o[:, 0:LANES] * (1.0 / o[:, LANES:2 * LANES])
                acc = o if acc is None else acc + o
            o_sc[step * pps + pp, qrows, :] = acc

    @pl.when(step == N_HEADS // 2 // pps - 1)
    def _():
        blocks = [o_sc[j] for j in range(N_HEADS // 2)]
        ssq = sum(jnp.sum(b * b, axis=-1, keepdims=True) for b in blocks)
        r = lax.rsqrt(ssq * (1.0 / ATTN_W) + EPS)
        for j, b in enumerate(blocks):
            sl = slice(j * LANES, (j + 1) * LANES)
            out_ref[:, sl] = (b * r * oga_ref[:, sl]).astype(BF16)


def _attention(q, k, ve, vo, oga, seq, kv_len, tq, group, pps, casts):
    n = q.shape[0]
    nq = seq // tq
    n_pairs = N_HEADS // 2
    grid = (n // (group * seq), nq, n_pairs // pps)
    n_steps = grid[0] * grid[1] * grid[2]
    linear = lambda b, i, j: ((b * grid[1] + i) * grid[2] + j, 0)
    cast_specs = [pl.BlockSpec((w.shape[0] // n_steps, w.shape[1]), linear) for w in casts]
    qtile = lambda b, i, j: (b * nq + i, j)
    kv = lambda b, i, j: (b, j)
    qk_w, v_w = pps * 2 * HEAD_PAD, pps * LANES
    in_specs = [
        pl.BlockSpec((group * tq, qk_w), qtile),
        pl.BlockSpec((group * kv_len, qk_w), kv),
        pl.BlockSpec((group * kv_len, v_w), kv),
        pl.BlockSpec((group * kv_len, v_w), kv),
        pl.BlockSpec((1, ATTN_W), lambda b, i, j: (0, 0)),
    ]
    outs = pl.pallas_call(
        functools.partial(_attn_kernel, group=group, kv_len=kv_len, tq=tq, pps=pps, n_casts=len(casts)),
        grid=grid,
        in_specs=in_specs + cast_specs,
        out_specs=[pl.BlockSpec((group * tq, ATTN_W), lambda b, i, j: (b * nq + i, 0))] + cast_specs,
        out_shape=[jax.ShapeDtypeStruct((n, ATTN_W), BF16)]
        + [jax.ShapeDtypeStruct(w.shape, BF16) for w in casts],
        scratch_shapes=[pltpu.VMEM((n_pairs, group * tq, LANES), F32)],
        compiler_params=_params(3),
        name="attn",
    )(q, k, ve, vo, oga, *casts)
    return outs[0], outs[1:]


def _ffn_kernel(*refs, nseg, seg, tiles_per_seq):
    halo = tiles_per_seq > 1
    if halo:
        (x_ref, an_ref, gm_ref, xp_ref, xn_ref, anp_ref, ann_ref, gmp_ref, gmn_ref, mod_ref, wo_ref, n2g_ref,
         wup_ref, cw_ref, cb_ref, wd_ref, fg_ref, out_ref, lhs_sc, a_sc, g_sc, mix_sc) = refs
    else:
        (x_ref, an_ref, gm_ref, mod_ref, wo_ref, n2g_ref,
         wup_ref, cw_ref, cb_ref, wd_ref, fg_ref, out_ref, lhs_sc, a_sc, g_sc) = refs
    tm = nseg * seg
    stride = seg + HALO
    edge = HALO // 2
    gate1 = mod_ref[:, 2 * D_MODEL:3 * D_MODEL]
    shift2 = mod_ref[:, 3 * D_MODEL:4 * D_MODEL]
    scale2 = mod_ref[:, 4 * D_MODEL:5 * D_MODEL]

    def residual_norm(x, y, keep=None):
        x1 = x + gate1 * y
        h2 = _rms(x1, n2g_ref[...]) * (1.0 + scale2) + shift2
        if keep is not None:
            h2 = jnp.where(keep, h2, 0.0)
        return x1, h2.astype(BF16)

    if halo:
        def neighbours(prev_ref, next_ref):
            return jnp.concatenate([prev_ref[...].astype(F32)[edge:HALO, :],
                                    next_ref[...].astype(F32)[0:edge, :]], axis=0)

        mix_sc[0:tm, 0:ATTN_W] = an_ref[...]
        mix_sc[0:tm, ATTN_W:D_MODEL] = gm_ref[...]
        mix_sc[tm:tm + HALO, 0:ATTN_W] = neighbours(anp_ref, ann_ref).astype(BF16)
        mix_sc[tm:tm + HALO, ATTN_W:D_MODEL] = neighbours(gmp_ref, gmn_ref).astype(BF16)
        y = _dot(mix_sc[...], wo_ref[...])
        x1, h2 = residual_norm(x_ref[...], y[0:tm])
        out_ref[...] = x1
        lhs_sc[0:tm, :] = h2
        pos = pl.program_id(0) % tiles_per_seq
        is_prev = lax.broadcasted_iota(jnp.int32, (HALO, 1), 0) < edge
        has_prev = (pos != 0).astype(jnp.int32)
        has_next = (pos != tiles_per_seq - 1).astype(jnp.int32)
        keep = jnp.where(is_prev, has_prev, has_next) > 0
        lhs_sc[tm:tm + HALO, :] = residual_norm(neighbours(xp_ref, xn_ref), y[tm:tm + HALO], keep)[1]
    else:
        y = _dot(an_ref[...], wo_ref[0:ATTN_W, :]) + _dot(gm_ref[...], wo_ref[ATTN_W:D_MODEL, :])
        x1, h2 = residual_norm(x_ref[...], y)
        out_ref[...] = x1
        lhs_sc[...] = h2

        @pl.when(pl.program_id(0) == 0)
        def _():
            zeros = jnp.zeros((A_SLOTS, 2 * FF_CHUNK // LANES, HALO, LANES), F32)
            a_sc[:, :, 0:edge, :] = zeros[:, :, 0:edge, :]
            for s in range(1, nseg):
                a_sc[:, :, s * stride - edge:s * stride + edge, :] = zeros
            a_sc[:, :, nseg * stride - edge:nseg * stride, :] = zeros[:, :, 0:edge, :]

    for c in range(N_FF_CHUNKS):
        slot = c % A_SLOTS
        for k, col0 in enumerate((c * FF_CHUNK, D_FF + c * FF_CHUNK)):
            a = _dot(lhs_sc[...], wup_ref[:, col0:col0 + FF_CHUNK])
            for j in range(FF_CHUNK // LANES):
                lanes = slice(j * LANES, (j + 1) * LANES)
                for s in range(nseg):
                    row0 = edge + s * stride
                    a_sc[slot, 2 * k + j, row0:row0 + seg, :] = a[s * seg:(s + 1) * seg, lanes]
                if halo:
                    a_sc[slot, 2 * k + j, 0:edge, :] = a[tm:tm + edge, lanes]
                    a_sc[slot, 2 * k + j, edge + seg:HALO + seg, :] = a[tm + edge:tm + HALO, lanes]
        for j in range(FF_CHUNK // LANES):
            gl = slice(c * FF_CHUNK + j * LANES, c * FF_CHUNK + (j + 1) * LANES)
            vl = slice(D_FF + gl.start, D_FF + gl.stop)
            for s in range(nseg):
                base = edge + s * stride

                def conv(k, lanes):
                    acc = cb_ref[:, lanes]
                    for t in range(CONV_W):
                        tap = slice(t * 2 * D_FF + lanes.start, t * 2 * D_FF + lanes.stop)
                        acc = acc + a_sc[slot, k, base - 1 + t:base - 1 + t + seg, :] * cw_ref[:, tap]
                    return acc

                g = _silu(conv(j, gl)) * conv(2 + j, vl)
                g_sc[s * seg:(s + 1) * seg, gl] = g.astype(BF16)

    x2 = out_ref[...] + mod_ref[:, 5 * D_MODEL:6 * D_MODEL] * _dot(g_sc[...], wd_ref[...])
    out_ref[...] = _rms(x2, fg_ref[...])


def _ffn(x, an, gm, mods3, mod_row, p, ffn_w, seq, tm):
    n = x.shape[0]
    tiles_per_seq = max(seq // tm, 1)
    seg = min(seq, tm)
    nseg = tm // seg
    rows = tm + (HALO if tiles_per_seq > 1 else 0)
    a_rows = nseg * (seg + HALO)
    const = lambda i: (0, 0)
    tile = lambda i: (i, 0)
    resident = dict(pipeline_mode=pl.Buffered(1))
    in_specs = [pl.BlockSpec((tm, D_MODEL), tile), pl.BlockSpec((tm, ATTN_W), tile),
                pl.BlockSpec((tm, GMLP_W), tile)]
    args = [x, an, gm]
    scratch = [pltpu.VMEM((rows, D_MODEL), BF16),
               pltpu.VMEM((A_SLOTS, 2 * FF_CHUNK // LANES, a_rows, LANES), F32),
               pltpu.VMEM((tm, D_FF), BF16)]
    if tiles_per_seq > 1:
        per = tm // HALO
        last = n // HALO - 1
        prev = lambda i: (jnp.maximum(i * per - 1, 0), 0)
        nxt = lambda i: (jnp.minimum((i + 1) * per, last), 0)
        for arr, width in ((x, D_MODEL), (an, ATTN_W), (gm, GMLP_W)):
            in_specs += [pl.BlockSpec((HALO, width), prev), pl.BlockSpec((HALO, width), nxt)]
            args += [arr, arr]
        scratch.append(pltpu.VMEM((rows, D_MODEL), BF16))
    in_specs += [
        pl.BlockSpec((None, 1, N_MODS * D_MODEL), lambda i: (mod_row(i), 0, 0)),
        pl.BlockSpec((D_MODEL, D_MODEL), const, **resident),
        pl.BlockSpec((1, D_MODEL), const),
        pl.BlockSpec((D_MODEL, 2 * D_FF), const, **resident),
        pl.BlockSpec((1, CONV_W * 2 * D_FF), const),
        pl.BlockSpec((1, 2 * D_FF), const),
        pl.BlockSpec((D_FF, D_MODEL), const, **resident),
        pl.BlockSpec((1, D_MODEL), const),
    ]
    w_o, w_up, w_down = ffn_w
    args += [mods3, w_o, p["norm2_g"], w_up, p["conv_w"], p["conv_b"], w_down, p["final_g"]]
    return pl.pallas_call(
        functools.partial(_ffn_kernel, nseg=nseg, seg=seg, tiles_per_seq=tiles_per_seq),
        grid=(n // tm,),
        in_specs=in_specs,
        out_specs=pl.BlockSpec((tm, D_MODEL), tile),
        out_shape=jax.ShapeDtypeStruct((n, D_MODEL), F32),
        scratch_shapes=scratch,
        compiler_params=_params(1),
        name="ffn_halo" if tiles_per_seq > 1 else "ffn",
    )(*args)


def _rope_tables(length):
    pos = np.arange(length)
    row = (pos // GRID_W).astype(np.float32)
    col = (pos % GRID_W).astype(np.float32)
    n_freq = QK_ROPE // 4
    inv = (np.float32(ROPE_THETA) ** (-(np.arange(n_freq, dtype=np.float32) / np.float32(n_freq)))).astype(np.float32)
    ang_r, ang_c = row[:, None] * inv, col[:, None] * inv
    cos32 = np.concatenate([np.cos(ang_r)] * 2 + [np.cos(ang_c)] * 2, axis=-1)
    sin32 = np.concatenate([-np.sin(ang_r), np.sin(ang_r), -np.sin(ang_c), np.sin(ang_c)], axis=-1)
    tail = LANES - ROPE_LANE0 - QK_ROPE
    cos_q = np.concatenate([np.ones((length, ROPE_LANE0)), cos32, np.zeros((length, tail))], -1)
    sin_q = np.concatenate([np.zeros((length, ROPE_LANE0)), sin32, np.zeros((length, tail))], -1)
    return jnp.asarray(cos_q, F32), jnp.asarray(sin_q, F32)


def _layer_params(l, norm1_g, q_norm_g, kv_norm_g, v_norm_g, b_s, out_g_attn,
                  out_g_gmlp, norm2_g, conv_w, conv_b, final_g):
    row = lambda a: a.reshape(1, -1)
    return {
        "norm1_g": row(norm1_g[l]), "q_norm_g": row(q_norm_g[l]),
        "kv_norm_g": row(kv_norm_g[l]), "v_norm_g": row(v_norm_g[l]), "b_s": b_s[l],
        "out_g_attn": row(out_g_attn[l]), "out_g_gmlp": row(out_g_gmlp[l]), "norm2_g": row(norm2_g[l]),
        "conv_w": conv_w[l].reshape(1, CONV_W * 2 * D_FF), "conv_b": row(conv_b[l]), "final_g": row(final_g),
    }


def kernel(x_prompt, x_sample, cache_ckv, cache_krope, c, c_ctx, ada_w, ada_b, norm1_g, w_in, q_norm_g, w_uq, kv_norm_g, w_ukv, v_norm_g, w_s, b_s, out_g_attn, out_g_gmlp, w_o, norm2_g, w_up, conv_w, conv_b, w_down, final_g):
    batch, seq, _ = x_prompt.shape
    dec_batch, dec_seq, _ = x_sample.shape
    depth, past = cache_ckv.shape[1], cache_ckv.shape[2]
    assert depth == 1 and dec_batch + 1 <= MOD_ROWS
    l = 0
    p = _layer_params(l, norm1_g, q_norm_g, kv_norm_g, v_norm_g, b_s, out_g_attn,
                      out_g_gmlp, norm2_g, conv_w, conv_b, final_g)

    mods3, p["w_qkv"], p["w_uv"], p["w_uq"], p["w_k"], p["w_ve"], p["w_vo"], w_s_b = _adaln(
        c_ctx.reshape(1, D_MODEL), c, ada_w[l], ada_b[l].reshape(1, -1), w_in[l].T, w_uq[l], w_ukv[l],
        w_s[l].reshape(GMLP_GROUPS * CHUNK, CHUNK))
    p["w_s"] = w_s_b.reshape(GMLP_GROUPS, CHUNK, CHUNK)
    tm_pre, tm_ffn = 1024, 512
    ctx_row = lambda i: 0
    lat_row = lambda tm: lambda i: 1 + i // (dec_seq // tm)

    xp = x_prompt.reshape(batch * seq, D_MODEL)
    xs = x_sample.reshape(dec_batch * dec_seq, D_MODEL)
    (qp, kp, vep, vop, gmp, ckv_p, krt_p), (w_down_b,) = _pre(xp, mods3, ctx_row, p, seq, tm_pre, [w_down[l]])
    cache_in = (*_rope_tables(dec_seq), cache_ckv[:, l].reshape(dec_batch * past, KV_RANK),
                jnp.swapaxes(cache_krope[:, l], 1, 2))
    (qs, ks, ves, vos, gms), (w_o_b,) = _pre(xs, mods3, lat_row(tm_pre), p, dec_seq, tm_pre, [w_o[l]],
                                             latent=cache_in)

    ans, (w_up_b,) = _attention(qs, ks, ves, vos, p["out_g_attn"], dec_seq, past + dec_seq, 1024, 1, 4, [w_up[l]])
    anp, _ = _attention(qp, kp, vep, vop, p["out_g_attn"], seq, seq, seq, 4, 4, [])
    ffn_w = (w_o_b, w_up_b, w_down_b)
    yp = _ffn(xp, anp, gmp, mods3, ctx_row, p, ffn_w, seq, tm_ffn)
    ys = _ffn(xs, ans, gms, mods3, lat_row(tm_ffn), p, ffn_w, dec_seq, tm_ffn)

    return (yp.reshape(batch, seq, D_MODEL), ys.reshape(dec_batch, dec_seq, D_MODEL),
            ckv_p.reshape(batch, 1, seq, KV_RANK), jnp.swapaxes(krt_p, 1, 2).reshape(batch, 1, seq, QK_ROPE))
```

```python
import functools

import numpy as np
import jax
import jax.numpy as jnp
from jax import lax
from jax.experimental import pallas as pl
from jax.experimental.pallas import tpu as pltpu

D_MODEL = 1024
GRID_W = 64
CHUNK = 128
N_HEADS = 8
QK_NOPE = 64
QK_ROPE = 32
V_DIM = 64
Q_RANK = 384
KV_RANK = 256
ATTN_W = N_HEADS * V_DIM
GMLP_W = D_MODEL - ATTN_W
GMLP_GROUPS = 4
GMLP_DG = GMLP_W // GMLP_GROUPS
D_FF = 2816
CONV_W = 3
ROPE_THETA = 10000.0
EPS = 1e-6
SM_SCALE = (QK_NOPE + QK_ROPE) ** -0.5
LOG2_E = 1.4426950408889634

LANES = 128
HEAD_PAD = LANES
ROPE_LANE0 = QK_NOPE
ROT_HALF = QK_ROPE // 4
QK_W = N_HEADS * HEAD_PAD
N_MODS = 6
MOD_ROWS = 8
ADALN_STEPS = 8
IN_QKV = Q_RANK + KV_RANK
FF_CHUNK = 256
HALO = 16
N_FF_CHUNKS = D_FF // FF_CHUNK
A_SLOTS = 4
V7X_VMEM_BYTES = 64 * 1024 * 1024
VMEM_LIMIT = V7X_VMEM_BYTES - 8 * 1024 * 1024

F32 = jnp.float32
BF16 = jnp.bfloat16


def _dot(a, b):
    return jnp.dot(a, b, preferred_element_type=F32)


def _dot_nt(a, b):
    return lax.dot_general(a, b, (((1,), (1,)), ((), ())), preferred_element_type=F32)


def _rms(x, g):
    return x * lax.rsqrt(jnp.mean(x * x, axis=-1, keepdims=True) + EPS) * g


def _gelu(x):
    return 0.5 * x * (1.0 + lax.erf(x * (0.5 ** 0.5)))


def _silu(x):
    return x * (1.0 / (1.0 + jnp.exp(-x)))


def _params(n_axes):
    return pltpu.CompilerParams(dimension_semantics=("arbitrary",) * n_axes,
                                vmem_limit_bytes=VMEM_LIMIT)


def _adaln_kernel(cctx_ref, c_ref, w_ref, b_ref, win_ref, wuq_ref, wukv_ref, ws_ref,
                  out_ref, wqkv_out, wuv_out, wuq_out, wk_out, wve_out, wvo_out, ws_out):
    pad = jnp.zeros((MOD_ROWS - 1 - c_ref.shape[0], D_MODEL), F32)
    cond = jnp.concatenate([cctx_ref[...], c_ref[...], pad], axis=0)
    s = _silu(cond).astype(BF16)
    out_ref[:, 0, :] = _dot(s, w_ref[...].astype(BF16)) + b_ref[...]
    @pl.when(pl.program_id(0) == 0)
    def _():
        kr = win_ref[IN_QKV:IN_QKV + QK_ROPE, :].astype(BF16)
        z = jnp.zeros_like(kr)
        wqkv_out[0:IN_QKV, :] = win_ref[0:IN_QKV, :].astype(BF16)
        for t, blk in enumerate((kr, z, kr, z)):
            wqkv_out[IN_QKV + t * QK_ROPE:IN_QKV + (t + 1) * QK_ROPE, :] = blk
        wuv_out[...] = win_ref[IN_QKV + QK_ROPE:, :].astype(BF16)

    w = wuq_ref[...]
    hw = QK_NOPE + QK_ROPE
    zq = jnp.zeros((w.shape[0], HEAD_PAD - hw), F32)
    wuq_out[...] = jnp.concatenate([blk for hd in range(N_HEADS) for blk in (w[:, hd * hw:(hd + 1) * hw], zq)],
                                   axis=1).astype(BF16)
    w = wukv_ref[...]
    zv = jnp.zeros((w.shape[0], V_DIM), F32)
    v_of = lambda hd: w[:, hd * HEAD_PAD + QK_NOPE:(hd + 1) * HEAD_PAD]
    wk_out[...] = jnp.concatenate([blk for hd in range(N_HEADS)
                                   for blk in (w[:, hd * HEAD_PAD:hd * HEAD_PAD + QK_NOPE], zv)], axis=1).astype(BF16)
    wve_out[...] = jnp.concatenate([blk for hd in range(0, N_HEADS, 2) for blk in (v_of(hd), zv)],
                                   axis=1).astype(BF16)
    wvo_out[...] = jnp.concatenate([blk for hd in range(1, N_HEADS, 2) for blk in (zv, v_of(hd))],
                                   axis=1).astype(BF16)
    ws_out[...] = ws_ref[...].astype(BF16)


def _adaln(c_ctx, c, ada_w, ada_b, w_in_t, w_uq, w_ukv, w_s):
    cols = N_MODS * D_MODEL // ADALN_STEPS
    row_block = lambda a, width=None: pl.BlockSpec((a.shape[0] // ADALN_STEPS, width or a.shape[1]),
                                                   lambda j: (j, 0))
    whole = lambda shape: pl.BlockSpec(shape, lambda j: (0, 0), pipeline_mode=pl.Buffered(1))
    return pl.pallas_call(
        _adaln_kernel,
        grid=(ADALN_STEPS,),
        in_specs=[
            pl.BlockSpec(c_ctx.shape, lambda j: (0, 0)),
            pl.BlockSpec(c.shape, lambda j: (0, 0)),
            pl.BlockSpec((D_MODEL, cols), lambda j: (0, j)),
            pl.BlockSpec((1, cols), lambda j: (0, j)),
            whole(w_in_t.shape), row_block(w_uq), row_block(w_ukv), row_block(w_s),
        ],
        out_specs=[pl.BlockSpec((MOD_ROWS, 1, cols), lambda j: (0, 0, j)),
                   whole((IN_QKV + LANES, D_MODEL)), whole((2 * GMLP_W, D_MODEL)),
                   row_block(w_uq, QK_W), row_block(w_ukv, QK_W), row_block(w_ukv, ATTN_W),
                   row_block(w_ukv, ATTN_W), row_block(w_s)],
        out_shape=[jax.ShapeDtypeStruct((MOD_ROWS, 1, N_MODS * D_MODEL), F32),
                   jax.ShapeDtypeStruct((IN_QKV + LANES, D_MODEL), BF16),
                   jax.ShapeDtypeStruct((2 * GMLP_W, D_MODEL), BF16),
                   jax.ShapeDtypeStruct((Q_RANK, QK_W), BF16),
                   jax.ShapeDtypeStruct((KV_RANK, QK_W), BF16),
                   jax.ShapeDtypeStruct((KV_RANK, ATTN_W), BF16),
                   jax.ShapeDtypeStruct((KV_RANK, ATTN_W), BF16),
                   jax.ShapeDtypeStruct(w_s.shape, BF16)],
        compiler_params=_params(1),
        name="adaln",
    )(c_ctx, c, ada_w, ada_b, w_in_t, w_uq, w_ukv, w_s)


def _pre_kernel(*refs, latent, tm, seq, n_casts):
    (x_ref, mod_ref, n1g_ref, wqkv_ref, wuv_ref, qng_ref, wuq_ref, kvg_ref, wk_ref, wve_ref, wvo_ref,
     vng_ref, ws_ref, bs_ref, ogg_ref) = refs[:15]
    n_in = 15 + (4 if latent else 0)
    n_out = 5 + (0 if latent else 2)
    cast_in = refs[n_in:n_in + n_casts]
    outs = refs[n_in + n_casts:n_in + n_casts + n_out]
    cast_out = refs[n_in + n_casts + n_out:n_in + 2 * n_casts + n_out]
    go_sc = refs[-1]
    q_out, k_out, ve_out, vo_out, gm_out = outs[:5]
    for src, dst in zip(cast_in, cast_out):
        dst[...] = src[...].astype(BF16)
    shift1 = mod_ref[:, 0:D_MODEL]
    scale1 = mod_ref[:, D_MODEL:2 * D_MODEL]
    h = (_rms(x_ref[...], n1g_ref[...] * (1.0 + scale1)) + shift1).astype(BF16)

    lane = lax.broadcasted_iota(jnp.int32, (1, LANES), 1)
    rope_lanes = lane >= ROPE_LANE0
    first_half = (lane % (2 * ROT_HALF)) < ROT_HALF

    def expand(ckv_b, kr_k, rows):
        kn = _dot(ckv_b, wk_ref[...])
        for hd in range(N_HEADS):
            sl = slice(hd * HEAD_PAD, (hd + 1) * HEAD_PAD)
            k_out[rows, sl] = (kn[:, sl] + kr_k).astype(BF16)
        ve_out[rows, :] = _dot(ckv_b, wve_ref[...]).astype(BF16)
        vo_out[rows, :] = _dot(ckv_b, wvo_ref[...]).astype(BF16)

    qkv = _dot_nt(h, wqkv_ref[...])
    qn = _rms(qkv[:, 0:Q_RANK], qng_ref[...]).astype(BF16)
    ckv = _rms(qkv[:, Q_RANK:IN_QKV], kvg_ref[...])
    q = _dot(qn, wuq_ref[...]) * (SM_SCALE * LOG2_E)
    kr = qkv[:, IN_QKV:IN_QKV + LANES]
    if latent:
        cos_ref, sin_ref, cckv_ref, ckrt_ref = refs[15:19]
        cos_q = cos_ref[...]

        def rotate(v, cos):
            swapped = jnp.where(first_half, pltpu.roll(v, LANES - ROT_HALF, 1), pltpu.roll(v, ROT_HALF, 1))
            return v * cos + swapped * sin_ref[...]

        kr_k = rotate(kr, jnp.where(rope_lanes, cos_q, 0.0))
        ckrt = ckrt_ref[...]
        past = ckrt.shape[1]
        padded = jnp.concatenate([ckrt, jnp.zeros((LANES - QK_ROPE, past), F32)], axis=0)
        expand(cckv_ref[...].astype(BF16), pltpu.roll(padded.T, ROPE_LANE0, 1), slice(0, past))
        own_rows = slice(past, past + tm)
    else:
        ckv_out, krt_out = outs[5:7]
        ckv_out[...] = ckv
        krt = kr.T
        for s in range(tm // seq):
            krt_out[s] = krt[0:QK_ROPE, s * seq:(s + 1) * seq]
        kr_k = jnp.where(rope_lanes, kr, 0.0)
        own_rows = slice(0, tm)
    expand(ckv.astype(BF16), kr_k, own_rows)
    for hd in range(N_HEADS):
        sl = slice(hd * HEAD_PAD, (hd + 1) * HEAD_PAD)
        qh = q[:, sl]
        if latent:
            qh = rotate(qh, cos_q)
        q_out[:, sl] = qh.astype(BF16)

    uv = _dot_nt(h, wuv_ref[...])
    gu = _gelu(uv[:, 0:GMLP_W])
    gv = _gelu(uv[:, GMLP_W:2 * GMLP_W])
    eye = (lax.broadcasted_iota(jnp.int32, (CHUNK, CHUNK), 0)
           == lax.broadcasted_iota(jnp.int32, (CHUNK, CHUNK), 1))
    for g in range(GMLP_GROUPS):
        sl = slice(g * GMLP_DG, (g + 1) * GMLP_DG)
        vg = gv[:, sl]
        vg = (vg * lax.rsqrt(jnp.mean(vg * vg, axis=-1, keepdims=True) + EPS) * vng_ref[:, sl]).astype(BF16)
        bias = jnp.sum(jnp.where(eye, bs_ref[g:g + 1, :], 0.0), axis=1, keepdims=True)
        chunks = [vg[n * CHUNK:(n + 1) * CHUNK, :] for n in range(tm // CHUNK)]
        s_all = _dot(ws_ref[g], jnp.concatenate(chunks, axis=1))
        for n in range(tm // CHUNK):
            rows = slice(n * CHUNK, (n + 1) * CHUNK)
            go_sc[rows, sl] = gu[rows, sl] * (s_all[:, n * GMLP_DG:(n + 1) * GMLP_DG] + bias)
    gm_out[...] = _rms(go_sc[...], ogg_ref[...]).astype(BF16)


def _pre(x, mods3, mod_row, p, seq, tm, casts, latent=None):
    n = x.shape[0]
    const = lambda i: (0, 0)
    tile = lambda i: (i, 0)
    in_specs = [
        pl.BlockSpec((tm, D_MODEL), tile),
        pl.BlockSpec((None, 1, N_MODS * D_MODEL), lambda i: (mod_row(i), 0, 0)),
        pl.BlockSpec((1, D_MODEL), const),
        pl.BlockSpec((IN_QKV + LANES, D_MODEL), const),
        pl.BlockSpec((2 * GMLP_W, D_MODEL), const),
        pl.BlockSpec((1, Q_RANK), const),
        pl.BlockSpec((Q_RANK, QK_W), const),
        pl.BlockSpec((1, KV_RANK), const),
        pl.BlockSpec((KV_RANK, QK_W), const),
        pl.BlockSpec((KV_RANK, ATTN_W), const),
        pl.BlockSpec((KV_RANK, ATTN_W), const),
        pl.BlockSpec((1, GMLP_W), const),
        pl.BlockSpec((GMLP_GROUPS, CHUNK, CHUNK), lambda i: (0, 0, 0)),
        pl.BlockSpec((GMLP_GROUPS, CHUNK), const),
        pl.BlockSpec((1, GMLP_W), const),
    ]
    args = [x, mods3, p["norm1_g"], p["w_qkv"], p["w_uv"], p["q_norm_g"], p["w_uq"],
            p["kv_norm_g"], p["w_k"], p["w_ve"], p["w_vo"], p["v_norm_g"], p["w_s"], p["b_s"], p["out_g_gmlp"]]
    past = 0 if latent is None else latent[3].shape[2]
    kv_rows = tm + past
    out_rows = (tm, kv_rows, kv_rows, kv_rows, tm)
    widths = (QK_W, QK_W, ATTN_W, ATTN_W, GMLP_W)
    out_specs = [pl.BlockSpec((r, w), tile) for r, w in zip(out_rows, widths)]
    out_shape = [jax.ShapeDtypeStruct((n // tm * r, w), BF16) for r, w in zip(out_rows, widths)]
    if latent is not None:
        cos_q, sin_q, cache_ckv, cache_krt = latent
        assert tm == seq == cos_q.shape[0]
        in_specs += [pl.BlockSpec((tm, LANES), const), pl.BlockSpec((tm, LANES), const),
                     pl.BlockSpec((past, KV_RANK), tile),
                     pl.BlockSpec((None, QK_ROPE, past), lambda i: (i, 0, 0))]
        args += [cos_q, sin_q, cache_ckv, cache_krt]
    else:
        out_specs += [pl.BlockSpec((tm, KV_RANK), tile),
                      pl.BlockSpec((tm // seq, QK_ROPE, seq), lambda i: (i, 0, 0))]
        out_shape += [jax.ShapeDtypeStruct((n, KV_RANK), F32),
                      jax.ShapeDtypeStruct((n // seq, QK_ROPE, seq), F32)]
    n_out = len(out_specs)
    cast_specs = [pl.BlockSpec((w.shape[0] // (n // tm), w.shape[1]), tile) for w in casts]
    outs = pl.pallas_call(
        functools.partial(_pre_kernel, latent=latent is not None, tm=tm, seq=seq, n_casts=len(casts)),
        grid=(n // tm,),
        in_specs=in_specs + cast_specs,
        out_specs=out_specs + cast_specs,
        out_shape=out_shape + [jax.ShapeDtypeStruct(w.shape, BF16) for w in casts],
        scratch_shapes=[pltpu.VMEM((tm, GMLP_W), F32)],
        compiler_params=_params(1),
        name="pre_latent" if latent is not None else "pre",
    )(*args, *casts)
    return outs[:n_out], outs[n_out:]


def _attn_kernel(*refs, group, kv_len, tq, n_casts):
    q_ref, k_ref, ve_ref, vo_ref, oga_ref = refs[:5]
    cast_in = refs[5:5 + n_casts]
    out_ref = refs[5 + n_casts]
    cast_out = refs[6 + n_casts:6 + 2 * n_casts]
    for src, dst in zip(cast_in, cast_out):
        dst[...] = src[...].astype(BF16)
    ones = jnp.ones((kv_len, LANES), BF16)
    for g in range(group):
        qrows = slice(g * tq, (g + 1) * tq)
        krows = slice(g * kv_len, (g + 1) * kv_len)
        pairs = []
        for pp in range(N_HEADS // 2):
            psl = slice(pp * LANES, (pp + 1) * LANES)
            acc = None
            for par, v_ref in enumerate((ve_ref, vo_ref)):
                hd = 2 * pp + par
                hsl = slice(hd * HEAD_PAD, (hd + 1) * HEAD_PAD)
                s = _dot_nt(q_ref[qrows, hsl], k_ref[krows, hsl])
                e = jnp.exp2(s - jnp.max(s, axis=-1, keepdims=True)).astype(BF16)
                o = _dot(e, jnp.concatenate([v_ref[krows, psl], ones], axis=1))
                o = o[:, 0:LANES] * (1.0 / o[:, LANES:2 * LANES])
                acc = o if acc is None else acc + o
            pairs.append(acc)
        ssq = sum(jnp.sum(b * b, axis=-1, keepdims=True) for b in pairs)
        r = lax.rsqrt(ssq * (1.0 / ATTN_W) + EPS)
        for j, b in enumerate(pairs):
            sl = slice(j * LANES, (j + 1) * LANES)
            out_ref[qrows, sl] = (b * r * oga_ref[:, sl]).astype(BF16)


def _attention(q, k, ve, vo, oga, seq, kv_len, tq, group, casts):
    n = q.shape[0]
    nq = seq // tq
    grid = (n // (group * seq), nq)
    linear = lambda b, i: (b * nq + i, 0)
    cast_specs = [pl.BlockSpec((w.shape[0] // (grid[0] * nq), w.shape[1]), linear) for w in casts]
    kv = lambda b, i: (b, 0)
    in_specs = [
        pl.BlockSpec((group * tq, QK_W), linear),
        pl.BlockSpec((group * kv_len, QK_W), kv),
        pl.BlockSpec((group * kv_len, ATTN_W), kv),
        pl.BlockSpec((group * kv_len, ATTN_W), kv),
        pl.BlockSpec((1, ATTN_W), lambda b, i: (0, 0)),
    ]
    outs = pl.pallas_call(
        functools.partial(_attn_kernel, group=group, kv_len=kv_len, tq=tq, n_casts=len(casts)),
        grid=grid,
        in_specs=in_specs + cast_specs,
        out_specs=[pl.BlockSpec((group * tq, ATTN_W), linear)] + cast_specs,
        out_shape=[jax.ShapeDtypeStruct((n, ATTN_W), BF16)]
        + [jax.ShapeDtypeStruct(w.shape, BF16) for w in casts],
        compiler_params=_params(2),
        name="attn",
    )(q, k, ve, vo, oga, *casts)
    return outs[0], outs[1:]


def _ffn_kernel(*refs, nseg, seg, tiles_per_seq):
    halo = tiles_per_seq > 1
    if halo:
        (x_ref, an_ref, gm_ref, xp_ref, xn_ref, anp_ref, ann_ref, gmp_ref, gmn_ref, mod_ref, wo_ref, n2g_ref,
         wup_ref, cw_ref, cb_ref, wd_ref, fg_ref, out_ref, lhs_sc, a_sc, g_sc, mix_sc) = refs
    else:
        (x_ref, an_ref, gm_ref, mod_ref, wo_ref, n2g_ref,
         wup_ref, cw_ref, cb_ref, wd_ref, fg_ref, out_ref, lhs_sc, a_sc, g_sc) = refs
    tm = nseg * seg
    stride = seg + HALO
    edge = HALO // 2
    gate1 = mod_ref[:, 2 * D_MODEL:3 * D_MODEL]
    shift2 = mod_ref[:, 3 * D_MODEL:4 * D_MODEL]
    scale2 = mod_ref[:, 4 * D_MODEL:5 * D_MODEL]

    def residual_norm(x, y, keep=None):
        x1 = x + gate1 * y
        h2 = _rms(x1, n2g_ref[...] * (1.0 + scale2)) + shift2
        if keep is not None:
            h2 = jnp.where(keep, h2, 0.0)
        return x1, h2.astype(BF16)

    if halo:
        def neighbours(prev_ref, next_ref):
            return jnp.concatenate([prev_ref[...].astype(F32)[edge:HALO, :],
                                    next_ref[...].astype(F32)[0:edge, :]], axis=0)

        mix_sc[0:tm, 0:ATTN_W] = an_ref[...]
        mix_sc[0:tm, ATTN_W:D_MODEL] = gm_ref[...]
        mix_sc[tm:tm + HALO, 0:ATTN_W] = neighbours(anp_ref, ann_ref).astype(BF16)
        mix_sc[tm:tm + HALO, ATTN_W:D_MODEL] = neighbours(gmp_ref, gmn_ref).astype(BF16)
        y = _dot(mix_sc[...], wo_ref[...])
        x1, h2 = residual_norm(x_ref[...], y[0:tm])
        out_ref[...] = x1
        lhs_sc[0:tm, :] = h2
        pos = pl.program_id(0) % tiles_per_seq
        is_prev = lax.broadcasted_iota(jnp.int32, (HALO, 1), 0) < edge
        has_prev = (pos != 0).astype(jnp.int32)
        has_next = (pos != tiles_per_seq - 1).astype(jnp.int32)
        keep = jnp.where(is_prev, has_prev, has_next) > 0
        lhs_sc[tm:tm + HALO, :] = residual_norm(neighbours(xp_ref, xn_ref), y[tm:tm + HALO], keep)[1]
    else:
        y = _dot(an_ref[...], wo_ref[0:ATTN_W, :]) + _dot(gm_ref[...], wo_ref[ATTN_W:D_MODEL, :])
        x1, h2 = residual_norm(x_ref[...], y)
        out_ref[...] = x1
        lhs_sc[...] = h2

        @pl.when(pl.program_id(0) == 0)
        def _():
            zeros = jnp.zeros((A_SLOTS, 2 * FF_CHUNK // LANES, HALO, LANES), F32)
            a_sc[:, :, 0:edge, :] = zeros[:, :, 0:edge, :]
            for s in range(1, nseg):
                a_sc[:, :, s * stride - edge:s * stride + edge, :] = zeros
            a_sc[:, :, nseg * stride - edge:nseg * stride, :] = zeros[:, :, 0:edge, :]

    for c in range(N_FF_CHUNKS):
        slot = c % A_SLOTS
        for k, col0 in enumerate((c * FF_CHUNK, D_FF + c * FF_CHUNK)):
            a = _dot(lhs_sc[...], wup_ref[:, col0:col0 + FF_CHUNK])
            for j in range(FF_CHUNK // LANES):
                lanes = slice(j * LANES, (j + 1) * LANES)
                for s in range(nseg):
                    row0 = edge + s * stride
                    a_sc[slot, 2 * k + j, row0:row0 + seg, :] = a[s * seg:(s + 1) * seg, lanes]
                if halo:
                    a_sc[slot, 2 * k + j, 0:edge, :] = a[tm:tm + edge, lanes]
                    a_sc[slot, 2 * k + j, edge + seg:HALO + seg, :] = a[tm + edge:tm + HALO, lanes]
        for j in range(FF_CHUNK // LANES):
            gl = slice(c * FF_CHUNK + j * LANES, c * FF_CHUNK + (j + 1) * LANES)
            vl = slice(D_FF + gl.start, D_FF + gl.stop)
            for s in range(nseg):
                base = edge + s * stride

                def conv(k, lanes):
                    acc = cb_ref[:, lanes]
                    for t in range(CONV_W):
                        tap = slice(t * 2 * D_FF + lanes.start, t * 2 * D_FF + lanes.stop)
                        acc = acc + a_sc[slot, k, base - 1 + t:base - 1 + t + seg, :] * cw_ref[:, tap]
                    return acc

                g = _silu(conv(j, gl)) * conv(2 + j, vl)
                g_sc[s * seg:(s + 1) * seg, gl] = g.astype(BF16)

    x2 = out_ref[...] + mod_ref[:, 5 * D_MODEL:6 * D_MODEL] * _dot(g_sc[...], wd_ref[...])
    out_ref[...] = _rms(x2, fg_ref[...])


def _ffn(x, an, gm, mods3, mod_row, p, ffn_w, seq, tm):
    n = x.shape[0]
    tiles_per_seq = max(seq // tm, 1)
    seg = min(seq, tm)
    nseg = tm // seg
    rows = tm + (HALO if tiles_per_seq > 1 else 0)
    a_rows = nseg * (seg + HALO)
    const = lambda i: (0, 0)
    tile = lambda i: (i, 0)
    resident = dict(pipeline_mode=pl.Buffered(1))
    in_specs = [pl.BlockSpec((tm, D_MODEL), tile), pl.BlockSpec((tm, ATTN_W), tile),
                pl.BlockSpec((tm, GMLP_W), tile)]
    args = [x, an, gm]
    scratch = [pltpu.VMEM((rows, D_MODEL), BF16),
               pltpu.VMEM((A_SLOTS, 2 * FF_CHUNK // LANES, a_rows, LANES), F32),
               pltpu.VMEM((tm, D_FF), BF16)]
    if tiles_per_seq > 1:
        per = tm // HALO
        last = n // HALO - 1
        prev = lambda i: (jnp.maximum(i * per - 1, 0), 0)
        nxt = lambda i: (jnp.minimum((i + 1) * per, last), 0)
        for arr, width in ((x, D_MODEL), (an, ATTN_W), (gm, GMLP_W)):
            in_specs += [pl.BlockSpec((HALO, width), prev), pl.BlockSpec((HALO, width), nxt)]
            args += [arr, arr]
        scratch.append(pltpu.VMEM((rows, D_MODEL), BF16))
    in_specs += [
        pl.BlockSpec((None, 1, N_MODS * D_MODEL), lambda i: (mod_row(i), 0, 0)),
        pl.BlockSpec((D_MODEL, D_MODEL), const, **resident),
        pl.BlockSpec((1, D_MODEL), const),
        pl.BlockSpec((D_MODEL, 2 * D_FF), const, **resident),
        pl.BlockSpec((1, CONV_W * 2 * D_FF), const),
        pl.BlockSpec((1, 2 * D_FF), const),
        pl.BlockSpec((D_FF, D_MODEL), const, **resident),
        pl.BlockSpec((1, D_MODEL), const),
    ]
    w_o, w_up, w_down = ffn_w
    args += [mods3, w_o, p["norm2_g"], w_up, p["conv_w"], p["conv_b"], w_down, p["final_g"]]
    return pl.pallas_call(
        functools.partial(_ffn_kernel, nseg=nseg, seg=seg, tiles_per_seq=tiles_per_seq),
        grid=(n // tm,),
        in_specs=in_specs,
        out_specs=pl.BlockSpec((tm, D_MODEL), tile),
        out_shape=jax.ShapeDtypeStruct((n, D_MODEL), F32),
        scratch_shapes=scratch,
        compiler_params=_params(1),
        name="ffn_halo" if tiles_per_seq > 1 else "ffn",
    )(*args)


def _rope_tables(length):
    pos = np.arange(length)
    row = (pos // GRID_W).astype(np.float32)
    col = (pos % GRID_W).astype(np.float32)
    n_freq = QK_ROPE // 4
    inv = (np.float32(ROPE_THETA) ** (-(np.arange(n_freq, dtype=np.float32) / np.float32(n_freq)))).astype(np.float32)
    ang_r, ang_c = row[:, None] * inv, col[:, None] * inv
    cos32 = np.concatenate([np.cos(ang_r)] * 2 + [np.cos(ang_c)] * 2, axis=-1)
    sin32 = np.concatenate([-np.sin(ang_r), np.sin(ang_r), -np.sin(ang_c), np.sin(ang_c)], axis=-1)
    tail = LANES - ROPE_LANE0 - QK_ROPE
    cos_q = np.concatenate([np.ones((length, ROPE_LANE0)), cos32, np.zeros((length, tail))], -1)
    sin_q = np.concatenate([np.zeros((length, ROPE_LANE0)), sin32, np.zeros((length, tail))], -1)
    return jnp.asarray(cos_q, F32), jnp.asarray(sin_q, F32)


def _layer_params(l, norm1_g, q_norm_g, kv_norm_g, v_norm_g, b_s, out_g_attn,
                  out_g_gmlp, norm2_g, conv_w, conv_b, final_g):
    row = lambda a: a.reshape(1, -1)
    return {
        "norm1_g": row(norm1_g[l]), "q_norm_g": row(q_norm_g[l]),
        "kv_norm_g": row(kv_norm_g[l]), "v_norm_g": row(v_norm_g[l]), "b_s": b_s[l],
        "out_g_attn": row(out_g_attn[l]), "out_g_gmlp": row(out_g_gmlp[l]), "norm2_g": row(norm2_g[l]),
        "conv_w": conv_w[l].reshape(1, CONV_W * 2 * D_FF), "conv_b": row(conv_b[l]), "final_g": row(final_g),
    }


def kernel(x_prompt, x_sample, cache_ckv, cache_krope, c, c_ctx, ada_w, ada_b, norm1_g, w_in, q_norm_g, w_uq, kv_norm_g, w_ukv, v_norm_g, w_s, b_s, out_g_attn, out_g_gmlp, w_o, norm2_g, w_up, conv_w, conv_b, w_down, final_g):
    batch, seq, _ = x_prompt.shape
    dec_batch, dec_seq, _ = x_sample.shape
    depth, past = cache_ckv.shape[1], cache_ckv.shape[2]
    assert depth == 1 and dec_batch + 1 <= MOD_ROWS
    l = 0
    p = _layer_params(l, norm1_g, q_norm_g, kv_norm_g, v_norm_g, b_s, out_g_attn,
                      out_g_gmlp, norm2_g, conv_w, conv_b, final_g)

    mods3, p["w_qkv"], p["w_uv"], p["w_uq"], p["w_k"], p["w_ve"], p["w_vo"], w_s_b = _adaln(
        c_ctx.reshape(1, D_MODEL), c, ada_w[l], ada_b[l].reshape(1, -1), w_in[l].T, w_uq[l], w_ukv[l],
        w_s[l].reshape(GMLP_GROUPS * CHUNK, CHUNK))
    p["w_s"] = w_s_b.reshape(GMLP_GROUPS, CHUNK, CHUNK)
    tm_pre, tm_ffn = 1024, 512
    ctx_row = lambda i: 0
    lat_row = lambda tm: lambda i: 1 + i // (dec_seq // tm)

    xp = x_prompt.reshape(batch * seq, D_MODEL)
    xs = x_sample.reshape(dec_batch * dec_seq, D_MODEL)
    (qp, kp, vep, vop, gmp, ckv_p, krt_p), (w_down_b,) = _pre(xp, mods3, ctx_row, p, seq, tm_pre, [w_down[l]])
    cache_in = (*_rope_tables(dec_seq), cache_ckv[:, l].reshape(dec_batch * past, KV_RANK),
                jnp.swapaxes(cache_krope[:, l], 1, 2))
    (qs, ks, ves, vos, gms), (w_o_b,) = _pre(xs, mods3, lat_row(tm_pre), p, dec_seq, tm_pre, [w_o[l]],
                                             latent=cache_in)

    ans, (w_up_b,) = _attention(qs, ks, ves, vos, p["out_g_attn"], dec_seq, past + dec_seq, 1024, 1, [w_up[l]])
    anp, _ = _attention(qp, kp, vep, vop, p["out_g_attn"], seq, seq, seq, 4, [])
    ffn_w = (w_o_b, w_up_b, w_down_b)
    yp = _ffn(xp, anp, gmp, mods3, ctx_row, p, ffn_w, seq, tm_ffn)
    ys = _ffn(xs, ans, gms, mods3, lat_row(tm_ffn), p, ffn_w, dec_seq, tm_ffn)

    return (yp.reshape(batch, seq, D_MODEL), ys.reshape(dec_batch, dec_seq, D_MODEL),
            ckv_p.reshape(batch, 1, seq, KV_RANK), jnp.swapaxes(krt_p, 1, 2).reshape(batch, 1, seq, QK_ROPE))
```

```python
import functools

import numpy as np
import jax
import jax.numpy as jnp
from jax import lax
from jax.experimental import pallas as pl
from jax.experimental.pallas import tpu as pltpu

D_MODEL = 1024
GRID_W = 64
CHUNK = 128
N_HEADS = 8
QK_NOPE = 64
QK_ROPE = 32
V_DIM = 64
Q_RANK = 384
KV_RANK = 256
ATTN_W = N_HEADS * V_DIM
GMLP_W = D_MODEL - ATTN_W
GMLP_GROUPS = 4
GMLP_DG = GMLP_W // GMLP_GROUPS
D_FF = 2816
CONV_W = 3
ROPE_THETA = 10000.0
EPS = 1e-6
SM_SCALE = (QK_NOPE + QK_ROPE) ** -0.5
LOG2_E = 1.4426950408889634

LANES = 128
HEAD_PAD = LANES
ROPE_LANE0 = QK_NOPE
ROT_HALF = QK_ROPE // 4
QK_W = N_HEADS * HEAD_PAD
N_MODS = 6
MOD_ROWS = 8
ADALN_STEPS = 8
IN_QKV = Q_RANK + KV_RANK
FF_CHUNK = 256
HALO = 16
N_FF_CHUNKS = D_FF // FF_CHUNK
A_SLOTS = 4
V7X_VMEM_BYTES = 64 * 1024 * 1024
VMEM_LIMIT = V7X_VMEM_BYTES - 8 * 1024 * 1024

F32 = jnp.float32
BF16 = jnp.bfloat16


def _dot(a, b):
    return jnp.dot(a, b, preferred_element_type=F32)


def _dot_nt(a, b):
    return lax.dot_general(a, b, (((1,), (1,)), ((), ())), preferred_element_type=F32)


def _rms(x, g):
    return x * lax.rsqrt(jnp.mean(x * x, axis=-1, keepdims=True) + EPS) * g


def _gelu(x):
    return 0.5 * x * (1.0 + lax.erf(x * (0.5 ** 0.5)))


def _silu(x):
    return x * (1.0 / (1.0 + jnp.exp(-x)))


def _params(n_axes):
    return pltpu.CompilerParams(dimension_semantics=("arbitrary",) * n_axes,
                                vmem_limit_bytes=VMEM_LIMIT)


def _adaln_kernel(cctx_ref, c_ref, w_ref, b_ref, win_ref, wuq_ref, wukv_ref, ws_ref,
                  out_ref, wqkv_out, wuv_out, wuq_out, wk_out, wve_out, wvo_out, ws_out):
    pad = jnp.zeros((MOD_ROWS - 1 - c_ref.shape[0], D_MODEL), F32)
    cond = jnp.concatenate([cctx_ref[...], c_ref[...], pad], axis=0)
    s = _silu(cond).astype(BF16)
    out_ref[:, 0, :] = _dot(s, w_ref[...].astype(BF16)) + b_ref[...]
    @pl.when(pl.program_id(0) == 0)
    def _():
        kr = win_ref[IN_QKV:IN_QKV + QK_ROPE, :].astype(BF16)
        z = jnp.zeros_like(kr)
        wqkv_out[0:IN_QKV, :] = win_ref[0:IN_QKV, :].astype(BF16)
        for t, blk in enumerate((kr, z, kr, z)):
            wqkv_out[IN_QKV + t * QK_ROPE:IN_QKV + (t + 1) * QK_ROPE, :] = blk
        wuv_out[...] = win_ref[IN_QKV + QK_ROPE:, :].astype(BF16)

    w = wuq_ref[...]
    hw = QK_NOPE + QK_ROPE
    zq = jnp.zeros((w.shape[0], HEAD_PAD - hw), F32)
    wuq_out[...] = jnp.concatenate([blk for hd in range(N_HEADS) for blk in (w[:, hd * hw:(hd + 1) * hw], zq)],
                                   axis=1).astype(BF16)
    w = wukv_ref[...]
    zv = jnp.zeros((w.shape[0], V_DIM), F32)
    v_of = lambda hd: w[:, hd * HEAD_PAD + QK_NOPE:(hd + 1) * HEAD_PAD]
    wk_out[...] = jnp.concatenate([blk for hd in range(N_HEADS)
                                   for blk in (w[:, hd * HEAD_PAD:hd * HEAD_PAD + QK_NOPE], zv)], axis=1).astype(BF16)
    wve_out[...] = jnp.concatenate([blk for hd in range(0, N_HEADS, 2) for blk in (v_of(hd), zv)],
                                   axis=1).astype(BF16)
    wvo_out[...] = jnp.concatenate([blk for hd in range(1, N_HEADS, 2) for blk in (zv, v_of(hd))],
                                   axis=1).astype(BF16)
    ws_out[...] = ws_ref[...].astype(BF16)


def _adaln(c_ctx, c, ada_w, ada_b, w_in_t, w_uq, w_ukv, w_s):
    cols = N_MODS * D_MODEL // ADALN_STEPS
    row_block = lambda a, width=None: pl.BlockSpec((a.shape[0] // ADALN_STEPS, width or a.shape[1]),
                                                   lambda j: (j, 0))
    whole = lambda shape: pl.BlockSpec(shape, lambda j: (0, 0), pipeline_mode=pl.Buffered(1))
    return pl.pallas_call(
        _adaln_kernel,
        grid=(ADALN_STEPS,),
        in_specs=[
            pl.BlockSpec(c_ctx.shape, lambda j: (0, 0)),
            pl.BlockSpec(c.shape, lambda j: (0, 0)),
            pl.BlockSpec((D_MODEL, cols), lambda j: (0, j)),
            pl.BlockSpec((1, cols), lambda j: (0, j)),
            whole(w_in_t.shape), row_block(w_uq), row_block(w_ukv), row_block(w_s),
        ],
        out_specs=[pl.BlockSpec((MOD_ROWS, 1, cols), lambda j: (0, 0, j)),
                   whole((IN_QKV + LANES, D_MODEL)), whole((2 * GMLP_W, D_MODEL)),
                   row_block(w_uq, QK_W), row_block(w_ukv, QK_W), row_block(w_ukv, ATTN_W),
                   row_block(w_ukv, ATTN_W), row_block(w_s)],
        out_shape=[jax.ShapeDtypeStruct((MOD_ROWS, 1, N_MODS * D_MODEL), F32),
                   jax.ShapeDtypeStruct((IN_QKV + LANES, D_MODEL), BF16),
                   jax.ShapeDtypeStruct((2 * GMLP_W, D_MODEL), BF16),
                   jax.ShapeDtypeStruct((Q_RANK, QK_W), BF16),
                   jax.ShapeDtypeStruct((KV_RANK, QK_W), BF16),
                   jax.ShapeDtypeStruct((KV_RANK, ATTN_W), BF16),
                   jax.ShapeDtypeStruct((KV_RANK, ATTN_W), BF16),
                   jax.ShapeDtypeStruct(w_s.shape, BF16)],
        compiler_params=_params(1),
        name="adaln",
    )(c_ctx, c, ada_w, ada_b, w_in_t, w_uq, w_ukv, w_s)


def _pre_kernel(*refs, latent, tm, seq, n_casts):
    (x_ref, mod_ref, n1g_ref, wqkv_ref, wuv_ref, qng_ref, wuq_ref, kvg_ref, wk_ref, wve_ref, wvo_ref,
     vng_ref, ws_ref, bs_ref, ogg_ref) = refs[:15]
    n_in = 15 + (4 if latent else 0)
    n_out = 5 + (0 if latent else 2)
    cast_in = refs[n_in:n_in + n_casts]
    outs = refs[n_in + n_casts:n_in + n_casts + n_out]
    cast_out = refs[n_in + n_casts + n_out:n_in + 2 * n_casts + n_out]
    go_sc = refs[-1]
    q_out, k_out, ve_out, vo_out, gm_out = outs[:5]
    for src, dst in zip(cast_in, cast_out):
        dst[...] = src[...].astype(BF16)
    shift1 = mod_ref[:, 0:D_MODEL]
    scale1 = mod_ref[:, D_MODEL:2 * D_MODEL]
    h = (_rms(x_ref[...], n1g_ref[...] * (1.0 + scale1)) + shift1).astype(BF16)

    lane = lax.broadcasted_iota(jnp.int32, (1, LANES), 1)
    rope_lanes = lane >= ROPE_LANE0
    first_half = (lane % (2 * ROT_HALF)) < ROT_HALF

    def expand(ckv_b, kr_k, rows):
        kn = _dot(ckv_b, wk_ref[...])
        for hd in range(N_HEADS):
            sl = slice(hd * HEAD_PAD, (hd + 1) * HEAD_PAD)
            k_out[rows, sl] = (kn[:, sl] + kr_k).astype(BF16)
        ve_out[rows, :] = _dot(ckv_b, wve_ref[...]).astype(BF16)
        vo_out[rows, :] = _dot(ckv_b, wvo_ref[...]).astype(BF16)

    qkv = _dot_nt(h, wqkv_ref[...])
    qn = _rms(qkv[:, 0:Q_RANK], qng_ref[...]).astype(BF16)
    ckv = _rms(qkv[:, Q_RANK:IN_QKV], kvg_ref[...])
    q = _dot(qn, wuq_ref[...]) * (SM_SCALE * LOG2_E)
    kr = qkv[:, IN_QKV:IN_QKV + LANES]
    if latent:
        cos_ref, sin_ref, cckv_ref, ckrt_ref = refs[15:19]
        cos_q = cos_ref[...]

        def rotate(v, cos):
            swapped = jnp.where(first_half, pltpu.roll(v, LANES - ROT_HALF, 1), pltpu.roll(v, ROT_HALF, 1))
            return v * cos + swapped * sin_ref[...]

        kr_k = rotate(kr, jnp.where(rope_lanes, cos_q, 0.0))
        ckrt = ckrt_ref[...]
        past = ckrt.shape[1]
        padded = jnp.concatenate([ckrt, jnp.zeros((LANES - QK_ROPE, past), F32)], axis=0)
        expand(cckv_ref[...].astype(BF16), pltpu.roll(padded.T, ROPE_LANE0, 1), slice(0, past))
        own_rows = slice(past, past + tm)
    else:
        ckv_out, krt_out = outs[5:7]
        ckv_out[...] = ckv
        krt = kr.T
        for s in range(tm // seq):
            krt_out[s] = krt[0:QK_ROPE, s * seq:(s + 1) * seq]
        kr_k = jnp.where(rope_lanes, kr, 0.0)
        own_rows = slice(0, tm)
    expand(ckv.astype(BF16), kr_k, own_rows)
    for hd in range(N_HEADS):
        sl = slice(hd * HEAD_PAD, (hd + 1) * HEAD_PAD)
        qh = q[:, sl]
        if latent:
            qh = rotate(qh, cos_q)
        q_out[:, sl] = qh.astype(BF16)

    uv = _dot_nt(h, wuv_ref[...])
    gu = _gelu(uv[:, 0:GMLP_W])
    gv = _gelu(uv[:, GMLP_W:2 * GMLP_W])
    eye = (lax.broadcasted_iota(jnp.int32, (CHUNK, CHUNK), 0)
           == lax.broadcasted_iota(jnp.int32, (CHUNK, CHUNK), 1))
    for g in range(GMLP_GROUPS):
        sl = slice(g * GMLP_DG, (g + 1) * GMLP_DG)
        vg = gv[:, sl]
        vg = (vg * lax.rsqrt(jnp.mean(vg * vg, axis=-1, keepdims=True) + EPS) * vng_ref[:, sl]).astype(BF16)
        bias = jnp.sum(jnp.where(eye, bs_ref[g:g + 1, :], 0.0), axis=1, keepdims=True)
        chunks = [vg[n * CHUNK:(n + 1) * CHUNK, :] for n in range(tm // CHUNK)]
        s_all = _dot(ws_ref[g], jnp.concatenate(chunks, axis=1))
        for n in range(tm // CHUNK):
            rows = slice(n * CHUNK, (n + 1) * CHUNK)
            go_sc[rows, sl] = gu[rows, sl] * (s_all[:, n * GMLP_DG:(n + 1) * GMLP_DG] + bias)
    gm_out[...] = _rms(go_sc[...], ogg_ref[...]).astype(BF16)


def _pre(x, mods3, mod_row, p, seq, tm, casts, latent=None):
    n = x.shape[0]
    const = lambda i: (0, 0)
    tile = lambda i: (i, 0)
    in_specs = [
        pl.BlockSpec((tm, D_MODEL), tile),
        pl.BlockSpec((None, 1, N_MODS * D_MODEL), lambda i: (mod_row(i), 0, 0)),
        pl.BlockSpec((1, D_MODEL), const),
        pl.BlockSpec((IN_QKV + LANES, D_MODEL), const),
        pl.BlockSpec((2 * GMLP_W, D_MODEL), const),
        pl.BlockSpec((1, Q_RANK), const),
        pl.BlockSpec((Q_RANK, QK_W), const),
        pl.BlockSpec((1, KV_RANK), const),
        pl.BlockSpec((KV_RANK, QK_W), const),
        pl.BlockSpec((KV_RANK, ATTN_W), const),
        pl.BlockSpec((KV_RANK, ATTN_W), const),
        pl.BlockSpec((1, GMLP_W), const),
        pl.BlockSpec((GMLP_GROUPS, CHUNK, CHUNK), lambda i: (0, 0, 0)),
        pl.BlockSpec((GMLP_GROUPS, CHUNK), const),
        pl.BlockSpec((1, GMLP_W), const),
    ]
    args = [x, mods3, p["norm1_g"], p["w_qkv"], p["w_uv"], p["q_norm_g"], p["w_uq"],
            p["kv_norm_g"], p["w_k"], p["w_ve"], p["w_vo"], p["v_norm_g"], p["w_s"], p["b_s"], p["out_g_gmlp"]]
    past = 0 if latent is None else latent[3].shape[2]
    kv_rows = tm + past
    out_rows = (tm, kv_rows, kv_rows, kv_rows, tm)
    widths = (QK_W, QK_W, ATTN_W, ATTN_W, GMLP_W)
    out_specs = [pl.BlockSpec((r, w), tile) for r, w in zip(out_rows, widths)]
    out_shape = [jax.ShapeDtypeStruct((n // tm * r, w), BF16) for r, w in zip(out_rows, widths)]
    if latent is not None:
        cos_q, sin_q, cache_ckv, cache_krt = latent
        assert tm == seq == cos_q.shape[0]
        in_specs += [pl.BlockSpec((tm, LANES), const), pl.BlockSpec((tm, LANES), const),
                     pl.BlockSpec((past, KV_RANK), tile),
                     pl.BlockSpec((None, QK_ROPE, past), lambda i: (i, 0, 0))]
        args += [cos_q, sin_q, cache_ckv, cache_krt]
    else:
        out_specs += [pl.BlockSpec((tm, KV_RANK), tile),
                      pl.BlockSpec((tm // seq, QK_ROPE, seq), lambda i: (i, 0, 0))]
        out_shape += [jax.ShapeDtypeStruct((n, KV_RANK), F32),
                      jax.ShapeDtypeStruct((n // seq, QK_ROPE, seq), F32)]
    n_out = len(out_specs)
    cast_specs = [pl.BlockSpec((w.shape[0] // (n // tm), w.shape[1]), tile) for w in casts]
    outs = pl.pallas_call(
        functools.partial(_pre_kernel, latent=latent is not None, tm=tm, seq=seq, n_casts=len(casts)),
        grid=(n // tm,),
        in_specs=in_specs + cast_specs,
        out_specs=out_specs + cast_specs,
        out_shape=out_shape + [jax.ShapeDtypeStruct(w.shape, BF16) for w in casts],
        scratch_shapes=[pltpu.VMEM((tm, GMLP_W), F32)],
        compiler_params=_params(1),
        name="pre_latent" if latent is not None else "pre",
    )(*args, *casts)
    return outs[:n_out], outs[n_out:]


def _attn_kernel(*refs, group, kv_len, tq, n_casts):
    q_ref, k_ref, ve_ref, vo_ref, oga_ref = refs[:5]
    cast_in = refs[5:5 + n_casts]
    out_ref = refs[5 + n_casts]
    cast_out = refs[6 + n_casts:6 + 2 * n_casts]
    for src, dst in zip(cast_in, cast_out):
        dst[...] = src[...].astype(BF16)
    ones = jnp.ones((kv_len, LANES), BF16)
    for g in range(group):
        qrows = slice(g * tq, (g + 1) * tq)
        krows = slice(g * kv_len, (g + 1) * kv_len)
        pairs = []
        for pp in range(N_HEADS // 2):
            psl = slice(pp * LANES, (pp + 1) * LANES)
            acc = None
            for par, v_ref in enumerate((ve_ref, vo_ref)):
                hd = 2 * pp + par
                hsl = slice(hd * HEAD_PAD, (hd + 1) * HEAD_PAD)
                s = _dot_nt(q_ref[qrows, hsl], k_ref[krows, hsl])
                e = jnp.exp2(s - jnp.max(s, axis=-1, keepdims=True)).astype(BF16)
                o = _dot(e, jnp.concatenate([v_ref[krows, psl], ones], axis=1))
                o = o[:, 0:LANES] * (1.0 / o[:, LANES:2 * LANES])
                acc = o if acc is None else acc + o
            pairs.append(acc)
        ssq = sum(jnp.sum(b * b, axis=-1, keepdims=True) for b in pairs)
        r = lax.rsqrt(ssq * (1.0 / ATTN_W) + EPS)
        for j, b in enumerate(pairs):
            sl = slice(j * LANES, (j + 1) * LANES)
            out_ref[qrows, sl] = (b * r * oga_ref[:, sl]).astype(BF16)


def _attention(q, k, ve, vo, oga, seq, kv_len, tq, group, casts):
    n = q.shape[0]
    nq = seq // tq
    grid = (n // (group * seq), nq)
    linear = lambda b, i: (b * nq + i, 0)
    cast_specs = [pl.BlockSpec((w.shape[0] // (grid[0] * nq), w.shape[1]), linear) for w in casts]
    kv = lambda b, i: (b, 0)
    in_specs = [
        pl.BlockSpec((group * tq, QK_W), linear),
        pl.BlockSpec((group * kv_len, QK_W), kv),
        pl.BlockSpec((group * kv_len, ATTN_W), kv),
        pl.BlockSpec((group * kv_len, ATTN_W), kv),
        pl.BlockSpec((1, ATTN_W), lambda b, i: (0, 0)),
    ]
    outs = pl.pallas_call(
        functools.partial(_attn_kernel, group=group, kv_len=kv_len, tq=tq, n_casts=len(casts)),
        grid=grid,
        in_specs=in_specs + cast_specs,
        out_specs=[pl.BlockSpec((group * tq, ATTN_W), linear)] + cast_specs,
        out_shape=[jax.ShapeDtypeStruct((n, ATTN_W), BF16)]
        + [jax.ShapeDtypeStruct(w.shape, BF16) for w in casts],
        compiler_params=_params(2),
        name="attn",
    )(q, k, ve, vo, oga, *casts)
    return outs[0], outs[1:]


def _ffn_kernel(*refs, nseg, seg, tiles_per_seq):
    halo = tiles_per_seq > 1
    if halo:
        (x_ref, an_ref, gm_ref, xp_ref, xn_ref, anp_ref, ann_ref, gmp_ref, gmn_ref, mod_ref, wo_ref, n2g_ref,
         wup_ref, cw_ref, cb_ref, wd_ref, fg_ref, out_ref, lhs_sc, a_sc, g_sc, mix_sc) = refs
    else:
        (x_ref, an_ref, gm_ref, mod_ref, wo_ref, n2g_ref,
         wup_ref, cw_ref, cb_ref, wd_ref, fg_ref, out_ref, lhs_sc, a_sc, g_sc) = refs
    tm = nseg * seg
    stride = seg + HALO
    edge = HALO // 2
    gate1 = mod_ref[:, 2 * D_MODEL:3 * D_MODEL]
    shift2 = mod_ref[:, 3 * D_MODEL:4 * D_MODEL]
    scale2 = mod_ref[:, 4 * D_MODEL:5 * D_MODEL]

    def residual_norm(x, y, keep=None):
        x1 = x + gate1 * y
        h2 = _rms(x1, n2g_ref[...] * (1.0 + scale2)) + shift2
        if keep is not None:
            h2 = jnp.where(keep, h2, 0.0)
        return x1, h2.astype(BF16)

    if halo:
        def neighbours(prev_ref, next_ref):
            return jnp.concatenate([prev_ref[...].astype(F32)[edge:HALO, :],
                                    next_ref[...].astype(F32)[0:edge, :]], axis=0)

        mix_sc[0:tm, 0:ATTN_W] = an_ref[...]
        mix_sc[0:tm, ATTN_W:D_MODEL] = gm_ref[...]
        mix_sc[tm:tm + HALO, 0:ATTN_W] = neighbours(anp_ref, ann_ref).astype(BF16)
        mix_sc[tm:tm + HALO, ATTN_W:D_MODEL] = neighbours(gmp_ref, gmn_ref).astype(BF16)
        y = _dot(mix_sc[...], wo_ref[...])
        x1, h2 = residual_norm(x_ref[...], y[0:tm])
        out_ref[...] = x1
        lhs_sc[0:tm, :] = h2
        pos = pl.program_id(0) % tiles_per_seq
        is_prev = lax.broadcasted_iota(jnp.int32, (HALO, 1), 0) < edge
        has_prev = (pos != 0).astype(jnp.int32)
        has_next = (pos != tiles_per_seq - 1).astype(jnp.int32)
        keep = jnp.where(is_prev, has_prev, has_next) > 0
        lhs_sc[tm:tm + HALO, :] = residual_norm(neighbours(xp_ref, xn_ref), y[tm:tm + HALO], keep)[1]
    else:
        @pl.when(pl.program_id(0) == 0)
        def _():
            zeros = jnp.zeros((A_SLOTS, 2 * FF_CHUNK // LANES, HALO, LANES), F32)
            a_sc[:, :, 0:edge, :] = zeros[:, :, 0:edge, :]
            for s in range(1, nseg):
                a_sc[:, :, s * stride - edge:s * stride + edge, :] = zeros
            a_sc[:, :, nseg * stride - edge:nseg * stride, :] = zeros[:, :, 0:edge, :]

        y = _dot(an_ref[...], wo_ref[0:ATTN_W, :]) + _dot(gm_ref[...], wo_ref[ATTN_W:D_MODEL, :])
        x1, h2 = residual_norm(x_ref[...], y)
        out_ref[...] = x1
        lhs_sc[...] = h2

    for c in range(N_FF_CHUNKS):
        slot = c % A_SLOTS
        for k, col0 in enumerate((c * FF_CHUNK, D_FF + c * FF_CHUNK)):
            a = _dot(lhs_sc[...], wup_ref[:, col0:col0 + FF_CHUNK])
            for j in range(FF_CHUNK // LANES):
                lanes = slice(j * LANES, (j + 1) * LANES)
                for s in range(nseg):
                    row0 = edge + s * stride
                    a_sc[slot, 2 * k + j, row0:row0 + seg, :] = a[s * seg:(s + 1) * seg, lanes]
                if halo:
                    a_sc[slot, 2 * k + j, 0:edge, :] = a[tm:tm + edge, lanes]
                    a_sc[slot, 2 * k + j, edge + seg:HALO + seg, :] = a[tm + edge:tm + HALO, lanes]
        for j in range(FF_CHUNK // LANES):
            gl = slice(c * FF_CHUNK + j * LANES, c * FF_CHUNK + (j + 1) * LANES)
            vl = slice(D_FF + gl.start, D_FF + gl.stop)
            for s in range(nseg):
                base = edge + s * stride

                def conv(k, lanes):
                    acc = cb_ref[:, lanes]
                    for t in range(CONV_W):
                        tap = slice(t * 2 * D_FF + lanes.start, t * 2 * D_FF + lanes.stop)
                        acc = acc + a_sc[slot, k, base - 1 + t:base - 1 + t + seg, :] * cw_ref[:, tap]
                    return acc

                g = _silu(conv(j, gl)) * conv(2 + j, vl)
                g_sc[s * seg:(s + 1) * seg, gl] = g.astype(BF16)

    x2 = out_ref[...] + mod_ref[:, 5 * D_MODEL:6 * D_MODEL] * _dot(g_sc[...], wd_ref[...])
    out_ref[...] = _rms(x2, fg_ref[...])


def _ffn(x, an, gm, mods3, mod_row, p, ffn_w, seq, tm):
    n = x.shape[0]
    tiles_per_seq = max(seq // tm, 1)
    seg = min(seq, tm)
    nseg = tm // seg
    rows = tm + (HALO if tiles_per_seq > 1 else 0)
    a_rows = nseg * (seg + HALO)
    const = lambda i: (0, 0)
    tile = lambda i: (i, 0)
    resident = dict(pipeline_mode=pl.Buffered(1))
    in_specs = [pl.BlockSpec((tm, D_MODEL), tile), pl.BlockSpec((tm, ATTN_W), tile),
                pl.BlockSpec((tm, GMLP_W), tile)]
    args = [x, an, gm]
    scratch = [pltpu.VMEM((rows, D_MODEL), BF16),
               pltpu.VMEM((A_SLOTS, 2 * FF_CHUNK // LANES, a_rows, LANES), F32),
               pltpu.VMEM((tm, D_FF), BF16)]
    if tiles_per_seq > 1:
        per = tm // HALO
        last = n // HALO - 1
        prev = lambda i: (jnp.maximum(i * per - 1, 0), 0)
        nxt = lambda i: (jnp.minimum((i + 1) * per, last), 0)
        for arr, width in ((x, D_MODEL), (an, ATTN_W), (gm, GMLP_W)):
            in_specs += [pl.BlockSpec((HALO, width), prev), pl.BlockSpec((HALO, width), nxt)]
            args += [arr, arr]
        scratch.append(pltpu.VMEM((rows, D_MODEL), BF16))
    in_specs += [
        pl.BlockSpec((None, 1, N_MODS * D_MODEL), lambda i: (mod_row(i), 0, 0)),
        pl.BlockSpec((D_MODEL, D_MODEL), const, **resident),
        pl.BlockSpec((1, D_MODEL), const),
        pl.BlockSpec((D_MODEL, 2 * D_FF), const, **resident),
        pl.BlockSpec((1, CONV_W * 2 * D_FF), const),
        pl.BlockSpec((1, 2 * D_FF), const),
        pl.BlockSpec((D_FF, D_MODEL), const, **resident),
        pl.BlockSpec((1, D_MODEL), const),
    ]
    w_o, w_up, w_down = ffn_w
    args += [mods3, w_o, p["norm2_g"], w_up, p["conv_w"], p["conv_b"], w_down, p["final_g"]]
    return pl.pallas_call(
        functools.partial(_ffn_kernel, nseg=nseg, seg=seg, tiles_per_seq=tiles_per_seq),
        grid=(n // tm,),
        in_specs=in_specs,
        out_specs=pl.BlockSpec((tm, D_MODEL), tile),
        out_shape=jax.ShapeDtypeStruct((n, D_MODEL), F32),
        scratch_shapes=scratch,
        compiler_params=_params(1),
        name="ffn_halo" if tiles_per_seq > 1 else "ffn",
    )(*args)


def _rope_tables(length):
    pos = np.arange(length)
    row = (pos // GRID_W).astype(np.float32)
    col = (pos % GRID_W).astype(np.float32)
    n_freq = QK_ROPE // 4
    inv = (np.float32(ROPE_THETA) ** (-(np.arange(n_freq, dtype=np.float32) / np.float32(n_freq)))).astype(np.float32)
    ang_r, ang_c = row[:, None] * inv, col[:, None] * inv
    cos32 = np.concatenate([np.cos(ang_r)] * 2 + [np.cos(ang_c)] * 2, axis=-1)
    sin32 = np.concatenate([-np.sin(ang_r), np.sin(ang_r), -np.sin(ang_c), np.sin(ang_c)], axis=-1)
    tail = LANES - ROPE_LANE0 - QK_ROPE
    cos_q = np.concatenate([np.ones((length, ROPE_LANE0)), cos32, np.zeros((length, tail))], -1)
    sin_q = np.concatenate([np.zeros((length, ROPE_LANE0)), sin32, np.zeros((length, tail))], -1)
    return jnp.asarray(cos_q, F32), jnp.asarray(sin_q, F32)


def _layer_params(l, norm1_g, q_norm_g, kv_norm_g, v_norm_g, b_s, out_g_attn,
                  out_g_gmlp, norm2_g, conv_w, conv_b, final_g):
    row = lambda a: a.reshape(1, -1)
    return {
        "norm1_g": row(norm1_g[l]), "q_norm_g": row(q_norm_g[l]),
        "kv_norm_g": row(kv_norm_g[l]), "v_norm_g": row(v_norm_g[l]), "b_s": b_s[l],
        "out_g_attn": row(out_g_attn[l]), "out_g_gmlp": row(out_g_gmlp[l]), "norm2_g": row(norm2_g[l]),
        "conv_w": conv_w[l].reshape(1, CONV_W * 2 * D_FF), "conv_b": row(conv_b[l]), "final_g": row(final_g),
    }


def kernel(x_prompt, x_sample, cache_ckv, cache_krope, c, c_ctx, ada_w, ada_b, norm1_g, w_in, q_norm_g, w_uq, kv_norm_g, w_ukv, v_norm_g, w_s, b_s, out_g_attn, out_g_gmlp, w_o, norm2_g, w_up, conv_w, conv_b, w_down, final_g):
    batch, seq, _ = x_prompt.shape
    dec_batch, dec_seq, _ = x_sample.shape
    depth, past = cache_ckv.shape[1], cache_ckv.shape[2]
    assert depth == 1 and dec_batch + 1 <= MOD_ROWS
    l = 0
    p = _layer_params(l, norm1_g, q_norm_g, kv_norm_g, v_norm_g, b_s, out_g_attn,
                      out_g_gmlp, norm2_g, conv_w, conv_b, final_g)

    mods3, p["w_qkv"], p["w_uv"], p["w_uq"], p["w_k"], p["w_ve"], p["w_vo"], w_s_b = _adaln(
        c_ctx.reshape(1, D_MODEL), c, ada_w[l], ada_b[l].reshape(1, -1), w_in[l].T, w_uq[l], w_ukv[l],
        w_s[l].reshape(GMLP_GROUPS * CHUNK, CHUNK))
    p["w_s"] = w_s_b.reshape(GMLP_GROUPS, CHUNK, CHUNK)
    tm_pre, tm_ffn = 1024, 512
    ctx_row = lambda i: 0
    lat_row = lambda tm: lambda i: 1 + i // (dec_seq // tm)

    xp = x_prompt.reshape(batch * seq, D_MODEL)
    xs = x_sample.reshape(dec_batch * dec_seq, D_MODEL)
    (qp, kp, vep, vop, gmp, ckv_p, krt_p), (w_down_b,) = _pre(xp, mods3, ctx_row, p, seq, tm_pre, [w_down[l]])
    cache_in = (*_rope_tables(dec_seq), cache_ckv[:, l].reshape(dec_batch * past, KV_RANK),
                jnp.swapaxes(cache_krope[:, l], 1, 2))
    (qs, ks, ves, vos, gms), (w_o_b,) = _pre(xs, mods3, lat_row(tm_pre), p, dec_seq, tm_pre, [w_o[l]],
                                             latent=cache_in)

    ans, (w_up_b,) = _attention(qs, ks, ves, vos, p["out_g_attn"], dec_seq, past + dec_seq, 1024, 1, [w_up[l]])
    anp, _ = _attention(qp, kp, vep, vop, p["out_g_attn"], seq, seq, seq, 4, [])
    ffn_w = (w_o_b, w_up_b, w_down_b)
    yp = _ffn(xp, anp, gmp, mods3, ctx_row, p, ffn_w, seq, tm_ffn)
    ys = _ffn(xs, ans, gms, mods3, lat_row(tm_ffn), p, ffn_w, dec_seq, tm_ffn)

    return (yp.reshape(batch, seq, D_MODEL), ys.reshape(dec_batch, dec_seq, D_MODEL),
            ckv_p.reshape(batch, 1, seq, KV_RANK), jnp.swapaxes(krt_p, 1, 2).reshape(batch, 1, seq, QK_ROPE))
```

```python
import functools

import numpy as np
import jax
import jax.numpy as jnp
from jax import lax
from jax.experimental import pallas as pl
from jax.experimental.pallas import tpu as pltpu

D_MODEL = 1024
GRID_W = 64
CHUNK = 128
N_HEADS = 8
QK_NOPE = 64
QK_ROPE = 32
V_DIM = 64
Q_RANK = 384
KV_RANK = 256
ATTN_W = N_HEADS * V_DIM
GMLP_W = D_MODEL - ATTN_W
GMLP_GROUPS = 4
GMLP_DG = GMLP_W // GMLP_GROUPS
D_FF = 2816
CONV_W = 3
ROPE_THETA = 10000.0
EPS = 1e-6
SM_SCALE = (QK_NOPE + QK_ROPE) ** -0.5
LOG2_E = 1.4426950408889634

LANES = 128
HEAD_PAD = LANES
ROPE_LANE0 = QK_NOPE
ROT_HALF = QK_ROPE // 4
QK_W = N_HEADS * HEAD_PAD
N_MODS = 6
MOD_ROWS = 8
ADALN_STEPS = 8
IN_QKV = Q_RANK + KV_RANK
FF_CHUNK = 256
HALO = 16
N_FF_CHUNKS = D_FF // FF_CHUNK
A_SLOTS = 4
V7X_VMEM_BYTES = 64 * 1024 * 1024
VMEM_LIMIT = V7X_VMEM_BYTES - 8 * 1024 * 1024

F32 = jnp.float32
BF16 = jnp.bfloat16


def _dot(a, b):
    return jnp.dot(a, b, preferred_element_type=F32)


def _dot_nt(a, b):
    return lax.dot_general(a, b, (((1,), (1,)), ((), ())), preferred_element_type=F32)


def _rms(x, g):
    return x * lax.rsqrt(jnp.mean(x * x, axis=-1, keepdims=True) + EPS) * g


def _gelu(x):
    return 0.5 * x * (1.0 + lax.erf(x * (0.5 ** 0.5)))


def _silu(x):
    return x * (1.0 / (1.0 + jnp.exp2(x * -LOG2_E)))


def _params(n_axes):
    return pltpu.CompilerParams(dimension_semantics=("arbitrary",) * n_axes,
                                vmem_limit_bytes=VMEM_LIMIT)


def _adaln_kernel(cctx_ref, c_ref, w_ref, b_ref, win_ref, wuq_ref, wukv_ref, ws_ref,
                  out_ref, wqkv_out, wuv_out, wuq_out, wk_out, wve_out, wvo_out, ws_out):
    pad = jnp.zeros((MOD_ROWS - 1 - c_ref.shape[0], D_MODEL), F32)
    cond = jnp.concatenate([cctx_ref[...], c_ref[...], pad], axis=0)
    s = _silu(cond).astype(BF16)
    out_ref[:, 0, :] = _dot(s, w_ref[...].astype(BF16)) + b_ref[...]
    @pl.when(pl.program_id(0) == 0)
    def _():
        kr = win_ref[IN_QKV:IN_QKV + QK_ROPE, :].astype(BF16)
        z = jnp.zeros_like(kr)
        wqkv_out[0:IN_QKV, :] = win_ref[0:IN_QKV, :].astype(BF16)
        for t, blk in enumerate((kr, z, kr, z)):
            wqkv_out[IN_QKV + t * QK_ROPE:IN_QKV + (t + 1) * QK_ROPE, :] = blk
        wuv_out[...] = win_ref[IN_QKV + QK_ROPE:, :].astype(BF16)

    w = wuq_ref[...]
    hw = QK_NOPE + QK_ROPE
    zq = jnp.zeros((w.shape[0], HEAD_PAD - hw), F32)
    wuq_out[...] = jnp.concatenate([blk for hd in range(N_HEADS) for blk in (w[:, hd * hw:(hd + 1) * hw], zq)],
                                   axis=1).astype(BF16)
    w = wukv_ref[...]
    zv = jnp.zeros((w.shape[0], V_DIM), F32)
    v_of = lambda hd: w[:, hd * HEAD_PAD + QK_NOPE:(hd + 1) * HEAD_PAD]
    wk_out[...] = jnp.concatenate([blk for hd in range(N_HEADS)
                                   for blk in (w[:, hd * HEAD_PAD:hd * HEAD_PAD + QK_NOPE], zv)], axis=1).astype(BF16)
    wve_out[...] = jnp.concatenate([blk for hd in range(0, N_HEADS, 2) for blk in (v_of(hd), zv)],
                                   axis=1).astype(BF16)
    wvo_out[...] = jnp.concatenate([blk for hd in range(1, N_HEADS, 2) for blk in (zv, v_of(hd))],
                                   axis=1).astype(BF16)
    ws_out[...] = ws_ref[...].astype(BF16)


def _adaln(c_ctx, c, ada_w, ada_b, w_in_t, w_uq, w_ukv, w_s):
    cols = N_MODS * D_MODEL // ADALN_STEPS
    row_block = lambda a, width=None: pl.BlockSpec((a.shape[0] // ADALN_STEPS, width or a.shape[1]),
                                                   lambda j: (j, 0))
    whole = lambda shape: pl.BlockSpec(shape, lambda j: (0, 0), pipeline_mode=pl.Buffered(1))
    return pl.pallas_call(
        _adaln_kernel,
        grid=(ADALN_STEPS,),
        in_specs=[
            pl.BlockSpec(c_ctx.shape, lambda j: (0, 0)),
            pl.BlockSpec(c.shape, lambda j: (0, 0)),
            pl.BlockSpec((D_MODEL, cols), lambda j: (0, j)),
            pl.BlockSpec((1, cols), lambda j: (0, j)),
            whole(w_in_t.shape), row_block(w_uq), row_block(w_ukv), row_block(w_s),
        ],
        out_specs=[pl.BlockSpec((MOD_ROWS, 1, cols), lambda j: (0, 0, j)),
                   whole((IN_QKV + LANES, D_MODEL)), whole((2 * GMLP_W, D_MODEL)),
                   row_block(w_uq, QK_W), row_block(w_ukv, QK_W), row_block(w_ukv, ATTN_W),
                   row_block(w_ukv, ATTN_W), row_block(w_s)],
        out_shape=[jax.ShapeDtypeStruct((MOD_ROWS, 1, N_MODS * D_MODEL), F32),
                   jax.ShapeDtypeStruct((IN_QKV + LANES, D_MODEL), BF16),
                   jax.ShapeDtypeStruct((2 * GMLP_W, D_MODEL), BF16),
                   jax.ShapeDtypeStruct((Q_RANK, QK_W), BF16),
                   jax.ShapeDtypeStruct((KV_RANK, QK_W), BF16),
                   jax.ShapeDtypeStruct((KV_RANK, ATTN_W), BF16),
                   jax.ShapeDtypeStruct((KV_RANK, ATTN_W), BF16),
                   jax.ShapeDtypeStruct(w_s.shape, BF16)],
        compiler_params=_params(1),
        name="adaln",
    )(c_ctx, c, ada_w, ada_b, w_in_t, w_uq, w_ukv, w_s)


def _pre_kernel(*refs, latent, tm, seq, n_casts):
    (x_ref, mod_ref, n1g_ref, wqkv_ref, wuv_ref, qng_ref, wuq_ref, kvg_ref, wk_ref, wve_ref, wvo_ref,
     vng_ref, ws_ref, bs_ref, ogg_ref) = refs[:15]
    n_in = 15 + (4 if latent else 0)
    n_out = 5 + (0 if latent else 2)
    cast_in = refs[n_in:n_in + n_casts]
    outs = refs[n_in + n_casts:n_in + n_casts + n_out]
    cast_out = refs[n_in + n_casts + n_out:n_in + 2 * n_casts + n_out]
    go_sc = refs[-1]
    q_out, k_out, ve_out, vo_out, gm_out = outs[:5]
    for src, dst in zip(cast_in, cast_out):
        dst[...] = src[...].astype(BF16)
    shift1 = mod_ref[:, 0:D_MODEL]
    scale1 = mod_ref[:, D_MODEL:2 * D_MODEL]
    h = (_rms(x_ref[...], n1g_ref[...] * (1.0 + scale1)) + shift1).astype(BF16)

    lane = lax.broadcasted_iota(jnp.int32, (1, LANES), 1)
    rope_lanes = lane >= ROPE_LANE0
    first_half = (lane % (2 * ROT_HALF)) < ROT_HALF

    def expand(ckv_b, kr_k, rows):
        kn = _dot(ckv_b, wk_ref[...])
        for hd in range(N_HEADS):
            sl = slice(hd * HEAD_PAD, (hd + 1) * HEAD_PAD)
            k_out[rows, sl] = (kn[:, sl] + kr_k).astype(BF16)
        ve_out[rows, :] = _dot(ckv_b, wve_ref[...]).astype(BF16)
        vo_out[rows, :] = _dot(ckv_b, wvo_ref[...]).astype(BF16)

    qkv = _dot_nt(h, wqkv_ref[...])
    qn = _rms(qkv[:, 0:Q_RANK], qng_ref[...]).astype(BF16)
    ckv = _rms(qkv[:, Q_RANK:IN_QKV], kvg_ref[...])
    q = _dot(qn, wuq_ref[...]) * (SM_SCALE * LOG2_E)
    kr = qkv[:, IN_QKV:IN_QKV + LANES]
    if latent:
        cos_ref, sin_ref, cckv_ref, ckrt_ref = refs[15:19]
        cos_q = cos_ref[...]

        def rotate(v, cos):
            swapped = jnp.where(first_half, pltpu.roll(v, LANES - ROT_HALF, 1), pltpu.roll(v, ROT_HALF, 1))
            return v * cos + swapped * sin_ref[...]

        kr_k = rotate(kr, jnp.where(rope_lanes, cos_q, 0.0))
        ckrt = ckrt_ref[...]
        past = ckrt.shape[1]
        padded = jnp.concatenate([ckrt, jnp.zeros((LANES - QK_ROPE, past), F32)], axis=0)
        expand(cckv_ref[...].astype(BF16), pltpu.roll(padded.T, ROPE_LANE0, 1), slice(0, past))
        own_rows = slice(past, past + tm)
    else:
        ckv_out, krt_out = outs[5:7]
        ckv_out[...] = ckv
        krt = kr.T
        for s in range(tm // seq):
            krt_out[s] = krt[0:QK_ROPE, s * seq:(s + 1) * seq]
        kr_k = jnp.where(rope_lanes, kr, 0.0)
        own_rows = slice(0, tm)
    expand(ckv.astype(BF16), kr_k, own_rows)
    for hd in range(N_HEADS):
        sl = slice(hd * HEAD_PAD, (hd + 1) * HEAD_PAD)
        qh = q[:, sl]
        if latent:
            qh = rotate(qh, cos_q)
        q_out[:, sl] = qh.astype(BF16)

    uv = _dot_nt(h, wuv_ref[...])
    gu = _gelu(uv[:, 0:GMLP_W])
    gv = _gelu(uv[:, GMLP_W:2 * GMLP_W])
    eye = (lax.broadcasted_iota(jnp.int32, (CHUNK, CHUNK), 0)
           == lax.broadcasted_iota(jnp.int32, (CHUNK, CHUNK), 1))
    for g in range(GMLP_GROUPS):
        sl = slice(g * GMLP_DG, (g + 1) * GMLP_DG)
        vg = gv[:, sl]
        vg = (vg * lax.rsqrt(jnp.mean(vg * vg, axis=-1, keepdims=True) + EPS) * vng_ref[:, sl]).astype(BF16)
        bias = jnp.sum(jnp.where(eye, bs_ref[g:g + 1, :], 0.0), axis=1, keepdims=True)
        chunks = [vg[n * CHUNK:(n + 1) * CHUNK, :] for n in range(tm // CHUNK)]
        s_all = _dot(ws_ref[g], jnp.concatenate(chunks, axis=1))
        for n in range(tm // CHUNK):
            rows = slice(n * CHUNK, (n + 1) * CHUNK)
            go_sc[rows, sl] = gu[rows, sl] * (s_all[:, n * GMLP_DG:(n + 1) * GMLP_DG] + bias)
    gm_out[...] = _rms(go_sc[...], ogg_ref[...]).astype(BF16)


def _pre(x, mods3, mod_row, p, seq, tm, casts, latent=None):
    n = x.shape[0]
    const = lambda i: (0, 0)
    tile = lambda i: (i, 0)
    in_specs = [
        pl.BlockSpec((tm, D_MODEL), tile),
        pl.BlockSpec((None, 1, N_MODS * D_MODEL), lambda i: (mod_row(i), 0, 0)),
        pl.BlockSpec((1, D_MODEL), const),
        pl.BlockSpec((IN_QKV + LANES, D_MODEL), const),
        pl.BlockSpec((2 * GMLP_W, D_MODEL), const),
        pl.BlockSpec((1, Q_RANK), const),
        pl.BlockSpec((Q_RANK, QK_W), const),
        pl.BlockSpec((1, KV_RANK), const),
        pl.BlockSpec((KV_RANK, QK_W), const),
        pl.BlockSpec((KV_RANK, ATTN_W), const),
        pl.BlockSpec((KV_RANK, ATTN_W), const),
        pl.BlockSpec((1, GMLP_W), const),
        pl.BlockSpec((GMLP_GROUPS, CHUNK, CHUNK), lambda i: (0, 0, 0)),
        pl.BlockSpec((GMLP_GROUPS, CHUNK), const),
        pl.BlockSpec((1, GMLP_W), const),
    ]
    args = [x, mods3, p["norm1_g"], p["w_qkv"], p["w_uv"], p["q_norm_g"], p["w_uq"],
            p["kv_norm_g"], p["w_k"], p["w_ve"], p["w_vo"], p["v_norm_g"], p["w_s"], p["b_s"], p["out_g_gmlp"]]
    past = 0 if latent is None else latent[3].shape[2]
    kv_rows = tm + past
    out_rows = (tm, kv_rows, kv_rows, kv_rows, tm)
    widths = (QK_W, QK_W, ATTN_W, ATTN_W, GMLP_W)
    out_specs = [pl.BlockSpec((r, w), tile) for r, w in zip(out_rows, widths)]
    out_shape = [jax.ShapeDtypeStruct((n // tm * r, w), BF16) for r, w in zip(out_rows, widths)]
    if latent is not None:
        cos_q, sin_q, cache_ckv, cache_krt = latent
        assert tm == seq == cos_q.shape[0]
        in_specs += [pl.BlockSpec((tm, LANES), const), pl.BlockSpec((tm, LANES), const),
                     pl.BlockSpec((past, KV_RANK), tile),
                     pl.BlockSpec((None, QK_ROPE, past), lambda i: (i, 0, 0))]
        args += [cos_q, sin_q, cache_ckv, cache_krt]
    else:
        out_specs += [pl.BlockSpec((tm, KV_RANK), tile),
                      pl.BlockSpec((tm // seq, QK_ROPE, seq), lambda i: (i, 0, 0))]
        out_shape += [jax.ShapeDtypeStruct((n, KV_RANK), F32),
                      jax.ShapeDtypeStruct((n // seq, QK_ROPE, seq), F32)]
    n_out = len(out_specs)
    cast_specs = [pl.BlockSpec((w.shape[0] // (n // tm), w.shape[1]), tile) for w in casts]
    outs = pl.pallas_call(
        functools.partial(_pre_kernel, latent=latent is not None, tm=tm, seq=seq, n_casts=len(casts)),
        grid=(n // tm,),
        in_specs=in_specs + cast_specs,
        out_specs=out_specs + cast_specs,
        out_shape=out_shape + [jax.ShapeDtypeStruct(w.shape, BF16) for w in casts],
        scratch_shapes=[pltpu.VMEM((tm, GMLP_W), F32)],
        compiler_params=_params(1),
        name="pre_latent" if latent is not None else "pre",
    )(*args, *casts)
    return outs[:n_out], outs[n_out:]


def _attn_kernel(*refs, group, kv_len, tq, n_casts):
    q_ref, k_ref, ve_ref, vo_ref, oga_ref = refs[:5]
    cast_in = refs[5:5 + n_casts]
    out_ref = refs[5 + n_casts]
    cast_out = refs[6 + n_casts:6 + 2 * n_casts]
    for src, dst in zip(cast_in, cast_out):
        dst[...] = src[...].astype(BF16)
    ones = jnp.ones((kv_len, LANES), BF16)
    for g in range(group):
        qrows = slice(g * tq, (g + 1) * tq)
        krows = slice(g * kv_len, (g + 1) * kv_len)
        pairs = []
        for pp in range(N_HEADS // 2):
            psl = slice(pp * LANES, (pp + 1) * LANES)
            acc = None
            for par, v_ref in enumerate((ve_ref, vo_ref)):
                hd = 2 * pp + par
                hsl = slice(hd * HEAD_PAD, (hd + 1) * HEAD_PAD)
                s = _dot_nt(q_ref[qrows, hsl], k_ref[krows, hsl])
                e = jnp.exp2(s - jnp.max(s, axis=-1, keepdims=True)).astype(BF16)
                o = _dot(e, jnp.concatenate([v_ref[krows, psl], ones], axis=1))
                o = o[:, 0:LANES] * (1.0 / o[:, LANES:2 * LANES])
                acc = o if acc is None else acc + o
            pairs.append(acc)
        ssq = sum(jnp.sum(b * b, axis=-1, keepdims=True) for b in pairs)
        r = lax.rsqrt(ssq * (1.0 / ATTN_W) + EPS)
        for j, b in enumerate(pairs):
            sl = slice(j * LANES, (j + 1) * LANES)
            out_ref[qrows, sl] = (b * r * oga_ref[:, sl]).astype(BF16)


def _attention(q, k, ve, vo, oga, seq, kv_len, tq, group, casts):
    n = q.shape[0]
    nq = seq // tq
    grid = (n // (group * seq), nq)
    linear = lambda b, i: (b * nq + i, 0)
    cast_specs = [pl.BlockSpec((w.shape[0] // (grid[0] * nq), w.shape[1]), linear) for w in casts]
    kv = lambda b, i: (b, 0)
    in_specs = [
        pl.BlockSpec((group * tq, QK_W), linear),
        pl.BlockSpec((group * kv_len, QK_W), kv),
        pl.BlockSpec((group * kv_len, ATTN_W), kv),
        pl.BlockSpec((group * kv_len, ATTN_W), kv),
        pl.BlockSpec((1, ATTN_W), lambda b, i: (0, 0)),
    ]
    outs = pl.pallas_call(
        functools.partial(_attn_kernel, group=group, kv_len=kv_len, tq=tq, n_casts=len(casts)),
        grid=grid,
        in_specs=in_specs + cast_specs,
        out_specs=[pl.BlockSpec((group * tq, ATTN_W), linear)] + cast_specs,
        out_shape=[jax.ShapeDtypeStruct((n, ATTN_W), BF16)]
        + [jax.ShapeDtypeStruct(w.shape, BF16) for w in casts],
        compiler_params=_params(2),
        name="attn",
    )(q, k, ve, vo, oga, *casts)
    return outs[0], outs[1:]


def _ffn_kernel(*refs, nseg, seg, tiles_per_seq):
    halo = tiles_per_seq > 1
    if halo:
        (x_ref, an_ref, gm_ref, xp_ref, xn_ref, anp_ref, ann_ref, gmp_ref, gmn_ref, mod_ref, wo_ref, n2g_ref,
         wup_ref, cw_ref, cb_ref, wd_ref, fg_ref, out_ref, lhs_sc, a_sc, g_sc, mix_sc) = refs
    else:
        (x_ref, an_ref, gm_ref, mod_ref, wo_ref, n2g_ref,
         wup_ref, cw_ref, cb_ref, wd_ref, fg_ref, out_ref, lhs_sc, a_sc, g_sc) = refs
    tm = nseg * seg
    stride = seg + HALO
    edge = HALO // 2
    gate1 = mod_ref[:, 2 * D_MODEL:3 * D_MODEL]
    shift2 = mod_ref[:, 3 * D_MODEL:4 * D_MODEL]
    scale2 = mod_ref[:, 4 * D_MODEL:5 * D_MODEL]

    def residual_norm(x, y, keep=None):
        x1 = x + gate1 * y
        h2 = _rms(x1, n2g_ref[...] * (1.0 + scale2)) + shift2
        if keep is not None:
            h2 = jnp.where(keep, h2, 0.0)
        return x1, h2.astype(BF16)

    if halo:
        def neighbours(prev_ref, next_ref):
            return jnp.concatenate([prev_ref[...].astype(F32)[edge:HALO, :],
                                    next_ref[...].astype(F32)[0:edge, :]], axis=0)

        mix_sc[0:tm, 0:ATTN_W] = an_ref[...]
        mix_sc[0:tm, ATTN_W:D_MODEL] = gm_ref[...]
        mix_sc[tm:tm + HALO, 0:ATTN_W] = neighbours(anp_ref, ann_ref).astype(BF16)
        mix_sc[tm:tm + HALO, ATTN_W:D_MODEL] = neighbours(gmp_ref, gmn_ref).astype(BF16)
        y = _dot(mix_sc[...], wo_ref[...])
        x1, h2 = residual_norm(x_ref[...], y[0:tm])
        out_ref[...] = x1
        lhs_sc[0:tm, :] = h2
        pos = pl.program_id(0) % tiles_per_seq
        is_prev = lax.broadcasted_iota(jnp.int32, (HALO, 1), 0) < edge
        has_prev = (pos != 0).astype(jnp.int32)
        has_next = (pos != tiles_per_seq - 1).astype(jnp.int32)
        keep = jnp.where(is_prev, has_prev, has_next) > 0
        lhs_sc[tm:tm + HALO, :] = residual_norm(neighbours(xp_ref, xn_ref), y[tm:tm + HALO], keep)[1]
    else:
        @pl.when(pl.program_id(0) == 0)
        def _():
            zeros = jnp.zeros((A_SLOTS, 2 * FF_CHUNK // LANES, HALO, LANES), F32)
            a_sc[:, :, 0:edge, :] = zeros[:, :, 0:edge, :]
            for s in range(1, nseg):
                a_sc[:, :, s * stride - edge:s * stride + edge, :] = zeros
            a_sc[:, :, nseg * stride - edge:nseg * stride, :] = zeros[:, :, 0:edge, :]

        y = _dot(an_ref[...], wo_ref[0:ATTN_W, :]) + _dot(gm_ref[...], wo_ref[ATTN_W:D_MODEL, :])
        x1, h2 = residual_norm(x_ref[...], y)
        out_ref[...] = x1
        lhs_sc[...] = h2

    for c in range(N_FF_CHUNKS):
        slot = c % A_SLOTS
        for k, col0 in enumerate((c * FF_CHUNK, D_FF + c * FF_CHUNK)):
            a = _dot(lhs_sc[...], wup_ref[:, col0:col0 + FF_CHUNK])
            for j in range(FF_CHUNK // LANES):
                lanes = slice(j * LANES, (j + 1) * LANES)
                for s in range(nseg):
                    row0 = edge + s * stride
                    a_sc[slot, 2 * k + j, row0:row0 + seg, :] = a[s * seg:(s + 1) * seg, lanes]
                if halo:
                    a_sc[slot, 2 * k + j, 0:edge, :] = a[tm:tm + edge, lanes]
                    a_sc[slot, 2 * k + j, edge + seg:HALO + seg, :] = a[tm + edge:tm + HALO, lanes]
        for j in range(FF_CHUNK // LANES):
            gl = slice(c * FF_CHUNK + j * LANES, c * FF_CHUNK + (j + 1) * LANES)
            vl = slice(D_FF + gl.start, D_FF + gl.stop)
            for s in range(nseg):
                base = edge + s * stride

                def conv(k, lanes):
                    acc = cb_ref[:, lanes]
                    for t in range(CONV_W):
                        tap = slice(t * 2 * D_FF + lanes.start, t * 2 * D_FF + lanes.stop)
                        acc = acc + a_sc[slot, k, base - 1 + t:base - 1 + t + seg, :] * cw_ref[:, tap]
                    return acc

                g = _silu(conv(j, gl)) * conv(2 + j, vl)
                g_sc[s * seg:(s + 1) * seg, gl] = g.astype(BF16)

    x2 = out_ref[...] + mod_ref[:, 5 * D_MODEL:6 * D_MODEL] * _dot(g_sc[...], wd_ref[...])
    out_ref[...] = _rms(x2, fg_ref[...])


def _ffn(x, an, gm, mods3, mod_row, p, ffn_w, seq, tm):
    n = x.shape[0]
    tiles_per_seq = max(seq // tm, 1)
    seg = min(seq, tm)
    nseg = tm // seg
    rows = tm + (HALO if tiles_per_seq > 1 else 0)
    a_rows = nseg * (seg + HALO)
    const = lambda i: (0, 0)
    tile = lambda i: (i, 0)
    resident = dict(pipeline_mode=pl.Buffered(1))
    in_specs = [pl.BlockSpec((tm, D_MODEL), tile), pl.BlockSpec((tm, ATTN_W), tile),
                pl.BlockSpec((tm, GMLP_W), tile)]
    args = [x, an, gm]
    scratch = [pltpu.VMEM((rows, D_MODEL), BF16),
               pltpu.VMEM((A_SLOTS, 2 * FF_CHUNK // LANES, a_rows, LANES), F32),
               pltpu.VMEM((tm, D_FF), BF16)]
    if tiles_per_seq > 1:
        per = tm // HALO
        last = n // HALO - 1
        prev = lambda i: (jnp.maximum(i * per - 1, 0), 0)
        nxt = lambda i: (jnp.minimum((i + 1) * per, last), 0)
        for arr, width in ((x, D_MODEL), (an, ATTN_W), (gm, GMLP_W)):
            in_specs += [pl.BlockSpec((HALO, width), prev), pl.BlockSpec((HALO, width), nxt)]
            args += [arr, arr]
        scratch.append(pltpu.VMEM((rows, D_MODEL), BF16))
    in_specs += [
        pl.BlockSpec((None, 1, N_MODS * D_MODEL), lambda i: (mod_row(i), 0, 0)),
        pl.BlockSpec((D_MODEL, D_MODEL), const, **resident),
        pl.BlockSpec((1, D_MODEL), const),
        pl.BlockSpec((D_MODEL, 2 * D_FF), const, **resident),
        pl.BlockSpec((1, CONV_W * 2 * D_FF), const),
        pl.BlockSpec((1, 2 * D_FF), const),
        pl.BlockSpec((D_FF, D_MODEL), const, **resident),
        pl.BlockSpec((1, D_MODEL), const),
    ]
    w_o, w_up, w_down = ffn_w
    args += [mods3, w_o, p["norm2_g"], w_up, p["conv_w"], p["conv_b"], w_down, p["final_g"]]
    return pl.pallas_call(
        functools.partial(_ffn_kernel, nseg=nseg, seg=seg, tiles_per_seq=tiles_per_seq),
        grid=(n // tm,),
        in_specs=in_specs,
        out_specs=pl.BlockSpec((tm, D_MODEL), tile),
        out_shape=jax.ShapeDtypeStruct((n, D_MODEL), F32),
        scratch_shapes=scratch,
        compiler_params=_params(1),
        name="ffn_halo" if tiles_per_seq > 1 else "ffn",
    )(*args)


def _rope_tables(length):
    pos = np.arange(length)
    row = (pos // GRID_W).astype(np.float32)
    col = (pos % GRID_W).astype(np.float32)
    n_freq = QK_ROPE // 4
    inv = (np.float32(ROPE_THETA) ** (-(np.arange(n_freq, dtype=np.float32) / np.float32(n_freq)))).astype(np.float32)
    ang_r, ang_c = row[:, None] * inv, col[:, None] * inv
    cos32 = np.concatenate([np.cos(ang_r)] * 2 + [np.cos(ang_c)] * 2, axis=-1)
    sin32 = np.concatenate([-np.sin(ang_r), np.sin(ang_r), -np.sin(ang_c), np.sin(ang_c)], axis=-1)
    tail = LANES - ROPE_LANE0 - QK_ROPE
    cos_q = np.concatenate([np.ones((length, ROPE_LANE0)), cos32, np.zeros((length, tail))], -1)
    sin_q = np.concatenate([np.zeros((length, ROPE_LANE0)), sin32, np.zeros((length, tail))], -1)
    return jnp.asarray(cos_q, F32), jnp.asarray(sin_q, F32)


def _layer_params(l, norm1_g, q_norm_g, kv_norm_g, v_norm_g, b_s, out_g_attn,
                  out_g_gmlp, norm2_g, conv_w, conv_b, final_g):
    row = lambda a: a.reshape(1, -1)
    return {
        "norm1_g": row(norm1_g[l]), "q_norm_g": row(q_norm_g[l]),
        "kv_norm_g": row(kv_norm_g[l]), "v_norm_g": row(v_norm_g[l]), "b_s": b_s[l],
        "out_g_attn": row(out_g_attn[l]), "out_g_gmlp": row(out_g_gmlp[l]), "norm2_g": row(norm2_g[l]),
        "conv_w": conv_w[l].reshape(1, CONV_W * 2 * D_FF), "conv_b": row(conv_b[l]), "final_g": row(final_g),
    }


def kernel(x_prompt, x_sample, cache_ckv, cache_krope, c, c_ctx, ada_w, ada_b, norm1_g, w_in, q_norm_g, w_uq, kv_norm_g, w_ukv, v_norm_g, w_s, b_s, out_g_attn, out_g_gmlp, w_o, norm2_g, w_up, conv_w, conv_b, w_down, final_g):
    batch, seq, _ = x_prompt.shape
    dec_batch, dec_seq, _ = x_sample.shape
    depth, past = cache_ckv.shape[1], cache_ckv.shape[2]
    assert depth == 1 and dec_batch + 1 <= MOD_ROWS
    l = 0
    p = _layer_params(l, norm1_g, q_norm_g, kv_norm_g, v_norm_g, b_s, out_g_attn,
                      out_g_gmlp, norm2_g, conv_w, conv_b, final_g)

    mods3, p["w_qkv"], p["w_uv"], p["w_uq"], p["w_k"], p["w_ve"], p["w_vo"], w_s_b = _adaln(
        c_ctx.reshape(1, D_MODEL), c, ada_w[l], ada_b[l].reshape(1, -1), w_in[l].T, w_uq[l], w_ukv[l],
        w_s[l].reshape(GMLP_GROUPS * CHUNK, CHUNK))
    p["w_s"] = w_s_b.reshape(GMLP_GROUPS, CHUNK, CHUNK)
    tm_pre, tm_ffn = 1024, 512
    ctx_row = lambda i: 0
    lat_row = lambda tm: lambda i: 1 + i // (dec_seq // tm)

    xp = x_prompt.reshape(batch * seq, D_MODEL)
    xs = x_sample.reshape(dec_batch * dec_seq, D_MODEL)
    (qp, kp, vep, vop, gmp, ckv_p, krt_p), (w_down_b,) = _pre(xp, mods3, ctx_row, p, seq, tm_pre, [w_down[l]])
    cache_in = (*_rope_tables(dec_seq), cache_ckv[:, l].reshape(dec_batch * past, KV_RANK),
                jnp.swapaxes(cache_krope[:, l], 1, 2))
    (qs, ks, ves, vos, gms), (w_o_b,) = _pre(xs, mods3, lat_row(tm_pre), p, dec_seq, tm_pre, [w_o[l]],
                                             latent=cache_in)

    ans, (w_up_b,) = _attention(qs, ks, ves, vos, p["out_g_attn"], dec_seq, past + dec_seq, 1024, 1, [w_up[l]])
    anp, _ = _attention(qp, kp, vep, vop, p["out_g_attn"], seq, seq, seq, 4, [])
    ffn_w = (w_o_b, w_up_b, w_down_b)
    yp = _ffn(xp, anp, gmp, mods3, ctx_row, p, ffn_w, seq, tm_ffn)
    ys = _ffn(xs, ans, gms, mods3, lat_row(tm_ffn), p, ffn_w, dec_seq, tm_ffn)

    return (yp.reshape(batch, seq, D_MODEL), ys.reshape(dec_batch, dec_seq, D_MODEL),
            ckv_p.reshape(batch, 1, seq, KV_RANK), jnp.swapaxes(krt_p, 1, 2).reshape(batch, 1, seq, QK_ROPE))
```

```python
import functools

import numpy as np
import jax
import jax.numpy as jnp
from jax import lax
from jax.experimental import pallas as pl
from jax.experimental.pallas import tpu as pltpu

D_MODEL = 1024
GRID_W = 64
CHUNK = 128
N_HEADS = 8
QK_NOPE = 64
QK_ROPE = 32
V_DIM = 64
Q_RANK = 384
KV_RANK = 256
ATTN_W = N_HEADS * V_DIM
GMLP_W = D_MODEL - ATTN_W
GMLP_GROUPS = 4
GMLP_DG = GMLP_W // GMLP_GROUPS
D_FF = 2816
CONV_W = 3
ROPE_THETA = 10000.0
EPS = 1e-6
SM_SCALE = (QK_NOPE + QK_ROPE) ** -0.5
LOG2_E = 1.4426950408889634

LANES = 128
HEAD_PAD = LANES
ROPE_LANE0 = QK_NOPE
ROT_HALF = QK_ROPE // 4
QK_W = N_HEADS * HEAD_PAD
N_MODS = 6
MOD_ROWS = 8
ADALN_STEPS = 8
IN_QKV = Q_RANK + KV_RANK
FF_CHUNK = 256
HALO = 16
N_FF_CHUNKS = D_FF // FF_CHUNK
A_SLOTS = 4
V7X_VMEM_BYTES = 64 * 1024 * 1024
VMEM_LIMIT = V7X_VMEM_BYTES - 8 * 1024 * 1024

F32 = jnp.float32
BF16 = jnp.bfloat16


def _dot(a, b):
    return jnp.dot(a, b, preferred_element_type=F32)


def _dot_nt(a, b):
    return lax.dot_general(a, b, (((1,), (1,)), ((), ())), preferred_element_type=F32)


def _rms(x, g):
    return x * lax.rsqrt(jnp.mean(x * x, axis=-1, keepdims=True) + EPS) * g


def _gelu(x):
    return 0.5 * x * (1.0 + lax.erf(x * (0.5 ** 0.5)))


def _silu(x):
    return x * (1.0 / (1.0 + jnp.exp2(x * -LOG2_E)))


def _params(n_axes):
    return pltpu.CompilerParams(dimension_semantics=("arbitrary",) * n_axes,
                                vmem_limit_bytes=VMEM_LIMIT)


def _adaln_kernel(cctx_ref, c_ref, w_ref, b_ref, win_ref, wuq_ref, wukv_ref, ws_ref,
                  out_ref, wqkv_out, wuv_out, wuq_out, wk_out, wve_out, wvo_out, ws_out):
    pad = jnp.zeros((MOD_ROWS - 1 - c_ref.shape[0], D_MODEL), F32)
    cond = jnp.concatenate([cctx_ref[...], c_ref[...], pad], axis=0)
    s = _silu(cond).astype(BF16)
    out_ref[:, 0, :] = _dot(s, w_ref[...].astype(BF16)) + b_ref[...]
    @pl.when(pl.program_id(0) == 0)
    def _():
        kr = win_ref[IN_QKV:IN_QKV + QK_ROPE, :].astype(BF16)
        z = jnp.zeros_like(kr)
        wqkv_out[0:IN_QKV, :] = win_ref[0:IN_QKV, :].astype(BF16)
        for t, blk in enumerate((kr, z, kr, z)):
            wqkv_out[IN_QKV + t * QK_ROPE:IN_QKV + (t + 1) * QK_ROPE, :] = blk
        wuv_out[...] = win_ref[IN_QKV + QK_ROPE:, :].astype(BF16)

    w = wuq_ref[...]
    hw = QK_NOPE + QK_ROPE
    zq = jnp.zeros((w.shape[0], HEAD_PAD - hw), F32)
    wuq_out[...] = jnp.concatenate([blk for hd in range(N_HEADS) for blk in (w[:, hd * hw:(hd + 1) * hw], zq)],
                                   axis=1).astype(BF16)
    w = wukv_ref[...]
    zv = jnp.zeros((w.shape[0], V_DIM), F32)
    v_of = lambda hd: w[:, hd * HEAD_PAD + QK_NOPE:(hd + 1) * HEAD_PAD]
    wk_out[...] = jnp.concatenate([blk for hd in range(N_HEADS)
                                   for blk in (w[:, hd * HEAD_PAD:hd * HEAD_PAD + QK_NOPE], zv)], axis=1).astype(BF16)
    wve_out[...] = jnp.concatenate([blk for hd in range(0, N_HEADS, 2) for blk in (v_of(hd), zv)],
                                   axis=1).astype(BF16)
    wvo_out[...] = jnp.concatenate([blk for hd in range(1, N_HEADS, 2) for blk in (zv, v_of(hd))],
                                   axis=1).astype(BF16)
    ws_out[...] = ws_ref[...].astype(BF16)


def _adaln(c_ctx, c, ada_w, ada_b, w_in_t, w_uq, w_ukv, w_s):
    cols = N_MODS * D_MODEL // ADALN_STEPS
    row_block = lambda a, width=None: pl.BlockSpec((a.shape[0] // ADALN_STEPS, width or a.shape[1]),
                                                   lambda j: (j, 0))
    whole = lambda shape: pl.BlockSpec(shape, lambda j: (0, 0), pipeline_mode=pl.Buffered(1))
    return pl.pallas_call(
        _adaln_kernel,
        grid=(ADALN_STEPS,),
        in_specs=[
            pl.BlockSpec(c_ctx.shape, lambda j: (0, 0)),
            pl.BlockSpec(c.shape, lambda j: (0, 0)),
            pl.BlockSpec((D_MODEL, cols), lambda j: (0, j)),
            pl.BlockSpec((1, cols), lambda j: (0, j)),
            whole(w_in_t.shape), row_block(w_uq), row_block(w_ukv), row_block(w_s),
        ],
        out_specs=[pl.BlockSpec((MOD_ROWS, 1, cols), lambda j: (0, 0, j)),
                   whole((IN_QKV + LANES, D_MODEL)), whole((2 * GMLP_W, D_MODEL)),
                   row_block(w_uq, QK_W), row_block(w_ukv, QK_W), row_block(w_ukv, ATTN_W),
                   row_block(w_ukv, ATTN_W), row_block(w_s)],
        out_shape=[jax.ShapeDtypeStruct((MOD_ROWS, 1, N_MODS * D_MODEL), F32),
                   jax.ShapeDtypeStruct((IN_QKV + LANES, D_MODEL), BF16),
                   jax.ShapeDtypeStruct((2 * GMLP_W, D_MODEL), BF16),
                   jax.ShapeDtypeStruct((Q_RANK, QK_W), BF16),
                   jax.ShapeDtypeStruct((KV_RANK, QK_W), BF16),
                   jax.ShapeDtypeStruct((KV_RANK, ATTN_W), BF16),
                   jax.ShapeDtypeStruct((KV_RANK, ATTN_W), BF16),
                   jax.ShapeDtypeStruct(w_s.shape, BF16)],
        compiler_params=_params(1),
        name="adaln",
    )(c_ctx, c, ada_w, ada_b, w_in_t, w_uq, w_ukv, w_s)


def _pre_kernel(*refs, latent, tm, seq, n_casts):
    (x_ref, mod_ref, n1g_ref, wqkv_ref, wuv_ref, qng_ref, wuq_ref, kvg_ref, wk_ref, wve_ref, wvo_ref,
     vng_ref, ws_ref, bs_ref, ogg_ref) = refs[:15]
    n_in = 15 + (6 if latent else 0)
    n_out = 5 + (0 if latent else 2)
    cast_in = refs[n_in:n_in + n_casts]
    outs = refs[n_in + n_casts:n_in + n_casts + n_out]
    cast_out = refs[n_in + n_casts + n_out:n_in + 2 * n_casts + n_out]
    go_sc = refs[-1]
    q_out, k_out, ve_out, vo_out, gm_out = outs[:5]
    for src, dst in zip(cast_in, cast_out):
        dst[...] = src[...].astype(BF16)
    shift1 = mod_ref[:, 0:D_MODEL]
    scale1 = mod_ref[:, D_MODEL:2 * D_MODEL]
    h = (_rms(x_ref[...], n1g_ref[...] * (1.0 + scale1)) + shift1).astype(BF16)

    lane = lax.broadcasted_iota(jnp.int32, (1, LANES), 1)
    rope_lanes = lane >= ROPE_LANE0
    first_half = (lane % (2 * ROT_HALF)) < ROT_HALF

    def expand(ckv_b, kr_k, rows):
        kn = _dot(ckv_b, wk_ref[...])
        for hd in range(N_HEADS):
            sl = slice(hd * HEAD_PAD, (hd + 1) * HEAD_PAD)
            k_out[rows, sl] = (kn[:, sl] + kr_k).astype(BF16)
        ve_out[rows, :] = _dot(ckv_b, wve_ref[...]).astype(BF16)
        vo_out[rows, :] = _dot(ckv_b, wvo_ref[...]).astype(BF16)

    qkv = _dot_nt(h, wqkv_ref[...])
    qn = _rms(qkv[:, 0:Q_RANK], qng_ref[...]).astype(BF16)
    ckv = _rms(qkv[:, Q_RANK:IN_QKV], kvg_ref[...])
    q = _dot(qn, wuq_ref[...])
    kr = qkv[:, IN_QKV:IN_QKV + LANES]
    if latent:
        cosq_ref, sinq_ref, cosk_ref, sink_ref, cckv_ref, ckrt_ref = refs[15:21]

        def rotate(v, cos_ref, sin_ref):
            swapped = jnp.where(first_half, pltpu.roll(v, LANES - ROT_HALF, 1), pltpu.roll(v, ROT_HALF, 1))
            return v * cos_ref[...] + swapped * sin_ref[...]

        kr_k = rotate(kr, cosk_ref, sink_ref)
        ckrt = ckrt_ref[...]
        past = ckrt.shape[1]
        padded = jnp.concatenate([ckrt, jnp.zeros((LANES - QK_ROPE, past), F32)], axis=0)
        expand(cckv_ref[...].astype(BF16), pltpu.roll(padded.T, ROPE_LANE0, 1), slice(0, past))
        own_rows = slice(past, past + tm)
    else:
        ckv_out, krt_out = outs[5:7]
        ckv_out[...] = ckv
        krt = kr.T
        for s in range(tm // seq):
            krt_out[s] = krt[0:QK_ROPE, s * seq:(s + 1) * seq]
        kr_k = jnp.where(rope_lanes, kr, 0.0)
        own_rows = slice(0, tm)
    expand(ckv.astype(BF16), kr_k, own_rows)
    for hd in range(N_HEADS):
        sl = slice(hd * HEAD_PAD, (hd + 1) * HEAD_PAD)
        qh = rotate(q[:, sl], cosq_ref, sinq_ref) if latent else q[:, sl] * (SM_SCALE * LOG2_E)
        q_out[:, sl] = qh.astype(BF16)

    uv = _dot_nt(h, wuv_ref[...])
    gu = _gelu(uv[:, 0:GMLP_W])
    gv = _gelu(uv[:, GMLP_W:2 * GMLP_W])
    eye = (lax.broadcasted_iota(jnp.int32, (CHUNK, CHUNK), 0)
           == lax.broadcasted_iota(jnp.int32, (CHUNK, CHUNK), 1))
    for g in range(GMLP_GROUPS):
        sl = slice(g * GMLP_DG, (g + 1) * GMLP_DG)
        vg = gv[:, sl]
        vg = (vg * lax.rsqrt(jnp.mean(vg * vg, axis=-1, keepdims=True) + EPS) * vng_ref[:, sl]).astype(BF16)
        bias = jnp.sum(jnp.where(eye, bs_ref[g:g + 1, :], 0.0), axis=1, keepdims=True)
        chunks = [vg[n * CHUNK:(n + 1) * CHUNK, :] for n in range(tm // CHUNK)]
        s_all = _dot(ws_ref[g], jnp.concatenate(chunks, axis=1))
        for n in range(tm // CHUNK):
            rows = slice(n * CHUNK, (n + 1) * CHUNK)
            go_sc[rows, sl] = gu[rows, sl] * (s_all[:, n * GMLP_DG:(n + 1) * GMLP_DG] + bias)
    gm_out[...] = _rms(go_sc[...], ogg_ref[...]).astype(BF16)


def _pre(x, mods3, mod_row, p, seq, tm, casts, latent=None):
    n = x.shape[0]
    const = lambda i: (0, 0)
    tile = lambda i: (i, 0)
    in_specs = [
        pl.BlockSpec((tm, D_MODEL), tile),
        pl.BlockSpec((None, 1, N_MODS * D_MODEL), lambda i: (mod_row(i), 0, 0)),
        pl.BlockSpec((1, D_MODEL), const),
        pl.BlockSpec((IN_QKV + LANES, D_MODEL), const),
        pl.BlockSpec((2 * GMLP_W, D_MODEL), const),
        pl.BlockSpec((1, Q_RANK), const),
        pl.BlockSpec((Q_RANK, QK_W), const),
        pl.BlockSpec((1, KV_RANK), const),
        pl.BlockSpec((KV_RANK, QK_W), const),
        pl.BlockSpec((KV_RANK, ATTN_W), const),
        pl.BlockSpec((KV_RANK, ATTN_W), const),
        pl.BlockSpec((1, GMLP_W), const),
        pl.BlockSpec((GMLP_GROUPS, CHUNK, CHUNK), lambda i: (0, 0, 0)),
        pl.BlockSpec((GMLP_GROUPS, CHUNK), const),
        pl.BlockSpec((1, GMLP_W), const),
    ]
    args = [x, mods3, p["norm1_g"], p["w_qkv"], p["w_uv"], p["q_norm_g"], p["w_uq"],
            p["kv_norm_g"], p["w_k"], p["w_ve"], p["w_vo"], p["v_norm_g"], p["w_s"], p["b_s"], p["out_g_gmlp"]]
    past = 0 if latent is None else latent[2].shape[2]
    kv_rows = tm + past
    out_rows = (tm, kv_rows, kv_rows, kv_rows, tm)
    widths = (QK_W, QK_W, ATTN_W, ATTN_W, GMLP_W)
    out_specs = [pl.BlockSpec((r, w), tile) for r, w in zip(out_rows, widths)]
    out_shape = [jax.ShapeDtypeStruct((n // tm * r, w), BF16) for r, w in zip(out_rows, widths)]
    if latent is not None:
        rope_tabs, cache_ckv, cache_krt = latent
        assert tm == seq == rope_tabs[0].shape[0]
        in_specs += [pl.BlockSpec((tm, LANES), const)] * len(rope_tabs) + [
            pl.BlockSpec((past, KV_RANK), tile),
            pl.BlockSpec((None, QK_ROPE, past), lambda i: (i, 0, 0))]
        args += [*rope_tabs, cache_ckv, cache_krt]
    else:
        out_specs += [pl.BlockSpec((tm, KV_RANK), tile),
                      pl.BlockSpec((tm // seq, QK_ROPE, seq), lambda i: (i, 0, 0))]
        out_shape += [jax.ShapeDtypeStruct((n, KV_RANK), F32),
                      jax.ShapeDtypeStruct((n // seq, QK_ROPE, seq), F32)]
    n_out = len(out_specs)
    cast_specs = [pl.BlockSpec((w.shape[0] // (n // tm), w.shape[1]), tile) for w in casts]
    outs = pl.pallas_call(
        functools.partial(_pre_kernel, latent=latent is not None, tm=tm, seq=seq, n_casts=len(casts)),
        grid=(n // tm,),
        in_specs=in_specs + cast_specs,
        out_specs=out_specs + cast_specs,
        out_shape=out_shape + [jax.ShapeDtypeStruct(w.shape, BF16) for w in casts],
        scratch_shapes=[pltpu.VMEM((tm, GMLP_W), F32)],
        compiler_params=_params(1),
        name="pre_latent" if latent is not None else "pre",
    )(*args, *casts)
    return outs[:n_out], outs[n_out:]


def _attn_kernel(*refs, group, kv_len, tq, n_casts):
    q_ref, k_ref, ve_ref, vo_ref, oga_ref = refs[:5]
    cast_in = refs[5:5 + n_casts]
    out_ref = refs[5 + n_casts]
    cast_out = refs[6 + n_casts:6 + 2 * n_casts]
    for src, dst in zip(cast_in, cast_out):
        dst[...] = src[...].astype(BF16)
    ones = jnp.ones((kv_len, LANES), BF16)
    for g in range(group):
        qrows = slice(g * tq, (g + 1) * tq)
        krows = slice(g * kv_len, (g + 1) * kv_len)
        pairs = []
        for pp in range(N_HEADS // 2):
            psl = slice(pp * LANES, (pp + 1) * LANES)
            acc = None
            for par, v_ref in enumerate((ve_ref, vo_ref)):
                hd = 2 * pp + par
                hsl = slice(hd * HEAD_PAD, (hd + 1) * HEAD_PAD)
                s = _dot_nt(q_ref[qrows, hsl], k_ref[krows, hsl])
                e = jnp.exp2(s - jnp.max(s, axis=-1, keepdims=True)).astype(BF16)
                o = _dot(e, jnp.concatenate([v_ref[krows, psl], ones], axis=1))
                o = o[:, 0:LANES] * (1.0 / o[:, LANES:2 * LANES])
                acc = o if acc is None else acc + o
            pairs.append(acc)
        ssq = sum(jnp.sum(b * b, axis=-1, keepdims=True) for b in pairs)
        r = lax.rsqrt(ssq * (1.0 / ATTN_W) + EPS)
        for j, b in enumerate(pairs):
            sl = slice(j * LANES, (j + 1) * LANES)
            out_ref[qrows, sl] = (b * r * oga_ref[:, sl]).astype(BF16)


def _attention(q, k, ve, vo, oga, seq, kv_len, tq, group, casts):
    n = q.shape[0]
    nq = seq // tq
    grid = (n // (group * seq), nq)
    linear = lambda b, i: (b * nq + i, 0)
    cast_specs = [pl.BlockSpec((w.shape[0] // (grid[0] * nq), w.shape[1]), linear) for w in casts]
    kv = lambda b, i: (b, 0)
    in_specs = [
        pl.BlockSpec((group * tq, QK_W), linear),
        pl.BlockSpec((group * kv_len, QK_W), kv),
        pl.BlockSpec((group * kv_len, ATTN_W), kv),
        pl.BlockSpec((group * kv_len, ATTN_W), kv),
        pl.BlockSpec((1, ATTN_W), lambda b, i: (0, 0)),
    ]
    outs = pl.pallas_call(
        functools.partial(_attn_kernel, group=group, kv_len=kv_len, tq=tq, n_casts=len(casts)),
        grid=grid,
        in_specs=in_specs + cast_specs,
        out_specs=[pl.BlockSpec((group * tq, ATTN_W), linear)] + cast_specs,
        out_shape=[jax.ShapeDtypeStruct((n, ATTN_W), BF16)]
        + [jax.ShapeDtypeStruct(w.shape, BF16) for w in casts],
        compiler_params=_params(2),
        name="attn",
    )(q, k, ve, vo, oga, *casts)
    return outs[0], outs[1:]


def _ffn_kernel(*refs, nseg, seg, tiles_per_seq):
    halo = tiles_per_seq > 1
    if halo:
        (x_ref, an_ref, gm_ref, xp_ref, xn_ref, anp_ref, ann_ref, gmp_ref, gmn_ref, mod_ref, wo_ref, n2g_ref,
         wup_ref, cw_ref, cb_ref, wd_ref, fg_ref, out_ref, lhs_sc, a_sc, g_sc, mix_sc) = refs
    else:
        (x_ref, an_ref, gm_ref, mod_ref, wo_ref, n2g_ref,
         wup_ref, cw_ref, cb_ref, wd_ref, fg_ref, out_ref, lhs_sc, a_sc, g_sc) = refs
    tm = nseg * seg
    stride = seg + HALO
    edge = HALO // 2
    gate1 = mod_ref[:, 2 * D_MODEL:3 * D_MODEL]
    shift2 = mod_ref[:, 3 * D_MODEL:4 * D_MODEL]
    scale2 = mod_ref[:, 4 * D_MODEL:5 * D_MODEL]

    def residual_norm(x, y, keep=None):
        x1 = x + gate1 * y
        h2 = _rms(x1, n2g_ref[...] * (1.0 + scale2)) + shift2
        if keep is not None:
            h2 = jnp.where(keep, h2, 0.0)
        return x1, h2.astype(BF16)

    if halo:
        def neighbours(prev_ref, next_ref):
            return jnp.concatenate([prev_ref[...].astype(F32)[edge:HALO, :],
                                    next_ref[...].astype(F32)[0:edge, :]], axis=0)

        mix_sc[0:tm, 0:ATTN_W] = an_ref[...]
        mix_sc[0:tm, ATTN_W:D_MODEL] = gm_ref[...]
        mix_sc[tm:tm + HALO, 0:ATTN_W] = neighbours(anp_ref, ann_ref).astype(BF16)
        mix_sc[tm:tm + HALO, ATTN_W:D_MODEL] = neighbours(gmp_ref, gmn_ref).astype(BF16)
        y = _dot(mix_sc[...], wo_ref[...])
        x1, h2 = residual_norm(x_ref[...], y[0:tm])
        out_ref[...] = x1
        lhs_sc[0:tm, :] = h2
        pos = pl.program_id(0) % tiles_per_seq
        is_prev = lax.broadcasted_iota(jnp.int32, (HALO, 1), 0) < edge
        has_prev = (pos != 0).astype(jnp.int32)
        has_next = (pos != tiles_per_seq - 1).astype(jnp.int32)
        keep = jnp.where(is_prev, has_prev, has_next) > 0
        lhs_sc[tm:tm + HALO, :] = residual_norm(neighbours(xp_ref, xn_ref), y[tm:tm + HALO], keep)[1]
    else:
        @pl.when(pl.program_id(0) == 0)
        def _():
            zeros = jnp.zeros((A_SLOTS, 2 * FF_CHUNK // LANES, HALO, LANES), F32)
            a_sc[:, :, 0:edge, :] = zeros[:, :, 0:edge, :]
            for s in range(1, nseg):
                a_sc[:, :, s * stride - edge:s * stride + edge, :] = zeros
            a_sc[:, :, nseg * stride - edge:nseg * stride, :] = zeros[:, :, 0:edge, :]

        y = _dot(an_ref[...], wo_ref[0:ATTN_W, :]) + _dot(gm_ref[...], wo_ref[ATTN_W:D_MODEL, :])
        x1, h2 = residual_norm(x_ref[...], y)
        out_ref[...] = x1
        lhs_sc[...] = h2

    for c in range(N_FF_CHUNKS):
        slot = c % A_SLOTS
        for k, col0 in enumerate((c * FF_CHUNK, D_FF + c * FF_CHUNK)):
            a = _dot(lhs_sc[...], wup_ref[:, col0:col0 + FF_CHUNK])
            for j in range(FF_CHUNK // LANES):
                lanes = slice(j * LANES, (j + 1) * LANES)
                for s in range(nseg):
                    row0 = edge + s * stride
                    a_sc[slot, 2 * k + j, row0:row0 + seg, :] = a[s * seg:(s + 1) * seg, lanes]
                if halo:
                    a_sc[slot, 2 * k + j, 0:edge, :] = a[tm:tm + edge, lanes]
                    a_sc[slot, 2 * k + j, edge + seg:HALO + seg, :] = a[tm + edge:tm + HALO, lanes]
        for j in range(FF_CHUNK // LANES):
            gl = slice(c * FF_CHUNK + j * LANES, c * FF_CHUNK + (j + 1) * LANES)
            vl = slice(D_FF + gl.start, D_FF + gl.stop)
            for s in range(nseg):
                base = edge + s * stride

                def conv(k, lanes):
                    acc = cb_ref[:, lanes]
                    for t in range(CONV_W):
                        tap = slice(t * 2 * D_FF + lanes.start, t * 2 * D_FF + lanes.stop)
                        acc = acc + a_sc[slot, k, base - 1 + t:base - 1 + t + seg, :] * cw_ref[:, tap]
                    return acc

                g = _silu(conv(j, gl)) * conv(2 + j, vl)
                g_sc[s * seg:(s + 1) * seg, gl] = g.astype(BF16)

    x2 = out_ref[...] + mod_ref[:, 5 * D_MODEL:6 * D_MODEL] * _dot(g_sc[...], wd_ref[...])
    out_ref[...] = _rms(x2, fg_ref[...])


def _ffn(x, an, gm, mods3, mod_row, p, ffn_w, seq, tm):
    n = x.shape[0]
    tiles_per_seq = max(seq // tm, 1)
    seg = min(seq, tm)
    nseg = tm // seg
    rows = tm + (HALO if tiles_per_seq > 1 else 0)
    a_rows = nseg * (seg + HALO)
    const = lambda i: (0, 0)
    tile = lambda i: (i, 0)
    resident = dict(pipeline_mode=pl.Buffered(1))
    in_specs = [pl.BlockSpec((tm, D_MODEL), tile), pl.BlockSpec((tm, ATTN_W), tile),
                pl.BlockSpec((tm, GMLP_W), tile)]
    args = [x, an, gm]
    scratch = [pltpu.VMEM((rows, D_MODEL), BF16),
               pltpu.VMEM((A_SLOTS, 2 * FF_CHUNK // LANES, a_rows, LANES), F32),
               pltpu.VMEM((tm, D_FF), BF16)]
    if tiles_per_seq > 1:
        per = tm // HALO
        last = n // HALO - 1
        prev = lambda i: (jnp.maximum(i * per - 1, 0), 0)
        nxt = lambda i: (jnp.minimum((i + 1) * per, last), 0)
        for arr, width in ((x, D_MODEL), (an, ATTN_W), (gm, GMLP_W)):
            in_specs += [pl.BlockSpec((HALO, width), prev), pl.BlockSpec((HALO, width), nxt)]
            args += [arr, arr]
        scratch.append(pltpu.VMEM((rows, D_MODEL), BF16))
    in_specs += [
        pl.BlockSpec((None, 1, N_MODS * D_MODEL), lambda i: (mod_row(i), 0, 0)),
        pl.BlockSpec((D_MODEL, D_MODEL), const, **resident),
        pl.BlockSpec((1, D_MODEL), const),
        pl.BlockSpec((D_MODEL, 2 * D_FF), const, **resident),
        pl.BlockSpec((1, CONV_W * 2 * D_FF), const),
        pl.BlockSpec((1, 2 * D_FF), const),
        pl.BlockSpec((D_FF, D_MODEL), const, **resident),
        pl.BlockSpec((1, D_MODEL), const),
    ]
    w_o, w_up, w_down = ffn_w
    args += [mods3, w_o, p["norm2_g"], w_up, p["conv_w"], p["conv_b"], w_down, p["final_g"]]
    return pl.pallas_call(
        functools.partial(_ffn_kernel, nseg=nseg, seg=seg, tiles_per_seq=tiles_per_seq),
        grid=(n // tm,),
        in_specs=in_specs,
        out_specs=pl.BlockSpec((tm, D_MODEL), tile),
        out_shape=jax.ShapeDtypeStruct((n, D_MODEL), F32),
        scratch_shapes=scratch,
        compiler_params=_params(1),
        name="ffn_halo" if tiles_per_seq > 1 else "ffn",
    )(*args)


def _rope_tables(length):
    pos = np.arange(length)
    row = (pos // GRID_W).astype(np.float32)
    col = (pos % GRID_W).astype(np.float32)
    n_freq = QK_ROPE // 4
    inv = (np.float32(ROPE_THETA) ** (-(np.arange(n_freq, dtype=np.float32) / np.float32(n_freq)))).astype(np.float32)
    ang_r, ang_c = row[:, None] * inv, col[:, None] * inv
    cos32 = np.concatenate([np.cos(ang_r)] * 2 + [np.cos(ang_c)] * 2, axis=-1)
    sin32 = np.concatenate([-np.sin(ang_r), np.sin(ang_r), -np.sin(ang_c), np.sin(ang_c)], axis=-1)
    tail = LANES - ROPE_LANE0 - QK_ROPE
    block = lambda nope, rope: np.concatenate([np.full((length, ROPE_LANE0), nope), rope,
                                               np.zeros((length, tail))], -1)
    scale = SM_SCALE * LOG2_E
    tables = (block(1.0, cos32) * scale, block(0.0, sin32) * scale, block(0.0, cos32), block(0.0, sin32))
    return tuple(jnp.asarray(t, F32) for t in tables)


def _layer_params(l, norm1_g, q_norm_g, kv_norm_g, v_norm_g, b_s, out_g_attn,
                  out_g_gmlp, norm2_g, conv_w, conv_b, final_g):
    row = lambda a: a.reshape(1, -1)
    return {
        "norm1_g": row(norm1_g[l]), "q_norm_g": row(q_norm_g[l]),
        "kv_norm_g": row(kv_norm_g[l]), "v_norm_g": row(v_norm_g[l]), "b_s": b_s[l],
        "out_g_attn": row(out_g_attn[l]), "out_g_gmlp": row(out_g_gmlp[l]), "norm2_g": row(norm2_g[l]),
        "conv_w": conv_w[l].reshape(1, CONV_W * 2 * D_FF), "conv_b": row(conv_b[l]), "final_g": row(final_g),
    }


def kernel(x_prompt, x_sample, cache_ckv, cache_krope, c, c_ctx, ada_w, ada_b, norm1_g, w_in, q_norm_g, w_uq, kv_norm_g, w_ukv, v_norm_g, w_s, b_s, out_g_attn, out_g_gmlp, w_o, norm2_g, w_up, conv_w, conv_b, w_down, final_g):
    batch, seq, _ = x_prompt.shape
    dec_batch, dec_seq, _ = x_sample.shape
    depth, past = cache_ckv.shape[1], cache_ckv.shape[2]
    assert depth == 1 and dec_batch + 1 <= MOD_ROWS
    l = 0
    p = _layer_params(l, norm1_g, q_norm_g, kv_norm_g, v_norm_g, b_s, out_g_attn,
                      out_g_gmlp, norm2_g, conv_w, conv_b, final_g)

    mods3, p["w_qkv"], p["w_uv"], p["w_uq"], p["w_k"], p["w_ve"], p["w_vo"], w_s_b = _adaln(
        c_ctx.reshape(1, D_MODEL), c, ada_w[l], ada_b[l].reshape(1, -1), w_in[l].T, w_uq[l], w_ukv[l],
        w_s[l].reshape(GMLP_GROUPS * CHUNK, CHUNK))
    p["w_s"] = w_s_b.reshape(GMLP_GROUPS, CHUNK, CHUNK)
    tm_pre, tm_ffn = 1024, 512
    ctx_row = lambda i: 0
    lat_row = lambda tm: lambda i: 1 + i // (dec_seq // tm)

    xp = x_prompt.reshape(batch * seq, D_MODEL)
    xs = x_sample.reshape(dec_batch * dec_seq, D_MODEL)
    (qp, kp, vep, vop, gmp, ckv_p, krt_p), (w_down_b,) = _pre(xp, mods3, ctx_row, p, seq, tm_pre, [w_down[l]])
    cache_in = (_rope_tables(dec_seq), cache_ckv[:, l].reshape(dec_batch * past, KV_RANK),
                jnp.swapaxes(cache_krope[:, l], 1, 2))
    (qs, ks, ves, vos, gms), (w_o_b,) = _pre(xs, mods3, lat_row(tm_pre), p, dec_seq, tm_pre, [w_o[l]],
                                             latent=cache_in)

    ans, (w_up_b,) = _attention(qs, ks, ves, vos, p["out_g_attn"], dec_seq, past + dec_seq, 1024, 1, [w_up[l]])
    anp, _ = _attention(qp, kp, vep, vop, p["out_g_attn"], seq, seq, seq, 4, [])
    ffn_w = (w_o_b, w_up_b, w_down_b)
    yp = _ffn(xp, anp, gmp, mods3, ctx_row, p, ffn_w, seq, tm_ffn)
    ys = _ffn(xs, ans, gms, mods3, lat_row(tm_ffn), p, ffn_w, dec_seq, tm_ffn)

    return (yp.reshape(batch, seq, D_MODEL), ys.reshape(dec_batch, dec_seq, D_MODEL),
            ckv_p.reshape(batch, 1, seq, KV_RANK), jnp.swapaxes(krt_p, 1, 2).reshape(batch, 1, seq, QK_ROPE))
```

```python
import functools

import numpy as np
import jax
import jax.numpy as jnp
from jax import lax
from jax.experimental import pallas as pl
from jax.experimental.pallas import tpu as pltpu

D_MODEL = 1024
GRID_W = 64
CHUNK = 128
N_HEADS = 8
QK_NOPE = 64
QK_ROPE = 32
V_DIM = 64
Q_RANK = 384
KV_RANK = 256
ATTN_W = N_HEADS * V_DIM
GMLP_W = D_MODEL - ATTN_W
GMLP_GROUPS = 4
GMLP_DG = GMLP_W // GMLP_GROUPS
D_FF = 2816
CONV_W = 3
ROPE_THETA = 10000.0
EPS = 1e-6
SM_SCALE = (QK_NOPE + QK_ROPE) ** -0.5
LOG2_E = 1.4426950408889634

LANES = 128
HEAD_PAD = LANES
ROPE_LANE0 = QK_NOPE
ROT_HALF = QK_ROPE // 4
QK_W = N_HEADS * HEAD_PAD
N_MODS = 6
MOD_ROWS = 8
ADALN_STEPS = 4
IN_QKV = Q_RANK + KV_RANK
FF_CHUNK = 256
HALO = 16
N_FF_CHUNKS = D_FF // FF_CHUNK
A_SLOTS = 4
V7X_VMEM_BYTES = 64 * 1024 * 1024
VMEM_LIMIT = V7X_VMEM_BYTES - 8 * 1024 * 1024

F32 = jnp.float32
BF16 = jnp.bfloat16


def _dot(a, b):
    return jnp.dot(a, b, preferred_element_type=F32)


def _dot_nt(a, b):
    return lax.dot_general(a, b, (((1,), (1,)), ((), ())), preferred_element_type=F32)


def _rms(x, g):
    return x * lax.rsqrt(jnp.mean(x * x, axis=-1, keepdims=True) + EPS) * g


def _gelu(x):
    return 0.5 * x * (1.0 + lax.erf(x * (0.5 ** 0.5)))


def _silu(x):
    return x * (1.0 / (1.0 + jnp.exp2(x * -LOG2_E)))


def _params(n_axes):
    return pltpu.CompilerParams(dimension_semantics=("arbitrary",) * n_axes,
                                vmem_limit_bytes=VMEM_LIMIT)


def _adaln_kernel(cctx_ref, c_ref, w_ref, b_ref, win_ref, wuq_ref, wukv_ref, ws_ref,
                  out_ref, wqkv_out, wuv_out, wuq_out, wk_out, wve_out, wvo_out, ws_out):
    pad = jnp.zeros((MOD_ROWS - 1 - c_ref.shape[0], D_MODEL), F32)
    cond = jnp.concatenate([cctx_ref[...], c_ref[...], pad], axis=0)
    s = _silu(cond).astype(BF16)
    out_ref[:, 0, :] = _dot(s, w_ref[...].astype(BF16)) + b_ref[...]
    @pl.when(pl.program_id(0) == 0)
    def _():
        kr = win_ref[IN_QKV:IN_QKV + QK_ROPE, :].astype(BF16)
        z = jnp.zeros_like(kr)
        wqkv_out[0:IN_QKV, :] = win_ref[0:IN_QKV, :].astype(BF16)
        for t, blk in enumerate((kr, z, kr, z)):
            wqkv_out[IN_QKV + t * QK_ROPE:IN_QKV + (t + 1) * QK_ROPE, :] = blk
        wuv_out[...] = win_ref[IN_QKV + QK_ROPE:, :].astype(BF16)

    w = wuq_ref[...]
    hw = QK_NOPE + QK_ROPE
    zq = jnp.zeros((w.shape[0], HEAD_PAD - hw), F32)
    wuq_out[...] = jnp.concatenate([blk for hd in range(N_HEADS) for blk in (w[:, hd * hw:(hd + 1) * hw], zq)],
                                   axis=1).astype(BF16)
    w = wukv_ref[...]
    zv = jnp.zeros((w.shape[0], V_DIM), F32)
    v_of = lambda hd: w[:, hd * HEAD_PAD + QK_NOPE:(hd + 1) * HEAD_PAD]
    wk_out[...] = jnp.concatenate([blk for hd in range(N_HEADS)
                                   for blk in (w[:, hd * HEAD_PAD:hd * HEAD_PAD + QK_NOPE], zv)], axis=1).astype(BF16)
    wve_out[...] = jnp.concatenate([blk for hd in range(0, N_HEADS, 2) for blk in (v_of(hd), zv)],
                                   axis=1).astype(BF16)
    wvo_out[...] = jnp.concatenate([blk for hd in range(1, N_HEADS, 2) for blk in (zv, v_of(hd))],
                                   axis=1).astype(BF16)
    ws_out[...] = ws_ref[...].astype(BF16)


def _adaln(c_ctx, c, ada_w, ada_b, w_in_t, w_uq, w_ukv, w_s):
    cols = N_MODS * D_MODEL // ADALN_STEPS
    row_block = lambda a, width=None: pl.BlockSpec((a.shape[0] // ADALN_STEPS, width or a.shape[1]),
                                                   lambda j: (j, 0))
    whole = lambda shape: pl.BlockSpec(shape, lambda j: (0, 0), pipeline_mode=pl.Buffered(1))
    return pl.pallas_call(
        _adaln_kernel,
        grid=(ADALN_STEPS,),
        in_specs=[
            pl.BlockSpec(c_ctx.shape, lambda j: (0, 0)),
            pl.BlockSpec(c.shape, lambda j: (0, 0)),
            pl.BlockSpec((D_MODEL, cols), lambda j: (0, j)),
            pl.BlockSpec((1, cols), lambda j: (0, j)),
            whole(w_in_t.shape), row_block(w_uq), row_block(w_ukv), row_block(w_s),
        ],
        out_specs=[pl.BlockSpec((MOD_ROWS, 1, cols), lambda j: (0, 0, j)),
                   whole((IN_QKV + LANES, D_MODEL)), whole((2 * GMLP_W, D_MODEL)),
                   row_block(w_uq, QK_W), row_block(w_ukv, QK_W), row_block(w_ukv, ATTN_W),
                   row_block(w_ukv, ATTN_W), row_block(w_s)],
        out_shape=[jax.ShapeDtypeStruct((MOD_ROWS, 1, N_MODS * D_MODEL), F32),
                   jax.ShapeDtypeStruct((IN_QKV + LANES, D_MODEL), BF16),
                   jax.ShapeDtypeStruct((2 * GMLP_W, D_MODEL), BF16),
                   jax.ShapeDtypeStruct((Q_RANK, QK_W), BF16),
                   jax.ShapeDtypeStruct((KV_RANK, QK_W), BF16),
                   jax.ShapeDtypeStruct((KV_RANK, ATTN_W), BF16),
                   jax.ShapeDtypeStruct((KV_RANK, ATTN_W), BF16),
                   jax.ShapeDtypeStruct(w_s.shape, BF16)],
        compiler_params=_params(1),
        name="adaln",
    )(c_ctx, c, ada_w, ada_b, w_in_t, w_uq, w_ukv, w_s)


def _pre_kernel(*refs, latent, tm, seq, n_casts):
    (x_ref, mod_ref, n1g_ref, wqkv_ref, wuv_ref, qng_ref, wuq_ref, kvg_ref, wk_ref, wve_ref, wvo_ref,
     vng_ref, ws_ref, bs_ref, ogg_ref) = refs[:15]
    n_in = 15 + (6 if latent else 0)
    n_out = 5 + (0 if latent else 2)
    cast_in = refs[n_in:n_in + n_casts]
    outs = refs[n_in + n_casts:n_in + n_casts + n_out]
    cast_out = refs[n_in + n_casts + n_out:n_in + 2 * n_casts + n_out]
    go_sc = refs[-1]
    q_out, k_out, ve_out, vo_out, gm_out = outs[:5]
    for src, dst in zip(cast_in, cast_out):
        dst[...] = src[...].astype(BF16)
    shift1 = mod_ref[:, 0:D_MODEL]
    scale1 = mod_ref[:, D_MODEL:2 * D_MODEL]
    h = (_rms(x_ref[...], n1g_ref[...] * (1.0 + scale1)) + shift1).astype(BF16)

    lane = lax.broadcasted_iota(jnp.int32, (1, LANES), 1)
    rope_lanes = lane >= ROPE_LANE0
    first_half = (lane % (2 * ROT_HALF)) < ROT_HALF

    def expand(ckv_b, kr_k, rows):
        kn = _dot(ckv_b, wk_ref[...])
        for hd in range(N_HEADS):
            sl = slice(hd * HEAD_PAD, (hd + 1) * HEAD_PAD)
            k_out[rows, sl] = (kn[:, sl] + kr_k).astype(BF16)
        ve_out[rows, :] = _dot(ckv_b, wve_ref[...]).astype(BF16)
        vo_out[rows, :] = _dot(ckv_b, wvo_ref[...]).astype(BF16)

    qkv = _dot_nt(h, wqkv_ref[...])
    qn = _rms(qkv[:, 0:Q_RANK], qng_ref[...]).astype(BF16)
    ckv = _rms(qkv[:, Q_RANK:IN_QKV], kvg_ref[...])
    q = _dot(qn, wuq_ref[...])
    kr = qkv[:, IN_QKV:IN_QKV + LANES]
    if latent:
        cosq_ref, sinq_ref, cosk_ref, sink_ref, cckv_ref, ckrt_ref = refs[15:21]

        def rotate(v, cos_ref, sin_ref):
            swapped = jnp.where(first_half, pltpu.roll(v, LANES - ROT_HALF, 1), pltpu.roll(v, ROT_HALF, 1))
            return v * cos_ref[...] + swapped * sin_ref[...]

        kr_k = rotate(kr, cosk_ref, sink_ref)
        ckrt = ckrt_ref[...]
        past = ckrt.shape[1]
        padded = jnp.concatenate([ckrt, jnp.zeros((LANES - QK_ROPE, past), F32)], axis=0)
        expand(cckv_ref[...].astype(BF16), pltpu.roll(padded.T, ROPE_LANE0, 1), slice(0, past))
        own_rows = slice(past, past + tm)
    else:
        ckv_out, krt_out = outs[5:7]
        ckv_out[...] = ckv
        krt = kr.T
        for s in range(tm // seq):
            krt_out[s] = krt[0:QK_ROPE, s * seq:(s + 1) * seq]
        kr_k = jnp.where(rope_lanes, kr, 0.0)
        own_rows = slice(0, tm)
    expand(ckv.astype(BF16), kr_k, own_rows)
    for hd in range(N_HEADS):
        sl = slice(hd * HEAD_PAD, (hd + 1) * HEAD_PAD)
        qh = rotate(q[:, sl], cosq_ref, sinq_ref) if latent else q[:, sl] * (SM_SCALE * LOG2_E)
        q_out[:, sl] = qh.astype(BF16)

    uv = _dot_nt(h, wuv_ref[...])
    gu = _gelu(uv[:, 0:GMLP_W])
    gv = _gelu(uv[:, GMLP_W:2 * GMLP_W])
    eye = (lax.broadcasted_iota(jnp.int32, (CHUNK, CHUNK), 0)
           == lax.broadcasted_iota(jnp.int32, (CHUNK, CHUNK), 1))
    for g in range(GMLP_GROUPS):
        sl = slice(g * GMLP_DG, (g + 1) * GMLP_DG)
        vg = gv[:, sl]
        vg = (vg * lax.rsqrt(jnp.mean(vg * vg, axis=-1, keepdims=True) + EPS) * vng_ref[:, sl]).astype(BF16)
        bias = jnp.sum(jnp.where(eye, bs_ref[g:g + 1, :], 0.0), axis=1, keepdims=True)
        chunks = [vg[n * CHUNK:(n + 1) * CHUNK, :] for n in range(tm // CHUNK)]
        s_all = _dot(ws_ref[g], jnp.concatenate(chunks, axis=1))
        for n in range(tm // CHUNK):
            rows = slice(n * CHUNK, (n + 1) * CHUNK)
            go_sc[rows, sl] = gu[rows, sl] * (s_all[:, n * GMLP_DG:(n + 1) * GMLP_DG] + bias)
    gm_out[...] = _rms(go_sc[...], ogg_ref[...]).astype(BF16)


def _pre(x, mods3, mod_row, p, seq, tm, casts, latent=None):
    n = x.shape[0]
    const = lambda i: (0, 0)
    tile = lambda i: (i, 0)
    in_specs = [
        pl.BlockSpec((tm, D_MODEL), tile),
        pl.BlockSpec((None, 1, N_MODS * D_MODEL), lambda i: (mod_row(i), 0, 0)),
        pl.BlockSpec((1, D_MODEL), const),
        pl.BlockSpec((IN_QKV + LANES, D_MODEL), const),
        pl.BlockSpec((2 * GMLP_W, D_MODEL), const),
        pl.BlockSpec((1, Q_RANK), const),
        pl.BlockSpec((Q_RANK, QK_W), const),
        pl.BlockSpec((1, KV_RANK), const),
        pl.BlockSpec((KV_RANK, QK_W), const),
        pl.BlockSpec((KV_RANK, ATTN_W), const),
        pl.BlockSpec((KV_RANK, ATTN_W), const),
        pl.BlockSpec((1, GMLP_W), const),
        pl.BlockSpec((GMLP_GROUPS, CHUNK, CHUNK), lambda i: (0, 0, 0)),
        pl.BlockSpec((GMLP_GROUPS, CHUNK), const),
        pl.BlockSpec((1, GMLP_W), const),
    ]
    args = [x, mods3, p["norm1_g"], p["w_qkv"], p["w_uv"], p["q_norm_g"], p["w_uq"],
            p["kv_norm_g"], p["w_k"], p["w_ve"], p["w_vo"], p["v_norm_g"], p["w_s"], p["b_s"], p["out_g_gmlp"]]
    past = 0 if latent is None else latent[2].shape[2]
    kv_rows = tm + past
    out_rows = (tm, kv_rows, kv_rows, kv_rows, tm)
    widths = (QK_W, QK_W, ATTN_W, ATTN_W, GMLP_W)
    out_specs = [pl.BlockSpec((r, w), tile) for r, w in zip(out_rows, widths)]
    out_shape = [jax.ShapeDtypeStruct((n // tm * r, w), BF16) for r, w in zip(out_rows, widths)]
    if latent is not None:
        rope_tabs, cache_ckv, cache_krt = latent
        assert tm == seq == rope_tabs[0].shape[0]
        in_specs += [pl.BlockSpec((tm, LANES), const)] * len(rope_tabs) + [
            pl.BlockSpec((past, KV_RANK), tile),
            pl.BlockSpec((None, QK_ROPE, past), lambda i: (i, 0, 0))]
        args += [*rope_tabs, cache_ckv, cache_krt]
    else:
        out_specs += [pl.BlockSpec((tm, KV_RANK), tile),
                      pl.BlockSpec((tm // seq, QK_ROPE, seq), lambda i: (i, 0, 0))]
        out_shape += [jax.ShapeDtypeStruct((n, KV_RANK), F32),
                      jax.ShapeDtypeStruct((n // seq, QK_ROPE, seq), F32)]
    n_out = len(out_specs)
    cast_specs = [pl.BlockSpec((w.shape[0] // (n // tm), w.shape[1]), tile) for w in casts]
    outs = pl.pallas_call(
        functools.partial(_pre_kernel, latent=latent is not None, tm=tm, seq=seq, n_casts=len(casts)),
        grid=(n // tm,),
        in_specs=in_specs + cast_specs,
        out_specs=out_specs + cast_specs,
        out_shape=out_shape + [jax.ShapeDtypeStruct(w.shape, BF16) for w in casts],
        scratch_shapes=[pltpu.VMEM((tm, GMLP_W), F32)],
        compiler_params=_params(1),
        name="pre_latent" if latent is not None else "pre",
    )(*args, *casts)
    return outs[:n_out], outs[n_out:]


def _attn_kernel(*refs, group, kv_len, tq, n_casts):
    q_ref, k_ref, ve_ref, vo_ref, oga_ref = refs[:5]
    cast_in = refs[5:5 + n_casts]
    out_ref = refs[5 + n_casts]
    cast_out = refs[6 + n_casts:6 + 2 * n_casts]
    for src, dst in zip(cast_in, cast_out):
        dst[...] = src[...].astype(BF16)
    ones = jnp.ones((kv_len, LANES), BF16)
    for g in range(group):
        qrows = slice(g * tq, (g + 1) * tq)
        krows = slice(g * kv_len, (g + 1) * kv_len)
        pairs = []
        for pp in range(N_HEADS // 2):
            psl = slice(pp * LANES, (pp + 1) * LANES)
            acc = None
            for par, v_ref in enumerate((ve_ref, vo_ref)):
                hd = 2 * pp + par
                hsl = slice(hd * HEAD_PAD, (hd + 1) * HEAD_PAD)
                s = _dot_nt(q_ref[qrows, hsl], k_ref[krows, hsl])
                e = jnp.exp2(s - jnp.max(s, axis=-1, keepdims=True)).astype(BF16)
                o = _dot(e, jnp.concatenate([v_ref[krows, psl], ones], axis=1))
                o = o[:, 0:LANES] * (1.0 / o[:, LANES:2 * LANES])
                acc = o if acc is None else acc + o
            pairs.append(acc)
        ssq = sum(jnp.sum(b * b, axis=-1, keepdims=True) for b in pairs)
        r = lax.rsqrt(ssq * (1.0 / ATTN_W) + EPS)
        for j, b in enumerate(pairs):
            sl = slice(j * LANES, (j + 1) * LANES)
            out_ref[qrows, sl] = (b * r * oga_ref[:, sl]).astype(BF16)


def _attention(q, k, ve, vo, oga, seq, kv_len, tq, group, casts):
    n = q.shape[0]
    nq = seq // tq
    grid = (n // (group * seq), nq)
    linear = lambda b, i: (b * nq + i, 0)
    cast_specs = [pl.BlockSpec((w.shape[0] // (grid[0] * nq), w.shape[1]), linear) for w in casts]
    kv = lambda b, i: (b, 0)
    in_specs = [
        pl.BlockSpec((group * tq, QK_W), linear),
        pl.BlockSpec((group * kv_len, QK_W), kv),
        pl.BlockSpec((group * kv_len, ATTN_W), kv),
        pl.BlockSpec((group * kv_len, ATTN_W), kv),
        pl.BlockSpec((1, ATTN_W), lambda b, i: (0, 0)),
    ]
    outs = pl.pallas_call(
        functools.partial(_attn_kernel, group=group, kv_len=kv_len, tq=tq, n_casts=len(casts)),
        grid=grid,
        in_specs=in_specs + cast_specs,
        out_specs=[pl.BlockSpec((group * tq, ATTN_W), linear)] + cast_specs,
        out_shape=[jax.ShapeDtypeStruct((n, ATTN_W), BF16)]
        + [jax.ShapeDtypeStruct(w.shape, BF16) for w in casts],
        compiler_params=_params(2),
        name="attn",
    )(q, k, ve, vo, oga, *casts)
    return outs[0], outs[1:]


def _ffn_kernel(*refs, nseg, seg, tiles_per_seq):
    halo = tiles_per_seq > 1
    if halo:
        (x_ref, an_ref, gm_ref, xp_ref, xn_ref, anp_ref, ann_ref, gmp_ref, gmn_ref, mod_ref, wo_ref, n2g_ref,
         wup_ref, cw_ref, cb_ref, wd_ref, fg_ref, out_ref, lhs_sc, a_sc, g_sc, mix_sc) = refs
    else:
        (x_ref, an_ref, gm_ref, mod_ref, wo_ref, n2g_ref,
         wup_ref, cw_ref, cb_ref, wd_ref, fg_ref, out_ref, lhs_sc, a_sc, g_sc) = refs
    tm = nseg * seg
    stride = seg + HALO
    edge = HALO // 2
    gate1 = mod_ref[:, 2 * D_MODEL:3 * D_MODEL]
    shift2 = mod_ref[:, 3 * D_MODEL:4 * D_MODEL]
    scale2 = mod_ref[:, 4 * D_MODEL:5 * D_MODEL]

    def residual_norm(x, y, keep=None):
        x1 = x + gate1 * y
        h2 = _rms(x1, n2g_ref[...] * (1.0 + scale2)) + shift2
        if keep is not None:
            h2 = jnp.where(keep, h2, 0.0)
        return x1, h2.astype(BF16)

    if halo:
        def neighbours(prev_ref, next_ref):
            return jnp.concatenate([prev_ref[...].astype(F32)[edge:HALO, :],
                                    next_ref[...].astype(F32)[0:edge, :]], axis=0)

        mix_sc[0:tm, 0:ATTN_W] = an_ref[...]
        mix_sc[0:tm, ATTN_W:D_MODEL] = gm_ref[...]
        mix_sc[tm:tm + HALO, 0:ATTN_W] = neighbours(anp_ref, ann_ref).astype(BF16)
        mix_sc[tm:tm + HALO, ATTN_W:D_MODEL] = neighbours(gmp_ref, gmn_ref).astype(BF16)
        y = _dot(mix_sc[...], wo_ref[...])
        x1, h2 = residual_norm(x_ref[...], y[0:tm])
        out_ref[...] = x1
        lhs_sc[0:tm, :] = h2
        pos = pl.program_id(0) % tiles_per_seq
        is_prev = lax.broadcasted_iota(jnp.int32, (HALO, 1), 0) < edge
        has_prev = (pos != 0).astype(jnp.int32)
        has_next = (pos != tiles_per_seq - 1).astype(jnp.int32)
        keep = jnp.where(is_prev, has_prev, has_next) > 0
        lhs_sc[tm:tm + HALO, :] = residual_norm(neighbours(xp_ref, xn_ref), y[tm:tm + HALO], keep)[1]
    else:
        @pl.when(pl.program_id(0) == 0)
        def _():
            zeros = jnp.zeros((A_SLOTS, 2 * FF_CHUNK // LANES, HALO, LANES), F32)
            a_sc[:, :, 0:edge, :] = zeros[:, :, 0:edge, :]
            for s in range(1, nseg):
                a_sc[:, :, s * stride - edge:s * stride + edge, :] = zeros
            a_sc[:, :, nseg * stride - edge:nseg * stride, :] = zeros[:, :, 0:edge, :]

        y = _dot(an_ref[...], wo_ref[0:ATTN_W, :]) + _dot(gm_ref[...], wo_ref[ATTN_W:D_MODEL, :])
        x1, h2 = residual_norm(x_ref[...], y)
        out_ref[...] = x1
        lhs_sc[...] = h2

    for c in range(N_FF_CHUNKS):
        slot = c % A_SLOTS
        for k, col0 in enumerate((c * FF_CHUNK, D_FF + c * FF_CHUNK)):
            a = _dot(lhs_sc[...], wup_ref[:, col0:col0 + FF_CHUNK])
            for j in range(FF_CHUNK // LANES):
                lanes = slice(j * LANES, (j + 1) * LANES)
                for s in range(nseg):
                    row0 = edge + s * stride
                    a_sc[slot, 2 * k + j, row0:row0 + seg, :] = a[s * seg:(s + 1) * seg, lanes]
                if halo:
                    a_sc[slot, 2 * k + j, 0:edge, :] = a[tm:tm + edge, lanes]
                    a_sc[slot, 2 * k + j, edge + seg:HALO + seg, :] = a[tm + edge:tm + HALO, lanes]
        for j in range(FF_CHUNK // LANES):
            gl = slice(c * FF_CHUNK + j * LANES, c * FF_CHUNK + (j + 1) * LANES)
            vl = slice(D_FF + gl.start, D_FF + gl.stop)
            for s in range(nseg):
                base = edge + s * stride

                def conv(k, lanes):
                    acc = cb_ref[:, lanes]
                    for t in range(CONV_W):
                        tap = slice(t * 2 * D_FF + lanes.start, t * 2 * D_FF + lanes.stop)
                        acc = acc + a_sc[slot, k, base - 1 + t:base - 1 + t + seg, :] * cw_ref[:, tap]
                    return acc

                g = _silu(conv(j, gl)) * conv(2 + j, vl)
                g_sc[s * seg:(s + 1) * seg, gl] = g.astype(BF16)

    x2 = out_ref[...] + mod_ref[:, 5 * D_MODEL:6 * D_MODEL] * _dot(g_sc[...], wd_ref[...])
    out_ref[...] = _rms(x2, fg_ref[...])


def _ffn(x, an, gm, mods3, mod_row, p, ffn_w, seq, tm):
    n = x.shape[0]
    tiles_per_seq = max(seq // tm, 1)
    seg = min(seq, tm)
    nseg = tm // seg
    rows = tm + (HALO if tiles_per_seq > 1 else 0)
    a_rows = nseg * (seg + HALO)
    const = lambda i: (0, 0)
    tile = lambda i: (i, 0)
    resident = dict(pipeline_mode=pl.Buffered(1))
    in_specs = [pl.BlockSpec((tm, D_MODEL), tile), pl.BlockSpec((tm, ATTN_W), tile),
                pl.BlockSpec((tm, GMLP_W), tile)]
    args = [x, an, gm]
    scratch = [pltpu.VMEM((rows, D_MODEL), BF16),
               pltpu.VMEM((A_SLOTS, 2 * FF_CHUNK // LANES, a_rows, LANES), F32),
               pltpu.VMEM((tm, D_FF), BF16)]
    if tiles_per_seq > 1:
        per = tm // HALO
        last = n // HALO - 1
        prev = lambda i: (jnp.maximum(i * per - 1, 0), 0)
        nxt = lambda i: (jnp.minimum((i + 1) * per, last), 0)
        for arr, width in ((x, D_MODEL), (an, ATTN_W), (gm, GMLP_W)):
            in_specs += [pl.BlockSpec((HALO, width), prev), pl.BlockSpec((HALO, width), nxt)]
            args += [arr, arr]
        scratch.append(pltpu.VMEM((rows, D_MODEL), BF16))
    in_specs += [
        pl.BlockSpec((None, 1, N_MODS * D_MODEL), lambda i: (mod_row(i), 0, 0)),
        pl.BlockSpec((D_MODEL, D_MODEL), const, **resident),
        pl.BlockSpec((1, D_MODEL), const),
        pl.BlockSpec((D_MODEL, 2 * D_FF), const, **resident),
        pl.BlockSpec((1, CONV_W * 2 * D_FF), const),
        pl.BlockSpec((1, 2 * D_FF), const),
        pl.BlockSpec((D_FF, D_MODEL), const, **resident),
        pl.BlockSpec((1, D_MODEL), const),
    ]
    w_o, w_up, w_down = ffn_w
    args += [mods3, w_o, p["norm2_g"], w_up, p["conv_w"], p["conv_b"], w_down, p["final_g"]]
    return pl.pallas_call(
        functools.partial(_ffn_kernel, nseg=nseg, seg=seg, tiles_per_seq=tiles_per_seq),
        grid=(n // tm,),
        in_specs=in_specs,
        out_specs=pl.BlockSpec((tm, D_MODEL), tile),
        out_shape=jax.ShapeDtypeStruct((n, D_MODEL), F32),
        scratch_shapes=scratch,
        compiler_params=_params(1),
        name="ffn_halo" if tiles_per_seq > 1 else "ffn",
    )(*args)


def _rope_tables(length):
    pos = np.arange(length)
    row = (pos // GRID_W).astype(np.float32)
    col = (pos % GRID_W).astype(np.float32)
    n_freq = QK_ROPE // 4
    inv = (np.float32(ROPE_THETA) ** (-(np.arange(n_freq, dtype=np.float32) / np.float32(n_freq)))).astype(np.float32)
    ang_r, ang_c = row[:, None] * inv, col[:, None] * inv
    cos32 = np.concatenate([np.cos(ang_r)] * 2 + [np.cos(ang_c)] * 2, axis=-1)
    sin32 = np.concatenate([-np.sin(ang_r), np.sin(ang_r), -np.sin(ang_c), np.sin(ang_c)], axis=-1)
    tail = LANES - ROPE_LANE0 - QK_ROPE
    block = lambda nope, rope: np.concatenate([np.full((length, ROPE_LANE0), nope), rope,
                                               np.zeros((length, tail))], -1)
    scale = SM_SCALE * LOG2_E
    tables = (block(1.0, cos32) * scale, block(0.0, sin32) * scale, block(0.0, cos32), block(0.0, sin32))
    return tuple(jnp.asarray(t, F32) for t in tables)


def _layer_params(l, norm1_g, q_norm_g, kv_norm_g, v_norm_g, b_s, out_g_attn,
                  out_g_gmlp, norm2_g, conv_w, conv_b, final_g):
    row = lambda a: a.reshape(1, -1)
    return {
        "norm1_g": row(norm1_g[l]), "q_norm_g": row(q_norm_g[l]),
        "kv_norm_g": row(kv_norm_g[l]), "v_norm_g": row(v_norm_g[l]), "b_s": b_s[l],
        "out_g_attn": row(out_g_attn[l]), "out_g_gmlp": row(out_g_gmlp[l]), "norm2_g": row(norm2_g[l]),
        "conv_w": conv_w[l].reshape(1, CONV_W * 2 * D_FF), "conv_b": row(conv_b[l]), "final_g": row(final_g),
    }


def kernel(x_prompt, x_sample, cache_ckv, cache_krope, c, c_ctx, ada_w, ada_b, norm1_g, w_in, q_norm_g, w_uq, kv_norm_g, w_ukv, v_norm_g, w_s, b_s, out_g_attn, out_g_gmlp, w_o, norm2_g, w_up, conv_w, conv_b, w_down, final_g):
    batch, seq, _ = x_prompt.shape
    dec_batch, dec_seq, _ = x_sample.shape
    depth, past = cache_ckv.shape[1], cache_ckv.shape[2]
    assert depth == 1 and dec_batch + 1 <= MOD_ROWS
    l = 0
    p = _layer_params(l, norm1_g, q_norm_g, kv_norm_g, v_norm_g, b_s, out_g_attn,
                      out_g_gmlp, norm2_g, conv_w, conv_b, final_g)

    mods3, p["w_qkv"], p["w_uv"], p["w_uq"], p["w_k"], p["w_ve"], p["w_vo"], w_s_b = _adaln(
        c_ctx.reshape(1, D_MODEL), c, ada_w[l], ada_b[l].reshape(1, -1), w_in[l].T, w_uq[l], w_ukv[l],
        w_s[l].reshape(GMLP_GROUPS * CHUNK, CHUNK))
    p["w_s"] = w_s_b.reshape(GMLP_GROUPS, CHUNK, CHUNK)
    tm_pre, tm_ffn = 1024, 512
    ctx_row = lambda i: 0
    lat_row = lambda tm: lambda i: 1 + i // (dec_seq // tm)

    xp = x_prompt.reshape(batch * seq, D_MODEL)
    xs = x_sample.reshape(dec_batch * dec_seq, D_MODEL)
    (qp, kp, vep, vop, gmp, ckv_p, krt_p), (w_down_b,) = _pre(xp, mods3, ctx_row, p, seq, tm_pre, [w_down[l]])
    cache_in = (_rope_tables(dec_seq), cache_ckv[:, l].reshape(dec_batch * past, KV_RANK),
                jnp.swapaxes(cache_krope[:, l], 1, 2))
    (qs, ks, ves, vos, gms), (w_o_b,) = _pre(xs, mods3, lat_row(tm_pre), p, dec_seq, tm_pre, [w_o[l]],
                                             latent=cache_in)

    ans, (w_up_b,) = _attention(qs, ks, ves, vos, p["out_g_attn"], dec_seq, past + dec_seq, 1024, 1, [w_up[l]])
    anp, _ = _attention(qp, kp, vep, vop, p["out_g_attn"], seq, seq, seq, 4, [])
    ffn_w = (w_o_b, w_up_b, w_down_b)
    yp = _ffn(xp, anp, gmp, mods3, ctx_row, p, ffn_w, seq, tm_ffn)
    ys = _ffn(xs, ans, gms, mods3, lat_row(tm_ffn), p, ffn_w, dec_seq, tm_ffn)

    return (yp.reshape(batch, seq, D_MODEL), ys.reshape(dec_batch, dec_seq, D_MODEL),
            ckv_p.reshape(batch, 1, seq, KV_RANK), jnp.swapaxes(krt_p, 1, 2).reshape(batch, 1, seq, QK_ROPE))
```

```python
import functools

import numpy as np
import jax
import jax.numpy as jnp
from jax import lax
from jax.experimental import pallas as pl
from jax.experimental.pallas import tpu as pltpu

D_MODEL = 1024
GRID_W = 64
CHUNK = 128
N_HEADS = 8
QK_NOPE = 64
QK_ROPE = 32
V_DIM = 64
Q_RANK = 384
KV_RANK = 256
ATTN_W = N_HEADS * V_DIM
GMLP_W = D_MODEL - ATTN_W
GMLP_GROUPS = 4
GMLP_DG = GMLP_W // GMLP_GROUPS
D_FF = 2816
CONV_W = 3
ROPE_THETA = 10000.0
EPS = 1e-6
SM_SCALE = (QK_NOPE + QK_ROPE) ** -0.5
LOG2_E = 1.4426950408889634

LANES = 128
HEAD_PAD = LANES
ROPE_LANE0 = QK_NOPE
ROT_HALF = QK_ROPE // 4
QK_W = N_HEADS * HEAD_PAD
N_MODS = 6
MOD_ROWS = 8
ADALN_STEPS = 8
IN_QKV = Q_RANK + KV_RANK
FF_CHUNK = 256
HALO = 16
N_FF_CHUNKS = D_FF // FF_CHUNK
A_SLOTS = 4
V7X_VMEM_BYTES = 64 * 1024 * 1024
VMEM_LIMIT = V7X_VMEM_BYTES - 8 * 1024 * 1024

F32 = jnp.float32
BF16 = jnp.bfloat16


def _dot(a, b):
    return jnp.dot(a, b, preferred_element_type=F32)


def _dot_nt(a, b):
    return lax.dot_general(a, b, (((1,), (1,)), ((), ())), preferred_element_type=F32)


def _rms(x, g):
    return x * lax.rsqrt(jnp.mean(x * x, axis=-1, keepdims=True) + EPS) * g


def _gelu(x):
    return 0.5 * x * (1.0 + lax.erf(x * (0.5 ** 0.5)))


def _silu(x):
    return x * (1.0 / (1.0 + jnp.exp2(x * -LOG2_E)))


def _params(n_axes):
    return pltpu.CompilerParams(dimension_semantics=("arbitrary",) * n_axes,
                                vmem_limit_bytes=VMEM_LIMIT)


def _adaln_kernel(cctx_ref, c_ref, w_ref, b_ref, win_ref, wuq_ref, wukv_ref, ws_ref,
                  out_ref, wqkv_out, wuv_out, wuq_out, wk_out, wve_out, wvo_out, ws_out):
    pad = jnp.zeros((MOD_ROWS - 1 - c_ref.shape[0], c_ref.shape[1]), F32)
    cond = jnp.concatenate([cctx_ref[...], c_ref[...], pad], axis=0)
    s = _silu(cond).astype(BF16)

    @pl.when(pl.program_id(0) == 0)
    def _():
        out_ref[:, 0, :] = jnp.broadcast_to(b_ref[...], (MOD_ROWS, N_MODS * D_MODEL))

    out_ref[:, 0, :] += _dot(s, w_ref[...].astype(BF16))
    @pl.when(pl.program_id(0) == 0)
    def _():
        kr = win_ref[IN_QKV:IN_QKV + QK_ROPE, :].astype(BF16)
        z = jnp.zeros_like(kr)
        wqkv_out[0:IN_QKV, :] = win_ref[0:IN_QKV, :].astype(BF16)
        for t, blk in enumerate((kr, z, kr, z)):
            wqkv_out[IN_QKV + t * QK_ROPE:IN_QKV + (t + 1) * QK_ROPE, :] = blk
        wuv_out[...] = win_ref[IN_QKV + QK_ROPE:, :].astype(BF16)

    w = wuq_ref[...]
    hw = QK_NOPE + QK_ROPE
    zq = jnp.zeros((w.shape[0], HEAD_PAD - hw), F32)
    wuq_out[...] = jnp.concatenate([blk for hd in range(N_HEADS) for blk in (w[:, hd * hw:(hd + 1) * hw], zq)],
                                   axis=1).astype(BF16)
    w = wukv_ref[...]
    zv = jnp.zeros((w.shape[0], V_DIM), F32)
    v_of = lambda hd: w[:, hd * HEAD_PAD + QK_NOPE:(hd + 1) * HEAD_PAD]
    wk_out[...] = jnp.concatenate([blk for hd in range(N_HEADS)
                                   for blk in (w[:, hd * HEAD_PAD:hd * HEAD_PAD + QK_NOPE], zv)], axis=1).astype(BF16)
    wve_out[...] = jnp.concatenate([blk for hd in range(0, N_HEADS, 2) for blk in (v_of(hd), zv)],
                                   axis=1).astype(BF16)
    wvo_out[...] = jnp.concatenate([blk for hd in range(1, N_HEADS, 2) for blk in (zv, v_of(hd))],
                                   axis=1).astype(BF16)
    ws_out[...] = ws_ref[...].astype(BF16)


def _adaln(c_ctx, c, ada_w, ada_b, w_in_t, w_uq, w_ukv, w_s):
    kb = D_MODEL // ADALN_STEPS
    row_block = lambda a, width=None: pl.BlockSpec((a.shape[0] // ADALN_STEPS, width or a.shape[1]),
                                                   lambda j: (j, 0))
    whole = lambda shape: pl.BlockSpec(shape, lambda j: (0, 0), pipeline_mode=pl.Buffered(1))
    return pl.pallas_call(
        _adaln_kernel,
        grid=(ADALN_STEPS,),
        in_specs=[
            pl.BlockSpec((1, kb), lambda j: (0, j)),
            pl.BlockSpec((c.shape[0], kb), lambda j: (0, j)),
            pl.BlockSpec((kb, N_MODS * D_MODEL), lambda j: (j, 0)),
            pl.BlockSpec((1, N_MODS * D_MODEL), lambda j: (0, 0)),
            whole(w_in_t.shape), row_block(w_uq), row_block(w_ukv), row_block(w_s),
        ],
        out_specs=[pl.BlockSpec((MOD_ROWS, 1, N_MODS * D_MODEL), lambda j: (0, 0, 0)),
                   whole((IN_QKV + LANES, D_MODEL)), whole((2 * GMLP_W, D_MODEL)),
                   row_block(w_uq, QK_W), row_block(w_ukv, QK_W), row_block(w_ukv, ATTN_W),
                   row_block(w_ukv, ATTN_W), row_block(w_s)],
        out_shape=[jax.ShapeDtypeStruct((MOD_ROWS, 1, N_MODS * D_MODEL), F32),
                   jax.ShapeDtypeStruct((IN_QKV + LANES, D_MODEL), BF16),
                   jax.ShapeDtypeStruct((2 * GMLP_W, D_MODEL), BF16),
                   jax.ShapeDtypeStruct((Q_RANK, QK_W), BF16),
                   jax.ShapeDtypeStruct((KV_RANK, QK_W), BF16),
                   jax.ShapeDtypeStruct((KV_RANK, ATTN_W), BF16),
                   jax.ShapeDtypeStruct((KV_RANK, ATTN_W), BF16),
                   jax.ShapeDtypeStruct(w_s.shape, BF16)],
        compiler_params=_params(1),
        name="adaln",
    )(c_ctx, c, ada_w, ada_b, w_in_t, w_uq, w_ukv, w_s)


def _pre_kernel(*refs, latent, tm, seq, n_casts):
    (x_ref, mod_ref, n1g_ref, wqkv_ref, wuv_ref, qng_ref, wuq_ref, kvg_ref, wk_ref, wve_ref, wvo_ref,
     vng_ref, ws_ref, bs_ref, ogg_ref) = refs[:15]
    n_in = 15 + (6 if latent else 0)
    n_out = 5 + (0 if latent else 2)
    cast_in = refs[n_in:n_in + n_casts]
    outs = refs[n_in + n_casts:n_in + n_casts + n_out]
    cast_out = refs[n_in + n_casts + n_out:n_in + 2 * n_casts + n_out]
    go_sc = refs[-1]
    q_out, k_out, ve_out, vo_out, gm_out = outs[:5]
    for src, dst in zip(cast_in, cast_out):
        dst[...] = src[...].astype(BF16)
    shift1 = mod_ref[:, 0:D_MODEL]
    scale1 = mod_ref[:, D_MODEL:2 * D_MODEL]
    h = (_rms(x_ref[...], n1g_ref[...] * (1.0 + scale1)) + shift1).astype(BF16)

    lane = lax.broadcasted_iota(jnp.int32, (1, LANES), 1)
    rope_lanes = lane >= ROPE_LANE0
    first_half = (lane % (2 * ROT_HALF)) < ROT_HALF

    def expand(ckv_b, kr_k, rows):
        kn = _dot(ckv_b, wk_ref[...])
        for hd in range(N_HEADS):
            sl = slice(hd * HEAD_PAD, (hd + 1) * HEAD_PAD)
            k_out[rows, sl] = (kn[:, sl] + kr_k).astype(BF16)
        ve_out[rows, :] = _dot(ckv_b, wve_ref[...]).astype(BF16)
        vo_out[rows, :] = _dot(ckv_b, wvo_ref[...]).astype(BF16)

    qkv = _dot_nt(h, wqkv_ref[...])
    qn = _rms(qkv[:, 0:Q_RANK], qng_ref[...]).astype(BF16)
    ckv = _rms(qkv[:, Q_RANK:IN_QKV], kvg_ref[...])
    q = _dot(qn, wuq_ref[...])
    kr = qkv[:, IN_QKV:IN_QKV + LANES]
    if latent:
        cosq_ref, sinq_ref, cosk_ref, sink_ref, cckv_ref, ckrt_ref = refs[15:21]

        def rotate(v, cos_ref, sin_ref):
            swapped = jnp.where(first_half, pltpu.roll(v, LANES - ROT_HALF, 1), pltpu.roll(v, ROT_HALF, 1))
            return v * cos_ref[...] + swapped * sin_ref[...]

        kr_k = rotate(kr, cosk_ref, sink_ref)
        ckrt = ckrt_ref[...]
        past = ckrt.shape[1]
        padded = jnp.concatenate([ckrt, jnp.zeros((LANES - QK_ROPE, past), F32)], axis=0)
        expand(cckv_ref[...].astype(BF16), pltpu.roll(padded.T, ROPE_LANE0, 1), slice(0, past))
        own_rows = slice(past, past + tm)
    else:
        ckv_out, krt_out = outs[5:7]
        ckv_out[...] = ckv
        krt = kr.T
        for s in range(tm // seq):
            krt_out[s] = krt[0:QK_ROPE, s * seq:(s + 1) * seq]
        kr_k = jnp.where(rope_lanes, kr, 0.0)
        own_rows = slice(0, tm)
    expand(ckv.astype(BF16), kr_k, own_rows)
    for hd in range(N_HEADS):
        sl = slice(hd * HEAD_PAD, (hd + 1) * HEAD_PAD)
        qh = rotate(q[:, sl], cosq_ref, sinq_ref) if latent else q[:, sl] * (SM_SCALE * LOG2_E)
        q_out[:, sl] = qh.astype(BF16)

    uv = _dot_nt(h, wuv_ref[...])
    gu = _gelu(uv[:, 0:GMLP_W])
    gv = _gelu(uv[:, GMLP_W:2 * GMLP_W])
    eye = (lax.broadcasted_iota(jnp.int32, (CHUNK, CHUNK), 0)
           == lax.broadcasted_iota(jnp.int32, (CHUNK, CHUNK), 1))
    for g in range(GMLP_GROUPS):
        sl = slice(g * GMLP_DG, (g + 1) * GMLP_DG)
        vg = gv[:, sl]
        vg = (vg * lax.rsqrt(jnp.mean(vg * vg, axis=-1, keepdims=True) + EPS) * vng_ref[:, sl]).astype(BF16)
        bias = jnp.sum(jnp.where(eye, bs_ref[g:g + 1, :], 0.0), axis=1, keepdims=True)
        chunks = [vg[n * CHUNK:(n + 1) * CHUNK, :] for n in range(tm // CHUNK)]
        s_all = _dot(ws_ref[g], jnp.concatenate(chunks, axis=1))
        for n in range(tm // CHUNK):
            rows = slice(n * CHUNK, (n + 1) * CHUNK)
            go_sc[rows, sl] = gu[rows, sl] * (s_all[:, n * GMLP_DG:(n + 1) * GMLP_DG] + bias)
    gm_out[...] = _rms(go_sc[...], ogg_ref[...]).astype(BF16)


def _pre(x, mods3, mod_row, p, seq, tm, casts, latent=None):
    n = x.shape[0]
    const = lambda i: (0, 0)
    tile = lambda i: (i, 0)
    in_specs = [
        pl.BlockSpec((tm, D_MODEL), tile),
        pl.BlockSpec((None, 1, N_MODS * D_MODEL), lambda i: (mod_row(i), 0, 0)),
        pl.BlockSpec((1, D_MODEL), const),
        pl.BlockSpec((IN_QKV + LANES, D_MODEL), const),
        pl.BlockSpec((2 * GMLP_W, D_MODEL), const),
        pl.BlockSpec((1, Q_RANK), const),
        pl.BlockSpec((Q_RANK, QK_W), const),
        pl.BlockSpec((1, KV_RANK), const),
        pl.BlockSpec((KV_RANK, QK_W), const),
        pl.BlockSpec((KV_RANK, ATTN_W), const),
        pl.BlockSpec((KV_RANK, ATTN_W), const),
        pl.BlockSpec((1, GMLP_W), const),
        pl.BlockSpec((GMLP_GROUPS, CHUNK, CHUNK), lambda i: (0, 0, 0)),
        pl.BlockSpec((GMLP_GROUPS, CHUNK), const),
        pl.BlockSpec((1, GMLP_W), const),
    ]
    args = [x, mods3, p["norm1_g"], p["w_qkv"], p["w_uv"], p["q_norm_g"], p["w_uq"],
            p["kv_norm_g"], p["w_k"], p["w_ve"], p["w_vo"], p["v_norm_g"], p["w_s"], p["b_s"], p["out_g_gmlp"]]
    past = 0 if latent is None else latent[2].shape[2]
    kv_rows = tm + past
    out_rows = (tm, kv_rows, kv_rows, kv_rows, tm)
    widths = (QK_W, QK_W, ATTN_W, ATTN_W, GMLP_W)
    out_specs = [pl.BlockSpec((r, w), tile) for r, w in zip(out_rows, widths)]
    out_shape = [jax.ShapeDtypeStruct((n // tm * r, w), BF16) for r, w in zip(out_rows, widths)]
    if latent is not None:
        rope_tabs, cache_ckv, cache_krt = latent
        assert tm == seq == rope_tabs[0].shape[0]
        in_specs += [pl.BlockSpec((tm, LANES), const)] * len(rope_tabs) + [
            pl.BlockSpec((past, KV_RANK), tile),
            pl.BlockSpec((None, QK_ROPE, past), lambda i: (i, 0, 0))]
        args += [*rope_tabs, cache_ckv, cache_krt]
    else:
        out_specs += [pl.BlockSpec((tm, KV_RANK), tile),
                      pl.BlockSpec((tm // seq, QK_ROPE, seq), lambda i: (i, 0, 0))]
        out_shape += [jax.ShapeDtypeStruct((n, KV_RANK), F32),
                      jax.ShapeDtypeStruct((n // seq, QK_ROPE, seq), F32)]
    n_out = len(out_specs)
    cast_specs = [pl.BlockSpec((w.shape[0] // (n // tm), w.shape[1]), tile) for w in casts]
    outs = pl.pallas_call(
        functools.partial(_pre_kernel, latent=latent is not None, tm=tm, seq=seq, n_casts=len(casts)),
        grid=(n // tm,),
        in_specs=in_specs + cast_specs,
        out_specs=out_specs + cast_specs,
        out_shape=out_shape + [jax.ShapeDtypeStruct(w.shape, BF16) for w in casts],
        scratch_shapes=[pltpu.VMEM((tm, GMLP_W), F32)],
        compiler_params=_params(1),
        name="pre_latent" if latent is not None else "pre",
    )(*args, *casts)
    return outs[:n_out], outs[n_out:]


def _attn_kernel(*refs, group, kv_len, tq, n_casts):
    q_ref, k_ref, ve_ref, vo_ref, oga_ref = refs[:5]
    cast_in = refs[5:5 + n_casts]
    out_ref = refs[5 + n_casts]
    cast_out = refs[6 + n_casts:6 + 2 * n_casts]
    for src, dst in zip(cast_in, cast_out):
        dst[...] = src[...].astype(BF16)
    ones = jnp.ones((kv_len, LANES), BF16)
    for g in range(group):
        qrows = slice(g * tq, (g + 1) * tq)
        krows = slice(g * kv_len, (g + 1) * kv_len)
        pairs = []
        for pp in range(N_HEADS // 2):
            psl = slice(pp * LANES, (pp + 1) * LANES)
            acc = None
            for par, v_ref in enumerate((ve_ref, vo_ref)):
                hd = 2 * pp + par
                hsl = slice(hd * HEAD_PAD, (hd + 1) * HEAD_PAD)
                s = _dot_nt(q_ref[qrows, hsl], k_ref[krows, hsl])
                e = jnp.exp2(s - jnp.max(s, axis=-1, keepdims=True)).astype(BF16)
                o = _dot(e, jnp.concatenate([v_ref[krows, psl], ones], axis=1))
                o = o[:, 0:LANES] * (1.0 / o[:, LANES:2 * LANES])
                acc = o if acc is None else acc + o
            pairs.append(acc)
        ssq = sum(jnp.sum(b * b, axis=-1, keepdims=True) for b in pairs)
        r = lax.rsqrt(ssq * (1.0 / ATTN_W) + EPS)
        for j, b in enumerate(pairs):
            sl = slice(j * LANES, (j + 1) * LANES)
            out_ref[qrows, sl] = (b * r * oga_ref[:, sl]).astype(BF16)


def _attention(q, k, ve, vo, oga, seq, kv_len, tq, group, casts):
    n = q.shape[0]
    nq = seq // tq
    grid = (n // (group * seq), nq)
    linear = lambda b, i: (b * nq + i, 0)
    cast_specs = [pl.BlockSpec((w.shape[0] // (grid[0] * nq), w.shape[1]), linear) for w in casts]
    kv = lambda b, i: (b, 0)
    in_specs = [
        pl.BlockSpec((group * tq, QK_W), linear),
        pl.BlockSpec((group * kv_len, QK_W), kv),
        pl.BlockSpec((group * kv_len, ATTN_W), kv),
        pl.BlockSpec((group * kv_len, ATTN_W), kv),
        pl.BlockSpec((1, ATTN_W), lambda b, i: (0, 0)),
    ]
    outs = pl.pallas_call(
        functools.partial(_attn_kernel, group=group, kv_len=kv_len, tq=tq, n_casts=len(casts)),
        grid=grid,
        in_specs=in_specs + cast_specs,
        out_specs=[pl.BlockSpec((group * tq, ATTN_W), linear)] + cast_specs,
        out_shape=[jax.ShapeDtypeStruct((n, ATTN_W), BF16)]
        + [jax.ShapeDtypeStruct(w.shape, BF16) for w in casts],
        compiler_params=_params(2),
        name="attn",
    )(q, k, ve, vo, oga, *casts)
    return outs[0], outs[1:]


def _ffn_kernel(*refs, nseg, seg, tiles_per_seq):
    halo = tiles_per_seq > 1
    if halo:
        (x_ref, an_ref, gm_ref, xp_ref, xn_ref, anp_ref, ann_ref, gmp_ref, gmn_ref, mod_ref, wo_ref, n2g_ref,
         wup_ref, cw_ref, cb_ref, wd_ref, fg_ref, out_ref, lhs_sc, a_sc, g_sc, mix_sc) = refs
    else:
        (x_ref, an_ref, gm_ref, mod_ref, wo_ref, n2g_ref,
         wup_ref, cw_ref, cb_ref, wd_ref, fg_ref, out_ref, lhs_sc, a_sc, g_sc) = refs
    tm = nseg * seg
    stride = seg + HALO
    edge = HALO // 2
    gate1 = mod_ref[:, 2 * D_MODEL:3 * D_MODEL]
    shift2 = mod_ref[:, 3 * D_MODEL:4 * D_MODEL]
    scale2 = mod_ref[:, 4 * D_MODEL:5 * D_MODEL]

    def residual_norm(x, y, keep=None):
        x1 = x + gate1 * y
        h2 = _rms(x1, n2g_ref[...] * (1.0 + scale2)) + shift2
        if keep is not None:
            h2 = jnp.where(keep, h2, 0.0)
        return x1, h2.astype(BF16)

    if halo:
        def neighbours(prev_ref, next_ref):
            return jnp.concatenate([prev_ref[...].astype(F32)[edge:HALO, :],
                                    next_ref[...].astype(F32)[0:edge, :]], axis=0)

        mix_sc[0:tm, 0:ATTN_W] = an_ref[...]
        mix_sc[0:tm, ATTN_W:D_MODEL] = gm_ref[...]
        mix_sc[tm:tm + HALO, 0:ATTN_W] = neighbours(anp_ref, ann_ref).astype(BF16)
        mix_sc[tm:tm + HALO, ATTN_W:D_MODEL] = neighbours(gmp_ref, gmn_ref).astype(BF16)
        y = _dot(mix_sc[...], wo_ref[...])
        x1, h2 = residual_norm(x_ref[...], y[0:tm])
        out_ref[...] = x1
        lhs_sc[0:tm, :] = h2
        pos = pl.program_id(0) % tiles_per_seq
        is_prev = lax.broadcasted_iota(jnp.int32, (HALO, 1), 0) < edge
        has_prev = (pos != 0).astype(jnp.int32)
        has_next = (pos != tiles_per_seq - 1).astype(jnp.int32)
        keep = jnp.where(is_prev, has_prev, has_next) > 0
        lhs_sc[tm:tm + HALO, :] = residual_norm(neighbours(xp_ref, xn_ref), y[tm:tm + HALO], keep)[1]
    else:
        @pl.when(pl.program_id(0) == 0)
        def _():
            zeros = jnp.zeros((A_SLOTS, 2 * FF_CHUNK // LANES, HALO, LANES), F32)
            a_sc[:, :, 0:edge, :] = zeros[:, :, 0:edge, :]
            for s in range(1, nseg):
                a_sc[:, :, s * stride - edge:s * stride + edge, :] = zeros
            a_sc[:, :, nseg * stride - edge:nseg * stride, :] = zeros[:, :, 0:edge, :]

        y = _dot(an_ref[...], wo_ref[0:ATTN_W, :]) + _dot(gm_ref[...], wo_ref[ATTN_W:D_MODEL, :])
        x1, h2 = residual_norm(x_ref[...], y)
        out_ref[...] = x1
        lhs_sc[...] = h2

    for c in range(N_FF_CHUNKS):
        slot = c % A_SLOTS
        for k, col0 in enumerate((c * FF_CHUNK, D_FF + c * FF_CHUNK)):
            a = _dot(lhs_sc[...], wup_ref[:, col0:col0 + FF_CHUNK])
            for j in range(FF_CHUNK // LANES):
                lanes = slice(j * LANES, (j + 1) * LANES)
                for s in range(nseg):
                    row0 = edge + s * stride
                    a_sc[slot, 2 * k + j, row0:row0 + seg, :] = a[s * seg:(s + 1) * seg, lanes]
                if halo:
                    a_sc[slot, 2 * k + j, 0:edge, :] = a[tm:tm + edge, lanes]
                    a_sc[slot, 2 * k + j, edge + seg:HALO + seg, :] = a[tm + edge:tm + HALO, lanes]
        for j in range(FF_CHUNK // LANES):
            gl = slice(c * FF_CHUNK + j * LANES, c * FF_CHUNK + (j + 1) * LANES)
            vl = slice(D_FF + gl.start, D_FF + gl.stop)
            for s in range(nseg):
                base = edge + s * stride

                def conv(k, lanes):
                    acc = cb_ref[:, lanes]
                    for t in range(CONV_W):
                        tap = slice(t * 2 * D_FF + lanes.start, t * 2 * D_FF + lanes.stop)
                        acc = acc + a_sc[slot, k, base - 1 + t:base - 1 + t + seg, :] * cw_ref[:, tap]
                    return acc

                g = _silu(conv(j, gl)) * conv(2 + j, vl)
                g_sc[s * seg:(s + 1) * seg, gl] = g.astype(BF16)

    x2 = out_ref[...] + mod_ref[:, 5 * D_MODEL:6 * D_MODEL] * _dot(g_sc[...], wd_ref[...])
    out_ref[...] = _rms(x2, fg_ref[...])


def _ffn(x, an, gm, mods3, mod_row, p, ffn_w, seq, tm):
    n = x.shape[0]
    tiles_per_seq = max(seq // tm, 1)
    seg = min(seq, tm)
    nseg = tm // seg
    rows = tm + (HALO if tiles_per_seq > 1 else 0)
    a_rows = nseg * (seg + HALO)
    const = lambda i: (0, 0)
    tile = lambda i: (i, 0)
    resident = dict(pipeline_mode=pl.Buffered(1))
    in_specs = [pl.BlockSpec((tm, D_MODEL), tile), pl.BlockSpec((tm, ATTN_W), tile),
                pl.BlockSpec((tm, GMLP_W), tile)]
    args = [x, an, gm]
    scratch = [pltpu.VMEM((rows, D_MODEL), BF16),
               pltpu.VMEM((A_SLOTS, 2 * FF_CHUNK // LANES, a_rows, LANES), F32),
               pltpu.VMEM((tm, D_FF), BF16)]
    if tiles_per_seq > 1:
        per = tm // HALO
        last = n // HALO - 1
        prev = lambda i: (jnp.maximum(i * per - 1, 0), 0)
        nxt = lambda i: (jnp.minimum((i + 1) * per, last), 0)
        for arr, width in ((x, D_MODEL), (an, ATTN_W), (gm, GMLP_W)):
            in_specs += [pl.BlockSpec((HALO, width), prev), pl.BlockSpec((HALO, width), nxt)]
            args += [arr, arr]
        scratch.append(pltpu.VMEM((rows, D_MODEL), BF16))
    in_specs += [
        pl.BlockSpec((None, 1, N_MODS * D_MODEL), lambda i: (mod_row(i), 0, 0)),
        pl.BlockSpec((D_MODEL, D_MODEL), const, **resident),
        pl.BlockSpec((1, D_MODEL), const),
        pl.BlockSpec((D_MODEL, 2 * D_FF), const, **resident),
        pl.BlockSpec((1, CONV_W * 2 * D_FF), const),
        pl.BlockSpec((1, 2 * D_FF), const),
        pl.BlockSpec((D_FF, D_MODEL), const, **resident),
        pl.BlockSpec((1, D_MODEL), const),
    ]
    w_o, w_up, w_down = ffn_w
    args += [mods3, w_o, p["norm2_g"], w_up, p["conv_w"], p["conv_b"], w_down, p["final_g"]]
    return pl.pallas_call(
        functools.partial(_ffn_kernel, nseg=nseg, seg=seg, tiles_per_seq=tiles_per_seq),
        grid=(n // tm,),
        in_specs=in_specs,
        out_specs=pl.BlockSpec((tm, D_MODEL), tile),
        out_shape=jax.ShapeDtypeStruct((n, D_MODEL), F32),
        scratch_shapes=scratch,
        compiler_params=_params(1),
        name="ffn_halo" if tiles_per_seq > 1 else "ffn",
    )(*args)


def _rope_tables(length):
    pos = np.arange(length)
    row = (pos // GRID_W).astype(np.float32)
    col = (pos % GRID_W).astype(np.float32)
    n_freq = QK_ROPE // 4
    inv = (np.float32(ROPE_THETA) ** (-(np.arange(n_freq, dtype=np.float32) / np.float32(n_freq)))).astype(np.float32)
    ang_r, ang_c = row[:, None] * inv, col[:, None] * inv
    cos32 = np.concatenate([np.cos(ang_r)] * 2 + [np.cos(ang_c)] * 2, axis=-1)
    sin32 = np.concatenate([-np.sin(ang_r), np.sin(ang_r), -np.sin(ang_c), np.sin(ang_c)], axis=-1)
    tail = LANES - ROPE_LANE0 - QK_ROPE
    block = lambda nope, rope: np.concatenate([np.full((length, ROPE_LANE0), nope), rope,
                                               np.zeros((length, tail))], -1)
    scale = SM_SCALE * LOG2_E
    tables = (block(1.0, cos32) * scale, block(0.0, sin32) * scale, block(0.0, cos32), block(0.0, sin32))
    return tuple(jnp.asarray(t, F32) for t in tables)


def _layer_params(l, norm1_g, q_norm_g, kv_norm_g, v_norm_g, b_s, out_g_attn,
                  out_g_gmlp, norm2_g, conv_w, conv_b, final_g):
    row = lambda a: a.reshape(1, -1)
    return {
        "norm1_g": row(norm1_g[l]), "q_norm_g": row(q_norm_g[l]),
        "kv_norm_g": row(kv_norm_g[l]), "v_norm_g": row(v_norm_g[l]), "b_s": b_s[l],
        "out_g_attn": row(out_g_attn[l]), "out_g_gmlp": row(out_g_gmlp[l]), "norm2_g": row(norm2_g[l]),
        "conv_w": conv_w[l].reshape(1, CONV_W * 2 * D_FF), "conv_b": row(conv_b[l]), "final_g": row(final_g),
    }


def kernel(x_prompt, x_sample, cache_ckv, cache_krope, c, c_ctx, ada_w, ada_b, norm1_g, w_in, q_norm_g, w_uq, kv_norm_g, w_ukv, v_norm_g, w_s, b_s, out_g_attn, out_g_gmlp, w_o, norm2_g, w_up, conv_w, conv_b, w_down, final_g):
    batch, seq, _ = x_prompt.shape
    dec_batch, dec_seq, _ = x_sample.shape
    depth, past = cache_ckv.shape[1], cache_ckv.shape[2]
    assert depth == 1 and dec_batch + 1 <= MOD_ROWS
    l = 0
    p = _layer_params(l, norm1_g, q_norm_g, kv_norm_g, v_norm_g, b_s, out_g_attn,
                      out_g_gmlp, norm2_g, conv_w, conv_b, final_g)

    mods3, p["w_qkv"], p["w_uv"], p["w_uq"], p["w_k"], p["w_ve"], p["w_vo"], w_s_b = _adaln(
        c_ctx.reshape(1, D_MODEL), c, ada_w[l], ada_b[l].reshape(1, -1), w_in[l].T, w_uq[l], w_ukv[l],
        w_s[l].reshape(GMLP_GROUPS * CHUNK, CHUNK))
    p["w_s"] = w_s_b.reshape(GMLP_GROUPS, CHUNK, CHUNK)
    tm_pre, tm_ffn = 1024, 512
    ctx_row = lambda i: 0
    lat_row = lambda tm: lambda i: 1 + i // (dec_seq // tm)

    xp = x_prompt.reshape(batch * seq, D_MODEL)
    xs = x_sample.reshape(dec_batch * dec_seq, D_MODEL)
    (qp, kp, vep, vop, gmp, ckv_p, krt_p), (w_down_b,) = _pre(xp, mods3, ctx_row, p, seq, tm_pre, [w_down[l]])
    cache_in = (_rope_tables(dec_seq), cache_ckv[:, l].reshape(dec_batch * past, KV_RANK),
                jnp.swapaxes(cache_krope[:, l], 1, 2))
    (qs, ks, ves, vos, gms), (w_o_b,) = _pre(xs, mods3, lat_row(tm_pre), p, dec_seq, tm_pre, [w_o[l]],
                                             latent=cache_in)

    ans, (w_up_b,) = _attention(qs, ks, ves, vos, p["out_g_attn"], dec_seq, past + dec_seq, 1024, 1, [w_up[l]])
    anp, _ = _attention(qp, kp, vep, vop, p["out_g_attn"], seq, seq, seq, 4, [])
    ffn_w = (w_o_b, w_up_b, w_down_b)
    yp = _ffn(xp, anp, gmp, mods3, ctx_row, p, ffn_w, seq, tm_ffn)
    ys = _ffn(xs, ans, gms, mods3, lat_row(tm_ffn), p, ffn_w, dec_seq, tm_ffn)

    return (yp.reshape(batch, seq, D_MODEL), ys.reshape(dec_batch, dec_seq, D_MODEL),
            ckv_p.reshape(batch, 1, seq, KV_RANK), jnp.swapaxes(krt_p, 1, 2).reshape(batch, 1, seq, QK_ROPE))
```

```python
import functools

import numpy as np
import jax
import jax.numpy as jnp
from jax import lax
from jax.experimental import pallas as pl
from jax.experimental.pallas import tpu as pltpu

D_MODEL = 1024
GRID_W = 64
CHUNK = 128
N_HEADS = 8
QK_NOPE = 64
QK_ROPE = 32
V_DIM = 64
Q_RANK = 384
KV_RANK = 256
ATTN_W = N_HEADS * V_DIM
GMLP_W = D_MODEL - ATTN_W
GMLP_GROUPS = 4
GMLP_DG = GMLP_W // GMLP_GROUPS
D_FF = 2816
CONV_W = 3
ROPE_THETA = 10000.0
EPS = 1e-6
SM_SCALE = (QK_NOPE + QK_ROPE) ** -0.5
LOG2_E = 1.4426950408889634

LANES = 128
HEAD_PAD = LANES
ROPE_LANE0 = QK_NOPE
ROT_HALF = QK_ROPE // 4
QK_W = N_HEADS * HEAD_PAD
N_MODS = 6
MOD_ROWS = 8
ADALN_STEPS = 4
IN_QKV = Q_RANK + KV_RANK
FF_CHUNK = 256
HALO = 16
N_FF_CHUNKS = D_FF // FF_CHUNK
A_SLOTS = 4
V7X_VMEM_BYTES = 64 * 1024 * 1024
VMEM_LIMIT = V7X_VMEM_BYTES - 8 * 1024 * 1024

F32 = jnp.float32
BF16 = jnp.bfloat16


def _dot(a, b):
    return jnp.dot(a, b, preferred_element_type=F32)


def _dot_nt(a, b):
    return lax.dot_general(a, b, (((1,), (1,)), ((), ())), preferred_element_type=F32)


def _rms(x, g):
    return x * lax.rsqrt(jnp.mean(x * x, axis=-1, keepdims=True) + EPS) * g


def _gelu(x):
    return 0.5 * x * (1.0 + lax.erf(x * (0.5 ** 0.5)))


def _silu(x):
    return x * (1.0 / (1.0 + jnp.exp2(x * -LOG2_E)))


def _params(n_axes):
    return pltpu.CompilerParams(dimension_semantics=("arbitrary",) * n_axes,
                                vmem_limit_bytes=VMEM_LIMIT)


def _adaln_kernel(cctx_ref, c_ref, w_ref, b_ref, win_ref, wuq_ref, wukv_ref, ws_ref,
                  out_ref, wqkv_out, wuv_out, wuq_out, wk_out, wve_out, wvo_out, ws_out):
    pad = jnp.zeros((MOD_ROWS - 1 - c_ref.shape[0], c_ref.shape[1]), F32)
    cond = jnp.concatenate([cctx_ref[...], c_ref[...], pad], axis=0)
    s = _silu(cond).astype(BF16)

    @pl.when(pl.program_id(0) == 0)
    def _():
        out_ref[:, 0, :] = jnp.broadcast_to(b_ref[...], (MOD_ROWS, N_MODS * D_MODEL))

    out_ref[:, 0, :] += _dot(s, w_ref[...].astype(BF16))
    @pl.when(pl.program_id(0) == 0)
    def _():
        kr = win_ref[IN_QKV:IN_QKV + QK_ROPE, :].astype(BF16)
        z = jnp.zeros_like(kr)
        wqkv_out[0:IN_QKV, :] = win_ref[0:IN_QKV, :].astype(BF16)
        for t, blk in enumerate((kr, z, kr, z)):
            wqkv_out[IN_QKV + t * QK_ROPE:IN_QKV + (t + 1) * QK_ROPE, :] = blk
        wuv_out[...] = win_ref[IN_QKV + QK_ROPE:, :].astype(BF16)

    w = wuq_ref[...]
    hw = QK_NOPE + QK_ROPE
    zq = jnp.zeros((w.shape[0], HEAD_PAD - hw), F32)
    wuq_out[...] = jnp.concatenate([blk for hd in range(N_HEADS) for blk in (w[:, hd * hw:(hd + 1) * hw], zq)],
                                   axis=1).astype(BF16)
    w = wukv_ref[...]
    zv = jnp.zeros((w.shape[0], V_DIM), F32)
    v_of = lambda hd: w[:, hd * HEAD_PAD + QK_NOPE:(hd + 1) * HEAD_PAD]
    wk_out[...] = jnp.concatenate([blk for hd in range(N_HEADS)
                                   for blk in (w[:, hd * HEAD_PAD:hd * HEAD_PAD + QK_NOPE], zv)], axis=1).astype(BF16)
    wve_out[...] = jnp.concatenate([blk for hd in range(0, N_HEADS, 2) for blk in (v_of(hd), zv)],
                                   axis=1).astype(BF16)
    wvo_out[...] = jnp.concatenate([blk for hd in range(1, N_HEADS, 2) for blk in (zv, v_of(hd))],
                                   axis=1).astype(BF16)
    ws_out[...] = ws_ref[...].astype(BF16)


def _adaln(c_ctx, c, ada_w, ada_b, w_in_t, w_uq, w_ukv, w_s):
    kb = D_MODEL // ADALN_STEPS
    row_block = lambda a, width=None: pl.BlockSpec((a.shape[0] // ADALN_STEPS, width or a.shape[1]),
                                                   lambda j: (j, 0))
    whole = lambda shape: pl.BlockSpec(shape, lambda j: (0, 0), pipeline_mode=pl.Buffered(1))
    return pl.pallas_call(
        _adaln_kernel,
        grid=(ADALN_STEPS,),
        in_specs=[
            pl.BlockSpec((1, kb), lambda j: (0, j)),
            pl.BlockSpec((c.shape[0], kb), lambda j: (0, j)),
            pl.BlockSpec((kb, N_MODS * D_MODEL), lambda j: (j, 0)),
            pl.BlockSpec((1, N_MODS * D_MODEL), lambda j: (0, 0)),
            whole(w_in_t.shape), row_block(w_uq), row_block(w_ukv), row_block(w_s),
        ],
        out_specs=[pl.BlockSpec((MOD_ROWS, 1, N_MODS * D_MODEL), lambda j: (0, 0, 0)),
                   whole((IN_QKV + LANES, D_MODEL)), whole((2 * GMLP_W, D_MODEL)),
                   row_block(w_uq, QK_W), row_block(w_ukv, QK_W), row_block(w_ukv, ATTN_W),
                   row_block(w_ukv, ATTN_W), row_block(w_s)],
        out_shape=[jax.ShapeDtypeStruct((MOD_ROWS, 1, N_MODS * D_MODEL), F32),
                   jax.ShapeDtypeStruct((IN_QKV + LANES, D_MODEL), BF16),
                   jax.ShapeDtypeStruct((2 * GMLP_W, D_MODEL), BF16),
                   jax.ShapeDtypeStruct((Q_RANK, QK_W), BF16),
                   jax.ShapeDtypeStruct((KV_RANK, QK_W), BF16),
                   jax.ShapeDtypeStruct((KV_RANK, ATTN_W), BF16),
                   jax.ShapeDtypeStruct((KV_RANK, ATTN_W), BF16),
                   jax.ShapeDtypeStruct(w_s.shape, BF16)],
        compiler_params=_params(1),
        name="adaln",
    )(c_ctx, c, ada_w, ada_b, w_in_t, w_uq, w_ukv, w_s)


def _pre_kernel(*refs, latent, tm, seq, n_casts):
    (x_ref, mod_ref, n1g_ref, wqkv_ref, wuv_ref, qng_ref, wuq_ref, kvg_ref, wk_ref, wve_ref, wvo_ref,
     vng_ref, ws_ref, bs_ref, ogg_ref) = refs[:15]
    n_in = 15 + (6 if latent else 0)
    n_out = 5 + (0 if latent else 2)
    cast_in = refs[n_in:n_in + n_casts]
    outs = refs[n_in + n_casts:n_in + n_casts + n_out]
    cast_out = refs[n_in + n_casts + n_out:n_in + 2 * n_casts + n_out]
    go_sc = refs[-1]
    q_out, k_out, ve_out, vo_out, gm_out = outs[:5]
    for src, dst in zip(cast_in, cast_out):
        dst[...] = src[...].astype(BF16)
    shift1 = mod_ref[:, 0:D_MODEL]
    scale1 = mod_ref[:, D_MODEL:2 * D_MODEL]
    h = (_rms(x_ref[...], n1g_ref[...] * (1.0 + scale1)) + shift1).astype(BF16)

    lane = lax.broadcasted_iota(jnp.int32, (1, LANES), 1)
    rope_lanes = lane >= ROPE_LANE0
    first_half = (lane % (2 * ROT_HALF)) < ROT_HALF

    def expand(ckv_b, kr_k, rows):
        kn = _dot(ckv_b, wk_ref[...])
        for hd in range(N_HEADS):
            sl = slice(hd * HEAD_PAD, (hd + 1) * HEAD_PAD)
            k_out[rows, sl] = (kn[:, sl] + kr_k).astype(BF16)
        ve_out[rows, :] = _dot(ckv_b, wve_ref[...]).astype(BF16)
        vo_out[rows, :] = _dot(ckv_b, wvo_ref[...]).astype(BF16)

    qkv = _dot_nt(h, wqkv_ref[...])
    qn = _rms(qkv[:, 0:Q_RANK], qng_ref[...]).astype(BF16)
    ckv = _rms(qkv[:, Q_RANK:IN_QKV], kvg_ref[...])
    q = _dot(qn, wuq_ref[...])
    kr = qkv[:, IN_QKV:IN_QKV + LANES]
    if latent:
        cosq_ref, sinq_ref, cosk_ref, sink_ref, cckv_ref, ckrt_ref = refs[15:21]

        def rotate(v, cos_ref, sin_ref):
            swapped = jnp.where(first_half, pltpu.roll(v, LANES - ROT_HALF, 1), pltpu.roll(v, ROT_HALF, 1))
            return v * cos_ref[...] + swapped * sin_ref[...]

        kr_k = rotate(kr, cosk_ref, sink_ref)
        ckrt = ckrt_ref[...]
        past = ckrt.shape[1]
        padded = jnp.concatenate([ckrt, jnp.zeros((LANES - QK_ROPE, past), F32)], axis=0)
        expand(cckv_ref[...].astype(BF16), pltpu.roll(padded.T, ROPE_LANE0, 1), slice(0, past))
        own_rows = slice(past, past + tm)
    else:
        ckv_out, krt_out = outs[5:7]
        ckv_out[...] = ckv
        krt = kr.T
        for s in range(tm // seq):
            krt_out[s] = krt[0:QK_ROPE, s * seq:(s + 1) * seq]
        kr_k = jnp.where(rope_lanes, kr, 0.0)
        own_rows = slice(0, tm)
    expand(ckv.astype(BF16), kr_k, own_rows)
    for hd in range(N_HEADS):
        sl = slice(hd * HEAD_PAD, (hd + 1) * HEAD_PAD)
        qh = rotate(q[:, sl], cosq_ref, sinq_ref) if latent else q[:, sl] * (SM_SCALE * LOG2_E)
        q_out[:, sl] = qh.astype(BF16)

    uv = _dot_nt(h, wuv_ref[...])
    gu = _gelu(uv[:, 0:GMLP_W])
    gv = _gelu(uv[:, GMLP_W:2 * GMLP_W])
    eye = (lax.broadcasted_iota(jnp.int32, (CHUNK, CHUNK), 0)
           == lax.broadcasted_iota(jnp.int32, (CHUNK, CHUNK), 1))
    for g in range(GMLP_GROUPS):
        sl = slice(g * GMLP_DG, (g + 1) * GMLP_DG)
        vg = gv[:, sl]
        vg = (vg * lax.rsqrt(jnp.mean(vg * vg, axis=-1, keepdims=True) + EPS) * vng_ref[:, sl]).astype(BF16)
        bias = jnp.sum(jnp.where(eye, bs_ref[g:g + 1, :], 0.0), axis=1, keepdims=True)
        chunks = [vg[n * CHUNK:(n + 1) * CHUNK, :] for n in range(tm // CHUNK)]
        s_all = _dot(ws_ref[g], jnp.concatenate(chunks, axis=1))
        for n in range(tm // CHUNK):
            rows = slice(n * CHUNK, (n + 1) * CHUNK)
            go_sc[rows, sl] = gu[rows, sl] * (s_all[:, n * GMLP_DG:(n + 1) * GMLP_DG] + bias)
    gm_out[...] = _rms(go_sc[...], ogg_ref[...]).astype(BF16)


def _pre(x, mods3, mod_row, p, seq, tm, casts, latent=None):
    n = x.shape[0]
    const = lambda i: (0, 0)
    tile = lambda i: (i, 0)
    in_specs = [
        pl.BlockSpec((tm, D_MODEL), tile),
        pl.BlockSpec((None, 1, N_MODS * D_MODEL), lambda i: (mod_row(i), 0, 0)),
        pl.BlockSpec((1, D_MODEL), const),
        pl.BlockSpec((IN_QKV + LANES, D_MODEL), const),
        pl.BlockSpec((2 * GMLP_W, D_MODEL), const),
        pl.BlockSpec((1, Q_RANK), const),
        pl.BlockSpec((Q_RANK, QK_W), const),
        pl.BlockSpec((1, KV_RANK), const),
        pl.BlockSpec((KV_RANK, QK_W), const),
        pl.BlockSpec((KV_RANK, ATTN_W), const),
        pl.BlockSpec((KV_RANK, ATTN_W), const),
        pl.BlockSpec((1, GMLP_W), const),
        pl.BlockSpec((GMLP_GROUPS, CHUNK, CHUNK), lambda i: (0, 0, 0)),
        pl.BlockSpec((GMLP_GROUPS, CHUNK), const),
        pl.BlockSpec((1, GMLP_W), const),
    ]
    args = [x, mods3, p["norm1_g"], p["w_qkv"], p["w_uv"], p["q_norm_g"], p["w_uq"],
            p["kv_norm_g"], p["w_k"], p["w_ve"], p["w_vo"], p["v_norm_g"], p["w_s"], p["b_s"], p["out_g_gmlp"]]
    past = 0 if latent is None else latent[2].shape[2]
    kv_rows = tm + past
    out_rows = (tm, kv_rows, kv_rows, kv_rows, tm)
    widths = (QK_W, QK_W, ATTN_W, ATTN_W, GMLP_W)
    out_specs = [pl.BlockSpec((r, w), tile) for r, w in zip(out_rows, widths)]
    out_shape = [jax.ShapeDtypeStruct((n // tm * r, w), BF16) for r, w in zip(out_rows, widths)]
    if latent is not None:
        rope_tabs, cache_ckv, cache_krt = latent
        assert tm == seq == rope_tabs[0].shape[0]
        in_specs += [pl.BlockSpec((tm, LANES), const)] * len(rope_tabs) + [
            pl.BlockSpec((past, KV_RANK), tile),
            pl.BlockSpec((None, QK_ROPE, past), lambda i: (i, 0, 0))]
        args += [*rope_tabs, cache_ckv, cache_krt]
    else:
        out_specs += [pl.BlockSpec((tm, KV_RANK), tile),
                      pl.BlockSpec((tm // seq, QK_ROPE, seq), lambda i: (i, 0, 0))]
        out_shape += [jax.ShapeDtypeStruct((n, KV_RANK), F32),
                      jax.ShapeDtypeStruct((n // seq, QK_ROPE, seq), F32)]
    n_out = len(out_specs)
    cast_specs = [pl.BlockSpec((w.shape[0] // (n // tm), w.shape[1]), tile) for w in casts]
    outs = pl.pallas_call(
        functools.partial(_pre_kernel, latent=latent is not None, tm=tm, seq=seq, n_casts=len(casts)),
        grid=(n // tm,),
        in_specs=in_specs + cast_specs,
        out_specs=out_specs + cast_specs,
        out_shape=out_shape + [jax.ShapeDtypeStruct(w.shape, BF16) for w in casts],
        scratch_shapes=[pltpu.VMEM((tm, GMLP_W), F32)],
        compiler_params=_params(1),
        name="pre_latent" if latent is not None else "pre",
    )(*args, *casts)
    return outs[:n_out], outs[n_out:]


def _attn_kernel(*refs, group, kv_len, tq, n_casts):
    q_ref, k_ref, ve_ref, vo_ref, oga_ref = refs[:5]
    cast_in = refs[5:5 + n_casts]
    out_ref = refs[5 + n_casts]
    cast_out = refs[6 + n_casts:6 + 2 * n_casts]
    for src, dst in zip(cast_in, cast_out):
        dst[...] = src[...].astype(BF16)
    ones = jnp.ones((kv_len, LANES), BF16)
    for g in range(group):
        qrows = slice(g * tq, (g + 1) * tq)
        krows = slice(g * kv_len, (g + 1) * kv_len)
        pairs = []
        for pp in range(N_HEADS // 2):
            psl = slice(pp * LANES, (pp + 1) * LANES)
            acc = None
            for par, v_ref in enumerate((ve_ref, vo_ref)):
                hd = 2 * pp + par
                hsl = slice(hd * HEAD_PAD, (hd + 1) * HEAD_PAD)
                s = _dot_nt(q_ref[qrows, hsl], k_ref[krows, hsl])
                e = jnp.exp2(s - jnp.max(s, axis=-1, keepdims=True)).astype(BF16)
                o = _dot(e, jnp.concatenate([v_ref[krows, psl], ones], axis=1))
                o = o[:, 0:LANES] * (1.0 / o[:, LANES:2 * LANES])
                acc = o if acc is None else acc + o
            pairs.append(acc)
        ssq = sum(jnp.sum(b * b, axis=-1, keepdims=True) for b in pairs)
        r = lax.rsqrt(ssq * (1.0 / ATTN_W) + EPS)
        for j, b in enumerate(pairs):
            sl = slice(j * LANES, (j + 1) * LANES)
            out_ref[qrows, sl] = (b * r * oga_ref[:, sl]).astype(BF16)


def _attention(q, k, ve, vo, oga, seq, kv_len, tq, group, casts):
    n = q.shape[0]
    nq = seq // tq
    grid = (n // (group * seq), nq)
    linear = lambda b, i: (b * nq + i, 0)
    cast_specs = [pl.BlockSpec((w.shape[0] // (grid[0] * nq), w.shape[1]), linear) for w in casts]
    kv = lambda b, i: (b, 0)
    in_specs = [
        pl.BlockSpec((group * tq, QK_W), linear),
        pl.BlockSpec((group * kv_len, QK_W), kv),
        pl.BlockSpec((group * kv_len, ATTN_W), kv),
        pl.BlockSpec((group * kv_len, ATTN_W), kv),
        pl.BlockSpec((1, ATTN_W), lambda b, i: (0, 0)),
    ]
    outs = pl.pallas_call(
        functools.partial(_attn_kernel, group=group, kv_len=kv_len, tq=tq, n_casts=len(casts)),
        grid=grid,
        in_specs=in_specs + cast_specs,
        out_specs=[pl.BlockSpec((group * tq, ATTN_W), linear)] + cast_specs,
        out_shape=[jax.ShapeDtypeStruct((n, ATTN_W), BF16)]
        + [jax.ShapeDtypeStruct(w.shape, BF16) for w in casts],
        compiler_params=_params(2),
        name="attn",
    )(q, k, ve, vo, oga, *casts)
    return outs[0], outs[1:]


def _ffn_kernel(*refs, nseg, seg, tiles_per_seq):
    halo = tiles_per_seq > 1
    if halo:
        (x_ref, an_ref, gm_ref, xp_ref, xn_ref, anp_ref, ann_ref, gmp_ref, gmn_ref, mod_ref, wo_ref, n2g_ref,
         wup_ref, cw_ref, cb_ref, wd_ref, fg_ref, out_ref, lhs_sc, a_sc, g_sc, mix_sc) = refs
    else:
        (x_ref, an_ref, gm_ref, mod_ref, wo_ref, n2g_ref,
         wup_ref, cw_ref, cb_ref, wd_ref, fg_ref, out_ref, lhs_sc, a_sc, g_sc) = refs
    tm = nseg * seg
    stride = seg + HALO
    edge = HALO // 2
    gate1 = mod_ref[:, 2 * D_MODEL:3 * D_MODEL]
    shift2 = mod_ref[:, 3 * D_MODEL:4 * D_MODEL]
    scale2 = mod_ref[:, 4 * D_MODEL:5 * D_MODEL]

    def residual_norm(x, y, keep=None):
        x1 = x + gate1 * y
        h2 = _rms(x1, n2g_ref[...] * (1.0 + scale2)) + shift2
        if keep is not None:
            h2 = jnp.where(keep, h2, 0.0)
        return x1, h2.astype(BF16)

    if halo:
        def neighbours(prev_ref, next_ref):
            return jnp.concatenate([prev_ref[...].astype(F32)[edge:HALO, :],
                                    next_ref[...].astype(F32)[0:edge, :]], axis=0)

        mix_sc[0:tm, 0:ATTN_W] = an_ref[...]
        mix_sc[0:tm, ATTN_W:D_MODEL] = gm_ref[...]
        mix_sc[tm:tm + HALO, 0:ATTN_W] = neighbours(anp_ref, ann_ref).astype(BF16)
        mix_sc[tm:tm + HALO, ATTN_W:D_MODEL] = neighbours(gmp_ref, gmn_ref).astype(BF16)
        y = _dot(mix_sc[...], wo_ref[...])
        x1, h2 = residual_norm(x_ref[...], y[0:tm])
        out_ref[...] = x1
        lhs_sc[0:tm, :] = h2
        pos = pl.program_id(0) % tiles_per_seq
        is_prev = lax.broadcasted_iota(jnp.int32, (HALO, 1), 0) < edge
        has_prev = (pos != 0).astype(jnp.int32)
        has_next = (pos != tiles_per_seq - 1).astype(jnp.int32)
        keep = jnp.where(is_prev, has_prev, has_next) > 0
        lhs_sc[tm:tm + HALO, :] = residual_norm(neighbours(xp_ref, xn_ref), y[tm:tm + HALO], keep)[1]
    else:
        @pl.when(pl.program_id(0) == 0)
        def _():
            zeros = jnp.zeros((A_SLOTS, 2 * FF_CHUNK // LANES, HALO, LANES), F32)
            a_sc[:, :, 0:edge, :] = zeros[:, :, 0:edge, :]
            for s in range(1, nseg):
                a_sc[:, :, s * stride - edge:s * stride + edge, :] = zeros
            a_sc[:, :, nseg * stride - edge:nseg * stride, :] = zeros[:, :, 0:edge, :]

        y = _dot(an_ref[...], wo_ref[0:ATTN_W, :]) + _dot(gm_ref[...], wo_ref[ATTN_W:D_MODEL, :])
        x1, h2 = residual_norm(x_ref[...], y)
        out_ref[...] = x1
        lhs_sc[...] = h2

    for c in range(N_FF_CHUNKS):
        slot = c % A_SLOTS
        for k, col0 in enumerate((c * FF_CHUNK, D_FF + c * FF_CHUNK)):
            a = _dot(lhs_sc[...], wup_ref[:, col0:col0 + FF_CHUNK])
            for j in range(FF_CHUNK // LANES):
                lanes = slice(j * LANES, (j + 1) * LANES)
                for s in range(nseg):
                    row0 = edge + s * stride
                    a_sc[slot, 2 * k + j, row0:row0 + seg, :] = a[s * seg:(s + 1) * seg, lanes]
                if halo:
                    a_sc[slot, 2 * k + j, 0:edge, :] = a[tm:tm + edge, lanes]
                    a_sc[slot, 2 * k + j, edge + seg:HALO + seg, :] = a[tm + edge:tm + HALO, lanes]
        for j in range(FF_CHUNK // LANES):
            gl = slice(c * FF_CHUNK + j * LANES, c * FF_CHUNK + (j + 1) * LANES)
            vl = slice(D_FF + gl.start, D_FF + gl.stop)
            for s in range(nseg):
                base = edge + s * stride

                def conv(k, lanes):
                    acc = cb_ref[:, lanes]
                    for t in range(CONV_W):
                        tap = slice(t * 2 * D_FF + lanes.start, t * 2 * D_FF + lanes.stop)
                        acc = acc + a_sc[slot, k, base - 1 + t:base - 1 + t + seg, :] * cw_ref[:, tap]
                    return acc

                g = _silu(conv(j, gl)) * conv(2 + j, vl)
                g_sc[s * seg:(s + 1) * seg, gl] = g.astype(BF16)

    x2 = out_ref[...] + mod_ref[:, 5 * D_MODEL:6 * D_MODEL] * _dot(g_sc[...], wd_ref[...])
    out_ref[...] = _rms(x2, fg_ref[...])


def _ffn(x, an, gm, mods3, mod_row, p, ffn_w, seq, tm):
    n = x.shape[0]
    tiles_per_seq = max(seq // tm, 1)
    seg = min(seq, tm)
    nseg = tm // seg
    rows = tm + (HALO if tiles_per_seq > 1 else 0)
    a_rows = nseg * (seg + HALO)
    const = lambda i: (0, 0)
    tile = lambda i: (i, 0)
    resident = dict(pipeline_mode=pl.Buffered(1))
    in_specs = [pl.BlockSpec((tm, D_MODEL), tile), pl.BlockSpec((tm, ATTN_W), tile),
                pl.BlockSpec((tm, GMLP_W), tile)]
    args = [x, an, gm]
    scratch = [pltpu.VMEM((rows, D_MODEL), BF16),
               pltpu.VMEM((A_SLOTS, 2 * FF_CHUNK // LANES, a_rows, LANES), F32),
               pltpu.VMEM((tm, D_FF), BF16)]
    if tiles_per_seq > 1:
        per = tm // HALO
        last = n // HALO - 1
        prev = lambda i: (jnp.maximum(i * per - 1, 0), 0)
        nxt = lambda i: (jnp.minimum((i + 1) * per, last), 0)
        for arr, width in ((x, D_MODEL), (an, ATTN_W), (gm, GMLP_W)):
            in_specs += [pl.BlockSpec((HALO, width), prev), pl.BlockSpec((HALO, width), nxt)]
            args += [arr, arr]
        scratch.append(pltpu.VMEM((rows, D_MODEL), BF16))
    in_specs += [
        pl.BlockSpec((None, 1, N_MODS * D_MODEL), lambda i: (mod_row(i), 0, 0)),
        pl.BlockSpec((D_MODEL, D_MODEL), const, **resident),
        pl.BlockSpec((1, D_MODEL), const),
        pl.BlockSpec((D_MODEL, 2 * D_FF), const, **resident),
        pl.BlockSpec((1, CONV_W * 2 * D_FF), const),
        pl.BlockSpec((1, 2 * D_FF), const),
        pl.BlockSpec((D_FF, D_MODEL), const, **resident),
        pl.BlockSpec((1, D_MODEL), const),
    ]
    w_o, w_up, w_down = ffn_w
    args += [mods3, w_o, p["norm2_g"], w_up, p["conv_w"], p["conv_b"], w_down, p["final_g"]]
    return pl.pallas_call(
        functools.partial(_ffn_kernel, nseg=nseg, seg=seg, tiles_per_seq=tiles_per_seq),
        grid=(n // tm,),
        in_specs=in_specs,
        out_specs=pl.BlockSpec((tm, D_MODEL), tile),
        out_shape=jax.ShapeDtypeStruct((n, D_MODEL), F32),
        scratch_shapes=scratch,
        compiler_params=_params(1),
        name="ffn_halo" if tiles_per_seq > 1 else "ffn",
    )(*args)


def _rope_tables(length):
    pos = np.arange(length)
    row = (pos // GRID_W).astype(np.float32)
    col = (pos % GRID_W).astype(np.float32)
    n_freq = QK_ROPE // 4
    inv = (np.float32(ROPE_THETA) ** (-(np.arange(n_freq, dtype=np.float32) / np.float32(n_freq)))).astype(np.float32)
    ang_r, ang_c = row[:, None] * inv, col[:, None] * inv
    cos32 = np.concatenate([np.cos(ang_r)] * 2 + [np.cos(ang_c)] * 2, axis=-1)
    sin32 = np.concatenate([-np.sin(ang_r), np.sin(ang_r), -np.sin(ang_c), np.sin(ang_c)], axis=-1)
    tail = LANES - ROPE_LANE0 - QK_ROPE
    block = lambda nope, rope: np.concatenate([np.full((length, ROPE_LANE0), nope), rope,
                                               np.zeros((length, tail))], -1)
    scale = SM_SCALE * LOG2_E
    tables = (block(1.0, cos32) * scale, block(0.0, sin32) * scale, block(0.0, cos32), block(0.0, sin32))
    return tuple(jnp.asarray(t, F32) for t in tables)


def _layer_params(l, norm1_g, q_norm_g, kv_norm_g, v_norm_g, b_s, out_g_attn,
                  out_g_gmlp, norm2_g, conv_w, conv_b, final_g):
    row = lambda a: a.reshape(1, -1)
    return {
        "norm1_g": row(norm1_g[l]), "q_norm_g": row(q_norm_g[l]),
        "kv_norm_g": row(kv_norm_g[l]), "v_norm_g": row(v_norm_g[l]), "b_s": b_s[l],
        "out_g_attn": row(out_g_attn[l]), "out_g_gmlp": row(out_g_gmlp[l]), "norm2_g": row(norm2_g[l]),
        "conv_w": conv_w[l].reshape(1, CONV_W * 2 * D_FF), "conv_b": row(conv_b[l]), "final_g": row(final_g),
    }


def kernel(x_prompt, x_sample, cache_ckv, cache_krope, c, c_ctx, ada_w, ada_b, norm1_g, w_in, q_norm_g, w_uq, kv_norm_g, w_ukv, v_norm_g, w_s, b_s, out_g_attn, out_g_gmlp, w_o, norm2_g, w_up, conv_w, conv_b, w_down, final_g):
    batch, seq, _ = x_prompt.shape
    dec_batch, dec_seq, _ = x_sample.shape
    depth, past = cache_ckv.shape[1], cache_ckv.shape[2]
    assert depth == 1 and dec_batch + 1 <= MOD_ROWS
    l = 0
    p = _layer_params(l, norm1_g, q_norm_g, kv_norm_g, v_norm_g, b_s, out_g_attn,
                      out_g_gmlp, norm2_g, conv_w, conv_b, final_g)

    mods3, p["w_qkv"], p["w_uv"], p["w_uq"], p["w_k"], p["w_ve"], p["w_vo"], w_s_b = _adaln(
        c_ctx.reshape(1, D_MODEL), c, ada_w[l], ada_b[l].reshape(1, -1), w_in[l].T, w_uq[l], w_ukv[l],
        w_s[l].reshape(GMLP_GROUPS * CHUNK, CHUNK))
    p["w_s"] = w_s_b.reshape(GMLP_GROUPS, CHUNK, CHUNK)
    tm_pre, tm_ffn = 1024, 512
    ctx_row = lambda i: 0
    lat_row = lambda tm: lambda i: 1 + i // (dec_seq // tm)

    xp = x_prompt.reshape(batch * seq, D_MODEL)
    xs = x_sample.reshape(dec_batch * dec_seq, D_MODEL)
    (qp, kp, vep, vop, gmp, ckv_p, krt_p), (w_down_b,) = _pre(xp, mods3, ctx_row, p, seq, tm_pre, [w_down[l]])
    cache_in = (_rope_tables(dec_seq), cache_ckv[:, l].reshape(dec_batch * past, KV_RANK),
                jnp.swapaxes(cache_krope[:, l], 1, 2))
    (qs, ks, ves, vos, gms), (w_o_b,) = _pre(xs, mods3, lat_row(tm_pre), p, dec_seq, tm_pre, [w_o[l]],
                                             latent=cache_in)

    ans, (w_up_b,) = _attention(qs, ks, ves, vos, p["out_g_attn"], dec_seq, past + dec_seq, 1024, 1, [w_up[l]])
    anp, _ = _attention(qp, kp, vep, vop, p["out_g_attn"], seq, seq, seq, 4, [])
    ffn_w = (w_o_b, w_up_b, w_down_b)
    yp = _ffn(xp, anp, gmp, mods3, ctx_row, p, ffn_w, seq, tm_ffn)
    ys = _ffn(xs, ans, gms, mods3, lat_row(tm_ffn), p, ffn_w, dec_seq, tm_ffn)

    return (yp.reshape(batch, seq, D_MODEL), ys.reshape(dec_batch, dec_seq, D_MODEL),
            ckv_p.reshape(batch, 1, seq, KV_RANK), jnp.swapaxes(krt_p, 1, 2).reshape(batch, 1, seq, QK_ROPE))
```
